```python
import math
import jax
import jax.numpy as jnp
from jax import lax
import numpy as np

D_MODEL = 1024
BATCH = 8
SEQ = 8192
DEPTH = 1
DEC_BATCH = 4
DEC_SEQ = 8192
PAST_LEN = 128

D_MIX = D_MODEL
D_HY = D_MIX // 2
HY_HEADS = 8
HY_ORDER = 2
D_GM = D_MIX - D_HY
GM_HEADS = 8
GM_HEAD_DIM = D_GM // GM_HEADS
GM_CHUNK = 128
D_IN_PROJ = (HY_ORDER + 1) * D_HY + 2 * D_GM
SHORT_CONV = 3
FILTER_EMB = 5
FILTER_WIDTH = 64
FILTER_INNER = 2
DECAY_TARGET = 1e-2
FAST_DECAY_PCT = 0.3
SLOW_DECAY_PCT = 1.5
N_GROUPS = 4
EXPERTS_PER_GROUP = 8
N_EXPERTS = N_GROUPS * EXPERTS_PER_GROUP
TOP_K = 2
D_EXPERT = D_MODEL // 2
EXPERT_BLOCK = 256
EPS = 1e-6

kernel_name = "hybrid_hyena_chunkgmlp_hiermoe_encoder"

F32 = jnp.float32


def rms_norm(x, g):
    x32 = x.astype(F32)
    return x32 * lax.rsqrt(jnp.mean(x32 * x32, axis=-1, keepdims=True) + EPS) * g.astype(F32)


def layer_norm(x, g, b):
    x32 = x.astype(F32)
    mu = jnp.mean(x32, axis=-1, keepdims=True)
    xc = x32 - mu
    var = jnp.mean(xc * xc, axis=-1, keepdims=True)
    return xc * lax.rsqrt(var + EPS) * g.astype(F32) + b.astype(F32)


def head_rms(y, n_heads):
    b, l, c = y.shape
    yh = y.astype(F32).reshape(b, l, n_heads, c // n_heads)
    yh = yh * lax.rsqrt(jnp.mean(yh * yh, axis=-1, keepdims=True) + EPS)
    return yh.reshape(b, l, c)


def hyena_filter_fft(L, w_emb, b_emb, w_inner, b_inner, freq, w_filt):
    t = jnp.linspace(0.0, 1.0, L, dtype=F32)[:, None]
    bands = (FILTER_EMB - 1) // 2
    fr = jnp.linspace(1e-4, bands - 1, bands, dtype=F32)[None, :]
    ang = (2.0 * math.pi / L) * jnp.arange(L, dtype=F32)[:, None] * fr
    emb = jnp.concatenate([t, jnp.cos(ang), -jnp.sin(ang)], axis=-1)
    freq = freq.astype(F32)
    hdn = jnp.sin(freq * (emb @ w_emb.astype(F32) + b_emb.astype(F32)))
    for i in range(FILTER_INNER):
        hdn = jnp.sin(freq * (hdn @ w_inner[i].astype(F32) + b_inner[i].astype(F32)))
    h = (hdn @ w_filt.astype(F32)).reshape(L, HY_ORDER, 2, D_HY)
    max_decay = math.log(DECAY_TARGET) / FAST_DECAY_PCT
    min_decay = math.log(DECAY_TARGET) / SLOW_DECAY_PCT
    deltas = jnp.abs(jnp.linspace(min_decay, max_decay, D_HY, dtype=F32))
    h = h * jnp.exp(-t * deltas)[:, None, None, :]
    fwd, bwd = h[:, :, 0], h[:, :, 1]
    k = jnp.concatenate([fwd, jnp.zeros((1, HY_ORDER, D_HY), F32), bwd[:0:-1]], axis=0)
    return jnp.fft.rfft(k, axis=0)


def long_conv(z, k_fft, skip):
    L = z.shape[1]
    zf = jnp.fft.rfft(z, n=2 * L, axis=1)
    y = jnp.fft.irfft(zf * k_fft[None], n=2 * L, axis=1)[:, :L]
    return y + z * skip


def hyena_mixer(hy, k_fft, short_w, short_b, skip):
    c = hy.shape[-1]
    pad = SHORT_CONV // 2
    hy = lax.conv_general_dilated(hy, short_w[:, None, :], (1,), [(pad, pad)],
                                  dimension_numbers=("NWC", "WIO", "NWC"),
                                  feature_group_count=c) + short_b
    parts = jnp.split(hy.astype(F32), HY_ORDER + 1, axis=-1)
    gates, z = parts[:-1], parts[-1]
    for n, gate in enumerate(gates):
        z = gate * long_conv(z, k_fft[:, n], skip[n].astype(F32))
    return z


def gmlp_mixer(gm, ln_g, ln_b, w_s, b_s):
    b, l, _ = gm.shape
    u, v = jnp.split(jax.nn.gelu(gm.astype(F32), approximate=False), 2, axis=-1)
    v = layer_norm(v, ln_g, ln_b)
    vb = v.reshape(b, l // GM_CHUNK, GM_CHUNK, GM_HEADS, GM_HEAD_DIM)
    s = jnp.einsum("hpq,bnqhd->bnphd", w_s.astype(F32), vb) \
        + b_s.astype(F32).T[None, None, :, :, None]
    return u * s.reshape(b, l, D_GM)


def hier_moe(h, w_group, b_group, w_router, b_router, w_gate, w_up, w_down):
    b, l, d = h.shape
    T = b * l
    ht = h.reshape(T, d)
    g_logits = (ht @ w_group).astype(F32) + b_group.astype(F32)
    grp = jnp.argmax(g_logits, axis=-1)
    p_grp = jnp.take_along_axis(jax.nn.softmax(g_logits, axis=-1), grp[:, None], axis=1)[:, 0]
    e_logits = ((ht @ w_router).astype(F32) + b_router.astype(F32)).reshape(T, N_GROUPS, EXPERTS_PER_GROUP)
    e_sel = jnp.take_along_axis(e_logits, grp[:, None, None], axis=1)[:, 0]
    top_val, top_idx = lax.top_k(e_sel, TOP_K)
    gates = p_grp[:, None] * jax.nn.softmax(top_val, axis=-1)
    eid = (grp[:, None] * EXPERTS_PER_GROUP + top_idx).astype(jnp.int32)

    A = T * TOP_K
    flat_e = eid.reshape(A)
    order = jnp.argsort(flat_e)
    sorted_e = flat_e[order]
    counts = jnp.bincount(flat_e, length=N_EXPERTS)
    pad_counts = (counts + EXPERT_BLOCK - 1) // EXPERT_BLOCK * EXPERT_BLOCK
    pad_end = jnp.cumsum(pad_counts)
    pad_start = pad_end - pad_counts
    start = jnp.cumsum(counts) - counts
    rank = jnp.arange(A, dtype=jnp.int32) - start[sorted_e]
    dest = pad_start[sorted_e] + rank
    n_blocks = -(-A // EXPERT_BLOCK) + N_EXPERTS
    P = n_blocks * EXPERT_BLOCK
    buf = jnp.zeros((P, d), ht.dtype).at[dest].set(ht[order // TOP_K])
    block_e = jnp.minimum(
        jnp.searchsorted(pad_end, jnp.arange(n_blocks, dtype=pad_end.dtype) * EXPERT_BLOCK, side="right"),
        N_EXPERTS - 1)

    def run_block(args):
        xb, e = args
        return (jax.nn.silu(xb @ w_gate[e]) * (xb @ w_up[e])) @ w_down[e]

    out = lax.map(run_block, (buf.reshape(n_blocks, EXPERT_BLOCK, d), block_e)).reshape(P, d)
    y_assign = jnp.zeros((A, d), out.dtype).at[order].set(out[dest]).reshape(T, TOP_K, d)
    y = jnp.einsum("tk,tkd->td", gates.astype(out.dtype), y_assign)
    return y.reshape(b, l, d)


def encoder_layer(x, p, i):
    b, l, _ = x.shape
    dt = x.dtype
    n = rms_norm(x, p["mix_norm_g"][i]).astype(dt)
    proj = n @ p["w_in"][i]
    hy = proj[..., :(HY_ORDER + 1) * D_HY]
    gm = proj[..., (HY_ORDER + 1) * D_HY:]
    k_fft = hyena_filter_fft(l, p["hy_filt_w_emb"][i], p["hy_filt_b_emb"][i], p["hy_filt_w_inner"][i],
                             p["hy_filt_b_inner"][i], p["hy_filt_freq"][i], p["hy_filt_w_out"][i])
    y_hy = hyena_mixer(hy, k_fft, p["hy_short_w"][i], p["hy_short_b"][i], p["hy_skip"][i])
    y_gm = gmlp_mixer(gm, p["gm_ln_g"][i], p["gm_ln_b"][i], p["gm_w_s"][i], p["gm_b_s"][i])
    mixed = jnp.concatenate([head_rms(y_hy, HY_HEADS), head_rms(y_gm, GM_HEADS)], axis=-1) \
        * p["mix_out_g"][i].astype(F32)
    x = x + (mixed.astype(dt) @ p["w_out"][i])
    n = rms_norm(x, p["ffn_norm_g"][i]).astype(dt)
    x = x + hier_moe(n, p["w_group"][i], p["b_group"][i], p["w_expert_router"][i], p["b_expert_router"][i],
                     p["w_e_gate"][i], p["w_e_up"][i], p["w_e_down"][i]).astype(dt)
    return x


def encoder(x, p, final_norm_g):
    for i in range(DEPTH):
        x = encoder_layer(x, p, i)
    return rms_norm(x, final_norm_g).astype(x.dtype)


def setup_inputs(seed: int = 0) -> dict:
    key = jax.random.key(seed)
    ks = jax.random.split(key, 32)

    def nrm(k, shape, scale):
        return scale * jax.random.normal(k, shape, jnp.float32)

    def gain(k, shape):
        return 1.0 + 0.05 * jax.random.normal(k, shape, jnp.float32)

    return {
        "x_prompt": nrm(ks[0], (BATCH, SEQ, D_MODEL), 1.0),
        "x_sample": nrm(ks[1], (DEC_BATCH, DEC_SEQ, D_MODEL), 1.0),
        "mix_norm_g": gain(ks[2], (DEPTH, D_MODEL)),
        "w_in": nrm(ks[3], (DEPTH, D_MODEL, D_IN_PROJ), D_MODEL ** -0.5),
        "hy_short_w": nrm(ks[4], (DEPTH, SHORT_CONV, (HY_ORDER + 1) * D_HY), SHORT_CONV ** -0.5),
        "hy_short_b": nrm(ks[5], (DEPTH, (HY_ORDER + 1) * D_HY), 0.02),
        "hy_filt_w_emb": nrm(ks[6], (DEPTH, FILTER_EMB, FILTER_WIDTH), FILTER_EMB ** -0.5),
        "hy_filt_b_emb": nrm(ks[7], (DEPTH, FILTER_WIDTH), 0.02),
        "hy_filt_w_inner": nrm(ks[8], (DEPTH, FILTER_INNER, FILTER_WIDTH, FILTER_WIDTH), FILTER_WIDTH ** -0.5),
        "hy_filt_b_inner": nrm(ks[9], (DEPTH, FILTER_INNER, FILTER_WIDTH), 0.02),
        "hy_filt_freq": gain(ks[10], (DEPTH, FILTER_WIDTH)),
        "hy_filt_w_out": nrm(ks[11], (DEPTH, FILTER_WIDTH, HY_ORDER * 2 * D_HY), 0.03 * FILTER_WIDTH ** -0.5),
        "hy_skip": nrm(ks[12], (DEPTH, HY_ORDER, D_HY), 0.5),
        "gm_ln_g": gain(ks[13], (DEPTH, D_GM)),
        "gm_ln_b": nrm(ks[14], (DEPTH, D_GM), 0.02),
        "gm_w_s": nrm(ks[15], (DEPTH, GM_HEADS, GM_CHUNK, GM_CHUNK), GM_CHUNK ** -0.5),
        "gm_b_s": nrm(ks[16], (DEPTH, GM_HEADS, GM_CHUNK), 0.02),
        "mix_out_g": gain(ks[17], (DEPTH, D_MIX)),
        "w_out": nrm(ks[18], (DEPTH, D_MIX, D_MODEL), D_MIX ** -0.5),
        "ffn_norm_g": gain(ks[19], (DEPTH, D_MODEL)),
        "w_group": nrm(ks[20], (DEPTH, D_MODEL, N_GROUPS), D_MODEL ** -0.5),
        "b_group": nrm(ks[21], (DEPTH, N_GROUPS), 0.01),
        "w_expert_router": nrm(ks[22], (DEPTH, D_MODEL, N_EXPERTS), D_MODEL ** -0.5),
        "b_expert_router": nrm(ks[23], (DEPTH, N_EXPERTS), 0.01),
        "w_e_gate": nrm(ks[24], (DEPTH, N_EXPERTS, D_MODEL, D_EXPERT), D_MODEL ** -0.5),
        "w_e_up": nrm(ks[25], (DEPTH, N_EXPERTS, D_MODEL, D_EXPERT), D_MODEL ** -0.5),
        "w_e_down": nrm(ks[26], (DEPTH, N_EXPERTS, D_EXPERT, D_MODEL), D_EXPERT ** -0.5),
        "final_norm_g": gain(ks[27], (D_MODEL,)),
    }


def reference(x_prompt, x_sample, mix_norm_g, w_in, hy_short_w, hy_short_b, hy_filt_w_emb, hy_filt_b_emb,
              hy_filt_w_inner, hy_filt_b_inner, hy_filt_freq, hy_filt_w_out, hy_skip, gm_ln_g, gm_ln_b,
              gm_w_s, gm_b_s, mix_out_g, w_out, ffn_norm_g, w_group, b_group, w_expert_router,
              b_expert_router, w_e_gate, w_e_up, w_e_down, final_norm_g):
    p = dict(mix_norm_g=mix_norm_g, w_in=w_in, hy_short_w=hy_short_w, hy_short_b=hy_short_b,
             hy_filt_w_emb=hy_filt_w_emb, hy_filt_b_emb=hy_filt_b_emb, hy_filt_w_inner=hy_filt_w_inner,
             hy_filt_b_inner=hy_filt_b_inner, hy_filt_freq=hy_filt_freq, hy_filt_w_out=hy_filt_w_out,
             hy_skip=hy_skip, gm_ln_g=gm_ln_g, gm_ln_b=gm_ln_b, gm_w_s=gm_w_s, gm_b_s=gm_b_s,
             mix_out_g=mix_out_g, w_out=w_out, ffn_norm_g=ffn_norm_g, w_group=w_group, b_group=b_group,
             w_expert_router=w_expert_router, b_expert_router=b_expert_router, w_e_gate=w_e_gate,
             w_e_up=w_e_up, w_e_down=w_e_down)
    y_prompt = encoder(x_prompt, p, final_norm_g)
    y_sample = encoder(x_sample, p, final_norm_g)
    return (y_prompt, y_sample)
```

```python
import functools
import math

import jax
import jax.numpy as jnp
from jax import lax
from jax.experimental import pallas as pl
from jax.experimental.pallas import tpu as pltpu

F32 = jnp.float32
BF16 = jnp.bfloat16
U32 = jnp.uint32
I32 = jnp.int32

EPS = 1e-6
HY_ORDER = 2
HY_HEADS = 8
GM_HEADS = 8
GM_CHUNK = 128
N_GROUPS = 4
EXPERTS_PER_GROUP = 8
N_EXPERTS = N_GROUPS * EXPERTS_PER_GROUP
TOP_K = 2
FILTER_EMB = 5
DECAY_TARGET = 1e-2
FAST_DECAY_PCT = 0.3
SLOW_DECAY_PCT = 1.5

LANES = 128
ROUTE_LANES = LANES
SLAB = 16
CONV_LANES = 256
ROW_TILE = 512
EXPERT_ROWS = 1024
HIGHEST = lax.Precision.HIGHEST
VMEM_LIMIT = 56 * 1024 * 1024


def _cparams(n_axes, vmem=None):
    return pltpu.CompilerParams(dimension_semantics=("arbitrary",) * n_axes,
                                vmem_limit_bytes=vmem)


def _inproj_kernel(x_ref, g_ref, w_ref, hy_ref, gm_ref, *, d_hy3):
    x = x_ref[...]
    ms = jnp.mean(x * x, axis=-1, keepdims=True)
    n = (x * lax.rsqrt(ms + EPS) * g_ref[...]).astype(BF16)
    p = jnp.dot(n, w_ref[...], preferred_element_type=F32)
    hy_ref[...] = p[:, :d_hy3]
    gm_ref[...] = p[:, d_hy3:].astype(BF16)


def _inproj(x2, g, w_bf, d_hy3):
    t, d = x2.shape
    dp = w_bf.shape[1]
    return pl.pallas_call(
        functools.partial(_inproj_kernel, d_hy3=d_hy3),
        grid=(t // ROW_TILE,),
        in_specs=[pl.BlockSpec((ROW_TILE, d), lambda i: (i, 0)),
                  pl.BlockSpec((1, d), lambda i: (0, 0)),
                  pl.BlockSpec((d, dp), lambda i: (0, 0))],
        out_specs=[pl.BlockSpec((ROW_TILE, d_hy3), lambda i: (i, 0)),
                   pl.BlockSpec((ROW_TILE, dp - d_hy3), lambda i: (i, 0))],
        out_shape=[jax.ShapeDtypeStruct((t, d_hy3), F32),
                   jax.ShapeDtypeStruct((t, dp - d_hy3), BF16)],
        compiler_params=_cparams(1, VMEM_LIMIT),
        name="inproj",
    )(x2, g, w_bf)


def _shortconv_kernel(v_ref, hp_ref, hn_ref, w_ref, b_ref, o_ref, *, tiles_per_seq):
    i = pl.program_id(0)
    v = v_ref[...]
    rows = v.shape[0]
    row = lax.broadcasted_iota(I32, (rows, 1), 0)
    pos = i % tiles_per_seq
    first = jnp.where(pos == 0, 0.0, 1.0)
    last = jnp.where(pos == tiles_per_seq - 1, 0.0, 1.0)
    prev = jnp.where(row == 0, hp_ref[7:8, :] * first, pltpu.roll(v, 1, 0))
    nxt = jnp.where(row == rows - 1, hn_ref[0:1, :] * last, pltpu.roll(v, rows - 1, 0))
    o_ref[...] = w_ref[0:1, :] * prev + w_ref[1:2, :] * v + w_ref[2:3, :] * nxt + b_ref[...]


def _shortconv(hy, w, b, seq_len):
    t, c = hy.shape
    tiles_per_seq = seq_len // ROW_TILE
    hb = ROW_TILE // 8
    nhb = t // 8
    return pl.pallas_call(
        functools.partial(_shortconv_kernel, tiles_per_seq=tiles_per_seq),
        grid=(t // ROW_TILE,),
        in_specs=[pl.BlockSpec((ROW_TILE, c), lambda i: (i, 0)),
                  pl.BlockSpec((8, c), lambda i: (jnp.maximum(i * hb - 1, 0), 0)),
                  pl.BlockSpec((8, c), lambda i: (jnp.minimum((i + 1) * hb, nhb - 1), 0)),
                  pl.BlockSpec((3, c), lambda i: (0, 0)),
                  pl.BlockSpec((1, c), lambda i: (0, 0))],
        out_specs=pl.BlockSpec((ROW_TILE, c), lambda i: (i, 0)),
        out_shape=jax.ShapeDtypeStruct((t, c), F32),
        compiler_params=_cparams(1, VMEM_LIMIT),
        name="shortconv",
    )(hy, hy, hy, w, b)


def _cos_sin(m, period):
    ang = m.astype(F32) * (2.0 * math.pi / period)
    return jnp.cos(ang), jnp.sin(ang)


def _stack_complex(mr, mi):
    top = jnp.concatenate([mr, -mi], axis=-1)
    bot = jnp.concatenate([mi, mr], axis=-1)
    return jnp.concatenate([top, bot], axis=-2)


def _dft_tables(nf):
    na = nf // 2
    n = nf * nf
    idx = jnp.arange(nf, dtype=I32)
    c, s = _cos_sin((idx[:, None] * idx[None, :]) % nf, nf)
    f2 = _stack_complex(c, -s)
    f2i = _stack_complex(c, s)
    b_ = idx[:, None, None]
    ka = idx[None, :, None]
    a_ = idx[None, None, :]
    m1 = (nf * a_ * ka + b_ * ka) % n
    c1, s1 = _cos_sin(m1, n)
    f1 = _stack_complex(c1[:, :, :na], -s1[:, :, :na])
    f1_full = jnp.concatenate([c1, -s1], axis=1)
    c3 = jnp.swapaxes(c1, 1, 2)[:, :na, :]
    s3 = jnp.swapaxes(s1, 1, 2)[:, :na, :]
    f3 = _stack_complex(c3, s3)
    return f1, f2, f2i, f3, f1_full


def _filter_taps_kernel(wemb_ref, bemb_ref, win_ref, bin_ref, freq_ref, wout_ref, delta_ref, o_ref,
                        *, seq_len, n_inner, d_hy):
    rows = o_ref.shape[0]
    j = pl.program_id(0) * rows + lax.broadcasted_iota(I32, (rows, 1), 0)
    tidx = jnp.where(j < seq_len, j, 2 * seq_len - j).astype(F32)
    t = tidx / float(seq_len - 1)
    fr0 = jnp.float32(1e-4)
    ang0 = (2.0 * math.pi / seq_len) * tidx * fr0
    ang1 = (2.0 * math.pi / seq_len) * tidx
    freq = freq_ref[...]
    pre = (t * wemb_ref[0:1, :] + jnp.cos(ang0) * wemb_ref[1:2, :] + jnp.cos(ang1) * wemb_ref[2:3, :]
           - jnp.sin(ang0) * wemb_ref[3:4, :] - jnp.sin(ang1) * wemb_ref[4:5, :] + bemb_ref[...])
    hdn = jnp.sin(freq * pre)
    for i in range(n_inner):
        hdn = jnp.sin(freq * (jnp.dot(hdn, win_ref[i], precision=HIGHEST,
                                      preferred_element_type=F32) + bin_ref[i:i + 1, :]))
    h = jnp.dot(hdn, wout_ref[...], precision=HIGHEST, preferred_element_type=F32)
    decay = jnp.exp(-t * delta_ref[...])
    for o in range(HY_ORDER):
        fwd = h[:, (2 * o) * d_hy:(2 * o + 1) * d_hy]
        bwd = h[:, (2 * o + 1) * d_hy:(2 * o + 2) * d_hy]
        val = jnp.where(j < seq_len, fwd, bwd) * decay
        o_ref[:, o * d_hy:(o + 1) * d_hy] = jnp.where(j == seq_len, 0.0, val)


def _filter_taps(wemb, bemb, win, bin_, freq, wout, deltas, seq_len, d_hy):
    rows = 1024
    n2 = 2 * seq_len
    width = wemb.shape[1]
    n_inner = win.shape[0]
    full = lambda *shape: pl.BlockSpec(shape, lambda i: (0,) * len(shape))
    return pl.pallas_call(
        functools.partial(_filter_taps_kernel, seq_len=seq_len, n_inner=n_inner, d_hy=d_hy),
        grid=(n2 // rows,),
        in_specs=[full(FILTER_EMB, width), full(1, width), full(n_inner, width, width),
                  full(n_inner, width), full(1, width), full(width, 2 * HY_ORDER * d_hy),
                  full(1, d_hy)],
        out_specs=pl.BlockSpec((rows, HY_ORDER * d_hy), lambda i: (i, 0)),
        out_shape=jax.ShapeDtypeStruct((n2, HY_ORDER * d_hy), F32),
        compiler_params=_cparams(1, VMEM_LIMIT),
        name="filter_taps",
    )(wemb, bemb, win, bin_, freq, wout, deltas)


def _filter_fft1_kernel(k_ref, f_ref, o_ref):
    for bl in range(SLAB):
        rhs = k_ref[:, bl, :]
        o_ref[bl] = jnp.dot(f_ref[bl], rhs, precision=HIGHEST, preferred_element_type=F32)


def _filter_fft2_kernel(g_ref, f_ref, o_ref, *, scale):
    for kl in range(SLAB):
        rhs = jnp.concatenate([g_ref[:, 0, kl, :], g_ref[:, 1, kl, :]], axis=0)
        o_ref[kl] = jnp.dot(f_ref[...], rhs, precision=HIGHEST, preferred_element_type=F32) * scale


def _filter_spectrum(taps, f1_full, f2, nf):
    c = taps.shape[1]
    k3 = taps.reshape(nf, nf, c)
    g = pl.pallas_call(
        _filter_fft1_kernel,
        grid=(nf // SLAB, c // CONV_LANES),
        in_specs=[pl.BlockSpec((nf, SLAB, CONV_LANES), lambda i, j: (0, i, j)),
                  pl.BlockSpec((SLAB, 2 * nf, nf), lambda i, j: (i, 0, 0))],
        out_specs=pl.BlockSpec((SLAB, 2 * nf, CONV_LANES), lambda i, j: (i, 0, j)),
        out_shape=jax.ShapeDtypeStruct((nf, 2 * nf, c), F32),
        compiler_params=_cparams(2, VMEM_LIMIT),
        name="filter_fft1",
    )(k3, f1_full)
    g4 = g.reshape(nf, 2, nf, c)
    return pl.pallas_call(
        functools.partial(_filter_fft2_kernel, scale=1.0 / (nf * nf)),
        grid=(nf // SLAB, c // CONV_LANES),
        in_specs=[pl.BlockSpec((nf, 2, SLAB, CONV_LANES), lambda i, j: (0, 0, i, j)),
                  pl.BlockSpec((2 * nf, 2 * nf), lambda i, j: (0, 0))],
        out_specs=pl.BlockSpec((SLAB, 2 * nf, CONV_LANES), lambda i, j: (i, 0, j)),
        out_shape=jax.ShapeDtypeStruct((nf, 2 * nf, c), F32),
        compiler_params=_cparams(2, VMEM_LIMIT),
        name="filter_fft2",
    )(g4, f2)


def _pack(re, im):
    rb = lax.bitcast_convert_type(re.astype(BF16).astype(F32), U32)
    ib = lax.bitcast_convert_type(im.astype(BF16).astype(F32), U32)
    return rb | (ib >> 16)


def _unpack(w):
    re = lax.bitcast_convert_type(w & jnp.uint32(0xFFFF0000), F32).astype(BF16)
    im = lax.bitcast_convert_type(w << 16, F32).astype(BF16)
    return re, im


def _longconv_kernel(z1_ref, f1_ref, kf_ref, f2_ref, f2i_ref, f3_ref, gate_ref, zs_ref, skip_ref,
                     o_ref, g_ref, *, nf):
    na = nf // 2
    ns = nf // SLAB
    halves = g_ref.shape[1]
    t = pl.program_id(2)

    def store_col(blk, bl, val):
        for h in range(halves):
            g_ref[blk, h, pl.ds(bl, nf, stride=SLAB), :] = val[:, h * LANES:(h + 1) * LANES]

    def load_col(blk, bl):
        return jnp.concatenate([g_ref[blk, h, pl.ds(bl, nf, stride=SLAB), :] for h in range(halves)],
                               axis=1)

    @pl.when(t < ns)
    def _phase1():
        for bl in range(SLAB):
            rhs = jnp.concatenate([z1_ref[0, :, bl, :], z1_ref[1, :, bl, :]], axis=0).astype(BF16)
            out = jnp.dot(f1_ref[bl], rhs, preferred_element_type=F32)
            store_col(t, bl, _pack(out[:nf], out[nf:]))

    @pl.when(jnp.logical_and(t >= ns, t < 2 * ns))
    def _phase2():
        i = t - ns

        def body(kl, carry):
            row0 = pl.multiple_of((i * SLAB + kl) * SLAB, SLAB)
            w = jnp.concatenate([g_ref[:, h, pl.ds(row0, SLAB), :].reshape(nf, LANES)
                                 for h in range(halves)], axis=1)
            re, im = _unpack(w)
            s = jnp.dot(f2_ref[...], jnp.concatenate([re, im], axis=0),
                        preferred_element_type=F32)
            sr, si = s[:nf], s[nf:]
            kr, ki = kf_ref[kl, :nf, :], kf_ref[kl, nf:, :]
            pr = (sr * kr - si * ki).astype(BF16)
            pi = (sr * ki + si * kr).astype(BF16)
            h = jnp.dot(f2i_ref[...], jnp.concatenate([pr, pi], axis=0),
                        preferred_element_type=F32)
            packed = _pack(h[:nf], h[nf:])
            for hf in range(halves):
                g_ref[:, hf, pl.ds(row0, SLAB), :] = packed[:, hf * LANES:(hf + 1) * LANES].reshape(
                    ns, SLAB, LANES)
            return carry

        lax.fori_loop(0, SLAB, body, 0)

    @pl.when(t >= 2 * ns)
    def _phase3():
        j = t - 2 * ns
        skip = skip_ref[...]
        for bl in range(SLAB):
            re, im = _unpack(load_col(j, bl))
            y = jnp.dot(f3_ref[bl], jnp.concatenate([re, im], axis=0),
                        preferred_element_type=F32)
            for s in range(2):
                zs = zs_ref[s, :, bl, :]
                o_ref[s, :, bl, :] = gate_ref[s, :, bl, :] * (y[s * na:(s + 1) * na] + zs * skip)


def _longconv(hy4, z4, kf, tables, skip, gate_col, z_col, kf_col, nf):
    f1, f2, f2i, f3 = tables
    b, na, _, _ = z4.shape
    c = skip.shape[1]
    ns = nf // SLAB
    nchunk = c // CONV_LANES
    p1 = lambda t: jnp.minimum(t, ns - 1)
    p2 = lambda t: jnp.clip(t - ns, 0, ns - 1)
    p3 = lambda t: jnp.clip(t - 2 * ns, 0, ns - 1)
    cl = CONV_LANES
    return pl.pallas_call(
        functools.partial(_longconv_kernel, nf=nf),
        grid=(b // 2, nchunk, 3 * ns),
        in_specs=[
            pl.BlockSpec((2, na, SLAB, cl), lambda p, q, t: (p, 0, p1(t), z_col + q)),
            pl.BlockSpec((SLAB, 2 * nf, 2 * na), lambda p, q, t: (p1(t), 0, 0)),
            pl.BlockSpec((SLAB, 2 * nf, cl), lambda p, q, t: (p2(t), 0, kf_col + q)),
            pl.BlockSpec((2 * nf, 2 * nf), lambda p, q, t: (0, 0)),
            pl.BlockSpec((2 * nf, 2 * nf), lambda p, q, t: (0, 0)),
            pl.BlockSpec((SLAB, 2 * na, 2 * nf), lambda p, q, t: (p3(t), 0, 0)),
            pl.BlockSpec((2, na, SLAB, cl), lambda p, q, t: (p, 0, p3(t), gate_col + q)),
            pl.BlockSpec((2, na, SLAB, cl), lambda p, q, t: (p, 0, p3(t), z_col + q)),
            pl.BlockSpec((1, cl), lambda p, q, t: (0, q)),
        ],
        out_specs=pl.BlockSpec((2, na, SLAB, cl), lambda p, q, t: (p, 0, p3(t), q)),
        out_shape=jax.ShapeDtypeStruct((b, na, nf, c), F32),
        scratch_shapes=[pltpu.VMEM((ns, cl // LANES, nf * SLAB, LANES), U32)],
        compiler_params=_cparams(3, VMEM_LIMIT),
        name="longconv",
    )(z4, f1, kf, f2, f2i, f3, hy4, z4, skip)


def _gelu_exact(x):
    return 0.5 * x * (1.0 + lax.erf(x * (1.0 / math.sqrt(2.0))))


def _gmlp_kernel(gm_ref, lng_ref, lnb_ref, ws_ref, bias_ref, o_ref, *, d_gm):
    g = _gelu_exact(gm_ref[...].astype(F32))
    u, v = g[:, :d_gm], g[:, d_gm:]
    mu = jnp.mean(v, axis=-1, keepdims=True)
    vc = v - mu
    var = jnp.mean(vc * vc, axis=-1, keepdims=True)
    vh = (vc * lax.rsqrt(var + EPS) * lng_ref[...] + lnb_ref[...]).astype(BF16)
    head = lax.broadcasted_iota(I32, (GM_CHUNK, d_gm), 1) // (d_gm // GM_HEADS)
    for c in range(gm_ref.shape[0] // GM_CHUNK):
        rows = slice(c * GM_CHUNK, (c + 1) * GM_CHUNK)
        r = jnp.dot(ws_ref[...], vh[rows], preferred_element_type=F32)
        s = r[:GM_CHUNK]
        for h in range(1, GM_HEADS):
            s = jnp.where(head == h, r[h * GM_CHUNK:(h + 1) * GM_CHUNK], s)
        o_ref[rows, :] = u[rows] * (s + bias_ref[...])


def _gmlp(gm, ln_g, ln_b, ws_stack, bias_t):
    t, c2 = gm.shape
    d_gm = c2 // 2
    return pl.pallas_call(
        functools.partial(_gmlp_kernel, d_gm=d_gm),
        grid=(t // ROW_TILE,),
        in_specs=[pl.BlockSpec((ROW_TILE, c2), lambda i: (i, 0)),
                  pl.BlockSpec((1, d_gm), lambda i: (0, 0)),
                  pl.BlockSpec((1, d_gm), lambda i: (0, 0)),
                  pl.BlockSpec(ws_stack.shape, lambda i: (0, 0)),
                  pl.BlockSpec(bias_t.shape, lambda i: (0, 0))],
        out_specs=pl.BlockSpec((ROW_TILE, d_gm), lambda i: (i, 0)),
        out_shape=jax.ShapeDtypeStruct((t, d_gm), F32),
        compiler_params=_cparams(1, VMEM_LIMIT),
        name="gmlp",
    )(gm, ln_g, ln_b, ws_stack, bias_t)


def _head_rms(y, bd):
    ms = jnp.dot((y * y).astype(BF16), bd, preferred_element_type=F32)
    return y * lax.rsqrt(ms + EPS)


def _route(logits):
    lane = lax.broadcasted_iota(I32, logits.shape, 1)
    neg = jnp.float32(-1e30)
    big = jnp.int32(ROUTE_LANES)
    gmask = lane < N_GROUPS
    gl = jnp.where(gmask, logits, neg)
    gmax = jnp.max(gl, axis=-1, keepdims=True)
    grp = jnp.min(jnp.where(jnp.logical_and(gl == gmax, gmask), lane, big), axis=-1, keepdims=True)
    psum = jnp.sum(jnp.where(gmask, jnp.exp(gl - gmax), 0.0), axis=-1, keepdims=True)
    p_grp = 1.0 / psum
    lo = N_GROUPS + EXPERTS_PER_GROUP * grp
    emask = jnp.logical_and(lane >= lo, lane < lo + EXPERTS_PER_GROUP)
    el = jnp.where(emask, logits, neg)
    m1 = jnp.max(el, axis=-1, keepdims=True)
    i1 = jnp.min(jnp.where(jnp.logical_and(el == m1, emask), lane, big), axis=-1, keepdims=True)
    emask2 = jnp.logical_and(emask, lane != i1)
    el2 = jnp.where(emask2, logits, neg)
    m2 = jnp.max(el2, axis=-1, keepdims=True)
    i2 = jnp.min(jnp.where(jnp.logical_and(el2 == m2, emask2), lane, big), axis=-1, keepdims=True)
    d = jnp.exp(m2 - m1)
    g1 = p_grp * (1.0 / (1.0 + d))
    g2 = p_grp * (d / (1.0 + d))
    e1 = (i1 - N_GROUPS).astype(F32)
    e2 = (i2 - N_GROUPS).astype(F32)
    return jnp.where(lane == 0, e1, jnp.where(lane == 1, e2, jnp.where(lane == 2, g1,
                     jnp.where(lane == 3, g2, 0.0))))


def _outproj_kernel(x_ref, yh_ref, yg_ref, mg_ref, bd_ref, wo_ref, fg_ref, wrh_ref, wrl_ref, br_ref,
                    x1_ref, n2_ref, route_ref, *, d_hy):
    bd = bd_ref[...]
    mg = mg_ref[...]
    mh = (_head_rms(yh_ref[...], bd) * mg[:, :d_hy]).astype(BF16)
    mgm = (_head_rms(yg_ref[...], bd) * mg[:, d_hy:]).astype(BF16)
    wo = wo_ref[...]
    x1 = (x_ref[...] + jnp.dot(mh, wo[:d_hy], preferred_element_type=F32)
          + jnp.dot(mgm, wo[d_hy:], preferred_element_type=F32))
    x1_ref[...] = x1
    ms = jnp.mean(x1 * x1, axis=-1, keepdims=True)
    n2 = x1 * lax.rsqrt(ms + EPS) * fg_ref[...]
    n2_ref[...] = n2
    hi = n2.astype(BF16)
    lo = (n2 - hi.astype(F32)).astype(BF16)
    wrh = wrh_ref[...]
    logits = (jnp.dot(hi, wrh, preferred_element_type=F32) + jnp.dot(lo, wrh, preferred_element_type=F32)
              + jnp.dot(hi, wrl_ref[...], preferred_element_type=F32) + br_ref[...])
    route_ref[...] = _route(logits)


def _outproj(x2, yh, yg, mix_g, bd, wo_bf, ffn_g, wr_hi, wr_lo, br):
    t, d = x2.shape
    d_hy = yh.shape[1]
    row = lambda w: pl.BlockSpec((ROW_TILE, w), lambda i: (i, 0))
    full = lambda a: pl.BlockSpec(a.shape, lambda i: (0,) * a.ndim)
    return pl.pallas_call(
        functools.partial(_outproj_kernel, d_hy=d_hy),
        grid=(t // ROW_TILE,),
        in_specs=[row(d), row(d_hy), row(yg.shape[1]), full(mix_g), full(bd), full(wo_bf), full(ffn_g),
                  full(wr_hi), full(wr_lo), full(br)],
        out_specs=[row(d), row(d), row(ROUTE_LANES)],
        out_shape=[jax.ShapeDtypeStruct((t, d), F32), jax.ShapeDtypeStruct((t, d), F32),
                   jax.ShapeDtypeStruct((t, ROUTE_LANES), F32)],
        compiler_params=_cparams(1, VMEM_LIMIT),
        name="outproj_router",
    )(x2, yh, yg, mix_g, bd, wo_bf, ffn_g, wr_hi, wr_lo, br)


def _lane_pick(vals, sel_lane):
    lane = lax.broadcasted_iota(I32, vals.shape, 1)
    return jnp.sum(jnp.where(lane == sel_lane, vals, 0.0), axis=-1, keepdims=True)


def _rank_kernel(route_ref, rank_ref, count_ref, carry_ref):
    @pl.when(pl.program_id(0) == 0)
    def _init():
        carry_ref[...] = jnp.zeros_like(carry_ref)

    route = route_ref[...]
    rows = route.shape[0]
    lane = lax.broadcasted_iota(I32, route.shape, 1)
    e0 = route[:, 0:1].astype(I32)
    e1 = route[:, 1:2].astype(I32)
    oh0 = lane == e0
    oh1 = lane == e1
    oh = jnp.where(jnp.logical_or(oh0, oh1), 1.0, 0.0)
    r = lax.broadcasted_iota(I32, (rows, rows), 0)
    c = lax.broadcasted_iota(I32, (rows, rows), 1)
    ltri = jnp.where(c < r, 1.0, 0.0).astype(BF16)
    tot = jnp.dot(ltri, oh.astype(BF16), preferred_element_type=F32) + carry_ref[...]
    rank0 = jnp.sum(jnp.where(oh0, tot, 0.0), axis=-1, keepdims=True)
    rank1 = jnp.sum(jnp.where(oh1, tot, 0.0), axis=-1, keepdims=True)
    rank_ref[...] = jnp.where(lane == 0, rank0, jnp.where(lane == 1, rank1, 0.0))
    carry_ref[...] += jnp.sum(oh, axis=0, keepdims=True)
    count_ref[...] = carry_ref[...]


def _ranks(route):
    t = route.shape[0]
    return pl.pallas_call(
        _rank_kernel,
        grid=(t // ROW_TILE,),
        in_specs=[pl.BlockSpec((ROW_TILE, ROUTE_LANES), lambda i: (i, 0))],
        out_specs=[pl.BlockSpec((ROW_TILE, ROUTE_LANES), lambda i: (i, 0)),
                   pl.BlockSpec((1, ROUTE_LANES), lambda i: (0, 0))],
        out_shape=[jax.ShapeDtypeStruct((t, ROUTE_LANES), F32),
                   jax.ShapeDtypeStruct((1, ROUTE_LANES), F32)],
        scratch_shapes=[pltpu.VMEM((1, ROUTE_LANES), F32)],
        compiler_params=_cparams(1),
        name="moe_rank",
    )(route)


def _dest_kernel(route_ref, rank_ref, pstart_ref, dest_ref):
    route = route_ref[...]
    rank = rank_ref[...]
    lane = lax.broadcasted_iota(I32, route.shape, 1)
    ps = jnp.broadcast_to(pstart_ref[...], route.shape)
    d0 = _lane_pick(ps, route[:, 0:1].astype(I32)) + rank[:, 0:1]
    d1 = _lane_pick(ps, route[:, 1:2].astype(I32)) + rank[:, 1:2]
    dest_ref[...] = jnp.where(lane == 0, d0, jnp.where(lane == 1, d1, 0.0)).astype(I32)


def _dests(route, rank, pstart):
    t = route.shape[0]
    row = pl.BlockSpec((ROW_TILE, ROUTE_LANES), lambda i: (i, 0))
    return pl.pallas_call(
        _dest_kernel,
        grid=(t // ROW_TILE,),
        in_specs=[row, row, pl.BlockSpec((1, ROUTE_LANES), lambda i: (0, 0))],
        out_specs=row,
        out_shape=jax.ShapeDtypeStruct((t, ROUTE_LANES), I32),
        compiler_params=_cparams(1),
        name="moe_dest",
    )(route, rank, pstart)


def _row_copy(src_ref, src_row, dst_ref, dst_row, sem):
    return pltpu.make_async_copy(src_ref.at[pl.ds(src_row, 1)], dst_ref.at[pl.ds(dst_row, 1)], sem)


def _dispatch_kernel(zrow_ref, zflag_ref, dest_ref, n2_ref, buf_ref, zero_ref, sem, zsem):
    i = pl.program_id(0)
    rows = dest_ref.shape[-1] // TOP_K

    def zero_copy(e):
        start = pl.multiple_of(zrow_ref[e], EXPERT_ROWS)
        return pltpu.make_async_copy(zero_ref, buf_ref.at[pl.ds(start, EXPERT_ROWS)], zsem)

    @pl.when(i == 0)
    def _zero_tail_blocks():
        zero_ref[...] = jnp.zeros_like(zero_ref)
        for e in range(N_EXPERTS):
            @pl.when(zflag_ref[e] > 0)
            def _():
                zero_copy(e).start()
        for e in range(N_EXPERTS):
            @pl.when(zflag_ref[e] > 0)
            def _():
                zero_copy(e).wait()

    base = i * rows

    def issue(r, carry):
        for k in range(TOP_K):
            _row_copy(n2_ref, base + r, buf_ref, dest_ref[0, 0, TOP_K * r + k], sem).start()
        return carry

    lax.fori_loop(0, rows, issue, 0)
    pltpu.make_async_copy(n2_ref.at[pl.ds(0, TOP_K * rows)], buf_ref.at[pl.ds(0, TOP_K * rows)],
                          sem).wait()


def _dispatch(n2, dest3, zrow, zflag, n_buf_rows):
    t, d = n2.shape
    rows = ROW_TILE
    return pl.pallas_call(
        _dispatch_kernel,
        grid_spec=pltpu.PrefetchScalarGridSpec(
            num_scalar_prefetch=2,
            grid=(t // rows,),
            in_specs=[pl.BlockSpec((1, 1, TOP_K * rows), lambda i, zr, zf: (i, 0, 0),
                                   memory_space=pltpu.SMEM),
                      pl.BlockSpec(memory_space=pl.ANY)],
            out_specs=pl.BlockSpec(memory_space=pl.ANY),
            scratch_shapes=[pltpu.VMEM((EXPERT_ROWS, d), F32), pltpu.SemaphoreType.DMA(()),
                            pltpu.SemaphoreType.DMA(())]),
        out_shape=jax.ShapeDtypeStruct((n_buf_rows, d), F32),
        compiler_params=_cparams(1, VMEM_LIMIT),
        name="moe_dispatch",
    )(zrow, zflag, dest3, n2)


def _ffn_kernel(be_ref, nu_ref, x_ref, wg_ref, wu_ref, wd_ref, o_ref):
    @pl.when(pl.program_id(0) < nu_ref[0])
    def _():
        xb = x_ref[...].astype(BF16)
        g = jnp.dot(xb, wg_ref[0], preferred_element_type=F32)
        u = jnp.dot(xb, wu_ref[0], preferred_element_type=F32)
        h = (g * (1.0 / (1.0 + jnp.exp(-g))) * u).astype(BF16)
        o_ref[...] = jnp.dot(h, wd_ref[0], preferred_element_type=F32)


def _expert_ffn(buf, block_e, n_used, wg, wu, wd):
    p, d = buf.shape
    de = wg.shape[2]
    nb = p // EXPERT_ROWS
    blk = lambda i, be, nu: jnp.minimum(i, nu[0] - 1)
    return pl.pallas_call(
        _ffn_kernel,
        grid_spec=pltpu.PrefetchScalarGridSpec(
            num_scalar_prefetch=2,
            grid=(nb,),
            in_specs=[pl.BlockSpec((EXPERT_ROWS, d), lambda i, be, nu: (blk(i, be, nu), 0)),
                      pl.BlockSpec((1, d, de), lambda i, be, nu: (be[blk(i, be, nu)], 0, 0)),
                      pl.BlockSpec((1, d, de), lambda i, be, nu: (be[blk(i, be, nu)], 0, 0)),
                      pl.BlockSpec((1, de, d), lambda i, be, nu: (be[blk(i, be, nu)], 0, 0))],
            out_specs=pl.BlockSpec((EXPERT_ROWS, d), lambda i, be, nu: (blk(i, be, nu), 0))),
        out_shape=jax.ShapeDtypeStruct((p, d), F32),
        compiler_params=_cparams(1, VMEM_LIMIT),
        name="expert_ffn",
    )(block_e, n_used, buf, wg, wu, wd)


def _combine_kernel(dest_ref, route_ref, x1_ref, fg_ref, obuf_ref, o_ref, r0_ref, r1_ref, sem):
    rows = x1_ref.shape[0]
    bufs = (r0_ref, r1_ref)

    def issue(r, carry):
        for k in range(TOP_K):
            _row_copy(obuf_ref, dest_ref[0, 0, TOP_K * r + k], bufs[k], r, sem).start()
        return carry

    lax.fori_loop(0, rows, issue, 0)
    for k in range(TOP_K):
        pltpu.make_async_copy(obuf_ref.at[pl.ds(0, rows)], bufs[k], sem).wait()
    route = route_ref[...]
    x2 = x1_ref[...] + route[:, 2:3] * r0_ref[...] + route[:, 3:4] * r1_ref[...]
    ms = jnp.mean(x2 * x2, axis=-1, keepdims=True)
    o_ref[...] = x2 * lax.rsqrt(ms + EPS) * fg_ref[...]


def _combine(dest3, route, x1, final_g, obuf):
    t, d = x1.shape
    rows = ROW_TILE
    return pl.pallas_call(
        _combine_kernel,
        grid=(t // rows,),
        in_specs=[pl.BlockSpec((1, 1, TOP_K * rows), lambda i: (i, 0, 0), memory_space=pltpu.SMEM),
                  pl.BlockSpec((rows, ROUTE_LANES), lambda i: (i, 0)),
                  pl.BlockSpec((rows, d), lambda i: (i, 0)),
                  pl.BlockSpec((1, d), lambda i: (0, 0)),
                  pl.BlockSpec(memory_space=pl.ANY)],
        out_specs=pl.BlockSpec((rows, d), lambda i: (i, 0)),
        out_shape=jax.ShapeDtypeStruct((t, d), F32),
        scratch_shapes=[pltpu.VMEM((rows, d), F32), pltpu.VMEM((rows, d), F32),
                        pltpu.SemaphoreType.DMA(())],
        compiler_params=_cparams(1, VMEM_LIMIT),
        name="moe_combine",
    )(dest3, route, x1, final_g, obuf)


def _hier_moe_and_norm(x1, n2, route, wg, wu, wd, final_g):
    t = x1.shape[0]
    a = t * TOP_K
    rank, counts = _ranks(route)
    counts = counts[0, :N_EXPERTS].astype(I32)
    pad_counts = (counts + EXPERT_ROWS - 1) // EXPERT_ROWS * EXPERT_ROWS
    pad_end = jnp.cumsum(pad_counts)
    pad_start = pad_end - pad_counts
    n_blocks = -(-a // EXPERT_ROWS) + N_EXPERTS
    block_e = jnp.minimum(
        jnp.searchsorted(pad_end, jnp.arange(n_blocks, dtype=I32) * EXPERT_ROWS, side="right"),
        N_EXPERTS - 1).astype(I32)
    n_used = (pad_end[-1:] // EXPERT_ROWS).astype(I32)
    pstart_row = jnp.zeros((1, ROUTE_LANES), F32).at[0, :N_EXPERTS].set(pad_start.astype(F32))
    dest = _dests(route, rank, pstart_row)
    dest3 = dest[:, :TOP_K].reshape(t // ROW_TILE, 1, TOP_K * ROW_TILE)
    zrow = (pad_end - EXPERT_ROWS).astype(I32)
    zflag = (pad_counts > 0).astype(I32)
    buf = _dispatch(n2, dest3, zrow, zflag, n_blocks * EXPERT_ROWS)
    obuf = _expert_ffn(buf, block_e, n_used, wg, wu, wd)
    return _combine(dest3, route, x1, final_g, obuf)


def _encoder(x, prm, kf, tables, nf):
    b, l, d = x.shape
    t = b * l
    d_hy = prm["skip"].shape[1]
    x2 = x.reshape(t, d)
    hy, gm = _inproj(x2, prm["mix_norm_g"], prm["w_in"], (HY_ORDER + 1) * d_hy)
    hy = _shortconv(hy, prm["short_w"], prm["short_b"], l)
    hy4 = hy.reshape(b, nf // 2, nf, hy.shape[1])
    z4 = hy4
    z_col = HY_ORDER * d_hy // CONV_LANES
    for o in range(HY_ORDER):
        z4 = _longconv(hy4, z4, kf, tables, prm["skip"][o:o + 1], gate_col=o * d_hy // CONV_LANES,
                       z_col=z_col, kf_col=o * d_hy // CONV_LANES, nf=nf)
        z_col = 0
    y_hy = z4.reshape(t, d_hy)
    y_gm = _gmlp(gm, prm["ln_g"], prm["ln_b"], prm["ws_stack"], prm["bias_t"])
    x1, n2, route = _outproj(x2, y_hy, y_gm, prm["mix_out_g"], prm["bd"], prm["w_out"], prm["ffn_norm_g"],
                             prm["wr_hi"], prm["wr_lo"], prm["br"])
    out = _hier_moe_and_norm(x1, n2, route, prm["w_e_gate"], prm["w_e_up"], prm["w_e_down"],
                             prm["final_norm_g"])
    return out.reshape(b, l, d)


def kernel(x_prompt, x_sample, mix_norm_g, w_in, hy_short_w, hy_short_b, hy_filt_w_emb, hy_filt_b_emb,
           hy_filt_w_inner, hy_filt_b_inner, hy_filt_freq, hy_filt_w_out, hy_skip, gm_ln_g, gm_ln_b,
           gm_w_s, gm_b_s, mix_out_g, w_out, ffn_norm_g, w_group, b_group, w_expert_router,
           b_expert_router, w_e_gate, w_e_up, w_e_down, final_norm_g):
    assert w_in.shape[0] == 1, "one layer"
    l = x_prompt.shape[1]
    assert x_sample.shape[1] == l
    nf = math.isqrt(2 * l)
    assert nf * nf == 2 * l and nf % SLAB == 0
    d = x_prompt.shape[2]
    d_hy = hy_skip.shape[2]
    d_gm = gm_ln_g.shape[1]
    head_dim = d_gm // GM_HEADS
    assert d_hy // HY_HEADS == head_dim and d_hy == d_gm

    tables = _dft_tables(nf)
    f1, f2, f2i, f3, f1_full = tables
    tables_bf = tuple(a.astype(BF16) for a in (f1, f2, f2i, f3))

    max_decay = math.log(DECAY_TARGET) / FAST_DECAY_PCT
    min_decay = math.log(DECAY_TARGET) / SLOW_DECAY_PCT
    deltas = jnp.abs(jnp.linspace(min_decay, max_decay, d_hy, dtype=F32))[None, :]
    taps = _filter_taps(hy_filt_w_emb[0], hy_filt_b_emb[0][None, :], hy_filt_w_inner[0], hy_filt_b_inner[0],
                        hy_filt_freq[0][None, :], hy_filt_w_out[0], deltas, l, d_hy)
    kf = _filter_spectrum(taps, f1_full, f2, nf)

    n_route = N_GROUPS + N_EXPERTS
    wr = jnp.zeros((d, ROUTE_LANES), F32).at[:, :n_route].set(
        jnp.concatenate([w_group[0], w_expert_router[0]], axis=1))
    wr_hi = wr.astype(BF16)
    wr_lo = (wr - wr_hi.astype(F32)).astype(BF16)
    br = jnp.zeros((1, ROUTE_LANES), F32).at[0, :n_route].set(jnp.concatenate([b_group[0], b_expert_router[0]]))
    hid = jnp.arange(d_hy, dtype=I32) // head_dim
    bd = jnp.where(hid[:, None] == hid[None, :], 1.0 / head_dim, 0.0).astype(BF16)

    prm = dict(
        mix_norm_g=mix_norm_g, w_in=w_in[0].astype(BF16), short_w=hy_short_w[0], short_b=hy_short_b,
        skip=hy_skip[0], ln_g=gm_ln_g, ln_b=gm_ln_b,
        ws_stack=gm_w_s[0].reshape(GM_HEADS * GM_CHUNK, GM_CHUNK).astype(BF16),
        bias_t=jnp.repeat(gm_b_s[0].T, head_dim, axis=1),
        mix_out_g=mix_out_g, bd=bd, w_out=w_out[0].astype(BF16), ffn_norm_g=ffn_norm_g,
        wr_hi=wr_hi, wr_lo=wr_lo, br=br,
        w_e_gate=w_e_gate[0].astype(BF16), w_e_up=w_e_up[0].astype(BF16), w_e_down=w_e_down[0].astype(BF16),
        final_norm_g=final_norm_g[None, :])
    y_prompt = _encoder(x_prompt, prm, kf, tables_bf, nf)
    y_sample = _encoder(x_sample, prm, kf, tables_bf, nf)
    return (y_prompt, y_sample)
```

```python
import functools
import math

import jax
import jax.numpy as jnp
from jax import lax
from jax.experimental import pallas as pl
from jax.experimental.pallas import tpu as pltpu

F32 = jnp.float32
BF16 = jnp.bfloat16
U32 = jnp.uint32
I32 = jnp.int32

EPS = 1e-6
HY_ORDER = 2
HY_HEADS = 8
GM_HEADS = 8
GM_CHUNK = 128
N_GROUPS = 4
EXPERTS_PER_GROUP = 8
N_EXPERTS = N_GROUPS * EXPERTS_PER_GROUP
TOP_K = 2
FILTER_EMB = 5
DECAY_TARGET = 1e-2
FAST_DECAY_PCT = 0.3
SLOW_DECAY_PCT = 1.5

LANES = 128
ROUTE_LANES = LANES
SLAB = 16
CONV_LANES = 256
ROW_TILE = 512
EXPERT_ROWS = 1024
HIGHEST = lax.Precision.HIGHEST
VMEM_LIMIT = 56 * 1024 * 1024


def _cparams(n_axes, vmem=None):
    return pltpu.CompilerParams(dimension_semantics=("arbitrary",) * n_axes,
                                vmem_limit_bytes=vmem)


def _inproj_kernel(x_ref, g_ref, w_ref, hy_ref, gm_ref, *, d_hy3):
    x = x_ref[...]
    ms = jnp.mean(x * x, axis=-1, keepdims=True)
    n = (x * lax.rsqrt(ms + EPS) * g_ref[...]).astype(BF16)
    p = jnp.dot(n, w_ref[...], preferred_element_type=F32)
    hy_ref[...] = p[:, :d_hy3]
    gm_ref[...] = p[:, d_hy3:].astype(BF16)


def _inproj(x2, g, w_bf, d_hy3):
    t, d = x2.shape
    dp = w_bf.shape[1]
    return pl.pallas_call(
        functools.partial(_inproj_kernel, d_hy3=d_hy3),
        grid=(t // ROW_TILE,),
        in_specs=[pl.BlockSpec((ROW_TILE, d), lambda i: (i, 0)),
                  pl.BlockSpec((1, d), lambda i: (0, 0)),
                  pl.BlockSpec((d, dp), lambda i: (0, 0))],
        out_specs=[pl.BlockSpec((ROW_TILE, d_hy3), lambda i: (i, 0)),
                   pl.BlockSpec((ROW_TILE, dp - d_hy3), lambda i: (i, 0))],
        out_shape=[jax.ShapeDtypeStruct((t, d_hy3), F32),
                   jax.ShapeDtypeStruct((t, dp - d_hy3), BF16)],
        compiler_params=_cparams(1, VMEM_LIMIT),
        name="inproj",
    )(x2, g, w_bf)


def _shortconv_kernel(v_ref, hp_ref, hn_ref, w_ref, b_ref, o_ref, *, tiles_per_seq):
    i = pl.program_id(0)
    v = v_ref[...]
    rows = v.shape[0]
    row = lax.broadcasted_iota(I32, (rows, 1), 0)
    pos = i % tiles_per_seq
    first = jnp.where(pos == 0, 0.0, 1.0)
    last = jnp.where(pos == tiles_per_seq - 1, 0.0, 1.0)
    prev = jnp.where(row == 0, hp_ref[7:8, :] * first, pltpu.roll(v, 1, 0))
    nxt = jnp.where(row == rows - 1, hn_ref[0:1, :] * last, pltpu.roll(v, rows - 1, 0))
    o_ref[...] = w_ref[0:1, :] * prev + w_ref[1:2, :] * v + w_ref[2:3, :] * nxt + b_ref[...]


def _shortconv(hy, w, b, seq_len):
    t, c = hy.shape
    tiles_per_seq = seq_len // ROW_TILE
    hb = ROW_TILE // 8
    nhb = t // 8
    return pl.pallas_call(
        functools.partial(_shortconv_kernel, tiles_per_seq=tiles_per_seq),
        grid=(t // ROW_TILE,),
        in_specs=[pl.BlockSpec((ROW_TILE, c), lambda i: (i, 0)),
                  pl.BlockSpec((8, c), lambda i: (jnp.maximum(i * hb - 1, 0), 0)),
                  pl.BlockSpec((8, c), lambda i: (jnp.minimum((i + 1) * hb, nhb - 1), 0)),
                  pl.BlockSpec((3, c), lambda i: (0, 0)),
                  pl.BlockSpec((1, c), lambda i: (0, 0))],
        out_specs=pl.BlockSpec((ROW_TILE, c), lambda i: (i, 0)),
        out_shape=jax.ShapeDtypeStruct((t, c), F32),
        compiler_params=_cparams(1, VMEM_LIMIT),
        name="shortconv",
    )(hy, hy, hy, w, b)


def _cos_sin(m, period):
    ang = m.astype(F32) * (2.0 * math.pi / period)
    return jnp.cos(ang), jnp.sin(ang)


def _stack_complex(mr, mi):
    top = jnp.concatenate([mr, -mi], axis=-1)
    bot = jnp.concatenate([mi, mr], axis=-1)
    return jnp.concatenate([top, bot], axis=-2)


def _dft_tables(nf):
    na = nf // 2
    n = nf * nf
    idx = jnp.arange(nf, dtype=I32)
    c, s = _cos_sin((idx[:, None] * idx[None, :]) % nf, nf)
    f2 = _stack_complex(c, -s)
    f2i = _stack_complex(c, s)
    b_ = idx[:, None, None]
    ka = idx[None, :, None]
    a_ = idx[None, None, :]
    m1 = (nf * a_ * ka + b_ * ka) % n
    c1, s1 = _cos_sin(m1, n)
    f1 = _stack_complex(c1[:, :, :na], -s1[:, :, :na])
    f1_full = jnp.concatenate([c1, -s1], axis=1)
    c3 = jnp.swapaxes(c1, 1, 2)[:, :na, :]
    s3 = jnp.swapaxes(s1, 1, 2)[:, :na, :]
    f3 = _stack_complex(c3, s3)
    return f1, f2, f2i, f3, f1_full


def _filter_taps_kernel(wemb_ref, bemb_ref, win_ref, bin_ref, freq_ref, wout_ref, delta_ref, o_ref,
                        *, seq_len, n_inner, d_hy):
    rows = o_ref.shape[0]
    j = pl.program_id(0) * rows + lax.broadcasted_iota(I32, (rows, 1), 0)
    tidx = jnp.where(j < seq_len, j, 2 * seq_len - j).astype(F32)
    t = tidx / float(seq_len - 1)
    fr0 = jnp.float32(1e-4)
    ang0 = (2.0 * math.pi / seq_len) * tidx * fr0
    ang1 = (2.0 * math.pi / seq_len) * tidx
    freq = freq_ref[...]
    pre = (t * wemb_ref[0:1, :] + jnp.cos(ang0) * wemb_ref[1:2, :] + jnp.cos(ang1) * wemb_ref[2:3, :]
           - jnp.sin(ang0) * wemb_ref[3:4, :] - jnp.sin(ang1) * wemb_ref[4:5, :] + bemb_ref[...])
    hdn = jnp.sin(freq * pre)
    for i in range(n_inner):
        hdn = jnp.sin(freq * (jnp.dot(hdn, win_ref[i], precision=HIGHEST,
                                      preferred_element_type=F32) + bin_ref[i:i + 1, :]))
    h = jnp.dot(hdn, wout_ref[...], precision=HIGHEST, preferred_element_type=F32)
    decay = jnp.exp(-t * delta_ref[...])
    for o in range(HY_ORDER):
        fwd = h[:, (2 * o) * d_hy:(2 * o + 1) * d_hy]
        bwd = h[:, (2 * o + 1) * d_hy:(2 * o + 2) * d_hy]
        val = jnp.where(j < seq_len, fwd, bwd) * decay
        o_ref[:, o * d_hy:(o + 1) * d_hy] = jnp.where(j == seq_len, 0.0, val)


def _filter_taps(wemb, bemb, win, bin_, freq, wout, deltas, seq_len, d_hy):
    rows = 1024
    n2 = 2 * seq_len
    width = wemb.shape[1]
    n_inner = win.shape[0]
    full = lambda *shape: pl.BlockSpec(shape, lambda i: (0,) * len(shape))
    return pl.pallas_call(
        functools.partial(_filter_taps_kernel, seq_len=seq_len, n_inner=n_inner, d_hy=d_hy),
        grid=(n2 // rows,),
        in_specs=[full(FILTER_EMB, width), full(1, width), full(n_inner, width, width),
                  full(n_inner, width), full(1, width), full(width, 2 * HY_ORDER * d_hy),
                  full(1, d_hy)],
        out_specs=pl.BlockSpec((rows, HY_ORDER * d_hy), lambda i: (i, 0)),
        out_shape=jax.ShapeDtypeStruct((n2, HY_ORDER * d_hy), F32),
        compiler_params=_cparams(1, VMEM_LIMIT),
        name="filter_taps",
    )(wemb, bemb, win, bin_, freq, wout, deltas)


def _filter_fft1_kernel(k_ref, f_ref, o_ref):
    for bl in range(SLAB):
        rhs = k_ref[:, bl, :]
        o_ref[bl] = jnp.dot(f_ref[bl], rhs, precision=HIGHEST, preferred_element_type=F32)


def _filter_fft2_kernel(g_ref, f_ref, o_ref, *, scale):
    for kl in range(SLAB):
        rhs = jnp.concatenate([g_ref[:, 0, kl, :], g_ref[:, 1, kl, :]], axis=0)
        o_ref[kl] = jnp.dot(f_ref[...], rhs, precision=HIGHEST, preferred_element_type=F32) * scale


def _filter_spectrum(taps, f1_full, f2, nf):
    c = taps.shape[1]
    k3 = taps.reshape(nf, nf, c)
    g = pl.pallas_call(
        _filter_fft1_kernel,
        grid=(nf // SLAB, c // CONV_LANES),
        in_specs=[pl.BlockSpec((nf, SLAB, CONV_LANES), lambda i, j: (0, i, j)),
                  pl.BlockSpec((SLAB, 2 * nf, nf), lambda i, j: (i, 0, 0))],
        out_specs=pl.BlockSpec((SLAB, 2 * nf, CONV_LANES), lambda i, j: (i, 0, j)),
        out_shape=jax.ShapeDtypeStruct((nf, 2 * nf, c), F32),
        compiler_params=_cparams(2, VMEM_LIMIT),
        name="filter_fft1",
    )(k3, f1_full)
    g4 = g.reshape(nf, 2, nf, c)
    return pl.pallas_call(
        functools.partial(_filter_fft2_kernel, scale=1.0 / (nf * nf)),
        grid=(nf // SLAB, c // CONV_LANES),
        in_specs=[pl.BlockSpec((nf, 2, SLAB, CONV_LANES), lambda i, j: (0, 0, i, j)),
                  pl.BlockSpec((2 * nf, 2 * nf), lambda i, j: (0, 0))],
        out_specs=pl.BlockSpec((SLAB, 2 * nf, CONV_LANES), lambda i, j: (i, 0, j)),
        out_shape=jax.ShapeDtypeStruct((nf, 2 * nf, c), F32),
        compiler_params=_cparams(2, VMEM_LIMIT),
        name="filter_fft2",
    )(g4, f2)


def _pack(re, im):
    rb = lax.bitcast_convert_type(re.astype(BF16).astype(F32), U32)
    ib = lax.bitcast_convert_type(im.astype(BF16).astype(F32), U32)
    return rb | (ib >> 16)


def _unpack(w):
    re = lax.bitcast_convert_type(w & jnp.uint32(0xFFFF0000), F32).astype(BF16)
    im = lax.bitcast_convert_type(w << 16, F32).astype(BF16)
    return re, im


def _longconv_kernel(z1_ref, f1_ref, kf_ref, f2_ref, f2i_ref, f3_ref, gate_ref, zs_ref, skip_ref,
                     o_ref, g_ref, *, nf):
    na = nf // 2
    ns = nf // SLAB
    halves = g_ref.shape[1]
    t = pl.program_id(2)

    def store_col(blk, bl, val):
        for h in range(halves):
            g_ref[blk, h, pl.ds(bl, nf, stride=SLAB), :] = val[:, h * LANES:(h + 1) * LANES]

    def load_col(blk, bl):
        return jnp.concatenate([g_ref[blk, h, pl.ds(bl, nf, stride=SLAB), :] for h in range(halves)],
                               axis=1)

    @pl.when(t < ns)
    def _phase1():
        for bl in range(SLAB):
            rhs = jnp.concatenate([z1_ref[0, :, bl, :], z1_ref[1, :, bl, :]], axis=0).astype(BF16)
            out = jnp.dot(f1_ref[bl], rhs, preferred_element_type=F32)
            store_col(t, bl, _pack(out[:nf], out[nf:]))

    @pl.when(jnp.logical_and(t >= ns, t < 2 * ns))
    def _phase2():
        i = t - ns

        def body(kl, carry):
            row0 = pl.multiple_of((i * SLAB + kl) * SLAB, SLAB)
            w = jnp.concatenate([g_ref[:, h, pl.ds(row0, SLAB), :].reshape(nf, LANES)
                                 for h in range(halves)], axis=1)
            re, im = _unpack(w)
            s = jnp.dot(f2_ref[...], jnp.concatenate([re, im], axis=0),
                        preferred_element_type=F32)
            sr, si = s[:nf], s[nf:]
            kr, ki = kf_ref[kl, :nf, :], kf_ref[kl, nf:, :]
            pr = (sr * kr - si * ki).astype(BF16)
            pi = (sr * ki + si * kr).astype(BF16)
            h = jnp.dot(f2i_ref[...], jnp.concatenate([pr, pi], axis=0),
                        preferred_element_type=F32)
            packed = _pack(h[:nf], h[nf:])
            for hf in range(halves):
                g_ref[:, hf, pl.ds(row0, SLAB), :] = packed[:, hf * LANES:(hf + 1) * LANES].reshape(
                    ns, SLAB, LANES)
            return carry

        lax.fori_loop(0, SLAB, body, 0)

    @pl.when(t >= 2 * ns)
    def _phase3():
        j = t - 2 * ns
        skip = skip_ref[...]
        for bl in range(SLAB):
            re, im = _unpack(load_col(j, bl))
            y = jnp.dot(f3_ref[bl], jnp.concatenate([re, im], axis=0),
                        preferred_element_type=F32)
            for s in range(2):
                zs = zs_ref[s, :, bl, :]
                o_ref[s, :, bl, :] = gate_ref[s, :, bl, :] * (y[s * na:(s + 1) * na] + zs * skip)


def _longconv(hy4, z4, kf, tables, skip, gate_col, z_col, kf_col, nf):
    f1, f2, f2i, f3 = tables
    b, na, _, _ = z4.shape
    c = skip.shape[1]
    ns = nf // SLAB
    nchunk = c // CONV_LANES
    p1 = lambda t: jnp.minimum(t, ns - 1)
    p2 = lambda t: jnp.clip(t - ns, 0, ns - 1)
    p3 = lambda t: jnp.clip(t - 2 * ns, 0, ns - 1)
    cl = CONV_LANES
    return pl.pallas_call(
        functools.partial(_longconv_kernel, nf=nf),
        grid=(b // 2, nchunk, 3 * ns),
        in_specs=[
            pl.BlockSpec((2, na, SLAB, cl), lambda p, q, t: (p, 0, p1(t), z_col + q)),
            pl.BlockSpec((SLAB, 2 * nf, 2 * na), lambda p, q, t: (p1(t), 0, 0)),
            pl.BlockSpec((SLAB, 2 * nf, cl), lambda p, q, t: (p2(t), 0, kf_col + q)),
            pl.BlockSpec((2 * nf, 2 * nf), lambda p, q, t: (0, 0)),
            pl.BlockSpec((2 * nf, 2 * nf), lambda p, q, t: (0, 0)),
            pl.BlockSpec((SLAB, 2 * na, 2 * nf), lambda p, q, t: (p3(t), 0, 0)),
            pl.BlockSpec((2, na, SLAB, cl), lambda p, q, t: (p, 0, p3(t), gate_col + q)),
            pl.BlockSpec((2, na, SLAB, cl), lambda p, q, t: (p, 0, p3(t), z_col + q)),
            pl.BlockSpec((1, cl), lambda p, q, t: (0, q)),
        ],
        out_specs=pl.BlockSpec((2, na, SLAB, cl), lambda p, q, t: (p, 0, p3(t), q)),
        out_shape=jax.ShapeDtypeStruct((b, na, nf, c), F32),
        scratch_shapes=[pltpu.VMEM((ns, cl // LANES, nf * SLAB, LANES), U32)],
        compiler_params=_cparams(3, VMEM_LIMIT),
        name="longconv",
    )(z4, f1, kf, f2, f2i, f3, hy4, z4, skip)


def _gelu_exact(x):
    return 0.5 * x * (1.0 + lax.erf(x * (1.0 / math.sqrt(2.0))))


def _gmlp_kernel(gm_ref, lng_ref, lnb_ref, ws_ref, bias_ref, o_ref, *, d_gm):
    g = _gelu_exact(gm_ref[...].astype(F32))
    u, v = g[:, :d_gm], g[:, d_gm:]
    mu = jnp.mean(v, axis=-1, keepdims=True)
    vc = v - mu
    var = jnp.mean(vc * vc, axis=-1, keepdims=True)
    vh = (vc * lax.rsqrt(var + EPS) * lng_ref[...] + lnb_ref[...]).astype(BF16)
    head = lax.broadcasted_iota(I32, (GM_CHUNK, d_gm), 1) // (d_gm // GM_HEADS)
    for c in range(gm_ref.shape[0] // GM_CHUNK):
        rows = slice(c * GM_CHUNK, (c + 1) * GM_CHUNK)
        r = jnp.dot(ws_ref[...], vh[rows], preferred_element_type=F32)
        s = r[:GM_CHUNK]
        for h in range(1, GM_HEADS):
            s = jnp.where(head == h, r[h * GM_CHUNK:(h + 1) * GM_CHUNK], s)
        o_ref[rows, :] = u[rows] * (s + bias_ref[...])


def _gmlp(gm, ln_g, ln_b, ws_stack, bias_t):
    t, c2 = gm.shape
    d_gm = c2 // 2
    return pl.pallas_call(
        functools.partial(_gmlp_kernel, d_gm=d_gm),
        grid=(t // ROW_TILE,),
        in_specs=[pl.BlockSpec((ROW_TILE, c2), lambda i: (i, 0)),
                  pl.BlockSpec((1, d_gm), lambda i: (0, 0)),
                  pl.BlockSpec((1, d_gm), lambda i: (0, 0)),
                  pl.BlockSpec(ws_stack.shape, lambda i: (0, 0)),
                  pl.BlockSpec(bias_t.shape, lambda i: (0, 0))],
        out_specs=pl.BlockSpec((ROW_TILE, d_gm), lambda i: (i, 0)),
        out_shape=jax.ShapeDtypeStruct((t, d_gm), F32),
        compiler_params=_cparams(1, VMEM_LIMIT),
        name="gmlp",
    )(gm, ln_g, ln_b, ws_stack, bias_t)


def _head_rms(y, bd):
    ms = jnp.dot((y * y).astype(BF16), bd, preferred_element_type=F32)
    return y * lax.rsqrt(ms + EPS)


def _route(logits):
    lane = lax.broadcasted_iota(I32, logits.shape, 1)
    neg = jnp.float32(-1e30)
    big = jnp.int32(ROUTE_LANES)
    gmask = lane < N_GROUPS
    gl = jnp.where(gmask, logits, neg)
    gmax = jnp.max(gl, axis=-1, keepdims=True)
    grp = jnp.min(jnp.where(jnp.logical_and(gl == gmax, gmask), lane, big), axis=-1, keepdims=True)
    psum = jnp.sum(jnp.where(gmask, jnp.exp(gl - gmax), 0.0), axis=-1, keepdims=True)
    p_grp = 1.0 / psum
    lo = N_GROUPS + EXPERTS_PER_GROUP * grp
    emask = jnp.logical_and(lane >= lo, lane < lo + EXPERTS_PER_GROUP)
    el = jnp.where(emask, logits, neg)
    m1 = jnp.max(el, axis=-1, keepdims=True)
    i1 = jnp.min(jnp.where(jnp.logical_and(el == m1, emask), lane, big), axis=-1, keepdims=True)
    emask2 = jnp.logical_and(emask, lane != i1)
    el2 = jnp.where(emask2, logits, neg)
    m2 = jnp.max(el2, axis=-1, keepdims=True)
    i2 = jnp.min(jnp.where(jnp.logical_and(el2 == m2, emask2), lane, big), axis=-1, keepdims=True)
    d = jnp.exp(m2 - m1)
    g1 = p_grp * (1.0 / (1.0 + d))
    g2 = p_grp * (d / (1.0 + d))
    e1 = (i1 - N_GROUPS).astype(F32)
    e2 = (i2 - N_GROUPS).astype(F32)
    return jnp.where(lane == 0, e1, jnp.where(lane == 1, e2, jnp.where(lane == 2, g1,
                     jnp.where(lane == 3, g2, 0.0))))


def _outproj_kernel(x_ref, yh_ref, yg_ref, mg_ref, bd_ref, wo_ref, fg_ref, wrh_ref, wrl_ref, br_ref,
                    x1_ref, n2_ref, route_ref, *, d_hy):
    bd = bd_ref[...]
    mg = mg_ref[...]
    mh = (_head_rms(yh_ref[...], bd) * mg[:, :d_hy]).astype(BF16)
    mgm = (_head_rms(yg_ref[...], bd) * mg[:, d_hy:]).astype(BF16)
    wo = wo_ref[...]
    x1 = (x_ref[...] + jnp.dot(mh, wo[:d_hy], preferred_element_type=F32)
          + jnp.dot(mgm, wo[d_hy:], preferred_element_type=F32))
    x1_ref[...] = x1
    ms = jnp.mean(x1 * x1, axis=-1, keepdims=True)
    n2 = x1 * lax.rsqrt(ms + EPS) * fg_ref[...]
    n2_ref[...] = n2
    hi = n2.astype(BF16)
    lo = (n2 - hi.astype(F32)).astype(BF16)
    wrh = wrh_ref[...]
    logits = (jnp.dot(hi, wrh, preferred_element_type=F32) + jnp.dot(lo, wrh, preferred_element_type=F32)
              + jnp.dot(hi, wrl_ref[...], preferred_element_type=F32) + br_ref[...])
    route_ref[...] = _route(logits)


def _outproj(x2, yh, yg, mix_g, bd, wo_bf, ffn_g, wr_hi, wr_lo, br):
    t, d = x2.shape
    d_hy = yh.shape[1]
    row = lambda w: pl.BlockSpec((ROW_TILE, w), lambda i: (i, 0))
    full = lambda a: pl.BlockSpec(a.shape, lambda i: (0,) * a.ndim)
    return pl.pallas_call(
        functools.partial(_outproj_kernel, d_hy=d_hy),
        grid=(t // ROW_TILE,),
        in_specs=[row(d), row(d_hy), row(yg.shape[1]), full(mix_g), full(bd), full(wo_bf), full(ffn_g),
                  full(wr_hi), full(wr_lo), full(br)],
        out_specs=[row(d), row(d), row(ROUTE_LANES)],
        out_shape=[jax.ShapeDtypeStruct((t, d), F32), jax.ShapeDtypeStruct((t, d), F32),
                   jax.ShapeDtypeStruct((t, ROUTE_LANES), F32)],
        compiler_params=_cparams(1, VMEM_LIMIT),
        name="outproj_router",
    )(x2, yh, yg, mix_g, bd, wo_bf, ffn_g, wr_hi, wr_lo, br)


def _lane_pick(vals, sel_lane):
    lane = lax.broadcasted_iota(I32, vals.shape, 1)
    return jnp.sum(jnp.where(lane == sel_lane, vals, 0.0), axis=-1, keepdims=True)


def _rank_kernel(route_ref, rank_ref, count_ref, carry_ref):
    @pl.when(pl.program_id(0) == 0)
    def _init():
        carry_ref[...] = jnp.zeros_like(carry_ref)

    route = route_ref[...]
    rows = route.shape[0]
    lane = lax.broadcasted_iota(I32, route.shape, 1)
    e0 = route[:, 0:1].astype(I32)
    e1 = route[:, 1:2].astype(I32)
    oh0 = lane == e0
    oh1 = lane == e1
    oh = jnp.where(jnp.logical_or(oh0, oh1), 1.0, 0.0)
    r = lax.broadcasted_iota(I32, (rows, rows), 0)
    c = lax.broadcasted_iota(I32, (rows, rows), 1)
    ltri = jnp.where(c < r, 1.0, 0.0).astype(BF16)
    tot = jnp.dot(ltri, oh.astype(BF16), preferred_element_type=F32) + carry_ref[...]
    rank0 = jnp.sum(jnp.where(oh0, tot, 0.0), axis=-1, keepdims=True)
    rank1 = jnp.sum(jnp.where(oh1, tot, 0.0), axis=-1, keepdims=True)
    rank_ref[...] = jnp.where(lane == 0, rank0, jnp.where(lane == 1, rank1, 0.0))
    carry_ref[...] += jnp.sum(oh, axis=0, keepdims=True)
    count_ref[...] = carry_ref[...]


def _ranks(route):
    t = route.shape[0]
    return pl.pallas_call(
        _rank_kernel,
        grid=(t // ROW_TILE,),
        in_specs=[pl.BlockSpec((ROW_TILE, ROUTE_LANES), lambda i: (i, 0))],
        out_specs=[pl.BlockSpec((ROW_TILE, ROUTE_LANES), lambda i: (i, 0)),
                   pl.BlockSpec((1, ROUTE_LANES), lambda i: (0, 0))],
        out_shape=[jax.ShapeDtypeStruct((t, ROUTE_LANES), F32),
                   jax.ShapeDtypeStruct((1, ROUTE_LANES), F32)],
        scratch_shapes=[pltpu.VMEM((1, ROUTE_LANES), F32)],
        compiler_params=_cparams(1),
        name="moe_rank",
    )(route)


def _dest_kernel(route_ref, rank_ref, pstart_ref, dest_ref):
    route = route_ref[...]
    rank = rank_ref[...]
    lane = lax.broadcasted_iota(I32, route.shape, 1)
    ps = jnp.broadcast_to(pstart_ref[...], route.shape)
    d0 = _lane_pick(ps, route[:, 0:1].astype(I32)) + rank[:, 0:1]
    d1 = _lane_pick(ps, route[:, 1:2].astype(I32)) + rank[:, 1:2]
    dest_ref[...] = jnp.where(lane == 0, d0, jnp.where(lane == 1, d1, 0.0)).astype(I32)


def _dests(route, rank, pstart):
    t = route.shape[0]
    row = pl.BlockSpec((ROW_TILE, ROUTE_LANES), lambda i: (i, 0))
    return pl.pallas_call(
        _dest_kernel,
        grid=(t // ROW_TILE,),
        in_specs=[row, row, pl.BlockSpec((1, ROUTE_LANES), lambda i: (0, 0))],
        out_specs=row,
        out_shape=jax.ShapeDtypeStruct((t, ROUTE_LANES), I32),
        compiler_params=_cparams(1),
        name="moe_dest",
    )(route, rank, pstart)


def _row_copy(src_ref, src_row, dst_ref, dst_row, sem):
    return pltpu.make_async_copy(src_ref.at[pl.ds(src_row, 1)], dst_ref.at[pl.ds(dst_row, 1)], sem)


def _dispatch_kernel(zrow_ref, zflag_ref, dest_ref, n2_ref, buf_ref, zero_ref, sem, zsem):
    i = pl.program_id(0)
    rows = dest_ref.shape[-1] // TOP_K

    def zero_copy(e):
        start = pl.multiple_of(zrow_ref[e], EXPERT_ROWS)
        return pltpu.make_async_copy(zero_ref, buf_ref.at[pl.ds(start, EXPERT_ROWS)], zsem)

    @pl.when(i == 0)
    def _zero_tail_blocks():
        zero_ref[...] = jnp.zeros_like(zero_ref)
        for e in range(N_EXPERTS):
            @pl.when(zflag_ref[e] > 0)
            def _():
                zero_copy(e).start()
        for e in range(N_EXPERTS):
            @pl.when(zflag_ref[e] > 0)
            def _():
                zero_copy(e).wait()

    def issue(r, carry):
        for k in range(TOP_K):
            _row_copy(n2_ref, r, buf_ref, dest_ref[0, 0, TOP_K * r + k], sem).start()
        return carry

    lax.fori_loop(0, rows, issue, 0)
    for k in range(TOP_K):
        pltpu.make_async_copy(n2_ref, buf_ref.at[pl.ds(0, rows)], sem).wait()


def _dispatch(n2, dest3, zrow, zflag, n_buf_rows):
    t, d = n2.shape
    rows = ROW_TILE
    return pl.pallas_call(
        _dispatch_kernel,
        grid_spec=pltpu.PrefetchScalarGridSpec(
            num_scalar_prefetch=2,
            grid=(t // rows,),
            in_specs=[pl.BlockSpec((1, 1, TOP_K * rows), lambda i, zr, zf: (i, 0, 0),
                                   memory_space=pltpu.SMEM),
                      pl.BlockSpec((rows, d), lambda i, zr, zf: (i, 0))],
            out_specs=pl.BlockSpec(memory_space=pl.ANY),
            scratch_shapes=[pltpu.VMEM((EXPERT_ROWS, d), F32), pltpu.SemaphoreType.DMA(()),
                            pltpu.SemaphoreType.DMA(())]),
        out_shape=jax.ShapeDtypeStruct((n_buf_rows, d), F32),
        compiler_params=_cparams(1, VMEM_LIMIT),
        name="moe_dispatch",
    )(zrow, zflag, dest3, n2)


def _ffn_kernel(be_ref, nu_ref, x_ref, wg_ref, wu_ref, wd_ref, o_ref):
    @pl.when(pl.program_id(0) < nu_ref[0])
    def _():
        xb = x_ref[...].astype(BF16)
        g = jnp.dot(xb, wg_ref[0], preferred_element_type=F32)
        u = jnp.dot(xb, wu_ref[0], preferred_element_type=F32)
        h = (g * (1.0 / (1.0 + jnp.exp(-g))) * u).astype(BF16)
        o_ref[...] = jnp.dot(h, wd_ref[0], preferred_element_type=F32)


def _expert_ffn(buf, block_e, n_used, wg, wu, wd):
    p, d = buf.shape
    de = wg.shape[2]
    nb = p // EXPERT_ROWS
    blk = lambda i, be, nu: jnp.minimum(i, nu[0] - 1)
    return pl.pallas_call(
        _ffn_kernel,
        grid_spec=pltpu.PrefetchScalarGridSpec(
            num_scalar_prefetch=2,
            grid=(nb,),
            in_specs=[pl.BlockSpec((EXPERT_ROWS, d), lambda i, be, nu: (blk(i, be, nu), 0)),
                      pl.BlockSpec((1, d, de), lambda i, be, nu: (be[blk(i, be, nu)], 0, 0)),
                      pl.BlockSpec((1, d, de), lambda i, be, nu: (be[blk(i, be, nu)], 0, 0)),
                      pl.BlockSpec((1, de, d), lambda i, be, nu: (be[blk(i, be, nu)], 0, 0))],
            out_specs=pl.BlockSpec((EXPERT_ROWS, d), lambda i, be, nu: (blk(i, be, nu), 0))),
        out_shape=jax.ShapeDtypeStruct((p, d), F32),
        compiler_params=_cparams(1, VMEM_LIMIT),
        name="expert_ffn",
    )(block_e, n_used, buf, wg, wu, wd)


def _combine_kernel(dest_ref, route_ref, x1_ref, fg_ref, obuf_ref, o_ref, r0_ref, r1_ref, sem):
    rows = x1_ref.shape[0]
    bufs = (r0_ref, r1_ref)

    def issue(r, carry):
        for k in range(TOP_K):
            _row_copy(obuf_ref, dest_ref[0, 0, TOP_K * r + k], bufs[k], r, sem).start()
        return carry

    lax.fori_loop(0, rows, issue, 0)
    for k in range(TOP_K):
        pltpu.make_async_copy(obuf_ref.at[pl.ds(0, rows)], bufs[k], sem).wait()
    route = route_ref[...]
    x2 = x1_ref[...] + route[:, 2:3] * r0_ref[...] + route[:, 3:4] * r1_ref[...]
    ms = jnp.mean(x2 * x2, axis=-1, keepdims=True)
    o_ref[...] = x2 * lax.rsqrt(ms + EPS) * fg_ref[...]


def _combine(dest3, route, x1, final_g, obuf):
    t, d = x1.shape
    rows = ROW_TILE
    return pl.pallas_call(
        _combine_kernel,
        grid=(t // rows,),
        in_specs=[pl.BlockSpec((1, 1, TOP_K * rows), lambda i: (i, 0, 0), memory_space=pltpu.SMEM),
                  pl.BlockSpec((rows, ROUTE_LANES), lambda i: (i, 0)),
                  pl.BlockSpec((rows, d), lambda i: (i, 0)),
                  pl.BlockSpec((1, d), lambda i: (0, 0)),
                  pl.BlockSpec(memory_space=pl.ANY)],
        out_specs=pl.BlockSpec((rows, d), lambda i: (i, 0)),
        out_shape=jax.ShapeDtypeStruct((t, d), F32),
        scratch_shapes=[pltpu.VMEM((rows, d), F32), pltpu.VMEM((rows, d), F32),
                        pltpu.SemaphoreType.DMA(())],
        compiler_params=_cparams(1, VMEM_LIMIT),
        name="moe_combine",
    )(dest3, route, x1, final_g, obuf)


def _hier_moe_and_norm(x1, n2, route, wg, wu, wd, final_g):
    t = x1.shape[0]
    a = t * TOP_K
    rank, counts = _ranks(route)
    counts = counts[0, :N_EXPERTS].astype(I32)
    pad_counts = (counts + EXPERT_ROWS - 1) // EXPERT_ROWS * EXPERT_ROWS
    pad_end = jnp.cumsum(pad_counts)
    pad_start = pad_end - pad_counts
    n_blocks = -(-a // EXPERT_ROWS) + N_EXPERTS
    block_e = jnp.minimum(
        jnp.searchsorted(pad_end, jnp.arange(n_blocks, dtype=I32) * EXPERT_ROWS, side="right"),
        N_EXPERTS - 1).astype(I32)
    n_used = (pad_end[-1:] // EXPERT_ROWS).astype(I32)
    pstart_row = jnp.zeros((1, ROUTE_LANES), F32).at[0, :N_EXPERTS].set(pad_start.astype(F32))
    dest = _dests(route, rank, pstart_row)
    dest3 = dest[:, :TOP_K].reshape(t // ROW_TILE, 1, TOP_K * ROW_TILE)
    zrow = (pad_end - EXPERT_ROWS).astype(I32)
    zflag = (pad_counts > 0).astype(I32)
    buf = _dispatch(n2, dest3, zrow, zflag, n_blocks * EXPERT_ROWS)
    obuf = _expert_ffn(buf, block_e, n_used, wg, wu, wd)
    return _combine(dest3, route, x1, final_g, obuf)


def _encoder(x, prm, kf, tables, nf):
    b, l, d = x.shape
    t = b * l
    d_hy = prm["skip"].shape[1]
    x2 = x.reshape(t, d)
    hy, gm = _inproj(x2, prm["mix_norm_g"], prm["w_in"], (HY_ORDER + 1) * d_hy)
    hy = _shortconv(hy, prm["short_w"], prm["short_b"], l)
    hy4 = hy.reshape(b, nf // 2, nf, hy.shape[1])
    z4 = hy4
    z_col = HY_ORDER * d_hy // CONV_LANES
    for o in range(HY_ORDER):
        z4 = _longconv(hy4, z4, kf, tables, prm["skip"][o:o + 1], gate_col=o * d_hy // CONV_LANES,
                       z_col=z_col, kf_col=o * d_hy // CONV_LANES, nf=nf)
        z_col = 0
    y_hy = z4.reshape(t, d_hy)
    y_gm = _gmlp(gm, prm["ln_g"], prm["ln_b"], prm["ws_stack"], prm["bias_t"])
    x1, n2, route = _outproj(x2, y_hy, y_gm, prm["mix_out_g"], prm["bd"], prm["w_out"], prm["ffn_norm_g"],
                             prm["wr_hi"], prm["wr_lo"], prm["br"])
    out = _hier_moe_and_norm(x1, n2, route, prm["w_e_gate"], prm["w_e_up"], prm["w_e_down"],
                             prm["final_norm_g"])
    return out.reshape(b, l, d)


def kernel(x_prompt, x_sample, mix_norm_g, w_in, hy_short_w, hy_short_b, hy_filt_w_emb, hy_filt_b_emb,
           hy_filt_w_inner, hy_filt_b_inner, hy_filt_freq, hy_filt_w_out, hy_skip, gm_ln_g, gm_ln_b,
           gm_w_s, gm_b_s, mix_out_g, w_out, ffn_norm_g, w_group, b_group, w_expert_router,
           b_expert_router, w_e_gate, w_e_up, w_e_down, final_norm_g):
    assert w_in.shape[0] == 1, "one layer"
    l = x_prompt.shape[1]
    assert x_sample.shape[1] == l
    nf = math.isqrt(2 * l)
    assert nf * nf == 2 * l and nf % SLAB == 0
    d = x_prompt.shape[2]
    d_hy = hy_skip.shape[2]
    d_gm = gm_ln_g.shape[1]
    head_dim = d_gm // GM_HEADS
    assert d_hy // HY_HEADS == head_dim and d_hy == d_gm

    tables = _dft_tables(nf)
    f1, f2, f2i, f3, f1_full = tables
    tables_bf = tuple(a.astype(BF16) for a in (f1, f2, f2i, f3))

    max_decay = math.log(DECAY_TARGET) / FAST_DECAY_PCT
    min_decay = math.log(DECAY_TARGET) / SLOW_DECAY_PCT
    deltas = jnp.abs(jnp.linspace(min_decay, max_decay, d_hy, dtype=F32))[None, :]
    taps = _filter_taps(hy_filt_w_emb[0], hy_filt_b_emb[0][None, :], hy_filt_w_inner[0], hy_filt_b_inner[0],
                        hy_filt_freq[0][None, :], hy_filt_w_out[0], deltas, l, d_hy)
    kf = _filter_spectrum(taps, f1_full, f2, nf)

    n_route = N_GROUPS + N_EXPERTS
    wr = jnp.zeros((d, ROUTE_LANES), F32).at[:, :n_route].set(
        jnp.concatenate([w_group[0], w_expert_router[0]], axis=1))
    wr_hi = wr.astype(BF16)
    wr_lo = (wr - wr_hi.astype(F32)).astype(BF16)
    br = jnp.zeros((1, ROUTE_LANES), F32).at[0, :n_route].set(jnp.concatenate([b_group[0], b_expert_router[0]]))
    hid = jnp.arange(d_hy, dtype=I32) // head_dim
    bd = jnp.where(hid[:, None] == hid[None, :], 1.0 / head_dim, 0.0).astype(BF16)

    prm = dict(
        mix_norm_g=mix_norm_g, w_in=w_in[0].astype(BF16), short_w=hy_short_w[0], short_b=hy_short_b,
        skip=hy_skip[0], ln_g=gm_ln_g, ln_b=gm_ln_b,
        ws_stack=gm_w_s[0].reshape(GM_HEADS * GM_CHUNK, GM_CHUNK).astype(BF16),
        bias_t=jnp.repeat(gm_b_s[0].T, head_dim, axis=1),
        mix_out_g=mix_out_g, bd=bd, w_out=w_out[0].astype(BF16), ffn_norm_g=ffn_norm_g,
        wr_hi=wr_hi, wr_lo=wr_lo, br=br,
        w_e_gate=w_e_gate[0].astype(BF16), w_e_up=w_e_up[0].astype(BF16), w_e_down=w_e_down[0].astype(BF16),
        final_norm_g=final_norm_g[None, :])
    y_prompt = _encoder(x_prompt, prm, kf, tables_bf, nf)
    y_sample = _encoder(x_sample, prm, kf, tables_bf, nf)
    return (y_prompt, y_sample)
```

```python
import functools
import math

import jax
import jax.numpy as jnp
from jax import lax
from jax.experimental import pallas as pl
from jax.experimental.pallas import tpu as pltpu

F32 = jnp.float32
BF16 = jnp.bfloat16
U32 = jnp.uint32
I32 = jnp.int32

EPS = 1e-6
HY_ORDER = 2
HY_HEADS = 8
GM_HEADS = 8
GM_CHUNK = 128
N_GROUPS = 4
EXPERTS_PER_GROUP = 8
N_EXPERTS = N_GROUPS * EXPERTS_PER_GROUP
TOP_K = 2
FILTER_EMB = 5
DECAY_TARGET = 1e-2
FAST_DECAY_PCT = 0.3
SLOW_DECAY_PCT = 1.5

LANES = 128
ROUTE_LANES = LANES
SLAB = 16
CONV_LANES = 256
PHASE2_UNROLL = 4
ROW_TILE = 512
EXPERT_ROWS = 1024
VMEM_LIMIT = 56 * 1024 * 1024


def _cparams(n_axes, vmem=None):
    return pltpu.CompilerParams(dimension_semantics=("arbitrary",) * n_axes,
                                vmem_limit_bytes=vmem)


def _inproj_kernel(x_ref, g_ref, w_ref, hy_ref, gm_ref, *, d_hy3):
    x = x_ref[...]
    ms = jnp.mean(x * x, axis=-1, keepdims=True)
    n = (x * lax.rsqrt(ms + EPS) * g_ref[...]).astype(BF16)
    p = jnp.dot(n, w_ref[...], preferred_element_type=F32)
    hy_ref[...] = p[:, :d_hy3]
    gm_ref[...] = p[:, d_hy3:].astype(BF16)


def _inproj(x2, g, w_bf, d_hy3):
    t, d = x2.shape
    dp = w_bf.shape[1]
    return pl.pallas_call(
        functools.partial(_inproj_kernel, d_hy3=d_hy3),
        grid=(t // ROW_TILE,),
        in_specs=[pl.BlockSpec((ROW_TILE, d), lambda i: (i, 0)),
                  pl.BlockSpec((1, d), lambda i: (0, 0)),
                  pl.BlockSpec((d, dp), lambda i: (0, 0))],
        out_specs=[pl.BlockSpec((ROW_TILE, d_hy3), lambda i: (i, 0)),
                   pl.BlockSpec((ROW_TILE, dp - d_hy3), lambda i: (i, 0))],
        out_shape=[jax.ShapeDtypeStruct((t, d_hy3), F32),
                   jax.ShapeDtypeStruct((t, dp - d_hy3), BF16)],
        compiler_params=_cparams(1, VMEM_LIMIT),
        name="inproj",
    )(x2, g, w_bf)


def _shortconv_kernel(v_ref, hp_ref, hn_ref, w_ref, b_ref, o_ref, *, tiles_per_seq):
    i = pl.program_id(0)
    v = v_ref[...]
    rows = v.shape[0]
    row = lax.broadcasted_iota(I32, (rows, 1), 0)
    pos = i % tiles_per_seq
    first = jnp.where(pos == 0, 0.0, 1.0)
    last = jnp.where(pos == tiles_per_seq - 1, 0.0, 1.0)
    prev = jnp.where(row == 0, hp_ref[7:8, :] * first, pltpu.roll(v, 1, 0))
    nxt = jnp.where(row == rows - 1, hn_ref[0:1, :] * last, pltpu.roll(v, rows - 1, 0))
    res = w_ref[0:1, :] * prev + w_ref[1:2, :] * v + w_ref[2:3, :] * nxt + b_ref[...]
    for c in range(o_ref.shape[1]):
        o_ref[0, c] = res[:, c * LANES:(c + 1) * LANES]


def _shortconv(hy, w, b, n_seq, seq_len):
    t, c = hy.shape
    tiles_per_seq = seq_len // ROW_TILE
    hb = ROW_TILE // 8
    nhb = t // 8
    return pl.pallas_call(
        functools.partial(_shortconv_kernel, tiles_per_seq=tiles_per_seq),
        grid=(t // ROW_TILE,),
        in_specs=[pl.BlockSpec((ROW_TILE, c), lambda i: (i, 0)),
                  pl.BlockSpec((8, c), lambda i: (jnp.maximum(i * hb - 1, 0), 0)),
                  pl.BlockSpec((8, c), lambda i: (jnp.minimum((i + 1) * hb, nhb - 1), 0)),
                  pl.BlockSpec((3, c), lambda i: (0, 0)),
                  pl.BlockSpec((1, c), lambda i: (0, 0))],
        out_specs=pl.BlockSpec((1, c // LANES, ROW_TILE, LANES),
                               lambda i: (i // tiles_per_seq, 0, i % tiles_per_seq, 0)),
        out_shape=jax.ShapeDtypeStruct((n_seq, c // LANES, seq_len, LANES), F32),
        compiler_params=_cparams(1, VMEM_LIMIT),
        name="shortconv",
    )(hy, hy, hy, w, b)


def _cos_sin(m, period):
    ang = m.astype(F32) * (2.0 * math.pi / period)
    return jnp.cos(ang), jnp.sin(ang)


def _stack_complex(mr, mi):
    top = jnp.concatenate([mr, -mi], axis=-1)
    bot = jnp.concatenate([mi, mr], axis=-1)
    return jnp.concatenate([top, bot], axis=-2)


def _dft_tables(nf):
    na = nf // 2
    n = nf * nf
    idx = jnp.arange(nf, dtype=I32)
    c, s = _cos_sin((idx[:, None] * idx[None, :]) % nf, nf)
    f2 = _stack_complex(c, -s)
    f2i = _stack_complex(c, s)
    b_ = idx[:, None, None]
    ka = idx[None, :, None]
    a_ = idx[None, None, :]
    m1 = (nf * a_ * ka + b_ * ka) % n
    c1, s1 = _cos_sin(m1, n)
    f1 = _stack_complex(c1[:, :, :na], -s1[:, :, :na])
    f1_full = jnp.concatenate([c1, -s1], axis=1)
    c3 = jnp.swapaxes(c1, 1, 2)[:, :na, :]
    s3 = jnp.swapaxes(s1, 1, 2)[:, :na, :]
    f3 = _stack_complex(c3, s3)
    return f1, f2, f2i, f3, f1_full


def _split_bf16(x):
    hi = x.astype(BF16)
    lo = (x - hi.astype(F32)).astype(BF16)
    return hi, lo


def _stack_split_lhs(a):
    hi, lo = _split_bf16(a)
    return jnp.concatenate([hi, lo, hi], axis=-1)


def _stack_split_rhs(b):
    hi, lo = _split_bf16(b)
    return jnp.concatenate([hi, hi, lo], axis=-2)


def _filter_taps_kernel(wemb_ref, bemb_ref, win_ref, bin_ref, freq_ref, wout_ref, delta_ref, o_ref,
                        *, seq_len, n_inner, d_hy):
    rows = o_ref.shape[0]
    j = pl.program_id(0) * rows + lax.broadcasted_iota(I32, (rows, 1), 0)
    tidx = jnp.where(j < seq_len, j, 2 * seq_len - j).astype(F32)
    t = tidx / float(seq_len - 1)
    fr0 = jnp.float32(1e-4)
    ang0 = (2.0 * math.pi / seq_len) * tidx * fr0
    ang1 = (2.0 * math.pi / seq_len) * tidx
    freq = freq_ref[...]
    pre = (t * wemb_ref[0:1, :] + jnp.cos(ang0) * wemb_ref[1:2, :] + jnp.cos(ang1) * wemb_ref[2:3, :]
           - jnp.sin(ang0) * wemb_ref[3:4, :] - jnp.sin(ang1) * wemb_ref[4:5, :] + bemb_ref[...])
    hdn = jnp.sin(freq * pre)
    for i in range(n_inner):
        hdn = jnp.sin(freq * (jnp.dot(_stack_split_lhs(hdn), win_ref[i], preferred_element_type=F32)
                              + bin_ref[i:i + 1, :]))
    h = jnp.dot(_stack_split_lhs(hdn), wout_ref[...], preferred_element_type=F32)
    decay = jnp.exp(-t * delta_ref[...])
    for o in range(HY_ORDER):
        fwd = h[:, (2 * o) * d_hy:(2 * o + 1) * d_hy]
        bwd = h[:, (2 * o + 1) * d_hy:(2 * o + 2) * d_hy]
        val = jnp.where(j < seq_len, fwd, bwd) * decay
        o_ref[:, o * d_hy:(o + 1) * d_hy] = jnp.where(j == seq_len, 0.0, val)


def _filter_taps(wemb, bemb, win, bin_, freq, wout, deltas, seq_len, d_hy):
    rows = 1024
    n2 = 2 * seq_len
    width = wemb.shape[1]
    n_inner = win.shape[0]
    full = lambda *shape: pl.BlockSpec(shape, lambda i: (0,) * len(shape))
    return pl.pallas_call(
        functools.partial(_filter_taps_kernel, seq_len=seq_len, n_inner=n_inner, d_hy=d_hy),
        grid=(n2 // rows,),
        in_specs=[full(FILTER_EMB, width), full(1, width), full(n_inner, 3 * width, width),
                  full(n_inner, width), full(1, width), full(3 * width, 2 * HY_ORDER * d_hy),
                  full(1, d_hy)],
        out_specs=pl.BlockSpec((rows, HY_ORDER * d_hy), lambda i: (i, 0)),
        out_shape=jax.ShapeDtypeStruct((n2, HY_ORDER * d_hy), F32),
        compiler_params=_cparams(1, VMEM_LIMIT),
        name="filter_taps",
    )(wemb, bemb, _stack_split_rhs(win), bin_, freq, _stack_split_rhs(wout), deltas)


def _filter_fft1_kernel(k_ref, f_ref, o_ref):
    for bl in range(SLAB):
        o_ref[bl] = jnp.dot(f_ref[bl], _stack_split_rhs(k_ref[:, bl, :]), preferred_element_type=F32)


def _filter_fft2_kernel(g_ref, f_ref, o_ref, *, scale):
    for kl in range(SLAB):
        rhs = _stack_split_rhs(jnp.concatenate([g_ref[:, 0, kl, :], g_ref[:, 1, kl, :]], axis=0))
        o_ref[kl] = jnp.dot(f_ref[...], rhs, preferred_element_type=F32) * scale


def _filter_spectrum(taps, f1_full, f2, nf):
    c = taps.shape[1]
    k3 = taps.reshape(nf, nf, c)
    g = pl.pallas_call(
        _filter_fft1_kernel,
        grid=(nf // SLAB, c // CONV_LANES),
        in_specs=[pl.BlockSpec((nf, SLAB, CONV_LANES), lambda i, j: (0, i, j)),
                  pl.BlockSpec((SLAB, 2 * nf, 3 * nf), lambda i, j: (i, 0, 0))],
        out_specs=pl.BlockSpec((SLAB, 2 * nf, CONV_LANES), lambda i, j: (i, 0, j)),
        out_shape=jax.ShapeDtypeStruct((nf, 2 * nf, c), F32),
        compiler_params=_cparams(2, VMEM_LIMIT),
        name="filter_fft1",
    )(k3, _stack_split_lhs(f1_full))
    g4 = g.reshape(nf, 2, nf, c)
    return pl.pallas_call(
        functools.partial(_filter_fft2_kernel, scale=1.0 / (nf * nf)),
        grid=(nf // SLAB, c // CONV_LANES),
        in_specs=[pl.BlockSpec((nf, 2, SLAB, CONV_LANES), lambda i, j: (0, 0, i, j)),
                  pl.BlockSpec((2 * nf, 6 * nf), lambda i, j: (0, 0))],
        out_specs=pl.BlockSpec((SLAB, 2 * nf, CONV_LANES), lambda i, j: (i, 0, j)),
        out_shape=jax.ShapeDtypeStruct((nf, 2 * nf, c), F32),
        compiler_params=_cparams(2, VMEM_LIMIT),
        name="filter_fft2",
    )(g4, _stack_split_lhs(f2))


def _pack(re, im):
    rb = lax.bitcast_convert_type(re.astype(BF16).astype(F32), U32)
    ib = lax.bitcast_convert_type(im.astype(BF16).astype(F32), U32)
    return rb | (ib >> 16)


def _unpack(w):
    re = lax.bitcast_convert_type(w & jnp.uint32(0xFFFF0000), F32).astype(BF16)
    im = lax.bitcast_convert_type(w << 16, F32).astype(BF16)
    return re, im


def _longconv_kernel(z1_ref, f1_ref, kf_ref, f2_ref, f2i_ref, f3_ref, gate_ref, zs_ref, skip_ref,
                     o_ref, g_ref, *, nf):
    na = nf // 2
    ns = nf // SLAB
    halves = g_ref.shape[1]
    t = pl.program_id(2)
    flat = lambda ref: ref.reshape(2, halves, na * SLAB, LANES)

    def load_rows(ref4, s, bl):
        return jnp.concatenate([ref4[s, h, pl.ds(bl, na, stride=SLAB), :] for h in range(halves)], axis=1)

    def store_col(blk, bl, val):
        for h in range(halves):
            g_ref[blk, h, pl.ds(bl, nf, stride=SLAB), :] = val[:, h * LANES:(h + 1) * LANES]

    def load_col(blk, bl):
        return jnp.concatenate([g_ref[blk, h, pl.ds(bl, nf, stride=SLAB), :] for h in range(halves)],
                               axis=1)

    @pl.when(t < ns)
    def _phase1():
        z1 = flat(z1_ref)
        for bl in range(SLAB):
            rhs = jnp.concatenate([load_rows(z1, 0, bl), load_rows(z1, 1, bl)], axis=0).astype(BF16)
            out = jnp.dot(f1_ref[bl], rhs, preferred_element_type=F32)
            store_col(t, bl, _pack(out[:nf], out[nf:]))

    @pl.when(jnp.logical_and(t >= ns, t < 2 * ns))
    def _phase2():
        i = t - ns

        def load_slab(ka):
            row0 = pl.multiple_of(ka * SLAB, SLAB)
            return jnp.concatenate([g_ref[:, h, pl.ds(row0, SLAB), :].reshape(nf, LANES)
                                    for h in range(halves)], axis=1)

        def transform(w, kl):
            re, im = _unpack(w)
            s = jnp.dot(f2_ref[...], jnp.concatenate([re, im], axis=0),
                        preferred_element_type=F32)
            sr, si = s[:nf], s[nf:]
            kr, ki = kf_ref[kl, :nf, :], kf_ref[kl, nf:, :]
            pr = (sr * kr - si * ki).astype(BF16)
            pi = (sr * ki + si * kr).astype(BF16)
            h = jnp.dot(f2i_ref[...], jnp.concatenate([pr, pi], axis=0),
                        preferred_element_type=F32)
            return _pack(h[:nf], h[nf:])

        def store_slab(ka, packed):
            row0 = pl.multiple_of(ka * SLAB, SLAB)
            for hf in range(halves):
                g_ref[:, hf, pl.ds(row0, SLAB), :] = packed[:, hf * LANES:(hf + 1) * LANES].reshape(
                    ns, SLAB, LANES)

        def body(grp, carry):
            kls = [grp * PHASE2_UNROLL + u for u in range(PHASE2_UNROLL)]
            ws = [load_slab(i * SLAB + kl) for kl in kls]
            outs = [transform(w, kl) for w, kl in zip(ws, kls)]
            for kl, packed in zip(kls, outs):
                store_slab(i * SLAB + kl, packed)
            return carry

        lax.fori_loop(0, SLAB // PHASE2_UNROLL, body, 0)

    @pl.when(t >= 2 * ns)
    def _phase3():
        j = t - 2 * ns
        skip = skip_ref[...]
        gate, zs, out = flat(gate_ref), flat(zs_ref), flat(o_ref)
        for bl in range(SLAB):
            re, im = _unpack(load_col(j, bl))
            y = jnp.dot(f3_ref[bl], jnp.concatenate([re, im], axis=0),
                        preferred_element_type=F32)
            for s in range(2):
                val = load_rows(gate, s, bl) * (y[s * na:(s + 1) * na] + load_rows(zs, s, bl) * skip)
                for h in range(halves):
                    out[s, h, pl.ds(bl, na, stride=SLAB), :] = val[:, h * LANES:(h + 1) * LANES]


def _longconv(gates5, z5, kf, tables, skip, gate_blk, z_blk, kf_col, nf):
    f1, f2, f2i, f3 = tables
    b, _, na, _, _ = z5.shape
    c = skip.shape[1]
    ns = nf // SLAB
    halves = CONV_LANES // LANES
    nchunk = c // CONV_LANES
    p1 = lambda t: jnp.minimum(t, ns - 1)
    p2 = lambda t: jnp.clip(t - ns, 0, ns - 1)
    p3 = lambda t: jnp.clip(t - 2 * ns, 0, ns - 1)
    cl = CONV_LANES
    blk5 = (2, halves, na, SLAB, LANES)
    return pl.pallas_call(
        functools.partial(_longconv_kernel, nf=nf),
        grid=(b // 2, nchunk, 3 * ns),
        in_specs=[
            pl.BlockSpec(blk5, lambda p, q, t: (p, z_blk + q, 0, p1(t), 0)),
            pl.BlockSpec((SLAB, 2 * nf, 2 * na), lambda p, q, t: (p1(t), 0, 0)),
            pl.BlockSpec((SLAB, 2 * nf, cl), lambda p, q, t: (p2(t), 0, kf_col + q)),
            pl.BlockSpec((2 * nf, 2 * nf), lambda p, q, t: (0, 0)),
            pl.BlockSpec((2 * nf, 2 * nf), lambda p, q, t: (0, 0)),
            pl.BlockSpec((SLAB, 2 * na, 2 * nf), lambda p, q, t: (p3(t), 0, 0)),
            pl.BlockSpec(blk5, lambda p, q, t: (p, gate_blk + q, 0, p3(t), 0)),
            pl.BlockSpec(blk5, lambda p, q, t: (p, z_blk + q, 0, p3(t), 0)),
            pl.BlockSpec((1, cl), lambda p, q, t: (0, q)),
        ],
        out_specs=pl.BlockSpec(blk5, lambda p, q, t: (p, q, 0, p3(t), 0)),
        out_shape=jax.ShapeDtypeStruct((b, c // LANES, na, nf, LANES), F32),
        scratch_shapes=[pltpu.VMEM((ns, halves, nf * SLAB, LANES), U32)],
        compiler_params=_cparams(3, VMEM_LIMIT),
        name="longconv",
    )(z5, f1, kf, f2, f2i, f3, gates5, z5, skip)


def _gelu_exact(x):
    return 0.5 * x * (1.0 + lax.erf(x * (1.0 / math.sqrt(2.0))))


def _gmlp_kernel(gm_ref, lng_ref, lnb_ref, ws_ref, bias_ref, o_ref, *, d_gm):
    g = _gelu_exact(gm_ref[...].astype(F32))
    u, v = g[:, :d_gm], g[:, d_gm:]
    mu = jnp.mean(v, axis=-1, keepdims=True)
    vc = v - mu
    var = jnp.mean(vc * vc, axis=-1, keepdims=True)
    vh = (vc * lax.rsqrt(var + EPS) * lng_ref[...] + lnb_ref[...]).astype(BF16)
    head = lax.broadcasted_iota(I32, (GM_CHUNK, d_gm), 1) // (d_gm // GM_HEADS)
    for c in range(gm_ref.shape[0] // GM_CHUNK):
        rows = slice(c * GM_CHUNK, (c + 1) * GM_CHUNK)
        r = jnp.dot(ws_ref[...], vh[rows], preferred_element_type=F32)
        s = r[:GM_CHUNK]
        for h in range(1, GM_HEADS):
            s = jnp.where(head == h, r[h * GM_CHUNK:(h + 1) * GM_CHUNK], s)
        o_ref[rows, :] = u[rows] * (s + bias_ref[...])


def _gmlp(gm, ln_g, ln_b, ws_stack, bias_t):
    t, c2 = gm.shape
    d_gm = c2 // 2
    return pl.pallas_call(
        functools.partial(_gmlp_kernel, d_gm=d_gm),
        grid=(t // ROW_TILE,),
        in_specs=[pl.BlockSpec((ROW_TILE, c2), lambda i: (i, 0)),
                  pl.BlockSpec((1, d_gm), lambda i: (0, 0)),
                  pl.BlockSpec((1, d_gm), lambda i: (0, 0)),
                  pl.BlockSpec(ws_stack.shape, lambda i: (0, 0)),
                  pl.BlockSpec(bias_t.shape, lambda i: (0, 0))],
        out_specs=pl.BlockSpec((ROW_TILE, d_gm), lambda i: (i, 0)),
        out_shape=jax.ShapeDtypeStruct((t, d_gm), F32),
        compiler_params=_cparams(1, VMEM_LIMIT),
        name="gmlp",
    )(gm, ln_g, ln_b, ws_stack, bias_t)


def _head_rms(y, bd):
    ms = jnp.dot((y * y).astype(BF16), bd, preferred_element_type=F32)
    return y * lax.rsqrt(ms + EPS)


def _route(logits):
    lane = lax.broadcasted_iota(I32, logits.shape, 1)
    neg = jnp.float32(-1e30)
    big = jnp.int32(ROUTE_LANES)
    gmask = lane < N_GROUPS
    gl = jnp.where(gmask, logits, neg)
    gmax = jnp.max(gl, axis=-1, keepdims=True)
    grp = jnp.min(jnp.where(jnp.logical_and(gl == gmax, gmask), lane, big), axis=-1, keepdims=True)
    psum = jnp.sum(jnp.where(gmask, jnp.exp(gl - gmax), 0.0), axis=-1, keepdims=True)
    p_grp = 1.0 / psum
    lo = N_GROUPS + EXPERTS_PER_GROUP * grp
    emask = jnp.logical_and(lane >= lo, lane < lo + EXPERTS_PER_GROUP)
    el = jnp.where(emask, logits, neg)
    m1 = jnp.max(el, axis=-1, keepdims=True)
    i1 = jnp.min(jnp.where(jnp.logical_and(el == m1, emask), lane, big), axis=-1, keepdims=True)
    emask2 = jnp.logical_and(emask, lane != i1)
    el2 = jnp.where(emask2, logits, neg)
    m2 = jnp.max(el2, axis=-1, keepdims=True)
    i2 = jnp.min(jnp.where(jnp.logical_and(el2 == m2, emask2), lane, big), axis=-1, keepdims=True)
    d = jnp.exp(m2 - m1)
    g1 = p_grp * (1.0 / (1.0 + d))
    g2 = p_grp * (d / (1.0 + d))
    e1 = (i1 - N_GROUPS).astype(F32)
    e2 = (i2 - N_GROUPS).astype(F32)
    return jnp.where(lane == 0, e1, jnp.where(lane == 1, e2, jnp.where(lane == 2, g1,
                     jnp.where(lane == 3, g2, 0.0))))


def _outproj_kernel(x_ref, yh_ref, yg_ref, mg_ref, bd_ref, wo_ref, fg_ref, wrh_ref, wrl_ref, br_ref,
                    x1_ref, n2_ref, route_ref, count_ref, carry_ref, *, d_hy):
    @pl.when(pl.program_id(0) == 0)
    def _init():
        carry_ref[...] = jnp.zeros_like(carry_ref)

    bd = bd_ref[...]
    mg = mg_ref[...]
    yh = jnp.concatenate([yh_ref[0, c] for c in range(yh_ref.shape[1])], axis=1)
    mh = (_head_rms(yh, bd) * mg[:, :d_hy]).astype(BF16)
    mgm = (_head_rms(yg_ref[...], bd) * mg[:, d_hy:]).astype(BF16)
    wo = wo_ref[...]
    x1 = (x_ref[...] + jnp.dot(mh, wo[:d_hy], preferred_element_type=F32)
          + jnp.dot(mgm, wo[d_hy:], preferred_element_type=F32))
    x1_ref[...] = x1
    ms = jnp.mean(x1 * x1, axis=-1, keepdims=True)
    n2 = x1 * lax.rsqrt(ms + EPS) * fg_ref[...]
    half = n2.shape[1] // 2
    n2_ref[...] = _pack(n2[:, :half], n2[:, half:])
    hi, lo = _split_bf16(n2)
    wrh = wrh_ref[...]
    logits = (jnp.dot(hi, wrh, preferred_element_type=F32) + jnp.dot(lo, wrh, preferred_element_type=F32)
              + jnp.dot(hi, wrl_ref[...], preferred_element_type=F32) + br_ref[...])
    route = _route(logits)

    rows = route.shape[0]
    lane = lax.broadcasted_iota(I32, route.shape, 1)
    oh0 = lane == route[:, 0:1].astype(I32)
    oh1 = lane == route[:, 1:2].astype(I32)
    oh = jnp.where(jnp.logical_or(oh0, oh1), 1.0, 0.0)
    r = lax.broadcasted_iota(I32, (rows, rows), 0)
    c = lax.broadcasted_iota(I32, (rows, rows), 1)
    ltri = jnp.where(c < r, 1.0, 0.0).astype(BF16)
    tot = jnp.dot(ltri, oh.astype(BF16), preferred_element_type=F32) + carry_ref[...]
    rank0 = jnp.sum(jnp.where(oh0, tot, 0.0), axis=-1, keepdims=True)
    rank1 = jnp.sum(jnp.where(oh1, tot, 0.0), axis=-1, keepdims=True)
    route_ref[...] = jnp.where(lane == 4, rank0, jnp.where(lane == 5, rank1, route))
    carry_ref[...] += jnp.sum(oh, axis=0, keepdims=True)
    count_ref[...] = carry_ref[...]


def _outproj(x2, yh, yg, mix_g, bd, wo_bf, ffn_g, wr_hi, wr_lo, br):
    t, d = x2.shape
    _, nch, seq_len, _ = yh.shape
    d_hy = nch * LANES
    tiles_per_seq = seq_len // ROW_TILE
    row = lambda w: pl.BlockSpec((ROW_TILE, w), lambda i: (i, 0))
    full = lambda a: pl.BlockSpec(a.shape, lambda i: (0,) * a.ndim)
    return pl.pallas_call(
        functools.partial(_outproj_kernel, d_hy=d_hy),
        grid=(t // ROW_TILE,),
        in_specs=[row(d),
                  pl.BlockSpec((1, nch, ROW_TILE, LANES),
                               lambda i: (i // tiles_per_seq, 0, i % tiles_per_seq, 0)),
                  row(yg.shape[1]), full(mix_g), full(bd), full(wo_bf), full(ffn_g),
                  full(wr_hi), full(wr_lo), full(br)],
        out_specs=[row(d), row(d // 2), row(ROUTE_LANES), pl.BlockSpec((1, ROUTE_LANES), lambda i: (0, 0))],
        out_shape=[jax.ShapeDtypeStruct((t, d), F32), jax.ShapeDtypeStruct((t, d // 2), U32),
                   jax.ShapeDtypeStruct((t, ROUTE_LANES), F32),
                   jax.ShapeDtypeStruct((1, ROUTE_LANES), F32)],
        scratch_shapes=[pltpu.VMEM((1, ROUTE_LANES), F32)],
        compiler_params=_cparams(1, VMEM_LIMIT),
        name="outproj_router",
    )(x2, yh, yg, mix_g, bd, wo_bf, ffn_g, wr_hi, wr_lo, br)


def _lane_pick(vals, sel_lane):
    lane = lax.broadcasted_iota(I32, vals.shape, 1)
    return jnp.sum(jnp.where(lane == sel_lane, vals, 0.0), axis=-1, keepdims=True)


def _dest_kernel(route_ref, pstart_ref, dest_ref):
    route = route_ref[...]
    lane = lax.broadcasted_iota(I32, route.shape, 1)
    ps = jnp.broadcast_to(pstart_ref[...], route.shape)
    d0 = _lane_pick(ps, route[:, 0:1].astype(I32)) + route[:, 4:5]
    d1 = _lane_pick(ps, route[:, 1:2].astype(I32)) + route[:, 5:6]
    dest_ref[...] = jnp.where(lane == 0, d0, jnp.where(lane == 1, d1, 0.0)).astype(I32)


def _dests(route, pstart):
    t = route.shape[0]
    rows = 4 * ROW_TILE
    row = pl.BlockSpec((rows, ROUTE_LANES), lambda i: (i, 0))
    return pl.pallas_call(
        _dest_kernel,
        grid=(t // rows,),
        in_specs=[row, pl.BlockSpec((1, ROUTE_LANES), lambda i: (0, 0))],
        out_specs=row,
        out_shape=jax.ShapeDtypeStruct((t, ROUTE_LANES), I32),
        compiler_params=_cparams(1),
        name="moe_dest",
    )(route, pstart)


def _row_copy(src_ref, src_row, dst_ref, dst_row, sem):
    return pltpu.make_async_copy(src_ref.at[pl.ds(src_row, 1)], dst_ref.at[pl.ds(dst_row, 1)], sem)


def _dispatch_kernel(zrow_ref, zflag_ref, nu_ref, dest_ref, n2_ref, buf_ref, zero_ref, sem, zsem):
    i = pl.program_id(0)
    rows = dest_ref.shape[-1] // TOP_K
    n_blocks = buf_ref.shape[0] // EXPERT_ROWS

    def zero_copy(start):
        start = pl.multiple_of(start, EXPERT_ROWS)
        return pltpu.make_async_copy(zero_ref, buf_ref.at[pl.ds(start, EXPERT_ROWS)], zsem)

    @pl.when(i == 0)
    def _zero_unfilled_blocks():
        zero_ref[...] = jnp.zeros_like(zero_ref)
        for e in range(N_EXPERTS):
            @pl.when(zflag_ref[e] > 0)
            def _():
                zero_copy(zrow_ref[e]).start()
        for e in range(N_EXPERTS):
            @pl.when(zflag_ref[e] > 0)
            def _():
                zero_copy(zrow_ref[e]).wait()

        def tail(b, carry):
            zero_copy(b * EXPERT_ROWS).start()
            zero_copy(b * EXPERT_ROWS).wait()
            return carry

        lax.fori_loop(nu_ref[0], n_blocks, tail, 0)

    def issue(r, carry):
        for k in range(TOP_K):
            _row_copy(n2_ref, r, buf_ref, dest_ref[0, 0, TOP_K * r + k], sem).start()
        return carry

    lax.fori_loop(0, rows, issue, 0)
    for k in range(TOP_K):
        pltpu.make_async_copy(n2_ref, buf_ref.at[pl.ds(0, rows)], sem).wait()


def _dispatch(n2, dest3, zrow, zflag, n_used, n_buf_rows):
    t, d = n2.shape
    rows = ROW_TILE
    return pl.pallas_call(
        _dispatch_kernel,
        grid_spec=pltpu.PrefetchScalarGridSpec(
            num_scalar_prefetch=3,
            grid=(t // rows,),
            in_specs=[pl.BlockSpec((1, 1, TOP_K * rows), lambda i, zr, zf, nu: (i, 0, 0),
                                   memory_space=pltpu.SMEM),
                      pl.BlockSpec((rows, d), lambda i, zr, zf, nu: (i, 0))],
            out_specs=pl.BlockSpec(memory_space=pl.ANY),
            scratch_shapes=[pltpu.VMEM((EXPERT_ROWS, d), n2.dtype), pltpu.SemaphoreType.DMA(()),
                            pltpu.SemaphoreType.DMA(())]),
        out_shape=jax.ShapeDtypeStruct((n_buf_rows, d), n2.dtype),
        compiler_params=_cparams(1, VMEM_LIMIT),
        name="moe_dispatch",
    )(zrow, zflag, n_used, dest3, n2)


def _ffn_kernel(be_ref, nu_ref, x_ref, wg_ref, wu_ref, wd_ref, o_ref):
    used = pl.program_id(0) < nu_ref[0]

    @pl.when(used)
    def _():
        xb = jnp.concatenate(_unpack(x_ref[...]), axis=1)
        g = jnp.dot(xb, wg_ref[0], preferred_element_type=F32)
        u = jnp.dot(xb, wu_ref[0], preferred_element_type=F32)
        h = (g * (1.0 / (1.0 + jnp.exp(-g))) * u).astype(BF16)
        o = jnp.dot(h, wd_ref[0], preferred_element_type=F32)
        half = o.shape[1] // 2
        o_ref[...] = _pack(o[:, :half], o[:, half:])

    @pl.when(jnp.logical_not(used))
    def _():
        o_ref[...] = jnp.zeros_like(o_ref)


def _expert_ffn(buf, block_e, n_used, wg, wu, wd):
    p, dh = buf.shape
    d, de = wg.shape[1], wg.shape[2]
    nb = p // EXPERT_ROWS
    blk = lambda i, be, nu: jnp.minimum(i, nu[0] - 1)
    return pl.pallas_call(
        _ffn_kernel,
        grid_spec=pltpu.PrefetchScalarGridSpec(
            num_scalar_prefetch=2,
            grid=(nb,),
            in_specs=[pl.BlockSpec((EXPERT_ROWS, dh), lambda i, be, nu: (blk(i, be, nu), 0)),
                      pl.BlockSpec((1, d, de), lambda i, be, nu: (be[blk(i, be, nu)], 0, 0)),
                      pl.BlockSpec((1, d, de), lambda i, be, nu: (be[blk(i, be, nu)], 0, 0)),
                      pl.BlockSpec((1, de, d), lambda i, be, nu: (be[blk(i, be, nu)], 0, 0))],
            out_specs=pl.BlockSpec((EXPERT_ROWS, dh), lambda i, be, nu: (i, 0))),
        out_shape=jax.ShapeDtypeStruct((p, dh), U32),
        compiler_params=_cparams(1, VMEM_LIMIT),
        name="expert_ffn",
    )(block_e, n_used, buf, wg, wu, wd)


def _combine_kernel(dest_ref, route_ref, x1_ref, fg_ref, obuf_ref, o_ref, r0_ref, r1_ref, sem):
    rows = x1_ref.shape[0]
    bufs = (r0_ref, r1_ref)

    def issue(r, carry):
        for k in range(TOP_K):
            _row_copy(obuf_ref, dest_ref[0, 0, TOP_K * r + k], bufs[k], r, sem).start()
        return carry

    lax.fori_loop(0, rows, issue, 0)
    for k in range(TOP_K):
        pltpu.make_async_copy(obuf_ref.at[pl.ds(0, rows)], bufs[k], sem).wait()
    route = route_ref[...]
    y0 = jnp.concatenate(_unpack(r0_ref[...]), axis=1).astype(F32)
    y1 = jnp.concatenate(_unpack(r1_ref[...]), axis=1).astype(F32)
    x2 = x1_ref[...] + route[:, 2:3] * y0 + route[:, 3:4] * y1
    ms = jnp.mean(x2 * x2, axis=-1, keepdims=True)
    o_ref[...] = x2 * lax.rsqrt(ms + EPS) * fg_ref[...]


def _combine(dest3, route, x1, final_g, obuf):
    t, d = x1.shape
    rows = ROW_TILE
    return pl.pallas_call(
        _combine_kernel,
        grid=(t // rows,),
        in_specs=[pl.BlockSpec((1, 1, TOP_K * rows), lambda i: (i, 0, 0), memory_space=pltpu.SMEM),
                  pl.BlockSpec((rows, ROUTE_LANES), lambda i: (i, 0)),
                  pl.BlockSpec((rows, d), lambda i: (i, 0)),
                  pl.BlockSpec((1, d), lambda i: (0, 0)),
                  pl.BlockSpec(memory_space=pl.ANY)],
        out_specs=pl.BlockSpec((rows, d), lambda i: (i, 0)),
        out_shape=jax.ShapeDtypeStruct((t, d), F32),
        scratch_shapes=[pltpu.VMEM((rows, d // 2), U32), pltpu.VMEM((rows, d // 2), U32),
                        pltpu.SemaphoreType.DMA(())],
        compiler_params=_cparams(1, VMEM_LIMIT),
        name="moe_combine",
    )(dest3, route, x1, final_g, obuf)


def _hier_moe_and_norm(x1, n2, route, counts, wg, wu, wd, final_g):
    t = x1.shape[0]
    a = t * TOP_K
    counts = counts[0, :N_EXPERTS].astype(I32)
    pad_counts = (counts + EXPERT_ROWS - 1) // EXPERT_ROWS * EXPERT_ROWS
    pad_end = jnp.cumsum(pad_counts)
    pad_start = pad_end - pad_counts
    n_blocks = -(-a // EXPERT_ROWS) + N_EXPERTS
    block_row = jnp.arange(n_blocks, dtype=I32) * EXPERT_ROWS
    block_e = jnp.minimum(jnp.sum((pad_end[None, :] <= block_row[:, None]).astype(I32), axis=1),
                          N_EXPERTS - 1)
    n_used = (pad_end[-1:] // EXPERT_ROWS).astype(I32)
    pstart_row = jnp.zeros((1, ROUTE_LANES), F32).at[0, :N_EXPERTS].set(pad_start.astype(F32))
    dest = _dests(route, pstart_row)
    dest3 = dest[:, :TOP_K].reshape(t // ROW_TILE, 1, TOP_K * ROW_TILE)
    zrow = (pad_end - EXPERT_ROWS).astype(I32)
    zflag = (pad_counts > 0).astype(I32)
    buf = _dispatch(n2, dest3, zrow, zflag, n_used, n_blocks * EXPERT_ROWS)
    obuf = _expert_ffn(buf, block_e, n_used, wg, wu, wd)
    return _combine(dest3, route, x1, final_g, obuf)


def _encoder(x, prm, kf, tables, nf):
    b, l, d = x.shape
    t = b * l
    d_hy = prm["skip"].shape[1]
    x2 = x.reshape(t, d)
    hy, gm = _inproj(x2, prm["mix_norm_g"], prm["w_in"], (HY_ORDER + 1) * d_hy)
    hy = _shortconv(hy, prm["short_w"], prm["short_b"], b, l)
    gates5 = hy.reshape(b, hy.shape[1], nf // 2, nf, LANES)
    z5 = gates5
    z_blk = HY_ORDER * d_hy // CONV_LANES
    for o in range(HY_ORDER):
        z5 = _longconv(gates5, z5, kf, tables, prm["skip"][o:o + 1], gate_blk=o * d_hy // CONV_LANES,
                       z_blk=z_blk, kf_col=o * d_hy // CONV_LANES, nf=nf)
        z_blk = 0
    y_hy = z5.reshape(b, d_hy // LANES, l, LANES)
    y_gm = _gmlp(gm, prm["ln_g"], prm["ln_b"], prm["ws_stack"], prm["bias_t"])
    x1, n2, route, counts = _outproj(x2, y_hy, y_gm, prm["mix_out_g"], prm["bd"], prm["w_out"], prm["ffn_norm_g"],
                             prm["wr_hi"], prm["wr_lo"], prm["br"])
    out = _hier_moe_and_norm(x1, n2, route, counts, prm["w_e_gate"], prm["w_e_up"], prm["w_e_down"],
                             prm["final_norm_g"])
    return out.reshape(b, l, d)


def kernel(x_prompt, x_sample, mix_norm_g, w_in, hy_short_w, hy_short_b, hy_filt_w_emb, hy_filt_b_emb,
           hy_filt_w_inner, hy_filt_b_inner, hy_filt_freq, hy_filt_w_out, hy_skip, gm_ln_g, gm_ln_b,
           gm_w_s, gm_b_s, mix_out_g, w_out, ffn_norm_g, w_group, b_group, w_expert_router,
           b_expert_router, w_e_gate, w_e_up, w_e_down, final_norm_g):
    assert w_in.shape[0] == 1, "one layer"
    l = x_prompt.shape[1]
    assert x_sample.shape[1] == l
    nf = math.isqrt(2 * l)
    assert nf * nf == 2 * l and nf % SLAB == 0
    d = x_prompt.shape[2]
    d_hy = hy_skip.shape[2]
    d_gm = gm_ln_g.shape[1]
    head_dim = d_gm // GM_HEADS
    assert d_hy // HY_HEADS == head_dim and d_hy == d_gm

    tables = _dft_tables(nf)
    f1, f2, f2i, f3, f1_full = tables
    tables_bf = tuple(a.astype(BF16) for a in (f1, f2, f2i, f3))

    max_decay = math.log(DECAY_TARGET) / FAST_DECAY_PCT
    min_decay = math.log(DECAY_TARGET) / SLOW_DECAY_PCT
    deltas = jnp.abs(jnp.linspace(min_decay, max_decay, d_hy, dtype=F32))[None, :]
    taps = _filter_taps(hy_filt_w_emb[0], hy_filt_b_emb[0][None, :], hy_filt_w_inner[0], hy_filt_b_inner[0],
                        hy_filt_freq[0][None, :], hy_filt_w_out[0], deltas, l, d_hy)
    kf = _filter_spectrum(taps, f1_full, f2, nf)

    n_route = N_GROUPS + N_EXPERTS
    wr = jnp.zeros((d, ROUTE_LANES), F32).at[:, :n_route].set(
        jnp.concatenate([w_group[0], w_expert_router[0]], axis=1))
    wr_hi = wr.astype(BF16)
    wr_lo = (wr - wr_hi.astype(F32)).astype(BF16)
    br = jnp.zeros((1, ROUTE_LANES), F32).at[0, :n_route].set(jnp.concatenate([b_group[0], b_expert_router[0]]))
    hid = jnp.arange(d_hy, dtype=I32) // head_dim
    bd = jnp.where(hid[:, None] == hid[None, :], 1.0 / head_dim, 0.0).astype(BF16)

    prm = dict(
        mix_norm_g=mix_norm_g, w_in=w_in[0].astype(BF16), short_w=hy_short_w[0], short_b=hy_short_b,
        skip=hy_skip[0], ln_g=gm_ln_g, ln_b=gm_ln_b,
        ws_stack=gm_w_s[0].reshape(GM_HEADS * GM_CHUNK, GM_CHUNK).astype(BF16),
        bias_t=jnp.repeat(gm_b_s[0].T, head_dim, axis=1),
        mix_out_g=mix_out_g, bd=bd, w_out=w_out[0].astype(BF16), ffn_norm_g=ffn_norm_g,
        wr_hi=wr_hi, wr_lo=wr_lo, br=br,
        w_e_gate=w_e_gate[0].astype(BF16), w_e_up=w_e_up[0].astype(BF16), w_e_down=w_e_down[0].astype(BF16),
        final_norm_g=final_norm_g[None, :])
    y_prompt = _encoder(x_prompt, prm, kf, tables_bf, nf)
    y_sample = _encoder(x_sample, prm, kf, tables_bf, nf)
    return (y_prompt, y_sample)
```

```python
import functools
import math

import jax
import jax.numpy as jnp
from jax import lax
from jax.experimental import pallas as pl
from jax.experimental.pallas import tpu as pltpu

F32 = jnp.float32
BF16 = jnp.bfloat16
U32 = jnp.uint32
I32 = jnp.int32

EPS = 1e-6
HY_ORDER = 2
HY_HEADS = 8
GM_HEADS = 8
GM_CHUNK = 128
N_GROUPS = 4
EXPERTS_PER_GROUP = 8
N_EXPERTS = N_GROUPS * EXPERTS_PER_GROUP
TOP_K = 2
FILTER_EMB = 5
DECAY_TARGET = 1e-2
FAST_DECAY_PCT = 0.3
SLOW_DECAY_PCT = 1.5

LANES = 128
ROUTE_LANES = LANES
SLAB = 16
CONV_LANES = 256
PHASE2_UNROLL = 4
ROW_TILE = 512
EXPERT_ROWS = 1024
CHUNK = 8
STAGE_ROWS = ROW_TILE * TOP_K + N_EXPERTS * CHUNK
VMEM_LIMIT = 56 * 1024 * 1024


def _cparams(n_axes, vmem=None):
    return pltpu.CompilerParams(dimension_semantics=("arbitrary",) * n_axes,
                                vmem_limit_bytes=vmem)


def _inproj_kernel(x_ref, g_ref, w_ref, hy_ref, gm_ref, *, d_hy3):
    x = x_ref[...]
    ms = jnp.mean(x * x, axis=-1, keepdims=True)
    n = (x * lax.rsqrt(ms + EPS) * g_ref[...]).astype(BF16)
    p = jnp.dot(n, w_ref[...], preferred_element_type=F32)
    hy_ref[...] = p[:, :d_hy3]
    gm_ref[...] = p[:, d_hy3:].astype(BF16)


def _inproj(x2, g, w_bf, d_hy3):
    t, d = x2.shape
    dp = w_bf.shape[1]
    return pl.pallas_call(
        functools.partial(_inproj_kernel, d_hy3=d_hy3),
        grid=(t // ROW_TILE,),
        in_specs=[pl.BlockSpec((ROW_TILE, d), lambda i: (i, 0)),
                  pl.BlockSpec((1, d), lambda i: (0, 0)),
                  pl.BlockSpec((d, dp), lambda i: (0, 0))],
        out_specs=[pl.BlockSpec((ROW_TILE, d_hy3), lambda i: (i, 0)),
                   pl.BlockSpec((ROW_TILE, dp - d_hy3), lambda i: (i, 0))],
        out_shape=[jax.ShapeDtypeStruct((t, d_hy3), F32),
                   jax.ShapeDtypeStruct((t, dp - d_hy3), BF16)],
        compiler_params=_cparams(1, VMEM_LIMIT),
        name="inproj",
    )(x2, g, w_bf)


def _shortconv_kernel(v_ref, hp_ref, hn_ref, w_ref, b_ref, o_ref, *, tiles_per_seq):
    i = pl.program_id(0)
    v = v_ref[...]
    rows = v.shape[0]
    row = lax.broadcasted_iota(I32, (rows, 1), 0)
    pos = i % tiles_per_seq
    first = jnp.where(pos == 0, 0.0, 1.0)
    last = jnp.where(pos == tiles_per_seq - 1, 0.0, 1.0)
    prev = jnp.where(row == 0, hp_ref[7:8, :] * first, pltpu.roll(v, 1, 0))
    nxt = jnp.where(row == rows - 1, hn_ref[0:1, :] * last, pltpu.roll(v, rows - 1, 0))
    res = w_ref[0:1, :] * prev + w_ref[1:2, :] * v + w_ref[2:3, :] * nxt + b_ref[...]
    for c in range(o_ref.shape[1]):
        o_ref[0, c] = res[:, c * LANES:(c + 1) * LANES]


def _shortconv(hy, w, b, n_seq, seq_len):
    t, c = hy.shape
    tiles_per_seq = seq_len // ROW_TILE
    hb = ROW_TILE // 8
    nhb = t // 8
    return pl.pallas_call(
        functools.partial(_shortconv_kernel, tiles_per_seq=tiles_per_seq),
        grid=(t // ROW_TILE,),
        in_specs=[pl.BlockSpec((ROW_TILE, c), lambda i: (i, 0)),
                  pl.BlockSpec((8, c), lambda i: (jnp.maximum(i * hb - 1, 0), 0)),
                  pl.BlockSpec((8, c), lambda i: (jnp.minimum((i + 1) * hb, nhb - 1), 0)),
                  pl.BlockSpec((3, c), lambda i: (0, 0)),
                  pl.BlockSpec((1, c), lambda i: (0, 0))],
        out_specs=pl.BlockSpec((1, c // LANES, ROW_TILE, LANES),
                               lambda i: (i // tiles_per_seq, 0, i % tiles_per_seq, 0)),
        out_shape=jax.ShapeDtypeStruct((n_seq, c // LANES, seq_len, LANES), F32),
        compiler_params=_cparams(1, VMEM_LIMIT),
        name="shortconv",
    )(hy, hy, hy, w, b)


def _cos_sin(m, period):
    ang = m.astype(F32) * (2.0 * math.pi / period)
    return jnp.cos(ang), jnp.sin(ang)


def _stack_complex(mr, mi):
    top = jnp.concatenate([mr, -mi], axis=-1)
    bot = jnp.concatenate([mi, mr], axis=-1)
    return jnp.concatenate([top, bot], axis=-2)


def _dft_tables(nf):
    na = nf // 2
    n = nf * nf
    idx = jnp.arange(nf, dtype=I32)
    c, s = _cos_sin((idx[:, None] * idx[None, :]) % nf, nf)
    f2 = _stack_complex(c, -s)
    f2i = _stack_complex(c, s)
    b_ = idx[:, None, None]
    ka = idx[None, :, None]
    a_ = idx[None, None, :]
    m1 = (nf * a_ * ka + b_ * ka) % n
    c1, s1 = _cos_sin(m1, n)
    f1 = _stack_complex(c1[:, :, :na], -s1[:, :, :na])
    f1_full = jnp.concatenate([c1, -s1], axis=1)
    c3 = jnp.swapaxes(c1, 1, 2)[:, :na, :]
    s3 = jnp.swapaxes(s1, 1, 2)[:, :na, :]
    f3 = _stack_complex(c3, s3)
    return f1, f2, f2i, f3, f1_full


def _split_bf16(x):
    hi = x.astype(BF16)
    lo = (x - hi.astype(F32)).astype(BF16)
    return hi, lo


def _stack_split_lhs(a):
    hi, lo = _split_bf16(a)
    return jnp.concatenate([hi, lo, hi], axis=-1)


def _stack_split_rhs(b):
    hi, lo = _split_bf16(b)
    return jnp.concatenate([hi, hi, lo], axis=-2)


def _filter_taps_kernel(wemb_ref, bemb_ref, win_ref, bin_ref, freq_ref, wout_ref, delta_ref, o_ref,
                        *, seq_len, n_inner, d_hy):
    rows = o_ref.shape[0]
    j = pl.program_id(0) * rows + lax.broadcasted_iota(I32, (rows, 1), 0)
    tidx = jnp.where(j < seq_len, j, 2 * seq_len - j).astype(F32)
    t = tidx / float(seq_len - 1)
    fr0 = jnp.float32(1e-4)
    ang0 = (2.0 * math.pi / seq_len) * tidx * fr0
    ang1 = (2.0 * math.pi / seq_len) * tidx
    freq = freq_ref[...]
    pre = (t * wemb_ref[0:1, :] + jnp.cos(ang0) * wemb_ref[1:2, :] + jnp.cos(ang1) * wemb_ref[2:3, :]
           - jnp.sin(ang0) * wemb_ref[3:4, :] - jnp.sin(ang1) * wemb_ref[4:5, :] + bemb_ref[...])
    hdn = jnp.sin(freq * pre)
    for i in range(n_inner):
        hdn = jnp.sin(freq * (jnp.dot(_stack_split_lhs(hdn), win_ref[i], preferred_element_type=F32)
                              + bin_ref[i:i + 1, :]))
    h = jnp.dot(_stack_split_lhs(hdn), wout_ref[...], preferred_element_type=F32)
    decay = jnp.exp(-t * delta_ref[...])
    for o in range(HY_ORDER):
        fwd = h[:, (2 * o) * d_hy:(2 * o + 1) * d_hy]
        bwd = h[:, (2 * o + 1) * d_hy:(2 * o + 2) * d_hy]
        val = jnp.where(j < seq_len, fwd, bwd) * decay
        o_ref[:, o * d_hy:(o + 1) * d_hy] = jnp.where(j == seq_len, 0.0, val)


def _filter_taps(wemb, bemb, win, bin_, freq, wout, deltas, seq_len, d_hy):
    rows = 1024
    n2 = 2 * seq_len
    width = wemb.shape[1]
    n_inner = win.shape[0]
    full = lambda *shape: pl.BlockSpec(shape, lambda i: (0,) * len(shape))
    return pl.pallas_call(
        functools.partial(_filter_taps_kernel, seq_len=seq_len, n_inner=n_inner, d_hy=d_hy),
        grid=(n2 // rows,),
        in_specs=[full(FILTER_EMB, width), full(1, width), full(n_inner, 3 * width, width),
                  full(n_inner, width), full(1, width), full(3 * width, 2 * HY_ORDER * d_hy),
                  full(1, d_hy)],
        out_specs=pl.BlockSpec((rows, HY_ORDER * d_hy), lambda i: (i, 0)),
        out_shape=jax.ShapeDtypeStruct((n2, HY_ORDER * d_hy), F32),
        compiler_params=_cparams(1, VMEM_LIMIT),
        name="filter_taps",
    )(wemb, bemb, _stack_split_rhs(win), bin_, freq, _stack_split_rhs(wout), deltas)


def _filter_fft1_kernel(k_ref, f_ref, o_ref):
    for bl in range(SLAB):
        o_ref[bl] = jnp.dot(f_ref[bl], _stack_split_rhs(k_ref[:, bl, :]), preferred_element_type=F32)


def _filter_fft2_kernel(g_ref, f_ref, o_ref, *, scale):
    for kl in range(SLAB):
        rhs = _stack_split_rhs(jnp.concatenate([g_ref[:, 0, kl, :], g_ref[:, 1, kl, :]], axis=0))
        o_ref[kl] = jnp.dot(f_ref[...], rhs, preferred_element_type=F32) * scale


def _filter_spectrum(taps, f1_full, f2, nf):
    c = taps.shape[1]
    k3 = taps.reshape(nf, nf, c)
    g = pl.pallas_call(
        _filter_fft1_kernel,
        grid=(nf // SLAB, c // CONV_LANES),
        in_specs=[pl.BlockSpec((nf, SLAB, CONV_LANES), lambda i, j: (0, i, j)),
                  pl.BlockSpec((SLAB, 2 * nf, 3 * nf), lambda i, j: (i, 0, 0))],
        out_specs=pl.BlockSpec((SLAB, 2 * nf, CONV_LANES), lambda i, j: (i, 0, j)),
        out_shape=jax.ShapeDtypeStruct((nf, 2 * nf, c), F32),
        compiler_params=_cparams(2, VMEM_LIMIT),
        name="filter_fft1",
    )(k3, _stack_split_lhs(f1_full))
    g4 = g.reshape(nf, 2, nf, c)
    return pl.pallas_call(
        functools.partial(_filter_fft2_kernel, scale=1.0 / (nf * nf)),
        grid=(nf // SLAB, c // CONV_LANES),
        in_specs=[pl.BlockSpec((nf, 2, SLAB, CONV_LANES), lambda i, j: (0, 0, i, j)),
                  pl.BlockSpec((2 * nf, 6 * nf), lambda i, j: (0, 0))],
        out_specs=pl.BlockSpec((SLAB, 2 * nf, CONV_LANES), lambda i, j: (i, 0, j)),
        out_shape=jax.ShapeDtypeStruct((nf, 2 * nf, c), F32),
        compiler_params=_cparams(2, VMEM_LIMIT),
        name="filter_fft2",
    )(g4, _stack_split_lhs(f2))


def _pack(re, im):
    rb = lax.bitcast_convert_type(re.astype(BF16).astype(F32), U32)
    ib = lax.bitcast_convert_type(im.astype(BF16).astype(F32), U32)
    return rb | (ib >> 16)


def _unpack(w):
    re = lax.bitcast_convert_type(w & jnp.uint32(0xFFFF0000), F32).astype(BF16)
    im = lax.bitcast_convert_type(w << 16, F32).astype(BF16)
    return re, im


def _longconv_kernel(z1_ref, f1_ref, kf_ref, f2_ref, f2i_ref, f3_ref, gate_ref, zs_ref, skip_ref,
                     o_ref, g_ref, *, nf):
    na = nf // 2
    ns = nf // SLAB
    halves = g_ref.shape[1]
    t = pl.program_id(2)
    flat = lambda ref: ref.reshape(2, halves, na * SLAB, LANES)

    def load_rows(ref4, s, bl):
        return jnp.concatenate([ref4[s, h, pl.ds(bl, na, stride=SLAB), :] for h in range(halves)], axis=1)

    def store_col(blk, bl, val):
        for h in range(halves):
            g_ref[blk, h, pl.ds(bl, nf, stride=SLAB), :] = val[:, h * LANES:(h + 1) * LANES]

    def load_col(blk, bl):
        return jnp.concatenate([g_ref[blk, h, pl.ds(bl, nf, stride=SLAB), :] for h in range(halves)],
                               axis=1)

    @pl.when(t < ns)
    def _phase1():
        z1 = flat(z1_ref)
        for bl in range(SLAB):
            rhs = jnp.concatenate([load_rows(z1, 0, bl), load_rows(z1, 1, bl)], axis=0).astype(BF16)
            out = jnp.dot(f1_ref[bl], rhs, preferred_element_type=F32)
            store_col(t, bl, _pack(out[:nf], out[nf:]))

    @pl.when(jnp.logical_and(t >= ns, t < 2 * ns))
    def _phase2():
        i = t - ns

        def load_slab(ka):
            row0 = pl.multiple_of(ka * SLAB, SLAB)
            return jnp.concatenate([g_ref[:, h, pl.ds(row0, SLAB), :].reshape(nf, LANES)
                                    for h in range(halves)], axis=1)

        def transform(w, kl):
            re, im = _unpack(w)
            s = jnp.dot(f2_ref[...], jnp.concatenate([re, im], axis=0),
                        preferred_element_type=F32)
            sr, si = s[:nf], s[nf:]
            kr, ki = kf_ref[kl, :nf, :], kf_ref[kl, nf:, :]
            pr = (sr * kr - si * ki).astype(BF16)
            pi = (sr * ki + si * kr).astype(BF16)
            h = jnp.dot(f2i_ref[...], jnp.concatenate([pr, pi], axis=0),
                        preferred_element_type=F32)
            return _pack(h[:nf], h[nf:])

        def store_slab(ka, packed):
            row0 = pl.multiple_of(ka * SLAB, SLAB)
            for hf in range(halves):
                g_ref[:, hf, pl.ds(row0, SLAB), :] = packed[:, hf * LANES:(hf + 1) * LANES].reshape(
                    ns, SLAB, LANES)

        def body(grp, carry):
            kls = [grp * PHASE2_UNROLL + u for u in range(PHASE2_UNROLL)]
            ws = [load_slab(i * SLAB + kl) for kl in kls]
            outs = [transform(w, kl) for w, kl in zip(ws, kls)]
            for kl, packed in zip(kls, outs):
                store_slab(i * SLAB + kl, packed)
            return carry

        lax.fori_loop(0, SLAB // PHASE2_UNROLL, body, 0)

    @pl.when(t >= 2 * ns)
    def _phase3():
        j = t - 2 * ns
        skip = skip_ref[...]
        gate, zs, out = flat(gate_ref), flat(zs_ref), flat(o_ref)
        for bl in range(SLAB):
            re, im = _unpack(load_col(j, bl))
            y = jnp.dot(f3_ref[bl], jnp.concatenate([re, im], axis=0),
                        preferred_element_type=F32)
            for s in range(2):
                val = load_rows(gate, s, bl) * (y[s * na:(s + 1) * na] + load_rows(zs, s, bl) * skip)
                for h in range(halves):
                    out[s, h, pl.ds(bl, na, stride=SLAB), :] = val[:, h * LANES:(h + 1) * LANES]


def _longconv(gates5, z5, kf, tables, skip, gate_blk, z_blk, kf_col, nf):
    f1, f2, f2i, f3 = tables
    b, _, na, _, _ = z5.shape
    c = skip.shape[1]
    ns = nf // SLAB
    halves = CONV_LANES // LANES
    nchunk = c // CONV_LANES
    p1 = lambda t: jnp.minimum(t, ns - 1)
    p2 = lambda t: jnp.clip(t - ns, 0, ns - 1)
    p3 = lambda t: jnp.clip(t - 2 * ns, 0, ns - 1)
    cl = CONV_LANES
    blk5 = (2, halves, na, SLAB, LANES)
    return pl.pallas_call(
        functools.partial(_longconv_kernel, nf=nf),
        grid=(b // 2, nchunk, 3 * ns),
        in_specs=[
            pl.BlockSpec(blk5, lambda p, q, t: (p, z_blk + q, 0, p1(t), 0)),
            pl.BlockSpec((SLAB, 2 * nf, 2 * na), lambda p, q, t: (p1(t), 0, 0)),
            pl.BlockSpec((SLAB, 2 * nf, cl), lambda p, q, t: (p2(t), 0, kf_col + q)),
            pl.BlockSpec((2 * nf, 2 * nf), lambda p, q, t: (0, 0)),
            pl.BlockSpec((2 * nf, 2 * nf), lambda p, q, t: (0, 0)),
            pl.BlockSpec((SLAB, 2 * na, 2 * nf), lambda p, q, t: (p3(t), 0, 0)),
            pl.BlockSpec(blk5, lambda p, q, t: (p, gate_blk + q, 0, p3(t), 0)),
            pl.BlockSpec(blk5, lambda p, q, t: (p, z_blk + q, 0, p3(t), 0)),
            pl.BlockSpec((1, cl), lambda p, q, t: (0, q)),
        ],
        out_specs=pl.BlockSpec(blk5, lambda p, q, t: (p, q, 0, p3(t), 0)),
        out_shape=jax.ShapeDtypeStruct((b, c // LANES, na, nf, LANES), F32),
        scratch_shapes=[pltpu.VMEM((ns, halves, nf * SLAB, LANES), U32)],
        compiler_params=_cparams(3, VMEM_LIMIT),
        name="longconv",
    )(z5, f1, kf, f2, f2i, f3, gates5, z5, skip)


def _gelu_exact(x):
    return 0.5 * x * (1.0 + lax.erf(x * (1.0 / math.sqrt(2.0))))


def _gmlp_kernel(gm_ref, lng_ref, lnb_ref, ws_ref, bias_ref, o_ref, *, d_gm):
    g = _gelu_exact(gm_ref[...].astype(F32))
    u, v = g[:, :d_gm], g[:, d_gm:]
    mu = jnp.mean(v, axis=-1, keepdims=True)
    vc = v - mu
    var = jnp.mean(vc * vc, axis=-1, keepdims=True)
    vh = (vc * lax.rsqrt(var + EPS) * lng_ref[...] + lnb_ref[...]).astype(BF16)
    head = lax.broadcasted_iota(I32, (GM_CHUNK, d_gm), 1) // (d_gm // GM_HEADS)
    for c in range(gm_ref.shape[0] // GM_CHUNK):
        rows = slice(c * GM_CHUNK, (c + 1) * GM_CHUNK)
        r = jnp.dot(ws_ref[...], vh[rows], preferred_element_type=F32)
        s = r[:GM_CHUNK]
        for h in range(1, GM_HEADS):
            s = jnp.where(head == h, r[h * GM_CHUNK:(h + 1) * GM_CHUNK], s)
        o_ref[rows, :] = u[rows] * (s + bias_ref[...])


def _gmlp(gm, ln_g, ln_b, ws_stack, bias_t):
    t, c2 = gm.shape
    d_gm = c2 // 2
    return pl.pallas_call(
        functools.partial(_gmlp_kernel, d_gm=d_gm),
        grid=(t // ROW_TILE,),
        in_specs=[pl.BlockSpec((ROW_TILE, c2), lambda i: (i, 0)),
                  pl.BlockSpec((1, d_gm), lambda i: (0, 0)),
                  pl.BlockSpec((1, d_gm), lambda i: (0, 0)),
                  pl.BlockSpec(ws_stack.shape, lambda i: (0, 0)),
                  pl.BlockSpec(bias_t.shape, lambda i: (0, 0))],
        out_specs=pl.BlockSpec((ROW_TILE, d_gm), lambda i: (i, 0)),
        out_shape=jax.ShapeDtypeStruct((t, d_gm), F32),
        compiler_params=_cparams(1, VMEM_LIMIT),
        name="gmlp",
    )(gm, ln_g, ln_b, ws_stack, bias_t)


def _head_rms(y, bd):
    ms = jnp.dot((y * y).astype(BF16), bd, preferred_element_type=F32)
    return y * lax.rsqrt(ms + EPS)


def _route(logits):
    lane = lax.broadcasted_iota(I32, logits.shape, 1)
    neg = jnp.float32(-1e30)
    big = jnp.int32(ROUTE_LANES)
    gmask = lane < N_GROUPS
    gl = jnp.where(gmask, logits, neg)
    gmax = jnp.max(gl, axis=-1, keepdims=True)
    grp = jnp.min(jnp.where(jnp.logical_and(gl == gmax, gmask), lane, big), axis=-1, keepdims=True)
    psum = jnp.sum(jnp.where(gmask, jnp.exp(gl - gmax), 0.0), axis=-1, keepdims=True)
    p_grp = 1.0 / psum
    lo = N_GROUPS + EXPERTS_PER_GROUP * grp
    emask = jnp.logical_and(lane >= lo, lane < lo + EXPERTS_PER_GROUP)
    el = jnp.where(emask, logits, neg)
    m1 = jnp.max(el, axis=-1, keepdims=True)
    i1 = jnp.min(jnp.where(jnp.logical_and(el == m1, emask), lane, big), axis=-1, keepdims=True)
    emask2 = jnp.logical_and(emask, lane != i1)
    el2 = jnp.where(emask2, logits, neg)
    m2 = jnp.max(el2, axis=-1, keepdims=True)
    i2 = jnp.min(jnp.where(jnp.logical_and(el2 == m2, emask2), lane, big), axis=-1, keepdims=True)
    d = jnp.exp(m2 - m1)
    g1 = p_grp * (1.0 / (1.0 + d))
    g2 = p_grp * (d / (1.0 + d))
    e1 = (i1 - N_GROUPS).astype(F32)
    e2 = (i2 - N_GROUPS).astype(F32)
    return jnp.where(lane == 0, e1, jnp.where(lane == 1, e2, jnp.where(lane == 2, g1,
                     jnp.where(lane == 3, g2, 0.0))))


def _lane_cumsum(v):
    lane = lax.broadcasted_iota(I32, v.shape, 1)
    sh = 1
    while sh < v.shape[1]:
        v = v + jnp.where(lane >= sh, pltpu.roll(v, sh, 1), 0.0)
        sh *= 2
    return v


def _outproj_kernel(x_ref, yh_ref, yg_ref, mg_ref, bd_ref, wo_ref, fg_ref, wrh_ref, wrl_ref, br_ref,
                    x1_ref, stage_ref, route_ref, count_ref, *, d_hy):
    bd = bd_ref[...]
    mg = mg_ref[...]
    yh = jnp.concatenate([yh_ref[0, c] for c in range(yh_ref.shape[1])], axis=1)
    mh = (_head_rms(yh, bd) * mg[:, :d_hy]).astype(BF16)
    mgm = (_head_rms(yg_ref[...], bd) * mg[:, d_hy:]).astype(BF16)
    wo = wo_ref[...]
    x1 = (x_ref[...] + jnp.dot(mh, wo[:d_hy], preferred_element_type=F32)
          + jnp.dot(mgm, wo[d_hy:], preferred_element_type=F32))
    x1_ref[...] = x1
    ms = jnp.mean(x1 * x1, axis=-1, keepdims=True)
    n2 = x1 * lax.rsqrt(ms + EPS) * fg_ref[...]
    hi, lo = _split_bf16(n2)
    wrh = wrh_ref[...]
    logits = (jnp.dot(hi, wrh, preferred_element_type=F32) + jnp.dot(lo, wrh, preferred_element_type=F32)
              + jnp.dot(hi, wrl_ref[...], preferred_element_type=F32) + br_ref[...])
    route = _route(logits)

    rows = route.shape[0]
    lane = lax.broadcasted_iota(I32, route.shape, 1)
    e0 = route[:, 0:1].astype(I32)
    e1 = route[:, 1:2].astype(I32)
    oh0 = lane == e0
    oh1 = lane == e1
    oh = jnp.where(jnp.logical_or(oh0, oh1), 1.0, 0.0)
    r = lax.broadcasted_iota(I32, (rows, rows), 0)
    c = lax.broadcasted_iota(I32, (rows, rows), 1)
    ltri = jnp.where(c < r, 1.0, 0.0).astype(BF16)
    before = jnp.dot(ltri, oh.astype(BF16), preferred_element_type=F32)
    cnt = jnp.sum(oh, axis=0, keepdims=True)
    run = jnp.floor((cnt + (CHUNK - 1)) * (1.0 / CHUNK)) * CHUNK
    run_start = _lane_cumsum(jnp.broadcast_to(run, (8, run.shape[1])))[0:1] - run
    tot = before + run_start
    loc0 = jnp.sum(jnp.where(oh0, tot, 0.0), axis=-1, keepdims=True)
    loc1 = jnp.sum(jnp.where(oh1, tot, 0.0), axis=-1, keepdims=True)
    route = jnp.where(lane == 4, loc0, jnp.where(lane == 5, loc1, route))
    route_ref[...] = route
    count_ref[pl.ds(pl.program_id(0), 1), :] = cnt

    rt = jnp.transpose(route)
    srow = lax.broadcasted_iota(I32, (STAGE_ROWS, rows), 0)
    p0 = srow == rt[4:5, :].astype(I32)
    p1 = srow == rt[5:6, :].astype(I32)
    perm = jnp.where(jnp.logical_or(p0, p1), 1.0, 0.0).astype(BF16)
    staged = jnp.dot(perm, hi, preferred_element_type=F32)
    gate = jnp.sum(jnp.where(p0, rt[2:3, :], 0.0) + jnp.where(p1, rt[3:4, :], 0.0), axis=-1,
                   keepdims=True)
    half = staged.shape[1] // 2
    mlane = lax.broadcasted_iota(I32, (STAGE_ROWS, LANES), 1)
    meta = jnp.where(mlane == 0, lax.bitcast_convert_type(jnp.broadcast_to(gate, (STAGE_ROWS, LANES)), U32),
                     jnp.uint32(0))
    stage_ref[...] = jnp.concatenate([_pack(staged[:, :half], staged[:, half:]), meta], axis=1)


def _outproj(x2, yh, yg, mix_g, bd, wo_bf, ffn_g, wr_hi, wr_lo, br):
    t, d = x2.shape
    _, nch, seq_len, _ = yh.shape
    d_hy = nch * LANES
    tiles_per_seq = seq_len // ROW_TILE
    n_tiles = t // ROW_TILE
    row = lambda w: pl.BlockSpec((ROW_TILE, w), lambda i: (i, 0))
    full = lambda a: pl.BlockSpec(a.shape, lambda i: (0,) * a.ndim)
    sw = d // 2 + LANES
    return pl.pallas_call(
        functools.partial(_outproj_kernel, d_hy=d_hy),
        grid=(n_tiles,),
        in_specs=[row(d),
                  pl.BlockSpec((1, nch, ROW_TILE, LANES),
                               lambda i: (i // tiles_per_seq, 0, i % tiles_per_seq, 0)),
                  row(yg.shape[1]), full(mix_g), full(bd), full(wo_bf), full(ffn_g),
                  full(wr_hi), full(wr_lo), full(br)],
        out_specs=[row(d), pl.BlockSpec((STAGE_ROWS, sw), lambda i: (i, 0)), row(ROUTE_LANES),
                   pl.BlockSpec((n_tiles, ROUTE_LANES), lambda i: (0, 0))],
        out_shape=[jax.ShapeDtypeStruct((t, d), F32),
                   jax.ShapeDtypeStruct((n_tiles * STAGE_ROWS, sw), U32),
                   jax.ShapeDtypeStruct((t, ROUTE_LANES), F32),
                   jax.ShapeDtypeStruct((n_tiles, ROUTE_LANES), F32)],
        compiler_params=_cparams(1, VMEM_LIMIT),
        name="outproj_router",
    )(x2, yh, yg, mix_g, bd, wo_bf, ffn_g, wr_hi, wr_lo, br)


def _moe_plan(tile_cnt, n_tokens):
    n_tiles = tile_cnt.shape[0]
    cnt = tile_cnt[:, :N_EXPERTS].astype(I32)
    run = (cnt + CHUNK - 1) // CHUNK * CHUNK
    run_start = jnp.cumsum(run, axis=1) - run
    used = jnp.sum(run, axis=1)
    e_rows = jnp.sum(run, axis=0)
    e_pad = (e_rows + EXPERT_ROWS - 1) // EXPERT_ROWS * EXPERT_ROWS
    e_end = jnp.cumsum(e_pad)
    e_start = e_end - e_pad
    n_blocks = -(-(n_tokens * TOP_K + n_tiles * N_EXPERTS * (CHUNK - 1)) // EXPERT_ROWS) + N_EXPERTS
    n_used = (e_end[-1:] // EXPERT_ROWS).astype(I32)
    block_row = jnp.arange(n_blocks, dtype=I32) * EXPERT_ROWS
    block_e = jnp.minimum(jnp.sum((e_end[None, :] <= block_row[:, None]).astype(I32), axis=1),
                          N_EXPERTS - 1)
    run_end_in_e = jnp.cumsum(run, axis=0)
    chunk_row = jnp.arange(n_blocks * EXPERT_ROWS // CHUNK, dtype=I32) * CHUNK
    ce = jnp.repeat(block_e, EXPERT_ROWS // CHUNK)
    q = chunk_row - e_start[ce]
    valid = q < e_rows[ce]
    ends = run_end_in_e.T[ce]
    tile = jnp.minimum(jnp.sum((ends <= q[:, None]).astype(I32), axis=1), n_tiles - 1)
    within = q - (run_end_in_e[tile, ce] - run[tile, ce])
    src_row = tile * STAGE_ROWS + run_start[tile, ce] + within
    zero_chunk = STAGE_ROWS // CHUNK - 1
    per_block = EXPERT_ROWS // CHUNK
    src_chunk = jnp.where(valid, src_row // CHUNK, zero_chunk).reshape(n_blocks, 1, per_block)
    dst_chunk = jnp.where(valid, src_row // CHUNK, 0).reshape(n_blocks, 1, per_block)
    n_valid = jnp.sum(valid.reshape(n_blocks, per_block).astype(I32), axis=1)
    return (block_e.astype(I32), n_used, n_valid.astype(I32), src_chunk.astype(I32),
            dst_chunk.astype(I32), used.astype(I32))


def _ffn_kernel(be_ref, nu_ref, nv_ref, used_ref, src_now_ref, src_next_ref, dst_ref, stage_ref,
                wg_ref, wu_ref, wd_ref, ost_ref, xbuf, obuf, zero_ref, in_sem, out_sem, zero_sem):
    del be_ref
    b = pl.program_id(0)
    nu = nu_ref[0]
    per_block = dst_ref.shape[-1]
    slot = b % 2

    def fetch(map_ref, s):
        def body(j, carry):
            row = pl.multiple_of(map_ref[0, 0, j] * CHUNK, CHUNK)
            pltpu.make_async_copy(stage_ref.at[pl.ds(row, CHUNK)], xbuf.at[s, pl.ds(j * CHUNK, CHUNK)],
                                  in_sem.at[s]).start()
            return carry
        lax.fori_loop(0, per_block, body, 0)

    def out_copy(j, chunk, s):
        row = pl.multiple_of(chunk * CHUNK, CHUNK)
        return pltpu.make_async_copy(obuf.at[s, pl.ds(j * CHUNK, CHUNK)], ost_ref.at[pl.ds(row, CHUNK)],
                                     out_sem.at[s])

    def drain(s, n):
        def body(j, carry):
            out_copy(0, 0, s).wait()
            return carry
        lax.fori_loop(0, n, body, 0)

    def zero_tails(wait):
        def tile(i, carry):
            def chunk(c, carry2):
                row = pl.multiple_of(i * STAGE_ROWS + c * CHUNK, CHUNK)
                cp = pltpu.make_async_copy(zero_ref, ost_ref.at[pl.ds(row, CHUNK)], zero_sem)
                if wait:
                    cp.wait()
                else:
                    cp.start()
                return carry2
            return lax.fori_loop(used_ref[i] // CHUNK, STAGE_ROWS // CHUNK, chunk, carry)
        lax.fori_loop(0, used_ref.shape[0], tile, 0)

    @pl.when(b == 0)
    def _prologue():
        fetch(src_now_ref, 0)
        zero_ref[...] = jnp.zeros_like(zero_ref)
        zero_tails(wait=False)
        zero_tails(wait=True)

    @pl.when(b + 1 < nu)
    def _prefetch_next():
        fetch(src_next_ref, 1 - slot)

    @pl.when(b < nu)
    def _compute():
        pltpu.make_async_copy(stage_ref.at[pl.ds(0, EXPERT_ROWS)], xbuf.at[slot], in_sem.at[slot]).wait()

        @pl.when(b >= 2)
        def _():
            drain(slot, nv_ref[b - 2])

        x = xbuf[slot]
        half = wg_ref.shape[1] // 2
        xb = jnp.concatenate(_unpack(x[:, :half]), axis=1)
        gate = lax.bitcast_convert_type(x[:, half:half + 1], F32)
        g = jnp.dot(xb, wg_ref[0], preferred_element_type=F32)
        u = jnp.dot(xb, wu_ref[0], preferred_element_type=F32)
        h = (g * (1.0 / (1.0 + jnp.exp(-g))) * u).astype(BF16)
        o = jnp.dot(h, wd_ref[0], preferred_element_type=F32) * gate
        obuf[slot] = _pack(o[:, :half], o[:, half:])

        def body(j, carry):
            out_copy(j, dst_ref[0, 0, j], slot).start()
            return carry
        lax.fori_loop(0, nv_ref[b], body, 0)

        @pl.when(b == nu - 1)
        def _():
            drain(slot, nv_ref[b])

            @pl.when(b >= 1)
            def _():
                drain(1 - slot, nv_ref[b - 1])


def _expert_ffn(stage, plan, wg, wu, wd):
    block_e, n_used, n_valid, src_chunk, dst_chunk, used = plan
    nb = block_e.shape[0]
    sw = stage.shape[1]
    d, de = wg.shape[1], wg.shape[2]
    per_block = EXPERT_ROWS // CHUNK
    cur = lambda i, be, nu: jnp.minimum(i, nu[0] - 1)
    wspec = lambda shape: pl.BlockSpec(shape, lambda i, be, nu, nv, us: (be[cur(i, be, nu)], 0, 0))
    smem = lambda imap: pl.BlockSpec((1, 1, per_block), imap, memory_space=pltpu.SMEM)
    return pl.pallas_call(
        _ffn_kernel,
        grid_spec=pltpu.PrefetchScalarGridSpec(
            num_scalar_prefetch=4,
            grid=(nb,),
            in_specs=[smem(lambda i, be, nu, nv, us: (i, 0, 0)),
                      smem(lambda i, be, nu, nv, us: (jnp.minimum(i + 1, nb - 1), 0, 0)),
                      smem(lambda i, be, nu, nv, us: (i, 0, 0)),
                      pl.BlockSpec(memory_space=pl.ANY),
                      wspec((1, d, de)), wspec((1, d, de)), wspec((1, de, d))],
            out_specs=pl.BlockSpec(memory_space=pl.ANY),
            scratch_shapes=[pltpu.VMEM((2, EXPERT_ROWS, sw), U32), pltpu.VMEM((2, EXPERT_ROWS, d // 2), U32),
                            pltpu.VMEM((CHUNK, d // 2), U32),
                            pltpu.SemaphoreType.DMA((2,)), pltpu.SemaphoreType.DMA((2,)),
                            pltpu.SemaphoreType.DMA(())]),
        out_shape=jax.ShapeDtypeStruct((stage.shape[0], d // 2), U32),
        compiler_params=_cparams(1, VMEM_LIMIT),
        name="expert_ffn",
    )(block_e, n_used, n_valid, used, src_chunk, src_chunk, dst_chunk, stage, wg, wu, wd)


def _combine_kernel(ost_ref, route_ref, x1_ref, fg_ref, o_ref):
    route = route_ref[...]
    rows = route.shape[0]
    col = lax.broadcasted_iota(I32, (rows, STAGE_ROWS), 1)
    sel = jnp.logical_or(col == route[:, 4:5].astype(I32), col == route[:, 5:6].astype(I32))
    perm = jnp.where(sel, 1.0, 0.0).astype(BF16)
    y = jnp.dot(perm, jnp.concatenate(_unpack(ost_ref[...]), axis=1), preferred_element_type=F32)
    x2 = x1_ref[...] + y
    ms = jnp.mean(x2 * x2, axis=-1, keepdims=True)
    o_ref[...] = x2 * lax.rsqrt(ms + EPS) * fg_ref[...]


def _combine(ostage, route, x1, final_g):
    t, d = x1.shape
    return pl.pallas_call(
        _combine_kernel,
        grid=(t // ROW_TILE,),
        in_specs=[pl.BlockSpec((STAGE_ROWS, d // 2), lambda i: (i, 0)),
                  pl.BlockSpec((ROW_TILE, ROUTE_LANES), lambda i: (i, 0)),
                  pl.BlockSpec((ROW_TILE, d), lambda i: (i, 0)),
                  pl.BlockSpec((1, d), lambda i: (0, 0))],
        out_specs=pl.BlockSpec((ROW_TILE, d), lambda i: (i, 0)),
        out_shape=jax.ShapeDtypeStruct((t, d), F32),
        compiler_params=_cparams(1, VMEM_LIMIT),
        name="moe_combine",
    )(ostage, route, x1, final_g)


def _hier_moe_and_norm(x1, stage, route, tile_cnt, wg, wu, wd, final_g):
    t, d = x1.shape
    ostage = _expert_ffn(stage, _moe_plan(tile_cnt, t), wg, wu, wd)
    return _combine(ostage, route, x1, final_g)


def _encoder(x, prm, kf, tables, nf):
    b, l, d = x.shape
    t = b * l
    d_hy = prm["skip"].shape[1]
    x2 = x.reshape(t, d)
    hy, gm = _inproj(x2, prm["mix_norm_g"], prm["w_in"], (HY_ORDER + 1) * d_hy)
    hy = _shortconv(hy, prm["short_w"], prm["short_b"], b, l)
    gates5 = hy.reshape(b, hy.shape[1], nf // 2, nf, LANES)
    z5 = gates5
    z_blk = HY_ORDER * d_hy // CONV_LANES
    for o in range(HY_ORDER):
        z5 = _longconv(gates5, z5, kf, tables, prm["skip"][o:o + 1], gate_blk=o * d_hy // CONV_LANES,
                       z_blk=z_blk, kf_col=o * d_hy // CONV_LANES, nf=nf)
        z_blk = 0
    y_hy = z5.reshape(b, d_hy // LANES, l, LANES)
    y_gm = _gmlp(gm, prm["ln_g"], prm["ln_b"], prm["ws_stack"], prm["bias_t"])
    x1, stage, route, tile_cnt = _outproj(x2, y_hy, y_gm, prm["mix_out_g"], prm["bd"], prm["w_out"], prm["ffn_norm_g"],
                             prm["wr_hi"], prm["wr_lo"], prm["br"])
    out = _hier_moe_and_norm(x1, stage, route, tile_cnt, prm["w_e_gate"], prm["w_e_up"], prm["w_e_down"],
                             prm["final_norm_g"])
    return out.reshape(b, l, d)


def kernel(x_prompt, x_sample, mix_norm_g, w_in, hy_short_w, hy_short_b, hy_filt_w_emb, hy_filt_b_emb,
           hy_filt_w_inner, hy_filt_b_inner, hy_filt_freq, hy_filt_w_out, hy_skip, gm_ln_g, gm_ln_b,
           gm_w_s, gm_b_s, mix_out_g, w_out, ffn_norm_g, w_group, b_group, w_expert_router,
           b_expert_router, w_e_gate, w_e_up, w_e_down, final_norm_g):
    assert w_in.shape[0] == 1, "one layer"
    l = x_prompt.shape[1]
    assert x_sample.shape[1] == l
    nf = math.isqrt(2 * l)
    assert nf * nf == 2 * l and nf % SLAB == 0
    d = x_prompt.shape[2]
    d_hy = hy_skip.shape[2]
    d_gm = gm_ln_g.shape[1]
    head_dim = d_gm // GM_HEADS
    assert d_hy // HY_HEADS == head_dim and d_hy == d_gm

    tables = _dft_tables(nf)
    f1, f2, f2i, f3, f1_full = tables
    tables_bf = tuple(a.astype(BF16) for a in (f1, f2, f2i, f3))

    max_decay = math.log(DECAY_TARGET) / FAST_DECAY_PCT
    min_decay = math.log(DECAY_TARGET) / SLOW_DECAY_PCT
    deltas = jnp.abs(jnp.linspace(min_decay, max_decay, d_hy, dtype=F32))[None, :]
    taps = _filter_taps(hy_filt_w_emb[0], hy_filt_b_emb[0][None, :], hy_filt_w_inner[0], hy_filt_b_inner[0],
                        hy_filt_freq[0][None, :], hy_filt_w_out[0], deltas, l, d_hy)
    kf = _filter_spectrum(taps, f1_full, f2, nf)

    n_route = N_GROUPS + N_EXPERTS
    wr = jnp.zeros((d, ROUTE_LANES), F32).at[:, :n_route].set(
        jnp.concatenate([w_group[0], w_expert_router[0]], axis=1))
    wr_hi = wr.astype(BF16)
    wr_lo = (wr - wr_hi.astype(F32)).astype(BF16)
    br = jnp.zeros((1, ROUTE_LANES), F32).at[0, :n_route].set(jnp.concatenate([b_group[0], b_expert_router[0]]))
    hid = jnp.arange(d_hy, dtype=I32) // head_dim
    bd = jnp.where(hid[:, None] == hid[None, :], 1.0 / head_dim, 0.0).astype(BF16)

    prm = dict(
        mix_norm_g=mix_norm_g, w_in=w_in[0].astype(BF16), short_w=hy_short_w[0], short_b=hy_short_b,
        skip=hy_skip[0], ln_g=gm_ln_g, ln_b=gm_ln_b,
        ws_stack=gm_w_s[0].reshape(GM_HEADS * GM_CHUNK, GM_CHUNK).astype(BF16),
        bias_t=jnp.repeat(gm_b_s[0].T, head_dim, axis=1),
        mix_out_g=mix_out_g, bd=bd, w_out=w_out[0].astype(BF16), ffn_norm_g=ffn_norm_g,
        wr_hi=wr_hi, wr_lo=wr_lo, br=br,
        w_e_gate=w_e_gate[0].astype(BF16), w_e_up=w_e_up[0].astype(BF16), w_e_down=w_e_down[0].astype(BF16),
        final_norm_g=final_norm_g[None, :])
    y_prompt = _encoder(x_prompt, prm, kf, tables_bf, nf)
    y_sample = _encoder(x_sample, prm, kf, tables_bf, nf)
    return (y_prompt, y_sample)
```

```python
import functools
import math

import jax
import jax.numpy as jnp
from jax import lax
from jax.experimental import pallas as pl
from jax.experimental.pallas import tpu as pltpu

F32 = jnp.float32
BF16 = jnp.bfloat16
U32 = jnp.uint32
I32 = jnp.int32

EPS = 1e-6
HY_ORDER = 2
HY_HEADS = 8
GM_HEADS = 8
GM_CHUNK = 128
N_GROUPS = 4
EXPERTS_PER_GROUP = 8
N_EXPERTS = N_GROUPS * EXPERTS_PER_GROUP
TOP_K = 2
FILTER_EMB = 5
DECAY_TARGET = 1e-2
FAST_DECAY_PCT = 0.3
SLOW_DECAY_PCT = 1.5

LANES = 128
ROUTE_LANES = LANES
SLAB = 16
CONV_LANES = 256
PHASE2_UNROLL = 8
ROW_TILE = 512
EXPERT_ROWS = 1024
CHUNK = 8
STAGE_ROWS = ROW_TILE * TOP_K + N_EXPERTS * CHUNK
VMEM_LIMIT = 56 * 1024 * 1024


def _cparams(n_axes, vmem=None):
    return pltpu.CompilerParams(dimension_semantics=("arbitrary",) * n_axes,
                                vmem_limit_bytes=vmem)


def _inproj_kernel(x_ref, g_ref, w_ref, hy_ref, gm_ref, *, d_hy3):
    x = x_ref[...]
    ms = jnp.mean(x * x, axis=-1, keepdims=True)
    n = (x * lax.rsqrt(ms + EPS) * g_ref[...]).astype(BF16)
    p = jnp.dot(n, w_ref[...], preferred_element_type=F32)
    hy_ref[...] = p[:, :d_hy3]
    gm_ref[...] = p[:, d_hy3:].astype(BF16)


def _inproj(x2, g, w_bf, d_hy3):
    t, d = x2.shape
    dp = w_bf.shape[1]
    return pl.pallas_call(
        functools.partial(_inproj_kernel, d_hy3=d_hy3),
        grid=(t // ROW_TILE,),
        in_specs=[pl.BlockSpec((ROW_TILE, d), lambda i: (i, 0)),
                  pl.BlockSpec((1, d), lambda i: (0, 0)),
                  pl.BlockSpec((d, dp), lambda i: (0, 0))],
        out_specs=[pl.BlockSpec((ROW_TILE, d_hy3), lambda i: (i, 0)),
                   pl.BlockSpec((ROW_TILE, dp - d_hy3), lambda i: (i, 0))],
        out_shape=[jax.ShapeDtypeStruct((t, d_hy3), F32),
                   jax.ShapeDtypeStruct((t, dp - d_hy3), BF16)],
        compiler_params=_cparams(1, VMEM_LIMIT),
        name="inproj",
    )(x2, g, w_bf)


def _shortconv_kernel(v_ref, hp_ref, hn_ref, w_ref, b_ref, o_ref, *, tiles_per_seq):
    i = pl.program_id(0)
    v = v_ref[...]
    rows = v.shape[0]
    row = lax.broadcasted_iota(I32, (rows, 1), 0)
    pos = i % tiles_per_seq
    first = jnp.where(pos == 0, 0.0, 1.0)
    last = jnp.where(pos == tiles_per_seq - 1, 0.0, 1.0)
    prev = jnp.where(row == 0, hp_ref[7:8, :] * first, pltpu.roll(v, 1, 0))
    nxt = jnp.where(row == rows - 1, hn_ref[0:1, :] * last, pltpu.roll(v, rows - 1, 0))
    res = w_ref[0:1, :] * prev + w_ref[1:2, :] * v + w_ref[2:3, :] * nxt + b_ref[...]
    for c in range(o_ref.shape[1]):
        o_ref[0, c] = res[:, c * LANES:(c + 1) * LANES]


def _shortconv(hy, w, b, n_seq, seq_len):
    t, c = hy.shape
    tiles_per_seq = seq_len // ROW_TILE
    hb = ROW_TILE // 8
    nhb = t // 8
    return pl.pallas_call(
        functools.partial(_shortconv_kernel, tiles_per_seq=tiles_per_seq),
        grid=(t // ROW_TILE,),
        in_specs=[pl.BlockSpec((ROW_TILE, c), lambda i: (i, 0)),
                  pl.BlockSpec((8, c), lambda i: (jnp.maximum(i * hb - 1, 0), 0)),
                  pl.BlockSpec((8, c), lambda i: (jnp.minimum((i + 1) * hb, nhb - 1), 0)),
                  pl.BlockSpec((3, c), lambda i: (0, 0)),
                  pl.BlockSpec((1, c), lambda i: (0, 0))],
        out_specs=pl.BlockSpec((1, c // LANES, ROW_TILE, LANES),
                               lambda i: (i // tiles_per_seq, 0, i % tiles_per_seq, 0)),
        out_shape=jax.ShapeDtypeStruct((n_seq, c // LANES, seq_len, LANES), F32),
        compiler_params=_cparams(1, VMEM_LIMIT),
        name="shortconv",
    )(hy, hy, hy, w, b)


def _cos_sin(m, period):
    ang = m.astype(F32) * (2.0 * math.pi / period)
    return jnp.cos(ang), jnp.sin(ang)


def _stack_complex(mr, mi):
    top = jnp.concatenate([mr, -mi], axis=-1)
    bot = jnp.concatenate([mi, mr], axis=-1)
    return jnp.concatenate([top, bot], axis=-2)


def _dft_tables(nf):
    na = nf // 2
    n = nf * nf
    idx = jnp.arange(nf, dtype=I32)
    c, s = _cos_sin((idx[:, None] * idx[None, :]) % nf, nf)
    f2 = _stack_complex(c, -s)
    f2i = _stack_complex(c, s)
    b_ = idx[:, None, None]
    ka = idx[None, :, None]
    a_ = idx[None, None, :]
    m1 = (nf * a_ * ka + b_ * ka) % n
    c1, s1 = _cos_sin(m1, n)
    f1 = _stack_complex(c1[:, :, :na], -s1[:, :, :na])
    f1_full = jnp.concatenate([c1, -s1], axis=1)
    c3 = jnp.swapaxes(c1, 1, 2)[:, :na, :]
    s3 = jnp.swapaxes(s1, 1, 2)[:, :na, :]
    f3 = _stack_complex(c3, s3)
    return f1, f2, f2i, f3, f1_full


def _split_bf16(x):
    hi = x.astype(BF16)
    lo = (x - hi.astype(F32)).astype(BF16)
    return hi, lo


def _stack_split_lhs(a):
    hi, lo = _split_bf16(a)
    return jnp.concatenate([hi, lo, hi], axis=-1)


def _stack_split_rhs(b):
    hi, lo = _split_bf16(b)
    return jnp.concatenate([hi, hi, lo], axis=-2)


def _filter_taps_kernel(wemb_ref, bemb_ref, win_ref, bin_ref, freq_ref, wout_ref, delta_ref, o_ref,
                        *, seq_len, n_inner, d_hy):
    rows = o_ref.shape[0]
    j = pl.program_id(0) * rows + lax.broadcasted_iota(I32, (rows, 1), 0)
    tidx = jnp.where(j < seq_len, j, 2 * seq_len - j).astype(F32)
    t = tidx / float(seq_len - 1)
    fr0 = jnp.float32(1e-4)
    ang0 = (2.0 * math.pi / seq_len) * tidx * fr0
    ang1 = (2.0 * math.pi / seq_len) * tidx
    freq = freq_ref[...]
    pre = (t * wemb_ref[0:1, :] + jnp.cos(ang0) * wemb_ref[1:2, :] + jnp.cos(ang1) * wemb_ref[2:3, :]
           - jnp.sin(ang0) * wemb_ref[3:4, :] - jnp.sin(ang1) * wemb_ref[4:5, :] + bemb_ref[...])
    hdn = jnp.sin(freq * pre)
    for i in range(n_inner):
        hdn = jnp.sin(freq * (jnp.dot(_stack_split_lhs(hdn), win_ref[i], preferred_element_type=F32)
                              + bin_ref[i:i + 1, :]))
    h = jnp.dot(_stack_split_lhs(hdn), wout_ref[...], preferred_element_type=F32)
    decay = jnp.exp(-t * delta_ref[...])
    for o in range(HY_ORDER):
        fwd = h[:, (2 * o) * d_hy:(2 * o + 1) * d_hy]
        bwd = h[:, (2 * o + 1) * d_hy:(2 * o + 2) * d_hy]
        val = jnp.where(j < seq_len, fwd, bwd) * decay
        o_ref[:, o * d_hy:(o + 1) * d_hy] = jnp.where(j == seq_len, 0.0, val)


def _filter_taps(wemb, bemb, win, bin_, freq, wout, deltas, seq_len, d_hy):
    rows = 1024
    n2 = 2 * seq_len
    width = wemb.shape[1]
    n_inner = win.shape[0]
    full = lambda *shape: pl.BlockSpec(shape, lambda i: (0,) * len(shape))
    return pl.pallas_call(
        functools.partial(_filter_taps_kernel, seq_len=seq_len, n_inner=n_inner, d_hy=d_hy),
        grid=(n2 // rows,),
        in_specs=[full(FILTER_EMB, width), full(1, width), full(n_inner, 3 * width, width),
                  full(n_inner, width), full(1, width), full(3 * width, 2 * HY_ORDER * d_hy),
                  full(1, d_hy)],
        out_specs=pl.BlockSpec((rows, HY_ORDER * d_hy), lambda i: (i, 0)),
        out_shape=jax.ShapeDtypeStruct((n2, HY_ORDER * d_hy), F32),
        compiler_params=_cparams(1, VMEM_LIMIT),
        name="filter_taps",
    )(wemb, bemb, _stack_split_rhs(win), bin_, freq, _stack_split_rhs(wout), deltas)


def _filter_fft1_kernel(k_ref, f_ref, o_ref):
    for bl in range(SLAB):
        o_ref[bl] = jnp.dot(f_ref[bl], _stack_split_rhs(k_ref[:, bl, :]), preferred_element_type=F32)


def _filter_fft2_kernel(g_ref, f_ref, o_ref, *, scale):
    for kl in range(SLAB):
        rhs = _stack_split_rhs(jnp.concatenate([g_ref[:, 0, kl, :], g_ref[:, 1, kl, :]], axis=0))
        o_ref[kl] = jnp.dot(f_ref[...], rhs, preferred_element_type=F32) * scale


def _filter_spectrum(taps, f1_full, f2, nf):
    c = taps.shape[1]
    k3 = taps.reshape(nf, nf, c)
    g = pl.pallas_call(
        _filter_fft1_kernel,
        grid=(nf // SLAB, c // CONV_LANES),
        in_specs=[pl.BlockSpec((nf, SLAB, CONV_LANES), lambda i, j: (0, i, j)),
                  pl.BlockSpec((SLAB, 2 * nf, 3 * nf), lambda i, j: (i, 0, 0))],
        out_specs=pl.BlockSpec((SLAB, 2 * nf, CONV_LANES), lambda i, j: (i, 0, j)),
        out_shape=jax.ShapeDtypeStruct((nf, 2 * nf, c), F32),
        compiler_params=_cparams(2, VMEM_LIMIT),
        name="filter_fft1",
    )(k3, _stack_split_lhs(f1_full))
    g4 = g.reshape(nf, 2, nf, c)
    return pl.pallas_call(
        functools.partial(_filter_fft2_kernel, scale=1.0 / (nf * nf)),
        grid=(nf // SLAB, c // CONV_LANES),
        in_specs=[pl.BlockSpec((nf, 2, SLAB, CONV_LANES), lambda i, j: (0, 0, i, j)),
                  pl.BlockSpec((2 * nf, 6 * nf), lambda i, j: (0, 0))],
        out_specs=pl.BlockSpec((SLAB, 2 * nf, CONV_LANES), lambda i, j: (i, 0, j)),
        out_shape=jax.ShapeDtypeStruct((nf, 2 * nf, c), F32),
        compiler_params=_cparams(2, VMEM_LIMIT),
        name="filter_fft2",
    )(g4, _stack_split_lhs(f2))


def _pack(re, im):
    rb = lax.bitcast_convert_type(re.astype(BF16).astype(F32), U32)
    ib = lax.bitcast_convert_type(im.astype(BF16).astype(F32), U32)
    return rb | (ib >> 16)


def _unpack(w):
    re = lax.bitcast_convert_type(w & jnp.uint32(0xFFFF0000), F32).astype(BF16)
    im = lax.bitcast_convert_type(w << 16, F32).astype(BF16)
    return re, im


def _longconv_kernel(z1_ref, f1_ref, kf_ref, f2_ref, f2i_ref, f3_ref, gate_ref, zs_ref, skip_ref,
                     o_ref, g_ref, *, nf):
    na = nf // 2
    ns = nf // SLAB
    halves = g_ref.shape[1]
    t = pl.program_id(2)
    flat = lambda ref: ref.reshape(2, halves, na * SLAB, LANES)

    def load_rows(ref4, s, bl):
        return jnp.concatenate([ref4[s, h, pl.ds(bl, na, stride=SLAB), :] for h in range(halves)], axis=1)

    def store_col(blk, bl, val):
        for h in range(halves):
            g_ref[blk, h, pl.ds(bl, nf, stride=SLAB), :] = val[:, h * LANES:(h + 1) * LANES]

    def load_col(blk, bl):
        return jnp.concatenate([g_ref[blk, h, pl.ds(bl, nf, stride=SLAB), :] for h in range(halves)],
                               axis=1)

    @pl.when(t < ns)
    def _phase1():
        z1 = flat(z1_ref)
        for bl in range(SLAB):
            rhs = jnp.concatenate([load_rows(z1, 0, bl), load_rows(z1, 1, bl)], axis=0).astype(BF16)
            out = jnp.dot(f1_ref[bl], rhs, preferred_element_type=F32)
            store_col(t, bl, _pack(out[:nf], out[nf:]))

    @pl.when(jnp.logical_and(t >= ns, t < 2 * ns))
    def _phase2():
        i = t - ns

        def load_slab(ka):
            row0 = pl.multiple_of(ka * SLAB, SLAB)
            return jnp.concatenate([g_ref[:, h, pl.ds(row0, SLAB), :].reshape(nf, LANES)
                                    for h in range(halves)], axis=1)

        def transform(w, kl):
            re, im = _unpack(w)
            s = jnp.dot(f2_ref[...], jnp.concatenate([re, im], axis=0),
                        preferred_element_type=F32)
            sr, si = s[:nf], s[nf:]
            kr, ki = kf_ref[kl, :nf, :], kf_ref[kl, nf:, :]
            pr = (sr * kr - si * ki).astype(BF16)
            pi = (sr * ki + si * kr).astype(BF16)
            h = jnp.dot(f2i_ref[...], jnp.concatenate([pr, pi], axis=0),
                        preferred_element_type=F32)
            return _pack(h[:nf], h[nf:])

        def store_slab(ka, packed):
            row0 = pl.multiple_of(ka * SLAB, SLAB)
            for hf in range(halves):
                g_ref[:, hf, pl.ds(row0, SLAB), :] = packed[:, hf * LANES:(hf + 1) * LANES].reshape(
                    ns, SLAB, LANES)

        def body(grp, carry):
            kls = [grp * PHASE2_UNROLL + u for u in range(PHASE2_UNROLL)]
            ws = [load_slab(i * SLAB + kl) for kl in kls]
            outs = [transform(w, kl) for w, kl in zip(ws, kls)]
            for kl, packed in zip(kls, outs):
                store_slab(i * SLAB + kl, packed)
            return carry

        lax.fori_loop(0, SLAB // PHASE2_UNROLL, body, 0)

    @pl.when(t >= 2 * ns)
    def _phase3():
        j = t - 2 * ns
        skip = skip_ref[...]
        gate, zs, out = flat(gate_ref), flat(zs_ref), flat(o_ref)
        for bl in range(SLAB):
            re, im = _unpack(load_col(j, bl))
            y = jnp.dot(f3_ref[bl], jnp.concatenate([re, im], axis=0),
                        preferred_element_type=F32)
            for s in range(2):
                val = load_rows(gate, s, bl) * (y[s * na:(s + 1) * na] + load_rows(zs, s, bl) * skip)
                for h in range(halves):
                    out[s, h, pl.ds(bl, na, stride=SLAB), :] = val[:, h * LANES:(h + 1) * LANES]


def _longconv(gates5, z5, kf, tables, skip, gate_blk, z_blk, kf_col, nf):
    f1, f2, f2i, f3 = tables
    b, _, na, _, _ = z5.shape
    c = skip.shape[1]
    ns = nf // SLAB
    halves = CONV_LANES // LANES
    nchunk = c // CONV_LANES
    p1 = lambda t: jnp.minimum(t, ns - 1)
    p2 = lambda t: jnp.clip(t - ns, 0, ns - 1)
    p3 = lambda t: jnp.clip(t - 2 * ns, 0, ns - 1)
    cl = CONV_LANES
    blk5 = (2, halves, na, SLAB, LANES)
    return pl.pallas_call(
        functools.partial(_longconv_kernel, nf=nf),
        grid=(b // 2, nchunk, 3 * ns),
        in_specs=[
            pl.BlockSpec(blk5, lambda p, q, t: (p, z_blk + q, 0, p1(t), 0)),
            pl.BlockSpec((SLAB, 2 * nf, 2 * na), lambda p, q, t: (p1(t), 0, 0)),
            pl.BlockSpec((SLAB, 2 * nf, cl), lambda p, q, t: (p2(t), 0, kf_col + q)),
            pl.BlockSpec((2 * nf, 2 * nf), lambda p, q, t: (0, 0)),
            pl.BlockSpec((2 * nf, 2 * nf), lambda p, q, t: (0, 0)),
            pl.BlockSpec((SLAB, 2 * na, 2 * nf), lambda p, q, t: (p3(t), 0, 0)),
            pl.BlockSpec(blk5, lambda p, q, t: (p, gate_blk + q, 0, p3(t), 0)),
            pl.BlockSpec(blk5, lambda p, q, t: (p, z_blk + q, 0, p3(t), 0)),
            pl.BlockSpec((1, cl), lambda p, q, t: (0, q)),
        ],
        out_specs=pl.BlockSpec(blk5, lambda p, q, t: (p, q, 0, p3(t), 0)),
        out_shape=jax.ShapeDtypeStruct((b, c // LANES, na, nf, LANES), F32),
        scratch_shapes=[pltpu.VMEM((ns, halves, nf * SLAB, LANES), U32)],
        compiler_params=_cparams(3, VMEM_LIMIT),
        name="longconv",
    )(z5, f1, kf, f2, f2i, f3, gates5, z5, skip)


def _gelu_exact(x):
    return 0.5 * x * (1.0 + lax.erf(x * (1.0 / math.sqrt(2.0))))


def _gmlp_kernel(gm_ref, lng_ref, lnb_ref, ws_ref, bias_ref, o_ref, *, d_gm):
    g = _gelu_exact(gm_ref[...].astype(F32))
    u, v = g[:, :d_gm], g[:, d_gm:]
    mu = jnp.mean(v, axis=-1, keepdims=True)
    vc = v - mu
    var = jnp.mean(vc * vc, axis=-1, keepdims=True)
    vh = (vc * lax.rsqrt(var + EPS) * lng_ref[...] + lnb_ref[...]).astype(BF16)
    head = lax.broadcasted_iota(I32, (GM_CHUNK, d_gm), 1) // (d_gm // GM_HEADS)
    for c in range(gm_ref.shape[0] // GM_CHUNK):
        rows = slice(c * GM_CHUNK, (c + 1) * GM_CHUNK)
        r = jnp.dot(ws_ref[...], vh[rows], preferred_element_type=F32)
        s = r[:GM_CHUNK]
        for h in range(1, GM_HEADS):
            s = jnp.where(head == h, r[h * GM_CHUNK:(h + 1) * GM_CHUNK], s)
        o_ref[rows, :] = u[rows] * (s + bias_ref[...])


def _gmlp(gm, ln_g, ln_b, ws_stack, bias_t):
    t, c2 = gm.shape
    d_gm = c2 // 2
    return pl.pallas_call(
        functools.partial(_gmlp_kernel, d_gm=d_gm),
        grid=(t // ROW_TILE,),
        in_specs=[pl.BlockSpec((ROW_TILE, c2), lambda i: (i, 0)),
                  pl.BlockSpec((1, d_gm), lambda i: (0, 0)),
                  pl.BlockSpec((1, d_gm), lambda i: (0, 0)),
                  pl.BlockSpec(ws_stack.shape, lambda i: (0, 0)),
                  pl.BlockSpec(bias_t.shape, lambda i: (0, 0))],
        out_specs=pl.BlockSpec((ROW_TILE, d_gm), lambda i: (i, 0)),
        out_shape=jax.ShapeDtypeStruct((t, d_gm), F32),
        compiler_params=_cparams(1, VMEM_LIMIT),
        name="gmlp",
    )(gm, ln_g, ln_b, ws_stack, bias_t)


def _head_rms(y, bd):
    ms = jnp.dot((y * y).astype(BF16), bd, preferred_element_type=F32)
    return y * lax.rsqrt(ms + EPS)


def _route(logits):
    lane = lax.broadcasted_iota(I32, logits.shape, 1)
    neg = jnp.float32(-1e30)
    big = jnp.int32(ROUTE_LANES)
    gmask = lane < N_GROUPS
    gl = jnp.where(gmask, logits, neg)
    gmax = jnp.max(gl, axis=-1, keepdims=True)
    grp = jnp.min(jnp.where(jnp.logical_and(gl == gmax, gmask), lane, big), axis=-1, keepdims=True)
    psum = jnp.sum(jnp.where(gmask, jnp.exp(gl - gmax), 0.0), axis=-1, keepdims=True)
    p_grp = 1.0 / psum
    lo = N_GROUPS + EXPERTS_PER_GROUP * grp
    emask = jnp.logical_and(lane >= lo, lane < lo + EXPERTS_PER_GROUP)
    el = jnp.where(emask, logits, neg)
    m1 = jnp.max(el, axis=-1, keepdims=True)
    i1 = jnp.min(jnp.where(jnp.logical_and(el == m1, emask), lane, big), axis=-1, keepdims=True)
    emask2 = jnp.logical_and(emask, lane != i1)
    el2 = jnp.where(emask2, logits, neg)
    m2 = jnp.max(el2, axis=-1, keepdims=True)
    i2 = jnp.min(jnp.where(jnp.logical_and(el2 == m2, emask2), lane, big), axis=-1, keepdims=True)
    d = jnp.exp(m2 - m1)
    g1 = p_grp * (1.0 / (1.0 + d))
    g2 = p_grp * (d / (1.0 + d))
    e1 = (i1 - N_GROUPS).astype(F32)
    e2 = (i2 - N_GROUPS).astype(F32)
    return jnp.where(lane == 0, e1, jnp.where(lane == 1, e2, jnp.where(lane == 2, g1,
                     jnp.where(lane == 3, g2, 0.0))))


def _lane_cumsum(v):
    lane = lax.broadcasted_iota(I32, v.shape, 1)
    sh = 1
    while sh < v.shape[1]:
        v = v + jnp.where(lane >= sh, pltpu.roll(v, sh, 1), 0.0)
        sh *= 2
    return v


def _outproj_kernel(x_ref, yh_ref, yg_ref, mg_ref, bd_ref, wo_ref, fg_ref, wr_ref, br_ref,
                    x1_ref, stage_ref, route_ref, count_ref, *, d_hy):
    bd = bd_ref[...]
    mg = mg_ref[...]
    yh = jnp.concatenate([yh_ref[0, c] for c in range(yh_ref.shape[1])], axis=1)
    mh = (_head_rms(yh, bd) * mg[:, :d_hy]).astype(BF16)
    mgm = (_head_rms(yg_ref[...], bd) * mg[:, d_hy:]).astype(BF16)
    wo = wo_ref[...]
    x1 = (x_ref[...] + jnp.dot(mh, wo[:d_hy], preferred_element_type=F32)
          + jnp.dot(mgm, wo[d_hy:], preferred_element_type=F32))
    x1_ref[...] = x1
    ms = jnp.mean(x1 * x1, axis=-1, keepdims=True)
    n2 = x1 * lax.rsqrt(ms + EPS) * fg_ref[...]
    hi, lo = _split_bf16(n2)
    wr = wr_ref[...]
    hw = jnp.dot(hi, wr, preferred_element_type=F32)
    logits = (hw[:, :ROUTE_LANES] + hw[:, ROUTE_LANES:]
              + jnp.dot(lo, wr[:, :ROUTE_LANES], preferred_element_type=F32) + br_ref[...])
    route = _route(logits)

    rows = route.shape[0]
    lane = lax.broadcasted_iota(I32, route.shape, 1)
    e0 = route[:, 0:1].astype(I32)
    e1 = route[:, 1:2].astype(I32)
    oh0 = lane == e0
    oh1 = lane == e1
    oh = jnp.where(jnp.logical_or(oh0, oh1), 1.0, 0.0)
    r = lax.broadcasted_iota(I32, (rows, rows), 0)
    c = lax.broadcasted_iota(I32, (rows, rows), 1)
    ltri = jnp.where(c < r, 1.0, 0.0).astype(BF16)
    before = jnp.dot(ltri, oh.astype(BF16), preferred_element_type=F32)
    cnt = jnp.sum(oh, axis=0, keepdims=True)
    run = jnp.floor((cnt + (CHUNK - 1)) * (1.0 / CHUNK)) * CHUNK
    run_start = _lane_cumsum(jnp.broadcast_to(run, (8, run.shape[1])))[0:1] - run
    tot = before + run_start
    loc0 = jnp.sum(jnp.where(oh0, tot, 0.0), axis=-1, keepdims=True)
    loc1 = jnp.sum(jnp.where(oh1, tot, 0.0), axis=-1, keepdims=True)
    route = jnp.where(lane == 4, loc0, jnp.where(lane == 5, loc1, route))
    route_ref[...] = route
    count_ref[pl.ds(pl.program_id(0), 1), :] = cnt

    rt = jnp.transpose(route)
    srow = lax.broadcasted_iota(I32, (STAGE_ROWS, rows), 0)
    p0 = srow == rt[4:5, :].astype(I32)
    p1 = srow == rt[5:6, :].astype(I32)
    perm = jnp.where(jnp.logical_or(p0, p1), 1.0, 0.0).astype(BF16)
    staged = jnp.dot(perm, hi, preferred_element_type=F32)
    gate = jnp.sum(jnp.where(p0, rt[2:3, :], 0.0) + jnp.where(p1, rt[3:4, :], 0.0), axis=-1,
                   keepdims=True)
    half = staged.shape[1] // 2
    mlane = lax.broadcasted_iota(I32, (STAGE_ROWS, LANES), 1)
    meta = jnp.where(mlane == 0, lax.bitcast_convert_type(jnp.broadcast_to(gate, (STAGE_ROWS, LANES)), U32),
                     jnp.uint32(0))
    stage_ref[...] = jnp.concatenate([_pack(staged[:, :half], staged[:, half:]), meta], axis=1)


def _outproj(x2, yh, yg, mix_g, bd, wo_bf, ffn_g, wr_cat, br):
    t, d = x2.shape
    _, nch, seq_len, _ = yh.shape
    d_hy = nch * LANES
    tiles_per_seq = seq_len // ROW_TILE
    n_tiles = t // ROW_TILE
    row = lambda w: pl.BlockSpec((ROW_TILE, w), lambda i: (i, 0))
    full = lambda a: pl.BlockSpec(a.shape, lambda i: (0,) * a.ndim)
    sw = d // 2 + LANES
    return pl.pallas_call(
        functools.partial(_outproj_kernel, d_hy=d_hy),
        grid=(n_tiles,),
        in_specs=[row(d),
                  pl.BlockSpec((1, nch, ROW_TILE, LANES),
                               lambda i: (i // tiles_per_seq, 0, i % tiles_per_seq, 0)),
                  row(yg.shape[1]), full(mix_g), full(bd), full(wo_bf), full(ffn_g),
                  full(wr_cat), full(br)],
        out_specs=[row(d), pl.BlockSpec((STAGE_ROWS, sw), lambda i: (i, 0)), row(ROUTE_LANES),
                   pl.BlockSpec((n_tiles, ROUTE_LANES), lambda i: (0, 0))],
        out_shape=[jax.ShapeDtypeStruct((t, d), F32),
                   jax.ShapeDtypeStruct((n_tiles * STAGE_ROWS, sw), U32),
                   jax.ShapeDtypeStruct((t, ROUTE_LANES), F32),
                   jax.ShapeDtypeStruct((n_tiles, ROUTE_LANES), F32)],
        compiler_params=_cparams(1, VMEM_LIMIT),
        name="outproj_router",
    )(x2, yh, yg, mix_g, bd, wo_bf, ffn_g, wr_cat, br)


def _moe_plan(tile_cnt, n_tokens):
    n_tiles = tile_cnt.shape[0]
    cnt = tile_cnt[:, :N_EXPERTS].astype(I32)
    run = (cnt + CHUNK - 1) // CHUNK * CHUNK
    run_start = jnp.cumsum(run, axis=1) - run
    used = jnp.sum(run, axis=1)
    e_rows = jnp.sum(run, axis=0)
    e_pad = (e_rows + EXPERT_ROWS - 1) // EXPERT_ROWS * EXPERT_ROWS
    e_end = jnp.cumsum(e_pad)
    e_start = e_end - e_pad
    n_blocks = -(-(n_tokens * TOP_K + n_tiles * N_EXPERTS * (CHUNK - 1)) // EXPERT_ROWS) + N_EXPERTS
    n_used = (e_end[-1:] // EXPERT_ROWS).astype(I32)
    block_row = jnp.arange(n_blocks, dtype=I32) * EXPERT_ROWS
    block_e = jnp.minimum(jnp.sum((e_end[None, :] <= block_row[:, None]).astype(I32), axis=1),
                          N_EXPERTS - 1)
    run_end_in_e = jnp.cumsum(run, axis=0)
    chunk_row = jnp.arange(n_blocks * EXPERT_ROWS // CHUNK, dtype=I32) * CHUNK
    e_of_chunk = jnp.repeat(block_e, EXPERT_ROWS // CHUNK)
    e_sel = jnp.arange(N_EXPERTS, dtype=I32)[None, :] == e_of_chunk[:, None]
    pick = lambda tbl: jnp.dot(e_sel.astype(F32), tbl.astype(F32),
                               precision=lax.Precision.HIGHEST).astype(I32)
    q = chunk_row - pick(e_start[:, None])[:, 0]
    valid = q < pick(e_rows[:, None])[:, 0]
    ends = pick(run_end_in_e.T)
    tile = jnp.minimum(jnp.sum((ends <= q[:, None]).astype(I32), axis=1), n_tiles - 1)
    t_sel = jnp.arange(n_tiles, dtype=I32)[None, :] == tile[:, None]
    at_tile = lambda tbl: jnp.sum(jnp.where(t_sel, pick(tbl.T), 0), axis=1)
    within = q - (at_tile(run_end_in_e) - at_tile(run))
    src_row = tile * STAGE_ROWS + at_tile(run_start) + within
    zero_chunk = STAGE_ROWS // CHUNK - 1
    per_block = EXPERT_ROWS // CHUNK
    src_chunk = jnp.where(valid, src_row // CHUNK, zero_chunk).reshape(n_blocks, 1, per_block)
    dst_chunk = jnp.where(valid, src_row // CHUNK, 0).reshape(n_blocks, 1, per_block)
    n_valid = jnp.sum(valid.reshape(n_blocks, per_block).astype(I32), axis=1)
    return (block_e.astype(I32), n_used, n_valid.astype(I32), src_chunk.astype(I32),
            dst_chunk.astype(I32), used.astype(I32))


def _ffn_kernel(be_ref, nu_ref, nv_ref, used_ref, src_now_ref, src_next_ref, dst_ref, stage_ref,
                wg_ref, wu_ref, wd_ref, ost_ref, xbuf, obuf, zero_ref, in_sem, out_sem, zero_sem):
    del be_ref
    b = pl.program_id(0)
    nu = nu_ref[0]
    per_block = dst_ref.shape[-1]
    slot = b % 2

    def fetch(map_ref, s):
        def body(j, carry):
            row = pl.multiple_of(map_ref[0, 0, j] * CHUNK, CHUNK)
            pltpu.make_async_copy(stage_ref.at[pl.ds(row, CHUNK)], xbuf.at[s, pl.ds(j * CHUNK, CHUNK)],
                                  in_sem.at[s]).start()
            return carry
        lax.fori_loop(0, per_block, body, 0, unroll=8)

    def out_copy(j, chunk, s):
        row = pl.multiple_of(chunk * CHUNK, CHUNK)
        return pltpu.make_async_copy(obuf.at[s, pl.ds(j * CHUNK, CHUNK)], ost_ref.at[pl.ds(row, CHUNK)],
                                     out_sem.at[s])

    def drain(s, n):
        @pl.when(n == per_block)
        def _():
            pltpu.make_async_copy(obuf.at[s], ost_ref.at[pl.ds(0, EXPERT_ROWS)], out_sem.at[s]).wait()

        @pl.when(n != per_block)
        def _():
            def body(j, carry):
                out_copy(0, 0, s).wait()
                return carry
            lax.fori_loop(0, n, body, 0)

    def zero_tails(wait):
        def tile(i, carry):
            def chunk(c, carry2):
                row = pl.multiple_of(i * STAGE_ROWS + c * CHUNK, CHUNK)
                cp = pltpu.make_async_copy(zero_ref, ost_ref.at[pl.ds(row, CHUNK)], zero_sem)
                if wait:
                    cp.wait()
                else:
                    cp.start()
                return carry2
            return lax.fori_loop(used_ref[i] // CHUNK, STAGE_ROWS // CHUNK, chunk, carry)
        lax.fori_loop(0, used_ref.shape[0], tile, 0)

    @pl.when(b == 0)
    def _prologue():
        fetch(src_now_ref, 0)
        zero_ref[...] = jnp.zeros_like(zero_ref)
        zero_tails(wait=False)
        zero_tails(wait=True)

    @pl.when(b + 1 < nu)
    def _prefetch_next():
        fetch(src_next_ref, 1 - slot)

    @pl.when(b < nu)
    def _compute():
        pltpu.make_async_copy(stage_ref.at[pl.ds(0, EXPERT_ROWS)], xbuf.at[slot], in_sem.at[slot]).wait()

        @pl.when(b >= 2)
        def _():
            drain(slot, nv_ref[b - 2])

        x = xbuf[slot]
        half = wg_ref.shape[1] // 2
        xb = jnp.concatenate(_unpack(x[:, :half]), axis=1)
        gate = lax.bitcast_convert_type(x[:, half:half + 1], F32)
        g = jnp.dot(xb, wg_ref[0], preferred_element_type=F32)
        u = jnp.dot(xb, wu_ref[0], preferred_element_type=F32)
        h = (g * (1.0 / (1.0 + jnp.exp(-g))) * u).astype(BF16)
        o = jnp.dot(h, wd_ref[0], preferred_element_type=F32) * gate
        obuf[slot] = _pack(o[:, :half], o[:, half:])

        def body(j, carry):
            out_copy(j, dst_ref[0, 0, j], slot).start()
            return carry

        @pl.when(nv_ref[b] == per_block)
        def _():
            lax.fori_loop(0, per_block, body, 0, unroll=8)

        @pl.when(nv_ref[b] != per_block)
        def _():
            lax.fori_loop(0, nv_ref[b], body, 0)

        @pl.when(b == nu - 1)
        def _():
            drain(slot, nv_ref[b])

            @pl.when(b >= 1)
            def _():
                drain(1 - slot, nv_ref[b - 1])


def _expert_ffn(stage, plan, wg, wu, wd):
    block_e, n_used, n_valid, src_chunk, dst_chunk, used = plan
    nb = block_e.shape[0]
    sw = stage.shape[1]
    d, de = wg.shape[1], wg.shape[2]
    per_block = EXPERT_ROWS // CHUNK
    cur = lambda i, be, nu: jnp.minimum(i, nu[0] - 1)
    wspec = lambda shape: pl.BlockSpec(shape, lambda i, be, nu, nv, us: (be[cur(i, be, nu)], 0, 0))
    smem = lambda imap: pl.BlockSpec((1, 1, per_block), imap, memory_space=pltpu.SMEM)
    return pl.pallas_call(
        _ffn_kernel,
        grid_spec=pltpu.PrefetchScalarGridSpec(
            num_scalar_prefetch=4,
            grid=(nb,),
            in_specs=[smem(lambda i, be, nu, nv, us: (i, 0, 0)),
                      smem(lambda i, be, nu, nv, us: (jnp.minimum(i + 1, nb - 1), 0, 0)),
                      smem(lambda i, be, nu, nv, us: (i, 0, 0)),
                      pl.BlockSpec(memory_space=pl.ANY),
                      wspec((1, d, de)), wspec((1, d, de)), wspec((1, de, d))],
            out_specs=pl.BlockSpec(memory_space=pl.ANY),
            scratch_shapes=[pltpu.VMEM((2, EXPERT_ROWS, sw), U32), pltpu.VMEM((2, EXPERT_ROWS, d // 2), U32),
                            pltpu.VMEM((CHUNK, d // 2), U32),
                            pltpu.SemaphoreType.DMA((2,)), pltpu.SemaphoreType.DMA((2,)),
                            pltpu.SemaphoreType.DMA(())]),
        out_shape=jax.ShapeDtypeStruct((stage.shape[0], d // 2), U32),
        compiler_params=_cparams(1, VMEM_LIMIT),
        name="expert_ffn",
    )(block_e, n_used, n_valid, used, src_chunk, src_chunk, dst_chunk, stage, wg, wu, wd)


def _combine_kernel(ost_ref, route_ref, x1_ref, fg_ref, o_ref):
    route = route_ref[...]
    rows = route.shape[0]
    col = lax.broadcasted_iota(I32, (rows, STAGE_ROWS), 1)
    sel = jnp.logical_or(col == route[:, 4:5].astype(I32), col == route[:, 5:6].astype(I32))
    perm = jnp.where(sel, 1.0, 0.0).astype(BF16)
    y = jnp.dot(perm, jnp.concatenate(_unpack(ost_ref[...]), axis=1), preferred_element_type=F32)
    x2 = x1_ref[...] + y
    ms = jnp.mean(x2 * x2, axis=-1, keepdims=True)
    o_ref[...] = x2 * lax.rsqrt(ms + EPS) * fg_ref[...]


def _combine(ostage, route, x1, final_g):
    t, d = x1.shape
    return pl.pallas_call(
        _combine_kernel,
        grid=(t // ROW_TILE,),
        in_specs=[pl.BlockSpec((STAGE_ROWS, d // 2), lambda i: (i, 0)),
                  pl.BlockSpec((ROW_TILE, ROUTE_LANES), lambda i: (i, 0)),
                  pl.BlockSpec((ROW_TILE, d), lambda i: (i, 0)),
                  pl.BlockSpec((1, d), lambda i: (0, 0))],
        out_specs=pl.BlockSpec((ROW_TILE, d), lambda i: (i, 0)),
        out_shape=jax.ShapeDtypeStruct((t, d), F32),
        compiler_params=_cparams(1, VMEM_LIMIT),
        name="moe_combine",
    )(ostage, route, x1, final_g)


def _hier_moe_and_norm(x1, stage, route, tile_cnt, wg, wu, wd, final_g):
    t, d = x1.shape
    ostage = _expert_ffn(stage, _moe_plan(tile_cnt, t), wg, wu, wd)
    return _combine(ostage, route, x1, final_g)


def _encoder(x, prm, kf, tables, nf):
    b, l, d = x.shape
    t = b * l
    d_hy = prm["skip"].shape[1]
    x2 = x.reshape(t, d)
    hy, gm = _inproj(x2, prm["mix_norm_g"], prm["w_in"], (HY_ORDER + 1) * d_hy)
    hy = _shortconv(hy, prm["short_w"], prm["short_b"], b, l)
    gates5 = hy.reshape(b, hy.shape[1], nf // 2, nf, LANES)
    z5 = gates5
    z_blk = HY_ORDER * d_hy // CONV_LANES
    for o in range(HY_ORDER):
        z5 = _longconv(gates5, z5, kf, tables, prm["skip"][o:o + 1], gate_blk=o * d_hy // CONV_LANES,
                       z_blk=z_blk, kf_col=o * d_hy // CONV_LANES, nf=nf)
        z_blk = 0
    y_hy = z5.reshape(b, d_hy // LANES, l, LANES)
    y_gm = _gmlp(gm, prm["ln_g"], prm["ln_b"], prm["ws_stack"], prm["bias_t"])
    x1, stage, route, tile_cnt = _outproj(x2, y_hy, y_gm, prm["mix_out_g"], prm["bd"], prm["w_out"], prm["ffn_norm_g"],
                             prm["wr_cat"], prm["br"])
    out = _hier_moe_and_norm(x1, stage, route, tile_cnt, prm["w_e_gate"], prm["w_e_up"], prm["w_e_down"],
                             prm["final_norm_g"])
    return out.reshape(b, l, d)


def kernel(x_prompt, x_sample, mix_norm_g, w_in, hy_short_w, hy_short_b, hy_filt_w_emb, hy_filt_b_emb,
           hy_filt_w_inner, hy_filt_b_inner, hy_filt_freq, hy_filt_w_out, hy_skip, gm_ln_g, gm_ln_b,
           gm_w_s, gm_b_s, mix_out_g, w_out, ffn_norm_g, w_group, b_group, w_expert_router,
           b_expert_router, w_e_gate, w_e_up, w_e_down, final_norm_g):
    assert w_in.shape[0] == 1, "one layer"
    l = x_prompt.shape[1]
    assert x_sample.shape[1] == l
    nf = math.isqrt(2 * l)
    assert nf * nf == 2 * l and nf % SLAB == 0
    d = x_prompt.shape[2]
    d_hy = hy_skip.shape[2]
    d_gm = gm_ln_g.shape[1]
    head_dim = d_gm // GM_HEADS
    assert d_hy // HY_HEADS == head_dim and d_hy == d_gm

    tables = _dft_tables(nf)
    f1, f2, f2i, f3, f1_full = tables
    tables_bf = tuple(a.astype(BF16) for a in (f1, f2, f2i, f3))

    max_decay = math.log(DECAY_TARGET) / FAST_DECAY_PCT
    min_decay = math.log(DECAY_TARGET) / SLOW_DECAY_PCT
    deltas = jnp.abs(jnp.linspace(min_decay, max_decay, d_hy, dtype=F32))[None, :]
    taps = _filter_taps(hy_filt_w_emb[0], hy_filt_b_emb[0][None, :], hy_filt_w_inner[0], hy_filt_b_inner[0],
                        hy_filt_freq[0][None, :], hy_filt_w_out[0], deltas, l, d_hy)
    kf = _filter_spectrum(taps, f1_full, f2, nf)

    n_route = N_GROUPS + N_EXPERTS
    wr = jnp.zeros((d, ROUTE_LANES), F32).at[:, :n_route].set(
        jnp.concatenate([w_group[0], w_expert_router[0]], axis=1))
    wr_hi = wr.astype(BF16)
    wr_lo = (wr - wr_hi.astype(F32)).astype(BF16)
    br = jnp.zeros((1, ROUTE_LANES), F32).at[0, :n_route].set(jnp.concatenate([b_group[0], b_expert_router[0]]))
    hid = jnp.arange(d_hy, dtype=I32) // head_dim
    bd = jnp.where(hid[:, None] == hid[None, :], 1.0 / head_dim, 0.0).astype(BF16)

    prm = dict(
        mix_norm_g=mix_norm_g, w_in=w_in[0].astype(BF16), short_w=hy_short_w[0], short_b=hy_short_b,
        skip=hy_skip[0], ln_g=gm_ln_g, ln_b=gm_ln_b,
        ws_stack=gm_w_s[0].reshape(GM_HEADS * GM_CHUNK, GM_CHUNK).astype(BF16),
        bias_t=jnp.repeat(gm_b_s[0].T, head_dim, axis=1),
        mix_out_g=mix_out_g, bd=bd, w_out=w_out[0].astype(BF16), ffn_norm_g=ffn_norm_g,
        wr_cat=jnp.concatenate([wr_hi, wr_lo], axis=1), br=br,
        w_e_gate=w_e_gate[0].astype(BF16), w_e_up=w_e_up[0].astype(BF16), w_e_down=w_e_down[0].astype(BF16),
        final_norm_g=final_norm_g[None, :])
    y_prompt = _encoder(x_prompt, prm, kf, tables_bf, nf)
    y_sample = _encoder(x_sample, prm, kf, tables_bf, nf)
    return (y_prompt, y_sample)
```

```python
import functools
import math

import jax
import jax.numpy as jnp
from jax import lax
from jax.experimental import pallas as pl
from jax.experimental.pallas import tpu as pltpu

F32 = jnp.float32
BF16 = jnp.bfloat16
U32 = jnp.uint32
I32 = jnp.int32

EPS = 1e-6
HY_ORDER = 2
HY_HEADS = 8
GM_HEADS = 8
GM_CHUNK = 128
N_GROUPS = 4
EXPERTS_PER_GROUP = 8
N_EXPERTS = N_GROUPS * EXPERTS_PER_GROUP
TOP_K = 2
FILTER_EMB = 5
DECAY_TARGET = 1e-2
FAST_DECAY_PCT = 0.3
SLOW_DECAY_PCT = 1.5

LANES = 128
ROUTE_LANES = LANES
SLAB = 16
CONV_LANES = 256
PHASE2_UNROLL = 8
ROW_TILE = 512
IN_A = 8
IN_SLABS = 4
EXPERT_ROWS = 1024
CHUNK = 8
STAGE_ROWS = ROW_TILE * TOP_K + N_EXPERTS * CHUNK
VMEM_LIMIT = 56 * 1024 * 1024


def _cparams(n_axes, vmem=None):
    return pltpu.CompilerParams(dimension_semantics=("arbitrary",) * n_axes,
                                vmem_limit_bytes=vmem)


def _inproj_kernel(x_ref, xp_ref, xn_ref, g_ref, w_ref, sw_ref, sb_ref, hy_ref, gm_ref, stage_ref,
                   *, nf, d_hy3, tiles_per_seq):
    i = pl.program_id(0)
    rows = x_ref.shape[0]
    pitch = stage_ref.shape[1] // IN_A
    halo = xp_ref.shape[0]
    x = jnp.concatenate([xp_ref[...], x_ref[...], xn_ref[...]], axis=0)
    ms = jnp.mean(x * x, axis=-1, keepdims=True)
    n = (x * lax.rsqrt(ms + EPS) * g_ref[...]).astype(BF16)
    gm = jnp.dot(n, w_ref[:, d_hy3:], preferred_element_type=F32)
    gm_ref[...] = gm[halo:halo + rows].astype(BF16)

    pos = i % tiles_per_seq
    has_prev = jnp.where(pos == 0, 0.0, 1.0)
    has_next = jnp.where(pos == tiles_per_seq - 1, 0.0, 1.0)
    row = lax.broadcasted_iota(I32, (rows, 1), 0)
    prev_w = jnp.where(row == 0, has_prev, 1.0)
    next_w = jnp.where(row == rows - 1, has_next, 1.0)
    n_slabs = stage_ref.shape[0]
    cols = n_slabs * LANES
    ext = rows + 2 * halo
    for c0 in range(0, d_hy3, cols):
        p = jnp.dot(n, w_ref[:, c0:c0 + cols], preferred_element_type=F32)
        prev = pltpu.roll(p, 1, 0)[halo:halo + rows] * prev_w
        nxt = pltpu.roll(p, ext - 1, 0)[halo:halo + rows] * next_w
        res = (sw_ref[0:1, c0:c0 + cols] * prev + sw_ref[1:2, c0:c0 + cols] * p[halo:halo + rows]
               + sw_ref[2:3, c0:c0 + cols] * nxt + sb_ref[:, c0:c0 + cols])
        for s in range(n_slabs):
            for al in range(IN_A):
                stage_ref[s, al * pitch:al * pitch + nf, :] = res[al * nf:(al + 1) * nf,
                                                                s * LANES:(s + 1) * LANES]
        for s in range(n_slabs):
            for b in range(nf):
                hy_ref[0, c0 // LANES + s, b] = stage_ref[s, pl.ds(b, IN_A, stride=pitch), :]


def _inproj(x2, g, w_bf, short_w, short_b, n_seq, seq_len, nf, d_hy3):
    t, d = x2.shape
    dp = w_bf.shape[1]
    rows = IN_A * nf
    tiles_per_seq = seq_len // rows
    hb = rows // 8
    nhb = t // 8
    full = lambda a: pl.BlockSpec(a.shape, lambda i: (0,) * a.ndim)
    return pl.pallas_call(
        functools.partial(_inproj_kernel, nf=nf, d_hy3=d_hy3, tiles_per_seq=tiles_per_seq),
        grid=(t // rows,),
        in_specs=[pl.BlockSpec((rows, d), lambda i: (i, 0)),
                  pl.BlockSpec((8, d), lambda i: (jnp.maximum(i * hb - 1, 0), 0)),
                  pl.BlockSpec((8, d), lambda i: (jnp.minimum((i + 1) * hb, nhb - 1), 0)),
                  full(g), full(w_bf), full(short_w), full(short_b)],
        out_specs=[pl.BlockSpec((1, d_hy3 // LANES, nf, IN_A, LANES),
                                lambda i: (i // tiles_per_seq, 0, 0, i % tiles_per_seq, 0)),
                   pl.BlockSpec((rows, dp - d_hy3), lambda i: (i, 0))],
        out_shape=[jax.ShapeDtypeStruct((n_seq, d_hy3 // LANES, nf, nf // 2, LANES), F32),
                   jax.ShapeDtypeStruct((t, dp - d_hy3), BF16)],
        scratch_shapes=[pltpu.VMEM((IN_SLABS, IN_A * (nf + 8), LANES), F32)],
        compiler_params=_cparams(1, VMEM_LIMIT),
        name="inproj_shortconv",
    )(x2, x2, x2, g, w_bf, short_w, short_b)


def _cos_sin(m, period):
    ang = m.astype(F32) * (2.0 * math.pi / period)
    return jnp.cos(ang), jnp.sin(ang)


def _stack_complex(mr, mi):
    top = jnp.concatenate([mr, -mi], axis=-1)
    bot = jnp.concatenate([mi, mr], axis=-1)
    return jnp.concatenate([top, bot], axis=-2)


def _dft_tables(nf):
    na = nf // 2
    n = nf * nf
    idx = jnp.arange(nf, dtype=I32)
    c, s = _cos_sin((idx[:, None] * idx[None, :]) % nf, nf)
    f2 = _stack_complex(c, -s)
    f2i = _stack_complex(c, s)
    b_ = idx[:, None, None]
    ka = idx[None, :, None]
    a_ = idx[None, None, :]
    m1 = (nf * a_ * ka + b_ * ka) % n
    c1, s1 = _cos_sin(m1, n)
    f1 = _stack_complex(c1[:, :, :na], -s1[:, :, :na])
    f1_full = jnp.concatenate([c1, -s1], axis=1)
    c3 = jnp.swapaxes(c1, 1, 2)[:, :na, :]
    s3 = jnp.swapaxes(s1, 1, 2)[:, :na, :]
    f3 = _stack_complex(c3, s3)
    return f1, f2, f2i, f3, f1_full


def _split_bf16(x):
    hi = x.astype(BF16)
    lo = (x - hi.astype(F32)).astype(BF16)
    return hi, lo


def _stack_split_lhs(a):
    hi, lo = _split_bf16(a)
    return jnp.concatenate([hi, lo, hi], axis=-1)


def _stack_split_rhs(b):
    hi, lo = _split_bf16(b)
    return jnp.concatenate([hi, hi, lo], axis=-2)


def _filter_taps_kernel(wemb_ref, bemb_ref, win_ref, bin_ref, freq_ref, wout_ref, delta_ref, o_ref,
                        *, seq_len, n_inner, d_hy):
    rows = o_ref.shape[0]
    j = pl.program_id(0) * rows + lax.broadcasted_iota(I32, (rows, 1), 0)
    tidx = jnp.where(j < seq_len, j, 2 * seq_len - j).astype(F32)
    t = tidx / float(seq_len - 1)
    fr0 = jnp.float32(1e-4)
    ang0 = (2.0 * math.pi / seq_len) * tidx * fr0
    ang1 = (2.0 * math.pi / seq_len) * tidx
    freq = freq_ref[...]
    pre = (t * wemb_ref[0:1, :] + jnp.cos(ang0) * wemb_ref[1:2, :] + jnp.cos(ang1) * wemb_ref[2:3, :]
           - jnp.sin(ang0) * wemb_ref[3:4, :] - jnp.sin(ang1) * wemb_ref[4:5, :] + bemb_ref[...])
    hdn = jnp.sin(freq * pre)
    for i in range(n_inner):
        hdn = jnp.sin(freq * (jnp.dot(_stack_split_lhs(hdn), win_ref[i], preferred_element_type=F32)
                              + bin_ref[i:i + 1, :]))
    h = jnp.dot(_stack_split_lhs(hdn), wout_ref[...], preferred_element_type=F32)
    decay = jnp.exp(-t * delta_ref[...])
    for o in range(HY_ORDER):
        fwd = h[:, (2 * o) * d_hy:(2 * o + 1) * d_hy]
        bwd = h[:, (2 * o + 1) * d_hy:(2 * o + 2) * d_hy]
        val = jnp.where(j < seq_len, fwd, bwd) * decay
        o_ref[:, o * d_hy:(o + 1) * d_hy] = jnp.where(j == seq_len, 0.0, val)


def _filter_taps(wemb, bemb, win, bin_, freq, wout, deltas, seq_len, d_hy):
    rows = 1024
    n2 = 2 * seq_len
    width = wemb.shape[1]
    n_inner = win.shape[0]
    full = lambda *shape: pl.BlockSpec(shape, lambda i: (0,) * len(shape))
    return pl.pallas_call(
        functools.partial(_filter_taps_kernel, seq_len=seq_len, n_inner=n_inner, d_hy=d_hy),
        grid=(n2 // rows,),
        in_specs=[full(FILTER_EMB, width), full(1, width), full(n_inner, 3 * width, width),
                  full(n_inner, width), full(1, width), full(3 * width, 2 * HY_ORDER * d_hy),
                  full(1, d_hy)],
        out_specs=pl.BlockSpec((rows, HY_ORDER * d_hy), lambda i: (i, 0)),
        out_shape=jax.ShapeDtypeStruct((n2, HY_ORDER * d_hy), F32),
        compiler_params=_cparams(1, VMEM_LIMIT),
        name="filter_taps",
    )(wemb, bemb, _stack_split_rhs(win), bin_, freq, _stack_split_rhs(wout), deltas)


def _filter_fft1_kernel(k_ref, f_ref, o_ref):
    for bl in range(SLAB):
        o_ref[bl] = jnp.dot(f_ref[bl], _stack_split_rhs(k_ref[:, bl, :]), preferred_element_type=F32)


def _filter_fft2_kernel(g_ref, f_ref, o_ref, *, scale):
    for kl in range(SLAB):
        rhs = _stack_split_rhs(jnp.concatenate([g_ref[:, 0, kl, :], g_ref[:, 1, kl, :]], axis=0))
        o_ref[kl] = jnp.dot(f_ref[...], rhs, preferred_element_type=F32) * scale


def _filter_spectrum(taps, f1_full, f2, nf):
    c = taps.shape[1]
    k3 = taps.reshape(nf, nf, c)
    g = pl.pallas_call(
        _filter_fft1_kernel,
        grid=(nf // SLAB, c // CONV_LANES),
        in_specs=[pl.BlockSpec((nf, SLAB, CONV_LANES), lambda i, j: (0, i, j)),
                  pl.BlockSpec((SLAB, 2 * nf, 3 * nf), lambda i, j: (i, 0, 0))],
        out_specs=pl.BlockSpec((SLAB, 2 * nf, CONV_LANES), lambda i, j: (i, 0, j)),
        out_shape=jax.ShapeDtypeStruct((nf, 2 * nf, c), F32),
        compiler_params=_cparams(2, VMEM_LIMIT),
        name="filter_fft1",
    )(k3, _stack_split_lhs(f1_full))
    g4 = g.reshape(nf, 2, nf, c)
    return pl.pallas_call(
        functools.partial(_filter_fft2_kernel, scale=1.0 / (nf * nf)),
        grid=(nf // SLAB, c // CONV_LANES),
        in_specs=[pl.BlockSpec((nf, 2, SLAB, CONV_LANES), lambda i, j: (0, 0, i, j)),
                  pl.BlockSpec((2 * nf, 6 * nf), lambda i, j: (0, 0))],
        out_specs=pl.BlockSpec((SLAB, 2 * nf, CONV_LANES), lambda i, j: (i, 0, j)),
        out_shape=jax.ShapeDtypeStruct((nf, 2 * nf, c), F32),
        compiler_params=_cparams(2, VMEM_LIMIT),
        name="filter_fft2",
    )(g4, _stack_split_lhs(f2))


def _pack(re, im):
    rb = lax.bitcast_convert_type(re.astype(BF16).astype(F32), U32)
    ib = lax.bitcast_convert_type(im.astype(BF16).astype(F32), U32)
    return rb | (ib >> 16)


def _unpack(w):
    re = lax.bitcast_convert_type(w & jnp.uint32(0xFFFF0000), F32).astype(BF16)
    im = lax.bitcast_convert_type(w << 16, F32).astype(BF16)
    return re, im


def _longconv_kernel(z1_ref, f1_ref, kf_ref, f2_ref, f2i_ref, f3_ref, gate_ref, zs_ref, skip_ref,
                     o_ref, g_ref, *, nf, out_bmajor):
    na = nf // 2
    ns = nf // SLAB
    halves = g_ref.shape[1]
    t = pl.program_id(2)

    def load_rows(ref5, s, bl):
        return jnp.concatenate([ref5[s, h, bl] for h in range(halves)], axis=1)

    def store_col(blk, bl, val):
        for h in range(halves):
            g_ref[blk, h, pl.ds(bl, nf, stride=SLAB), :] = val[:, h * LANES:(h + 1) * LANES]

    def load_col(blk, bl):
        return jnp.concatenate([g_ref[blk, h, pl.ds(bl, nf, stride=SLAB), :] for h in range(halves)],
                               axis=1)

    @pl.when(t < ns)
    def _phase1():
        for bl in range(SLAB):
            rhs = jnp.concatenate([load_rows(z1_ref, 0, bl), load_rows(z1_ref, 1, bl)],
                                  axis=0).astype(BF16)
            out = jnp.dot(f1_ref[bl], rhs, preferred_element_type=F32)
            store_col(t, bl, _pack(out[:nf], out[nf:]))

    @pl.when(jnp.logical_and(t >= ns, t < 2 * ns))
    def _phase2():
        i = t - ns

        def load_slab(ka):
            row0 = pl.multiple_of(ka * SLAB, SLAB)
            return jnp.concatenate([g_ref[:, h, pl.ds(row0, SLAB), :].reshape(nf, LANES)
                                    for h in range(halves)], axis=1)

        def transform(w, kl):
            re, im = _unpack(w)
            s = jnp.dot(f2_ref[...], jnp.concatenate([re, im], axis=0),
                        preferred_element_type=F32)
            sr, si = s[:nf], s[nf:]
            kr, ki = kf_ref[kl, :nf, :], kf_ref[kl, nf:, :]
            pr = (sr * kr - si * ki).astype(BF16)
            pi = (sr * ki + si * kr).astype(BF16)
            h = jnp.dot(f2i_ref[...], jnp.concatenate([pr, pi], axis=0),
                        preferred_element_type=F32)
            return _pack(h[:nf], h[nf:])

        def store_slab(ka, packed):
            row0 = pl.multiple_of(ka * SLAB, SLAB)
            for hf in range(halves):
                g_ref[:, hf, pl.ds(row0, SLAB), :] = packed[:, hf * LANES:(hf + 1) * LANES].reshape(
                    ns, SLAB, LANES)

        def body(grp, carry):
            kls = [grp * PHASE2_UNROLL + u for u in range(PHASE2_UNROLL)]
            ws = [load_slab(i * SLAB + kl) for kl in kls]
            outs = [transform(w, kl) for w, kl in zip(ws, kls)]
            for kl, packed in zip(kls, outs):
                store_slab(i * SLAB + kl, packed)
            return carry

        lax.fori_loop(0, SLAB // PHASE2_UNROLL, body, 0)

    @pl.when(t >= 2 * ns)
    def _phase3():
        j = t - 2 * ns
        skip = skip_ref[...]
        out_flat = None if out_bmajor else o_ref.reshape(2, halves, na * SLAB, LANES)
        for bl in range(SLAB):
            re, im = _unpack(load_col(j, bl))
            y = jnp.dot(f3_ref[bl], jnp.concatenate([re, im], axis=0),
                        preferred_element_type=F32)
            for s in range(2):
                val = load_rows(gate_ref, s, bl) * (y[s * na:(s + 1) * na]
                                                    + load_rows(zs_ref, s, bl) * skip)
                for h in range(halves):
                    piece = val[:, h * LANES:(h + 1) * LANES]
                    if out_bmajor:
                        o_ref[s, h, bl] = piece
                    else:
                        out_flat[s, h, pl.ds(bl, na, stride=SLAB), :] = piece


def _longconv(gates5, z5, kf, tables, skip, gate_blk, z_blk, kf_col, nf, out_bmajor):
    f1, f2, f2i, f3 = tables
    b, _, _, na, _ = z5.shape
    c = skip.shape[1]
    ns = nf // SLAB
    halves = CONV_LANES // LANES
    nchunk = c // CONV_LANES
    p1 = lambda t: jnp.minimum(t, ns - 1)
    p2 = lambda t: jnp.clip(t - ns, 0, ns - 1)
    p3 = lambda t: jnp.clip(t - 2 * ns, 0, ns - 1)
    cl = CONV_LANES
    blk5 = (2, halves, SLAB, na, LANES)
    if out_bmajor:
        out_spec = pl.BlockSpec(blk5, lambda p, q, t: (p, q, p3(t), 0, 0))
        out_shape = (b, c // LANES, nf, na, LANES)
    else:
        out_spec = pl.BlockSpec((2, halves, na, SLAB, LANES), lambda p, q, t: (p, q, 0, p3(t), 0))
        out_shape = (b, c // LANES, na, nf, LANES)
    return pl.pallas_call(
        functools.partial(_longconv_kernel, nf=nf, out_bmajor=out_bmajor),
        grid=(b // 2, nchunk, 3 * ns),
        in_specs=[
            pl.BlockSpec(blk5, lambda p, q, t: (p, z_blk + q, p1(t), 0, 0)),
            pl.BlockSpec((SLAB, 2 * nf, 2 * na), lambda p, q, t: (p1(t), 0, 0)),
            pl.BlockSpec((SLAB, 2 * nf, cl), lambda p, q, t: (p2(t), 0, kf_col + q)),
            pl.BlockSpec((2 * nf, 2 * nf), lambda p, q, t: (0, 0)),
            pl.BlockSpec((2 * nf, 2 * nf), lambda p, q, t: (0, 0)),
            pl.BlockSpec((SLAB, 2 * na, 2 * nf), lambda p, q, t: (p3(t), 0, 0)),
            pl.BlockSpec(blk5, lambda p, q, t: (p, gate_blk + q, p3(t), 0, 0)),
            pl.BlockSpec(blk5, lambda p, q, t: (p, z_blk + q, p3(t), 0, 0)),
            pl.BlockSpec((1, cl), lambda p, q, t: (0, q)),
        ],
        out_specs=out_spec,
        out_shape=jax.ShapeDtypeStruct(out_shape, F32),
        scratch_shapes=[pltpu.VMEM((ns, halves, nf * SLAB, LANES), U32)],
        compiler_params=_cparams(3, VMEM_LIMIT),
        name="longconv",
    )(z5, f1, kf, f2, f2i, f3, gates5, z5, skip)


def _gelu_exact(x):
    return 0.5 * x * (1.0 + lax.erf(x * (1.0 / math.sqrt(2.0))))


def _gmlp_kernel(gm_ref, lng_ref, lnb_ref, ws_ref, bias_ref, o_ref, *, d_gm):
    g = _gelu_exact(gm_ref[...].astype(F32))
    u, v = g[:, :d_gm], g[:, d_gm:]
    mu = jnp.mean(v, axis=-1, keepdims=True)
    vc = v - mu
    var = jnp.mean(vc * vc, axis=-1, keepdims=True)
    vh = (vc * lax.rsqrt(var + EPS) * lng_ref[...] + lnb_ref[...]).astype(BF16)
    head = lax.broadcasted_iota(I32, (GM_CHUNK, d_gm), 1) // (d_gm // GM_HEADS)
    for c in range(gm_ref.shape[0] // GM_CHUNK):
        rows = slice(c * GM_CHUNK, (c + 1) * GM_CHUNK)
        r = jnp.dot(ws_ref[...], vh[rows], preferred_element_type=F32)
        s = r[:GM_CHUNK]
        for h in range(1, GM_HEADS):
            s = jnp.where(head == h, r[h * GM_CHUNK:(h + 1) * GM_CHUNK], s)
        o_ref[rows, :] = u[rows] * (s + bias_ref[...])


def _gmlp(gm, ln_g, ln_b, ws_stack, bias_t):
    t, c2 = gm.shape
    d_gm = c2 // 2
    return pl.pallas_call(
        functools.partial(_gmlp_kernel, d_gm=d_gm),
        grid=(t // ROW_TILE,),
        in_specs=[pl.BlockSpec((ROW_TILE, c2), lambda i: (i, 0)),
                  pl.BlockSpec((1, d_gm), lambda i: (0, 0)),
                  pl.BlockSpec((1, d_gm), lambda i: (0, 0)),
                  pl.BlockSpec(ws_stack.shape, lambda i: (0, 0)),
                  pl.BlockSpec(bias_t.shape, lambda i: (0, 0))],
        out_specs=pl.BlockSpec((ROW_TILE, d_gm), lambda i: (i, 0)),
        out_shape=jax.ShapeDtypeStruct((t, d_gm), F32),
        compiler_params=_cparams(1, VMEM_LIMIT),
        name="gmlp",
    )(gm, ln_g, ln_b, ws_stack, bias_t)


def _head_rms(y, bd):
    ms = jnp.dot((y * y).astype(BF16), bd, preferred_element_type=F32)
    return y * lax.rsqrt(ms + EPS)


def _route(logits):
    lane = lax.broadcasted_iota(I32, logits.shape, 1)
    neg = jnp.float32(-1e30)
    big = jnp.int32(ROUTE_LANES)
    gmask = lane < N_GROUPS
    gl = jnp.where(gmask, logits, neg)
    gmax = jnp.max(gl, axis=-1, keepdims=True)
    grp = jnp.min(jnp.where(jnp.logical_and(gl == gmax, gmask), lane, big), axis=-1, keepdims=True)
    psum = jnp.sum(jnp.where(gmask, jnp.exp(gl - gmax), 0.0), axis=-1, keepdims=True)
    p_grp = 1.0 / psum
    lo = N_GROUPS + EXPERTS_PER_GROUP * grp
    emask = jnp.logical_and(lane >= lo, lane < lo + EXPERTS_PER_GROUP)
    el = jnp.where(emask, logits, neg)
    m1 = jnp.max(el, axis=-1, keepdims=True)
    i1 = jnp.min(jnp.where(jnp.logical_and(el == m1, emask), lane, big), axis=-1, keepdims=True)
    emask2 = jnp.logical_and(emask, lane != i1)
    el2 = jnp.where(emask2, logits, neg)
    m2 = jnp.max(el2, axis=-1, keepdims=True)
    i2 = jnp.min(jnp.where(jnp.logical_and(el2 == m2, emask2), lane, big), axis=-1, keepdims=True)
    d = jnp.exp(m2 - m1)
    g1 = p_grp * (1.0 / (1.0 + d))
    g2 = p_grp * (d / (1.0 + d))
    e1 = (i1 - N_GROUPS).astype(F32)
    e2 = (i2 - N_GROUPS).astype(F32)
    return jnp.where(lane == 0, e1, jnp.where(lane == 1, e2, jnp.where(lane == 2, g1,
                     jnp.where(lane == 3, g2, 0.0))))


def _lane_cumsum(v):
    lane = lax.broadcasted_iota(I32, v.shape, 1)
    sh = 1
    while sh < v.shape[1]:
        v = v + jnp.where(lane >= sh, pltpu.roll(v, sh, 1), 0.0)
        sh *= 2
    return v


def _outproj_kernel(x_ref, yh_ref, yg_ref, mg_ref, bd_ref, wo_ref, fg_ref, wr_ref, br_ref,
                    x1_ref, stage_ref, route_ref, count_ref, *, d_hy):
    bd = bd_ref[...]
    mg = mg_ref[...]
    yh = jnp.concatenate([yh_ref[0, c] for c in range(yh_ref.shape[1])], axis=1)
    mh = (_head_rms(yh, bd) * mg[:, :d_hy]).astype(BF16)
    mgm = (_head_rms(yg_ref[...], bd) * mg[:, d_hy:]).astype(BF16)
    wo = wo_ref[...]
    x1 = (x_ref[...] + jnp.dot(mh, wo[:d_hy], preferred_element_type=F32)
          + jnp.dot(mgm, wo[d_hy:], preferred_element_type=F32))
    x1_ref[...] = x1
    ms = jnp.mean(x1 * x1, axis=-1, keepdims=True)
    n2 = x1 * lax.rsqrt(ms + EPS) * fg_ref[...]
    hi, lo = _split_bf16(n2)
    wr = wr_ref[...]
    hw = jnp.dot(hi, wr, preferred_element_type=F32)
    logits = (hw[:, :ROUTE_LANES] + hw[:, ROUTE_LANES:]
              + jnp.dot(lo, wr[:, :ROUTE_LANES], preferred_element_type=F32) + br_ref[...])
    route = _route(logits)

    rows = route.shape[0]
    lane = lax.broadcasted_iota(I32, route.shape, 1)
    e0 = route[:, 0:1].astype(I32)
    e1 = route[:, 1:2].astype(I32)
    oh0 = lane == e0
    oh1 = lane == e1
    oh = jnp.where(jnp.logical_or(oh0, oh1), 1.0, 0.0)
    r = lax.broadcasted_iota(I32, (rows, rows), 0)
    c = lax.broadcasted_iota(I32, (rows, rows), 1)
    ltri = jnp.where(c < r, 1.0, 0.0).astype(BF16)
    before = jnp.dot(ltri, oh.astype(BF16), preferred_element_type=F32)
    cnt = jnp.sum(oh, axis=0, keepdims=True)
    run = jnp.floor((cnt + (CHUNK - 1)) * (1.0 / CHUNK)) * CHUNK
    run_start = _lane_cumsum(jnp.broadcast_to(run, (8, run.shape[1])))[0:1] - run
    tot = before + run_start
    loc0 = jnp.sum(jnp.where(oh0, tot, 0.0), axis=-1, keepdims=True)
    loc1 = jnp.sum(jnp.where(oh1, tot, 0.0), axis=-1, keepdims=True)
    route = jnp.where(lane == 4, loc0, jnp.where(lane == 5, loc1, route))
    route_ref[...] = route
    count_ref[pl.ds(pl.program_id(0), 1), :] = cnt

    rt = jnp.transpose(route)
    srow = lax.broadcasted_iota(I32, (STAGE_ROWS, rows), 0)
    p0 = srow == rt[4:5, :].astype(I32)
    p1 = srow == rt[5:6, :].astype(I32)
    perm = jnp.where(jnp.logical_or(p0, p1), 1.0, 0.0).astype(BF16)
    staged = jnp.dot(perm, hi, preferred_element_type=F32)
    gate = jnp.sum(jnp.where(p0, rt[2:3, :], 0.0) + jnp.where(p1, rt[3:4, :], 0.0), axis=-1,
                   keepdims=True)
    half = staged.shape[1] // 2
    mlane = lax.broadcasted_iota(I32, (STAGE_ROWS, LANES), 1)
    meta = jnp.where(mlane == 0, lax.bitcast_convert_type(jnp.broadcast_to(gate, (STAGE_ROWS, LANES)), U32),
                     jnp.uint32(0))
    stage_ref[...] = jnp.concatenate([_pack(staged[:, :half], staged[:, half:]), meta], axis=1)


def _outproj(x2, yh, yg, mix_g, bd, wo_bf, ffn_g, wr_cat, br):
    t, d = x2.shape
    _, nch, seq_len, _ = yh.shape
    d_hy = nch * LANES
    tiles_per_seq = seq_len // ROW_TILE
    n_tiles = t // ROW_TILE
    row = lambda w: pl.BlockSpec((ROW_TILE, w), lambda i: (i, 0))
    full = lambda a: pl.BlockSpec(a.shape, lambda i: (0,) * a.ndim)
    sw = d // 2 + LANES
    return pl.pallas_call(
        functools.partial(_outproj_kernel, d_hy=d_hy),
        grid=(n_tiles,),
        in_specs=[row(d),
                  pl.BlockSpec((1, nch, ROW_TILE, LANES),
                               lambda i: (i // tiles_per_seq, 0, i % tiles_per_seq, 0)),
                  row(yg.shape[1]), full(mix_g), full(bd), full(wo_bf), full(ffn_g),
                  full(wr_cat), full(br)],
        out_specs=[row(d), pl.BlockSpec((STAGE_ROWS, sw), lambda i: (i, 0)), row(ROUTE_LANES),
                   pl.BlockSpec((n_tiles, ROUTE_LANES), lambda i: (0, 0))],
        out_shape=[jax.ShapeDtypeStruct((t, d), F32),
                   jax.ShapeDtypeStruct((n_tiles * STAGE_ROWS, sw), U32),
                   jax.ShapeDtypeStruct((t, ROUTE_LANES), F32),
                   jax.ShapeDtypeStruct((n_tiles, ROUTE_LANES), F32)],
        compiler_params=_cparams(1, VMEM_LIMIT),
        name="outproj_router",
    )(x2, yh, yg, mix_g, bd, wo_bf, ffn_g, wr_cat, br)


def _moe_plan(tile_cnt, n_tokens):
    n_tiles = tile_cnt.shape[0]
    cnt = tile_cnt[:, :N_EXPERTS].astype(I32)
    run = (cnt + CHUNK - 1) // CHUNK * CHUNK
    run_start = jnp.cumsum(run, axis=1) - run
    used = jnp.sum(run, axis=1)
    e_rows = jnp.sum(run, axis=0)
    e_pad = (e_rows + EXPERT_ROWS - 1) // EXPERT_ROWS * EXPERT_ROWS
    e_end = jnp.cumsum(e_pad)
    e_start = e_end - e_pad
    n_blocks = -(-(n_tokens * TOP_K + n_tiles * N_EXPERTS * (CHUNK - 1)) // EXPERT_ROWS) + N_EXPERTS
    n_used = (e_end[-1:] // EXPERT_ROWS).astype(I32)
    block_row = jnp.arange(n_blocks, dtype=I32) * EXPERT_ROWS
    block_e = jnp.minimum(jnp.sum((e_end[None, :] <= block_row[:, None]).astype(I32), axis=1),
                          N_EXPERTS - 1)
    run_end_in_e = jnp.cumsum(run, axis=0)
    chunk_row = jnp.arange(n_blocks * EXPERT_ROWS // CHUNK, dtype=I32) * CHUNK
    e_of_chunk = jnp.repeat(block_e, EXPERT_ROWS // CHUNK)
    e_sel = jnp.arange(N_EXPERTS, dtype=I32)[None, :] == e_of_chunk[:, None]
    pick = lambda tbl: jnp.dot(e_sel.astype(F32), tbl.astype(F32),
                               precision=lax.Precision.HIGHEST).astype(I32)
    q = chunk_row - pick(e_start[:, None])[:, 0]
    valid = q < pick(e_rows[:, None])[:, 0]
    ends = pick(run_end_in_e.T)
    tile = jnp.minimum(jnp.sum((ends <= q[:, None]).astype(I32), axis=1), n_tiles - 1)
    t_sel = jnp.arange(n_tiles, dtype=I32)[None, :] == tile[:, None]
    at_tile = lambda tbl: jnp.sum(jnp.where(t_sel, pick(tbl.T), 0), axis=1)
    within = q - (at_tile(run_end_in_e) - at_tile(run))
    src_row = tile * STAGE_ROWS + at_tile(run_start) + within
    zero_chunk = STAGE_ROWS // CHUNK - 1
    per_block = EXPERT_ROWS // CHUNK
    src_chunk = jnp.where(valid, src_row // CHUNK, zero_chunk).reshape(n_blocks, 1, per_block)
    dst_chunk = jnp.where(valid, src_row // CHUNK, 0).reshape(n_blocks, 1, per_block)
    n_valid = jnp.sum(valid.reshape(n_blocks, per_block).astype(I32), axis=1)
    return (block_e.astype(I32), n_used, n_valid.astype(I32), src_chunk.astype(I32),
            dst_chunk.astype(I32), used.astype(I32))


def _ffn_kernel(be_ref, nu_ref, nv_ref, used_ref, src_now_ref, src_next_ref, dst_ref, stage_ref,
                wg_ref, wu_ref, wd_ref, ost_ref, xbuf, obuf, zero_ref, in_sem, out_sem, zero_sem):
    del be_ref
    b = pl.program_id(0)
    nu = nu_ref[0]
    per_block = dst_ref.shape[-1]
    slot = b % 2

    def fetch(map_ref, s):
        def body(j, carry):
            row = pl.multiple_of(map_ref[0, 0, j] * CHUNK, CHUNK)
            pltpu.make_async_copy(stage_ref.at[pl.ds(row, CHUNK)], xbuf.at[s, pl.ds(j * CHUNK, CHUNK)],
                                  in_sem.at[s]).start()
            return carry
        lax.fori_loop(0, per_block, body, 0, unroll=8)

    def out_copy(j, chunk, s):
        row = pl.multiple_of(chunk * CHUNK, CHUNK)
        return pltpu.make_async_copy(obuf.at[s, pl.ds(j * CHUNK, CHUNK)], ost_ref.at[pl.ds(row, CHUNK)],
                                     out_sem.at[s])

    def drain(s, n):
        @pl.when(n == per_block)
        def _():
            pltpu.make_async_copy(obuf.at[s], ost_ref.at[pl.ds(0, EXPERT_ROWS)], out_sem.at[s]).wait()

        @pl.when(n != per_block)
        def _():
            def body(j, carry):
                out_copy(0, 0, s).wait()
                return carry
            lax.fori_loop(0, n, body, 0)

    def zero_tails(wait):
        def tile(i, carry):
            def chunk(c, carry2):
                row = pl.multiple_of(i * STAGE_ROWS + c * CHUNK, CHUNK)
                cp = pltpu.make_async_copy(zero_ref, ost_ref.at[pl.ds(row, CHUNK)], zero_sem)
                if wait:
                    cp.wait()
                else:
                    cp.start()
                return carry2
            return lax.fori_loop(used_ref[i] // CHUNK, STAGE_ROWS // CHUNK, chunk, carry)
        lax.fori_loop(0, used_ref.shape[0], tile, 0)

    @pl.when(b == 0)
    def _prologue():
        fetch(src_now_ref, 0)
        zero_ref[...] = jnp.zeros_like(zero_ref)
        zero_tails(wait=False)
        zero_tails(wait=True)

    @pl.when(b + 1 < nu)
    def _prefetch_next():
        fetch(src_next_ref, 1 - slot)

    @pl.when(b < nu)
    def _compute():
        pltpu.make_async_copy(stage_ref.at[pl.ds(0, EXPERT_ROWS)], xbuf.at[slot], in_sem.at[slot]).wait()

        @pl.when(b >= 2)
        def _():
            drain(slot, nv_ref[b - 2])

        x = xbuf[slot]
        half = wg_ref.shape[1] // 2
        xb = jnp.concatenate(_unpack(x[:, :half]), axis=1)
        gate = lax.bitcast_convert_type(x[:, half:half + 1], F32)
        g = jnp.dot(xb, wg_ref[0], preferred_element_type=F32)
        u = jnp.dot(xb, wu_ref[0], preferred_element_type=F32)
        h = (g * (1.0 / (1.0 + jnp.exp(-g))) * u).astype(BF16)
        o = jnp.dot(h, wd_ref[0], preferred_element_type=F32) * gate
        obuf[slot] = _pack(o[:, :half], o[:, half:])

        def body(j, carry):
            out_copy(j, dst_ref[0, 0, j], slot).start()
            return carry

        @pl.when(nv_ref[b] == per_block)
        def _():
            lax.fori_loop(0, per_block, body, 0, unroll=8)

        @pl.when(nv_ref[b] != per_block)
        def _():
            lax.fori_loop(0, nv_ref[b], body, 0)

        @pl.when(b == nu - 1)
        def _():
            drain(slot, nv_ref[b])

            @pl.when(b >= 1)
            def _():
                drain(1 - slot, nv_ref[b - 1])


def _expert_ffn(stage, plan, wg, wu, wd):
    block_e, n_used, n_valid, src_chunk, dst_chunk, used = plan
    nb = block_e.shape[0]
    sw = stage.shape[1]
    d, de = wg.shape[1], wg.shape[2]
    per_block = EXPERT_ROWS // CHUNK
    cur = lambda i, be, nu: jnp.minimum(i, nu[0] - 1)
    wspec = lambda shape: pl.BlockSpec(shape, lambda i, be, nu, nv, us: (be[cur(i, be, nu)], 0, 0))
    smem = lambda imap: pl.BlockSpec((1, 1, per_block), imap, memory_space=pltpu.SMEM)
    return pl.pallas_call(
        _ffn_kernel,
        grid_spec=pltpu.PrefetchScalarGridSpec(
            num_scalar_prefetch=4,
            grid=(nb,),
            in_specs=[smem(lambda i, be, nu, nv, us: (i, 0, 0)),
                      smem(lambda i, be, nu, nv, us: (jnp.minimum(i + 1, nb - 1), 0, 0)),
                      smem(lambda i, be, nu, nv, us: (i, 0, 0)),
                      pl.BlockSpec(memory_space=pl.ANY),
                      wspec((1, d, de)), wspec((1, d, de)), wspec((1, de, d))],
            out_specs=pl.BlockSpec(memory_space=pl.ANY),
            scratch_shapes=[pltpu.VMEM((2, EXPERT_ROWS, sw), U32), pltpu.VMEM((2, EXPERT_ROWS, d // 2), U32),
                            pltpu.VMEM((CHUNK, d // 2), U32),
                            pltpu.SemaphoreType.DMA((2,)), pltpu.SemaphoreType.DMA((2,)),
                            pltpu.SemaphoreType.DMA(())]),
        out_shape=jax.ShapeDtypeStruct((stage.shape[0], d // 2), U32),
        compiler_params=_cparams(1, VMEM_LIMIT),
        name="expert_ffn",
    )(block_e, n_used, n_valid, used, src_chunk, src_chunk, dst_chunk, stage, wg, wu, wd)


def _combine_kernel(ost_ref, route_ref, x1_ref, fg_ref, o_ref):
    route = route_ref[...]
    rows = route.shape[0]
    col = lax.broadcasted_iota(I32, (rows, STAGE_ROWS), 1)
    sel = jnp.logical_or(col == route[:, 4:5].astype(I32), col == route[:, 5:6].astype(I32))
    perm = jnp.where(sel, 1.0, 0.0).astype(BF16)
    y = jnp.dot(perm, jnp.concatenate(_unpack(ost_ref[...]), axis=1), preferred_element_type=F32)
    x2 = x1_ref[...] + y
    ms = jnp.mean(x2 * x2, axis=-1, keepdims=True)
    o_ref[...] = x2 * lax.rsqrt(ms + EPS) * fg_ref[...]


def _combine(ostage, route, x1, final_g):
    t, d = x1.shape
    return pl.pallas_call(
        _combine_kernel,
        grid=(t // ROW_TILE,),
        in_specs=[pl.BlockSpec((STAGE_ROWS, d // 2), lambda i: (i, 0)),
                  pl.BlockSpec((ROW_TILE, ROUTE_LANES), lambda i: (i, 0)),
                  pl.BlockSpec((ROW_TILE, d), lambda i: (i, 0)),
                  pl.BlockSpec((1, d), lambda i: (0, 0))],
        out_specs=pl.BlockSpec((ROW_TILE, d), lambda i: (i, 0)),
        out_shape=jax.ShapeDtypeStruct((t, d), F32),
        compiler_params=_cparams(1, VMEM_LIMIT),
        name="moe_combine",
    )(ostage, route, x1, final_g)


def _hier_moe_and_norm(x1, stage, route, tile_cnt, wg, wu, wd, final_g):
    t, d = x1.shape
    ostage = _expert_ffn(stage, _moe_plan(tile_cnt, t), wg, wu, wd)
    return _combine(ostage, route, x1, final_g)


def _encoder(x, prm, kf, tables, nf):
    b, l, d = x.shape
    t = b * l
    d_hy = prm["skip"].shape[1]
    x2 = x.reshape(t, d)
    gates5, gm = _inproj(x2, prm["mix_norm_g"], prm["w_in"], prm["short_w"], prm["short_b"], b, l, nf,
                         (HY_ORDER + 1) * d_hy)
    z5 = gates5
    z_blk = HY_ORDER * d_hy // CONV_LANES
    for o in range(HY_ORDER):
        z5 = _longconv(gates5, z5, kf, tables, prm["skip"][o:o + 1], gate_blk=o * d_hy // CONV_LANES,
                       z_blk=z_blk, kf_col=o * d_hy // CONV_LANES, nf=nf, out_bmajor=o < HY_ORDER - 1)
        z_blk = 0
    y_hy = z5.reshape(b, d_hy // LANES, l, LANES)
    y_gm = _gmlp(gm, prm["ln_g"], prm["ln_b"], prm["ws_stack"], prm["bias_t"])
    x1, stage, route, tile_cnt = _outproj(x2, y_hy, y_gm, prm["mix_out_g"], prm["bd"], prm["w_out"], prm["ffn_norm_g"],
                             prm["wr_cat"], prm["br"])
    out = _hier_moe_and_norm(x1, stage, route, tile_cnt, prm["w_e_gate"], prm["w_e_up"], prm["w_e_down"],
                             prm["final_norm_g"])
    return out.reshape(b, l, d)


def kernel(x_prompt, x_sample, mix_norm_g, w_in, hy_short_w, hy_short_b, hy_filt_w_emb, hy_filt_b_emb,
           hy_filt_w_inner, hy_filt_b_inner, hy_filt_freq, hy_filt_w_out, hy_skip, gm_ln_g, gm_ln_b,
           gm_w_s, gm_b_s, mix_out_g, w_out, ffn_norm_g, w_group, b_group, w_expert_router,
           b_expert_router, w_e_gate, w_e_up, w_e_down, final_norm_g):
    assert w_in.shape[0] == 1, "one layer"
    l = x_prompt.shape[1]
    assert x_sample.shape[1] == l
    nf = math.isqrt(2 * l)
    assert nf * nf == 2 * l and nf % SLAB == 0
    d = x_prompt.shape[2]
    d_hy = hy_skip.shape[2]
    d_gm = gm_ln_g.shape[1]
    head_dim = d_gm // GM_HEADS
    assert d_hy // HY_HEADS == head_dim and d_hy == d_gm

    tables = _dft_tables(nf)
    f1, f2, f2i, f3, f1_full = tables
    tables_bf = tuple(a.astype(BF16) for a in (f1, f2, f2i, f3))

    max_decay = math.log(DECAY_TARGET) / FAST_DECAY_PCT
    min_decay = math.log(DECAY_TARGET) / SLOW_DECAY_PCT
    deltas = jnp.abs(jnp.linspace(min_decay, max_decay, d_hy, dtype=F32))[None, :]
    taps = _filter_taps(hy_filt_w_emb[0], hy_filt_b_emb[0][None, :], hy_filt_w_inner[0], hy_filt_b_inner[0],
                        hy_filt_freq[0][None, :], hy_filt_w_out[0], deltas, l, d_hy)
    kf = _filter_spectrum(taps, f1_full, f2, nf)

    n_route = N_GROUPS + N_EXPERTS
    wr = jnp.zeros((d, ROUTE_LANES), F32).at[:, :n_route].set(
        jnp.concatenate([w_group[0], w_expert_router[0]], axis=1))
    wr_hi = wr.astype(BF16)
    wr_lo = (wr - wr_hi.astype(F32)).astype(BF16)
    br = jnp.zeros((1, ROUTE_LANES), F32).at[0, :n_route].set(jnp.concatenate([b_group[0], b_expert_router[0]]))
    hid = jnp.arange(d_hy, dtype=I32) // head_dim
    bd = jnp.where(hid[:, None] == hid[None, :], 1.0 / head_dim, 0.0).astype(BF16)

    prm = dict(
        mix_norm_g=mix_norm_g, w_in=w_in[0].astype(BF16), short_w=hy_short_w[0], short_b=hy_short_b,
        skip=hy_skip[0], ln_g=gm_ln_g, ln_b=gm_ln_b,
        ws_stack=gm_w_s[0].reshape(GM_HEADS * GM_CHUNK, GM_CHUNK).astype(BF16),
        bias_t=jnp.repeat(gm_b_s[0].T, head_dim, axis=1),
        mix_out_g=mix_out_g, bd=bd, w_out=w_out[0].astype(BF16), ffn_norm_g=ffn_norm_g,
        wr_cat=jnp.concatenate([wr_hi, wr_lo], axis=1), br=br,
        w_e_gate=w_e_gate[0].astype(BF16), w_e_up=w_e_up[0].astype(BF16), w_e_down=w_e_down[0].astype(BF16),
        final_norm_g=final_norm_g[None, :])
    y_prompt = _encoder(x_prompt, prm, kf, tables_bf, nf)
    y_sample = _encoder(x_sample, prm, kf, tables_bf, nf)
    return (y_prompt, y_sample)
```

```python
import functools
import math

import jax
import jax.numpy as jnp
from jax import lax
from jax.experimental import pallas as pl
from jax.experimental.pallas import tpu as pltpu

F32 = jnp.float32
BF16 = jnp.bfloat16
U32 = jnp.uint32
I32 = jnp.int32

EPS = 1e-6
HY_ORDER = 2
HY_HEADS = 8
GM_HEADS = 8
GM_CHUNK = 128
N_GROUPS = 4
EXPERTS_PER_GROUP = 8
N_EXPERTS = N_GROUPS * EXPERTS_PER_GROUP
TOP_K = 2
FILTER_EMB = 5
DECAY_TARGET = 1e-2
FAST_DECAY_PCT = 0.3
SLOW_DECAY_PCT = 1.5

LANES = 128
ROUTE_LANES = LANES
SLAB = 16
CONV_LANES = 256
PHASE2_UNROLL = 8
ROW_TILE = 512
IN_A = 8
IN_SLABS = 4
EXPERT_ROWS = 1024
CHUNK = 8
STAGE_ROWS = ROW_TILE * TOP_K + N_EXPERTS * CHUNK
VMEM_LIMIT = 56 * 1024 * 1024


def _cparams(n_axes, vmem=None):
    return pltpu.CompilerParams(dimension_semantics=("arbitrary",) * n_axes,
                                vmem_limit_bytes=vmem)


def _inproj_kernel(x_ref, xp_ref, xn_ref, g_ref, w_ref, sw_ref, sb_ref, hy_ref, gm_ref, stage_ref,
                   *, nf, d_hy3, tiles_per_seq):
    i = pl.program_id(0)
    rows = x_ref.shape[0]
    pitch = stage_ref.shape[1] // IN_A
    halo = xp_ref.shape[0]
    x = jnp.concatenate([xp_ref[...], x_ref[...], xn_ref[...]], axis=0)
    ms = jnp.mean(x * x, axis=-1, keepdims=True)
    n = (x * lax.rsqrt(ms + EPS) * g_ref[...]).astype(BF16)
    gm = jnp.dot(n, w_ref[:, d_hy3:], preferred_element_type=F32)
    gm_ref[...] = gm[halo:halo + rows].astype(BF16)

    pos = i % tiles_per_seq
    has_prev = jnp.where(pos == 0, 0.0, 1.0)
    has_next = jnp.where(pos == tiles_per_seq - 1, 0.0, 1.0)
    row = lax.broadcasted_iota(I32, (rows, 1), 0)
    prev_w = jnp.where(row == 0, has_prev, 1.0)
    next_w = jnp.where(row == rows - 1, has_next, 1.0)
    n_slabs = stage_ref.shape[0]
    cols = n_slabs * LANES
    ext = rows + 2 * halo
    for c0 in range(0, d_hy3, cols):
        p = jnp.dot(n, w_ref[:, c0:c0 + cols], preferred_element_type=F32)
        prev = pltpu.roll(p, 1, 0)[halo:halo + rows] * prev_w
        nxt = pltpu.roll(p, ext - 1, 0)[halo:halo + rows] * next_w
        res = (sw_ref[0:1, c0:c0 + cols] * prev + sw_ref[1:2, c0:c0 + cols] * p[halo:halo + rows]
               + sw_ref[2:3, c0:c0 + cols] * nxt + sb_ref[:, c0:c0 + cols])
        for s in range(n_slabs):
            for al in range(IN_A):
                stage_ref[s, al * pitch:al * pitch + nf, :] = res[al * nf:(al + 1) * nf,
                                                                s * LANES:(s + 1) * LANES]
        for s in range(n_slabs):
            for b in range(nf):
                hy_ref[0, c0 // LANES + s, b] = stage_ref[s, pl.ds(b, IN_A, stride=pitch), :]


def _inproj(x2, g, w_bf, short_w, short_b, n_seq, seq_len, nf, d_hy3):
    t, d = x2.shape
    dp = w_bf.shape[1]
    rows = IN_A * nf
    tiles_per_seq = seq_len // rows
    hb = rows // 8
    nhb = t // 8
    full = lambda a: pl.BlockSpec(a.shape, lambda i: (0,) * a.ndim)
    return pl.pallas_call(
        functools.partial(_inproj_kernel, nf=nf, d_hy3=d_hy3, tiles_per_seq=tiles_per_seq),
        grid=(t // rows,),
        in_specs=[pl.BlockSpec((rows, d), lambda i: (i, 0)),
                  pl.BlockSpec((8, d), lambda i: (jnp.maximum(i * hb - 1, 0), 0)),
                  pl.BlockSpec((8, d), lambda i: (jnp.minimum((i + 1) * hb, nhb - 1), 0)),
                  full(g), full(w_bf), full(short_w), full(short_b)],
        out_specs=[pl.BlockSpec((1, d_hy3 // LANES, nf, IN_A, LANES),
                                lambda i: (i // tiles_per_seq, 0, 0, i % tiles_per_seq, 0)),
                   pl.BlockSpec((rows, dp - d_hy3), lambda i: (i, 0))],
        out_shape=[jax.ShapeDtypeStruct((n_seq, d_hy3 // LANES, nf, nf // 2, LANES), F32),
                   jax.ShapeDtypeStruct((t, dp - d_hy3), BF16)],
        scratch_shapes=[pltpu.VMEM((IN_SLABS, IN_A * (nf + 8), LANES), F32)],
        compiler_params=_cparams(1, VMEM_LIMIT),
        name="inproj_shortconv",
    )(x2, x2, x2, g, w_bf, short_w, short_b)


def _cos_sin(m, period):
    ang = m.astype(F32) * (2.0 * math.pi / period)
    return jnp.cos(ang), jnp.sin(ang)


def _stack_complex(mr, mi):
    top = jnp.concatenate([mr, -mi], axis=-1)
    bot = jnp.concatenate([mi, mr], axis=-1)
    return jnp.concatenate([top, bot], axis=-2)


def _dft_tables(nf):
    na = nf // 2
    n = nf * nf
    idx = jnp.arange(nf, dtype=I32)
    c, s = _cos_sin((idx[:, None] * idx[None, :]) % nf, nf)
    f2 = _stack_complex(c, -s)
    f2i = _stack_complex(c, s)
    b_ = idx[:, None, None]
    ka = idx[None, :, None]
    a_ = idx[None, None, :]
    m1 = (nf * a_ * ka + b_ * ka) % n
    c1, s1 = _cos_sin(m1, n)
    f1 = _stack_complex(c1[:, :, :na], -s1[:, :, :na])
    f1_full = jnp.concatenate([c1, -s1], axis=1)
    c3 = jnp.swapaxes(c1, 1, 2)[:, :na, :]
    s3 = jnp.swapaxes(s1, 1, 2)[:, :na, :]
    f3 = _stack_complex(c3, s3)
    return f1, f2, f2i, f3, f1_full


def _split_bf16(x):
    hi = x.astype(BF16)
    lo = (x - hi.astype(F32)).astype(BF16)
    return hi, lo


def _stack_split_lhs(a):
    hi, lo = _split_bf16(a)
    return jnp.concatenate([hi, lo, hi], axis=-1)


def _stack_split_rhs(b):
    hi, lo = _split_bf16(b)
    return jnp.concatenate([hi, hi, lo], axis=-2)


def _filter_taps_kernel(wemb_ref, bemb_ref, win_ref, bin_ref, freq_ref, wout_ref, delta_ref, o_ref,
                        *, seq_len, n_inner, d_hy):
    rows = o_ref.shape[1]
    j = pl.program_id(0) * rows + lax.broadcasted_iota(I32, (rows, 1), 0)
    tidx = jnp.where(j < seq_len, j, 2 * seq_len - j).astype(F32)
    t = tidx / float(seq_len - 1)
    fr0 = jnp.float32(1e-4)
    ang0 = (2.0 * math.pi / seq_len) * tidx * fr0
    ang1 = (2.0 * math.pi / seq_len) * tidx
    freq = freq_ref[...]
    pre = (t * wemb_ref[0:1, :] + jnp.cos(ang0) * wemb_ref[1:2, :] + jnp.cos(ang1) * wemb_ref[2:3, :]
           - jnp.sin(ang0) * wemb_ref[3:4, :] - jnp.sin(ang1) * wemb_ref[4:5, :] + bemb_ref[...])
    hdn = jnp.sin(freq * pre)
    for i in range(n_inner):
        hdn = jnp.sin(freq * (jnp.dot(_stack_split_lhs(hdn), win_ref[i], preferred_element_type=F32)
                              + bin_ref[i:i + 1, :]))
    h = jnp.dot(_stack_split_lhs(hdn), wout_ref[...], preferred_element_type=F32)
    decay = jnp.exp(-t * delta_ref[...])
    for o in range(HY_ORDER):
        fwd = h[:, (2 * o) * d_hy:(2 * o + 1) * d_hy]
        bwd = h[:, (2 * o + 1) * d_hy:(2 * o + 2) * d_hy]
        val = jnp.where(j == seq_len, 0.0, jnp.where(j < seq_len, fwd, bwd) * decay)
        for cc in range(d_hy // LANES):
            o_ref[o * (d_hy // LANES) + cc] = val[:, cc * LANES:(cc + 1) * LANES]


def _filter_taps(wemb, bemb, win, bin_, freq, wout, deltas, seq_len, d_hy):
    rows = 1024
    n2 = 2 * seq_len
    width = wemb.shape[1]
    n_inner = win.shape[0]
    full = lambda *shape: pl.BlockSpec(shape, lambda i: (0,) * len(shape))
    return pl.pallas_call(
        functools.partial(_filter_taps_kernel, seq_len=seq_len, n_inner=n_inner, d_hy=d_hy),
        grid=(n2 // rows,),
        in_specs=[full(FILTER_EMB, width), full(1, width), full(n_inner, 3 * width, width),
                  full(n_inner, width), full(1, width), full(3 * width, 2 * HY_ORDER * d_hy),
                  full(1, d_hy)],
        out_specs=pl.BlockSpec((HY_ORDER * d_hy // LANES, rows, LANES), lambda i: (0, i, 0)),
        out_shape=jax.ShapeDtypeStruct((HY_ORDER * d_hy // LANES, n2, LANES), F32),
        compiler_params=_cparams(1, VMEM_LIMIT),
        name="filter_taps",
    )(wemb, bemb, _stack_split_rhs(win), bin_, freq, _stack_split_rhs(wout), deltas)


def _filter_fft1_kernel(k_ref, f_ref, o_ref, *, nf):
    halves = k_ref.shape[0]
    k = k_ref.reshape(halves, nf * SLAB, LANES)
    g = o_ref.reshape(halves, 2, nf * SLAB, LANES)
    for bl in range(SLAB):
        col = jnp.concatenate([k[h, pl.ds(bl, nf, stride=SLAB), :] for h in range(halves)], axis=1)
        res = jnp.dot(f_ref[bl], _stack_split_rhs(col), preferred_element_type=F32)
        for h in range(halves):
            for ri in range(2):
                g[h, ri, pl.ds(bl, nf, stride=SLAB), :] = res[ri * nf:(ri + 1) * nf,
                                                              h * LANES:(h + 1) * LANES]


def _filter_fft2_kernel(g_ref, f_ref, skip_ref, o_ref, *, nf, scale):
    halves = g_ref.shape[0]
    lag0 = jnp.concatenate([jnp.broadcast_to(skip_ref[...], (nf, skip_ref.shape[1])),
                            jnp.zeros((nf, skip_ref.shape[1]), F32)], axis=0)
    for kl in range(SLAB):
        rhs = jnp.concatenate([jnp.concatenate([g_ref[h, ri, kl] for h in range(halves)], axis=1)
                               for ri in range(2)], axis=0)
        res = jnp.dot(f_ref[...], _stack_split_rhs(rhs), preferred_element_type=F32)
        o_ref[kl] = ((res + lag0) * scale).astype(o_ref.dtype)


def _filter_spectrum(taps, skip, f1_full, f2, nf):
    nch = taps.shape[0]
    c = nch * LANES
    halves = CONV_LANES // LANES
    k4 = taps.reshape(nch, nf, nf, LANES)
    g = pl.pallas_call(
        functools.partial(_filter_fft1_kernel, nf=nf),
        grid=(nf // SLAB, nch // halves),
        in_specs=[pl.BlockSpec((halves, nf, SLAB, LANES), lambda i, j: (j, 0, i, 0)),
                  pl.BlockSpec((SLAB, 2 * nf, 3 * nf), lambda i, j: (i, 0, 0))],
        out_specs=pl.BlockSpec((halves, 2, nf, SLAB, LANES), lambda i, j: (j, 0, 0, i, 0)),
        out_shape=jax.ShapeDtypeStruct((nch, 2, nf, nf, LANES), F32),
        compiler_params=_cparams(2, VMEM_LIMIT),
        name="filter_fft1",
    )(k4, _stack_split_lhs(f1_full))
    return pl.pallas_call(
        functools.partial(_filter_fft2_kernel, nf=nf, scale=1.0 / (nf * nf)),
        grid=(nf // SLAB, nch // halves),
        in_specs=[pl.BlockSpec((halves, 2, SLAB, nf, LANES), lambda i, j: (j, 0, i, 0, 0)),
                  pl.BlockSpec((2 * nf, 6 * nf), lambda i, j: (0, 0)),
                  pl.BlockSpec((1, CONV_LANES), lambda i, j: (0, j))],
        out_specs=pl.BlockSpec((SLAB, 2 * nf, CONV_LANES), lambda i, j: (i, 0, j)),
        out_shape=jax.ShapeDtypeStruct((nf, 2 * nf, c), BF16),
        compiler_params=_cparams(2, VMEM_LIMIT),
        name="filter_fft2",
    )(g, _stack_split_lhs(f2), skip)


def _pack(re, im):
    rb = lax.bitcast_convert_type(re.astype(BF16).astype(F32), U32)
    ib = lax.bitcast_convert_type(im.astype(BF16).astype(F32), U32)
    return rb | (ib >> 16)


def _unpack(w):
    re = lax.bitcast_convert_type(w & jnp.uint32(0xFFFF0000), F32).astype(BF16)
    im = lax.bitcast_convert_type(w << 16, F32).astype(BF16)
    return re, im


def _longconv_kernel(z1_ref, f1_ref, kf_ref, f2_ref, f2i_ref, f3_ref, gate_ref, o_ref, g_ref,
                     *, nf, out_bmajor):
    na = nf // 2
    ns = nf // SLAB
    halves = g_ref.shape[1]
    t = pl.program_id(2)

    def load_rows(ref5, s, bl):
        return jnp.concatenate([ref5[s, h, bl] for h in range(halves)], axis=1)

    def store_col(blk, bl, val):
        for h in range(halves):
            g_ref[blk, h, pl.ds(bl, nf, stride=SLAB), :] = val[:, h * LANES:(h + 1) * LANES]

    def load_col(blk, bl):
        return jnp.concatenate([g_ref[blk, h, pl.ds(bl, nf, stride=SLAB), :] for h in range(halves)],
                               axis=1)

    @pl.when(t < ns)
    def _phase1():
        for bl in range(SLAB):
            rhs = jnp.concatenate([load_rows(z1_ref, 0, bl), load_rows(z1_ref, 1, bl)],
                                  axis=0).astype(BF16)
            out = jnp.dot(f1_ref[bl], rhs, preferred_element_type=F32)
            store_col(t, bl, _pack(out[:nf], out[nf:]))

    @pl.when(jnp.logical_and(t >= ns, t < 2 * ns))
    def _phase2():
        i = t - ns

        def load_slab(ka):
            row0 = pl.multiple_of(ka * SLAB, SLAB)
            return jnp.concatenate([g_ref[:, h, pl.ds(row0, SLAB), :].reshape(nf, LANES)
                                    for h in range(halves)], axis=1)

        def transform(w, kl):
            re, im = _unpack(w)
            s = jnp.dot(f2_ref[...], jnp.concatenate([re, im], axis=0),
                        preferred_element_type=F32)
            sr, si = s[:nf], s[nf:]
            kr, ki = kf_ref[kl, :nf, :], kf_ref[kl, nf:, :]
            pr = (sr * kr - si * ki).astype(BF16)
            pi = (sr * ki + si * kr).astype(BF16)
            h = jnp.dot(f2i_ref[...], jnp.concatenate([pr, pi], axis=0),
                        preferred_element_type=F32)
            return _pack(h[:nf], h[nf:])

        def store_slab(ka, packed):
            row0 = pl.multiple_of(ka * SLAB, SLAB)
            for hf in range(halves):
                g_ref[:, hf, pl.ds(row0, SLAB), :] = packed[:, hf * LANES:(hf + 1) * LANES].reshape(
                    ns, SLAB, LANES)

        def body(grp, carry):
            kls = [grp * PHASE2_UNROLL + u for u in range(PHASE2_UNROLL)]
            ws = [load_slab(i * SLAB + kl) for kl in kls]
            outs = [transform(w, kl) for w, kl in zip(ws, kls)]
            for kl, packed in zip(kls, outs):
                store_slab(i * SLAB + kl, packed)
            return carry

        lax.fori_loop(0, SLAB // PHASE2_UNROLL, body, 0)

    @pl.when(t >= 2 * ns)
    def _phase3():
        j = t - 2 * ns
        out_flat = None if out_bmajor else o_ref.reshape(2, halves, na * SLAB, LANES)
        for bl in range(SLAB):
            re, im = _unpack(load_col(j, bl))
            y = jnp.dot(f3_ref[bl], jnp.concatenate([re, im], axis=0),
                        preferred_element_type=F32)
            for s in range(2):
                val = load_rows(gate_ref, s, bl) * y[s * na:(s + 1) * na]
                for h in range(halves):
                    piece = val[:, h * LANES:(h + 1) * LANES]
                    if out_bmajor:
                        o_ref[s, h, bl] = piece.astype(o_ref.dtype)
                    else:
                        out_flat[s, h, pl.ds(bl, na, stride=SLAB), :] = piece


def _longconv(gates5, z5, kf, tables, c, gate_blk, z_blk, kf_col, nf, out_bmajor):
    f1, f2, f2i, f3 = tables
    b, _, _, na, _ = z5.shape
    ns = nf // SLAB
    halves = CONV_LANES // LANES
    nchunk = c // CONV_LANES
    p1 = lambda t: jnp.minimum(t, ns - 1)
    p2 = lambda t: jnp.clip(t - ns, 0, ns - 1)
    p3 = lambda t: jnp.clip(t - 2 * ns, 0, ns - 1)
    cl = CONV_LANES
    blk5 = (2, halves, SLAB, na, LANES)
    if out_bmajor:
        out_spec = pl.BlockSpec(blk5, lambda p, q, t: (p, q, p3(t), 0, 0))
        out_shape = (b, c // LANES, nf, na, LANES)
    else:
        out_spec = pl.BlockSpec((2, halves, na, SLAB, LANES), lambda p, q, t: (p, q, 0, p3(t), 0))
        out_shape = (b, c // LANES, na, nf, LANES)
    return pl.pallas_call(
        functools.partial(_longconv_kernel, nf=nf, out_bmajor=out_bmajor),
        grid=(b // 2, nchunk, 3 * ns),
        in_specs=[
            pl.BlockSpec(blk5, lambda p, q, t: (p, z_blk + q, p1(t), 0, 0)),
            pl.BlockSpec((SLAB, 2 * nf, 2 * na), lambda p, q, t: (p1(t), 0, 0)),
            pl.BlockSpec((SLAB, 2 * nf, cl), lambda p, q, t: (p2(t), 0, kf_col + q)),
            pl.BlockSpec((2 * nf, 2 * nf), lambda p, q, t: (0, 0)),
            pl.BlockSpec((2 * nf, 2 * nf), lambda p, q, t: (0, 0)),
            pl.BlockSpec((SLAB, 2 * na, 2 * nf), lambda p, q, t: (p3(t), 0, 0)),
            pl.BlockSpec(blk5, lambda p, q, t: (p, gate_blk + q, p3(t), 0, 0)),
        ],
        out_specs=out_spec,
        out_shape=jax.ShapeDtypeStruct(out_shape, BF16 if out_bmajor else F32),
        scratch_shapes=[pltpu.VMEM((ns, halves, nf * SLAB, LANES), U32)],
        compiler_params=_cparams(3, VMEM_LIMIT),
        name="longconv",
    )(z5, f1, kf, f2, f2i, f3, gates5)


def _gelu_exact(x):
    return 0.5 * x * (1.0 + lax.erf(x * (1.0 / math.sqrt(2.0))))


def _gmlp_kernel(gm_ref, lng_ref, lnb_ref, ws_ref, bias_ref, o_ref, *, d_gm):
    g = _gelu_exact(gm_ref[...].astype(F32))
    u, v = g[:, :d_gm], g[:, d_gm:]
    mu = jnp.mean(v, axis=-1, keepdims=True)
    vc = v - mu
    var = jnp.mean(vc * vc, axis=-1, keepdims=True)
    vh = (vc * lax.rsqrt(var + EPS) * lng_ref[...] + lnb_ref[...]).astype(BF16)
    head = lax.broadcasted_iota(I32, (GM_CHUNK, d_gm), 1) // (d_gm // GM_HEADS)
    for c in range(gm_ref.shape[0] // GM_CHUNK):
        rows = slice(c * GM_CHUNK, (c + 1) * GM_CHUNK)
        r = jnp.dot(ws_ref[...], vh[rows], preferred_element_type=F32)
        s = r[:GM_CHUNK]
        for h in range(1, GM_HEADS):
            s = jnp.where(head == h, r[h * GM_CHUNK:(h + 1) * GM_CHUNK], s)
        o_ref[rows, :] = u[rows] * (s + bias_ref[...])


def _gmlp(gm, ln_g, ln_b, ws_stack, bias_t):
    t, c2 = gm.shape
    d_gm = c2 // 2
    return pl.pallas_call(
        functools.partial(_gmlp_kernel, d_gm=d_gm),
        grid=(t // ROW_TILE,),
        in_specs=[pl.BlockSpec((ROW_TILE, c2), lambda i: (i, 0)),
                  pl.BlockSpec((1, d_gm), lambda i: (0, 0)),
                  pl.BlockSpec((1, d_gm), lambda i: (0, 0)),
                  pl.BlockSpec(ws_stack.shape, lambda i: (0, 0)),
                  pl.BlockSpec(bias_t.shape, lambda i: (0, 0))],
        out_specs=pl.BlockSpec((ROW_TILE, d_gm), lambda i: (i, 0)),
        out_shape=jax.ShapeDtypeStruct((t, d_gm), F32),
        compiler_params=_cparams(1, VMEM_LIMIT),
        name="gmlp",
    )(gm, ln_g, ln_b, ws_stack, bias_t)


def _head_rms(y, bd):
    ms = jnp.dot((y * y).astype(BF16), bd, preferred_element_type=F32)
    return y * lax.rsqrt(ms + EPS)


def _route(logits):
    lane = lax.broadcasted_iota(I32, logits.shape, 1)
    neg = jnp.float32(-1e30)
    big = jnp.int32(ROUTE_LANES)
    gmask = lane < N_GROUPS
    gl = jnp.where(gmask, logits, neg)
    gmax = jnp.max(gl, axis=-1, keepdims=True)
    grp = jnp.min(jnp.where(jnp.logical_and(gl == gmax, gmask), lane, big), axis=-1, keepdims=True)
    psum = jnp.sum(jnp.where(gmask, jnp.exp(gl - gmax), 0.0), axis=-1, keepdims=True)
    p_grp = 1.0 / psum
    lo = N_GROUPS + EXPERTS_PER_GROUP * grp
    emask = jnp.logical_and(lane >= lo, lane < lo + EXPERTS_PER_GROUP)
    el = jnp.where(emask, logits, neg)
    m1 = jnp.max(el, axis=-1, keepdims=True)
    i1 = jnp.min(jnp.where(jnp.logical_and(el == m1, emask), lane, big), axis=-1, keepdims=True)
    emask2 = jnp.logical_and(emask, lane != i1)
    el2 = jnp.where(emask2, logits, neg)
    m2 = jnp.max(el2, axis=-1, keepdims=True)
    i2 = jnp.min(jnp.where(jnp.logical_and(el2 == m2, emask2), lane, big), axis=-1, keepdims=True)
    d = jnp.exp(m2 - m1)
    g1 = p_grp * (1.0 / (1.0 + d))
    g2 = p_grp * (d / (1.0 + d))
    e1 = (i1 - N_GROUPS).astype(F32)
    e2 = (i2 - N_GROUPS).astype(F32)
    return jnp.where(lane == 0, e1, jnp.where(lane == 1, e2, jnp.where(lane == 2, g1,
                     jnp.where(lane == 3, g2, 0.0))))


def _lane_cumsum(v):
    lane = lax.broadcasted_iota(I32, v.shape, 1)
    sh = 1
    while sh < v.shape[1]:
        v = v + jnp.where(lane >= sh, pltpu.roll(v, sh, 1), 0.0)
        sh *= 2
    return v


def _outproj_kernel(x_ref, yh_ref, yg_ref, mg_ref, bd_ref, wo_ref, fg_ref, wr_ref, br_ref,
                    x1_ref, stage_ref, route_ref, count_ref, *, d_hy):
    bd = bd_ref[...]
    mg = mg_ref[...]
    yh = jnp.concatenate([yh_ref[0, c] for c in range(yh_ref.shape[1])], axis=1)
    mh = (_head_rms(yh, bd) * mg[:, :d_hy]).astype(BF16)
    mgm = (_head_rms(yg_ref[...], bd) * mg[:, d_hy:]).astype(BF16)
    wo = wo_ref[...]
    x1 = (x_ref[...] + jnp.dot(mh, wo[:d_hy], preferred_element_type=F32)
          + jnp.dot(mgm, wo[d_hy:], preferred_element_type=F32))
    x1_ref[...] = x1
    ms = jnp.mean(x1 * x1, axis=-1, keepdims=True)
    n2 = x1 * lax.rsqrt(ms + EPS) * fg_ref[...]
    hi, lo = _split_bf16(n2)
    wr = wr_ref[...]
    hw = jnp.dot(hi, wr, preferred_element_type=F32)
    logits = (hw[:, :ROUTE_LANES] + hw[:, ROUTE_LANES:]
              + jnp.dot(lo, wr[:, :ROUTE_LANES], preferred_element_type=F32) + br_ref[...])
    route = _route(logits)

    rows = route.shape[0]
    lane = lax.broadcasted_iota(I32, route.shape, 1)
    e0 = route[:, 0:1].astype(I32)
    e1 = route[:, 1:2].astype(I32)
    oh0 = lane == e0
    oh1 = lane == e1
    oh = jnp.where(jnp.logical_or(oh0, oh1), 1.0, 0.0)
    r = lax.broadcasted_iota(I32, (rows, rows), 0)
    c = lax.broadcasted_iota(I32, (rows, rows), 1)
    ltri = jnp.where(c < r, 1.0, 0.0).astype(BF16)
    before = jnp.dot(ltri, oh.astype(BF16), preferred_element_type=F32)
    cnt = jnp.sum(oh, axis=0, keepdims=True)
    run = jnp.floor((cnt + (CHUNK - 1)) * (1.0 / CHUNK)) * CHUNK
    run_start = _lane_cumsum(jnp.broadcast_to(run, (8, run.shape[1])))[0:1] - run
    tot = before + run_start
    loc0 = jnp.sum(jnp.where(oh0, tot, 0.0), axis=-1, keepdims=True)
    loc1 = jnp.sum(jnp.where(oh1, tot, 0.0), axis=-1, keepdims=True)
    route = jnp.where(lane == 4, loc0, jnp.where(lane == 5, loc1, route))
    route_ref[...] = route
    count_ref[pl.ds(pl.program_id(0), 1), :] = cnt

    rt = jnp.transpose(route)
    srow = lax.broadcasted_iota(I32, (STAGE_ROWS, rows), 0)
    p0 = srow == rt[4:5, :].astype(I32)
    p1 = srow == rt[5:6, :].astype(I32)
    perm = jnp.where(jnp.logical_or(p0, p1), 1.0, 0.0).astype(BF16)
    staged = jnp.dot(perm, hi, preferred_element_type=F32)
    gate = jnp.sum(jnp.where(p0, rt[2:3, :], 0.0) + jnp.where(p1, rt[3:4, :], 0.0), axis=-1,
                   keepdims=True)
    half = staged.shape[1] // 2
    mlane = lax.broadcasted_iota(I32, (STAGE_ROWS, LANES), 1)
    meta = jnp.where(mlane == 0, lax.bitcast_convert_type(jnp.broadcast_to(gate, (STAGE_ROWS, LANES)), U32),
                     jnp.uint32(0))
    stage_ref[...] = jnp.concatenate([_pack(staged[:, :half], staged[:, half:]), meta], axis=1)


def _outproj(x2, yh, yg, mix_g, bd, wo_bf, ffn_g, wr_cat, br):
    t, d = x2.shape
    _, nch, seq_len, _ = yh.shape
    d_hy = nch * LANES
    tiles_per_seq = seq_len // ROW_TILE
    n_tiles = t // ROW_TILE
    row = lambda w: pl.BlockSpec((ROW_TILE, w), lambda i: (i, 0))
    full = lambda a: pl.BlockSpec(a.shape, lambda i: (0,) * a.ndim)
    sw = d // 2 + LANES
    return pl.pallas_call(
        functools.partial(_outproj_kernel, d_hy=d_hy),
        grid=(n_tiles,),
        in_specs=[row(d),
                  pl.BlockSpec((1, nch, ROW_TILE, LANES),
                               lambda i: (i // tiles_per_seq, 0, i % tiles_per_seq, 0)),
                  row(yg.shape[1]), full(mix_g), full(bd), full(wo_bf), full(ffn_g),
                  full(wr_cat), full(br)],
        out_specs=[row(d), pl.BlockSpec((STAGE_ROWS, sw), lambda i: (i, 0)), row(ROUTE_LANES),
                   pl.BlockSpec((n_tiles, ROUTE_LANES), lambda i: (0, 0))],
        out_shape=[jax.ShapeDtypeStruct((t, d), F32),
                   jax.ShapeDtypeStruct((n_tiles * STAGE_ROWS, sw), U32),
                   jax.ShapeDtypeStruct((t, ROUTE_LANES), F32),
                   jax.ShapeDtypeStruct((n_tiles, ROUTE_LANES), F32)],
        compiler_params=_cparams(1, VMEM_LIMIT),
        name="outproj_router",
    )(x2, yh, yg, mix_g, bd, wo_bf, ffn_g, wr_cat, br)


def _moe_plan(tile_cnt, n_tokens):
    n_tiles = tile_cnt.shape[0]
    cnt = tile_cnt[:, :N_EXPERTS].astype(I32)
    run = (cnt + CHUNK - 1) // CHUNK * CHUNK
    run_start = jnp.cumsum(run, axis=1) - run
    used = jnp.sum(run, axis=1)
    e_rows = jnp.sum(run, axis=0)
    e_pad = (e_rows + EXPERT_ROWS - 1) // EXPERT_ROWS * EXPERT_ROWS
    e_end = jnp.cumsum(e_pad)
    e_start = e_end - e_pad
    n_blocks = -(-(n_tokens * TOP_K + n_tiles * N_EXPERTS * (CHUNK - 1)) // EXPERT_ROWS) + N_EXPERTS
    n_used = (e_end[-1:] // EXPERT_ROWS).astype(I32)
    block_row = jnp.arange(n_blocks, dtype=I32) * EXPERT_ROWS
    block_e = jnp.minimum(jnp.sum((e_end[None, :] <= block_row[:, None]).astype(I32), axis=1),
                          N_EXPERTS - 1)
    run_end_in_e = jnp.cumsum(run, axis=0)
    chunk_row = jnp.arange(n_blocks * EXPERT_ROWS // CHUNK, dtype=I32) * CHUNK
    e_of_chunk = jnp.repeat(block_e, EXPERT_ROWS // CHUNK)
    e_sel = jnp.arange(N_EXPERTS, dtype=I32)[None, :] == e_of_chunk[:, None]
    pick = lambda tbl: jnp.dot(e_sel.astype(F32), tbl.astype(F32),
                               precision=lax.Precision.HIGHEST).astype(I32)
    q = chunk_row - pick(e_start[:, None])[:, 0]
    valid = q < pick(e_rows[:, None])[:, 0]
    ends = pick(run_end_in_e.T)
    tile = jnp.minimum(jnp.sum((ends <= q[:, None]).astype(I32), axis=1), n_tiles - 1)
    t_sel = jnp.arange(n_tiles, dtype=I32)[None, :] == tile[:, None]
    at_tile = lambda tbl: jnp.sum(jnp.where(t_sel, pick(tbl.T), 0), axis=1)
    within = q - (at_tile(run_end_in_e) - at_tile(run))
    src_row = tile * STAGE_ROWS + at_tile(run_start) + within
    zero_chunk = STAGE_ROWS // CHUNK - 1
    per_block = EXPERT_ROWS // CHUNK
    src_chunk = jnp.where(valid, src_row // CHUNK, zero_chunk).reshape(n_blocks, 1, per_block)
    dst_chunk = jnp.where(valid, src_row // CHUNK, 0).reshape(n_blocks, 1, per_block)
    n_valid = jnp.sum(valid.reshape(n_blocks, per_block).astype(I32), axis=1)
    return (block_e.astype(I32), n_used, n_valid.astype(I32), src_chunk.astype(I32),
            dst_chunk.astype(I32), used.astype(I32))


def _ffn_kernel(be_ref, nu_ref, nv_ref, used_ref, src_now_ref, src_next_ref, dst_ref, stage_ref,
                wg_ref, wu_ref, wd_ref, ost_ref, xbuf, obuf, zero_ref, in_sem, out_sem, zero_sem):
    del be_ref
    b = pl.program_id(0)
    nu = nu_ref[0]
    per_block = dst_ref.shape[-1]
    slot = b % 2

    def fetch(map_ref, s):
        def body(j, carry):
            row = pl.multiple_of(map_ref[0, 0, j] * CHUNK, CHUNK)
            pltpu.make_async_copy(stage_ref.at[pl.ds(row, CHUNK)], xbuf.at[s, pl.ds(j * CHUNK, CHUNK)],
                                  in_sem.at[s]).start()
            return carry
        lax.fori_loop(0, per_block, body, 0, unroll=8)

    def out_copy(j, chunk, s):
        row = pl.multiple_of(chunk * CHUNK, CHUNK)
        return pltpu.make_async_copy(obuf.at[s, pl.ds(j * CHUNK, CHUNK)], ost_ref.at[pl.ds(row, CHUNK)],
                                     out_sem.at[s])

    def drain(s, n):
        @pl.when(n == per_block)
        def _():
            pltpu.make_async_copy(obuf.at[s], ost_ref.at[pl.ds(0, EXPERT_ROWS)], out_sem.at[s]).wait()

        @pl.when(n != per_block)
        def _():
            def body(j, carry):
                out_copy(0, 0, s).wait()
                return carry
            lax.fori_loop(0, n, body, 0)

    def zero_tails(wait):
        def tile(i, carry):
            def chunk(c, carry2):
                row = pl.multiple_of(i * STAGE_ROWS + c * CHUNK, CHUNK)
                cp = pltpu.make_async_copy(zero_ref, ost_ref.at[pl.ds(row, CHUNK)], zero_sem)
                if wait:
                    cp.wait()
                else:
                    cp.start()
                return carry2
            return lax.fori_loop(used_ref[i] // CHUNK, STAGE_ROWS // CHUNK, chunk, carry)
        lax.fori_loop(0, used_ref.shape[0], tile, 0)

    @pl.when(b == 0)
    def _prologue():
        fetch(src_now_ref, 0)
        zero_ref[...] = jnp.zeros_like(zero_ref)
        zero_tails(wait=False)
        zero_tails(wait=True)

    @pl.when(b + 1 < nu)
    def _prefetch_next():
        fetch(src_next_ref, 1 - slot)

    @pl.when(b < nu)
    def _compute():
        pltpu.make_async_copy(stage_ref.at[pl.ds(0, EXPERT_ROWS)], xbuf.at[slot], in_sem.at[slot]).wait()

        @pl.when(b >= 2)
        def _():
            drain(slot, nv_ref[b - 2])

        x = xbuf[slot]
        half = wg_ref.shape[1] // 2
        xb = jnp.concatenate(_unpack(x[:, :half]), axis=1)
        gate = lax.bitcast_convert_type(x[:, half:half + 1], F32)
        g = jnp.dot(xb, wg_ref[0], preferred_element_type=F32)
        u = jnp.dot(xb, wu_ref[0], preferred_element_type=F32)
        h = (g * (1.0 / (1.0 + jnp.exp(-g))) * u).astype(BF16)
        o = jnp.dot(h, wd_ref[0], preferred_element_type=F32) * gate
        obuf[slot] = _pack(o[:, :half], o[:, half:])

        def body(j, carry):
            out_copy(j, dst_ref[0, 0, j], slot).start()
            return carry

        @pl.when(nv_ref[b] == per_block)
        def _():
            lax.fori_loop(0, per_block, body, 0, unroll=8)

        @pl.when(nv_ref[b] != per_block)
        def _():
            lax.fori_loop(0, nv_ref[b], body, 0)

        @pl.when(b == nu - 1)
        def _():
            drain(slot, nv_ref[b])

            @pl.when(b >= 1)
            def _():
                drain(1 - slot, nv_ref[b - 1])


def _expert_ffn(stage, plan, wg, wu, wd):
    block_e, n_used, n_valid, src_chunk, dst_chunk, used = plan
    nb = block_e.shape[0]
    sw = stage.shape[1]
    d, de = wg.shape[1], wg.shape[2]
    per_block = EXPERT_ROWS // CHUNK
    cur = lambda i, be, nu: jnp.minimum(i, nu[0] - 1)
    wspec = lambda shape: pl.BlockSpec(shape, lambda i, be, nu, nv, us: (be[cur(i, be, nu)], 0, 0))
    smem = lambda imap: pl.BlockSpec((1, 1, per_block), imap, memory_space=pltpu.SMEM)
    return pl.pallas_call(
        _ffn_kernel,
        grid_spec=pltpu.PrefetchScalarGridSpec(
            num_scalar_prefetch=4,
            grid=(nb,),
            in_specs=[smem(lambda i, be, nu, nv, us: (i, 0, 0)),
                      smem(lambda i, be, nu, nv, us: (jnp.minimum(i + 1, nb - 1), 0, 0)),
                      smem(lambda i, be, nu, nv, us: (i, 0, 0)),
                      pl.BlockSpec(memory_space=pl.ANY),
                      wspec((1, d, de)), wspec((1, d, de)), wspec((1, de, d))],
            out_specs=pl.BlockSpec(memory_space=pl.ANY),
            scratch_shapes=[pltpu.VMEM((2, EXPERT_ROWS, sw), U32), pltpu.VMEM((2, EXPERT_ROWS, d // 2), U32),
                            pltpu.VMEM((CHUNK, d // 2), U32),
                            pltpu.SemaphoreType.DMA((2,)), pltpu.SemaphoreType.DMA((2,)),
                            pltpu.SemaphoreType.DMA(())]),
        out_shape=jax.ShapeDtypeStruct((stage.shape[0], d // 2), U32),
        compiler_params=_cparams(1, VMEM_LIMIT),
        name="expert_ffn",
    )(block_e, n_used, n_valid, used, src_chunk, src_chunk, dst_chunk, stage, wg, wu, wd)


def _combine_kernel(ost_ref, route_ref, x1_ref, fg_ref, o_ref):
    route = route_ref[...]
    rows = route.shape[0]
    col = lax.broadcasted_iota(I32, (rows, STAGE_ROWS), 1)
    sel = jnp.logical_or(col == route[:, 4:5].astype(I32), col == route[:, 5:6].astype(I32))
    perm = jnp.where(sel, 1.0, 0.0).astype(BF16)
    y = jnp.dot(perm, jnp.concatenate(_unpack(ost_ref[...]), axis=1), preferred_element_type=F32)
    x2 = x1_ref[...] + y
    ms = jnp.mean(x2 * x2, axis=-1, keepdims=True)
    o_ref[...] = x2 * lax.rsqrt(ms + EPS) * fg_ref[...]


def _combine(ostage, route, x1, final_g):
    t, d = x1.shape
    return pl.pallas_call(
        _combine_kernel,
        grid=(t // ROW_TILE,),
        in_specs=[pl.BlockSpec((STAGE_ROWS, d // 2), lambda i: (i, 0)),
                  pl.BlockSpec((ROW_TILE, ROUTE_LANES), lambda i: (i, 0)),
                  pl.BlockSpec((ROW_TILE, d), lambda i: (i, 0)),
                  pl.BlockSpec((1, d), lambda i: (0, 0))],
        out_specs=pl.BlockSpec((ROW_TILE, d), lambda i: (i, 0)),
        out_shape=jax.ShapeDtypeStruct((t, d), F32),
        compiler_params=_cparams(1, VMEM_LIMIT),
        name="moe_combine",
    )(ostage, route, x1, final_g)


def _hier_moe_and_norm(x1, stage, route, tile_cnt, wg, wu, wd, final_g):
    t, d = x1.shape
    ostage = _expert_ffn(stage, _moe_plan(tile_cnt, t), wg, wu, wd)
    return _combine(ostage, route, x1, final_g)


def _encoder(x, prm, kf, tables, nf):
    b, l, d = x.shape
    t = b * l
    d_hy = prm["skip"].shape[1]
    x2 = x.reshape(t, d)
    gates5, gm = _inproj(x2, prm["mix_norm_g"], prm["w_in"], prm["short_w"], prm["short_b"], b, l, nf,
                         (HY_ORDER + 1) * d_hy)
    z5 = gates5
    z_blk = HY_ORDER * d_hy // CONV_LANES
    for o in range(HY_ORDER):
        z5 = _longconv(gates5, z5, kf, tables, d_hy, gate_blk=o * d_hy // CONV_LANES,
                       z_blk=z_blk, kf_col=o * d_hy // CONV_LANES, nf=nf, out_bmajor=o < HY_ORDER - 1)
        z_blk = 0
    y_hy = z5.reshape(b, d_hy // LANES, l, LANES)
    y_gm = _gmlp(gm, prm["ln_g"], prm["ln_b"], prm["ws_stack"], prm["bias_t"])
    x1, stage, route, tile_cnt = _outproj(x2, y_hy, y_gm, prm["mix_out_g"], prm["bd"], prm["w_out"], prm["ffn_norm_g"],
                             prm["wr_cat"], prm["br"])
    out = _hier_moe_and_norm(x1, stage, route, tile_cnt, prm["w_e_gate"], prm["w_e_up"], prm["w_e_down"],
                             prm["final_norm_g"])
    return out.reshape(b, l, d)


def kernel(x_prompt, x_sample, mix_norm_g, w_in, hy_short_w, hy_short_b, hy_filt_w_emb, hy_filt_b_emb,
           hy_filt_w_inner, hy_filt_b_inner, hy_filt_freq, hy_filt_w_out, hy_skip, gm_ln_g, gm_ln_b,
           gm_w_s, gm_b_s, mix_out_g, w_out, ffn_norm_g, w_group, b_group, w_expert_router,
           b_expert_router, w_e_gate, w_e_up, w_e_down, final_norm_g):
    assert w_in.shape[0] == 1, "one layer"
    l = x_prompt.shape[1]
    assert x_sample.shape[1] == l
    nf = math.isqrt(2 * l)
    assert nf * nf == 2 * l and nf % SLAB == 0
    d = x_prompt.shape[2]
    d_hy = hy_skip.shape[2]
    d_gm = gm_ln_g.shape[1]
    head_dim = d_gm // GM_HEADS
    assert d_hy // HY_HEADS == head_dim and d_hy == d_gm

    tables = _dft_tables(nf)
    f1, f2, f2i, f3, f1_full = tables
    tables_bf = tuple(a.astype(BF16) for a in (f1, f2, f2i, f3))

    max_decay = math.log(DECAY_TARGET) / FAST_DECAY_PCT
    min_decay = math.log(DECAY_TARGET) / SLOW_DECAY_PCT
    deltas = jnp.abs(jnp.linspace(min_decay, max_decay, d_hy, dtype=F32))[None, :]
    taps = _filter_taps(hy_filt_w_emb[0], hy_filt_b_emb[0][None, :], hy_filt_w_inner[0], hy_filt_b_inner[0],
                        hy_filt_freq[0][None, :], hy_filt_w_out[0], deltas, l, d_hy)
    kf = _filter_spectrum(taps, hy_skip[0].reshape(1, HY_ORDER * d_hy), f1_full, f2, nf)

    n_route = N_GROUPS + N_EXPERTS
    wr = jnp.zeros((d, ROUTE_LANES), F32).at[:, :n_route].set(
        jnp.concatenate([w_group[0], w_expert_router[0]], axis=1))
    wr_hi = wr.astype(BF16)
    wr_lo = (wr - wr_hi.astype(F32)).astype(BF16)
    br = jnp.zeros((1, ROUTE_LANES), F32).at[0, :n_route].set(jnp.concatenate([b_group[0], b_expert_router[0]]))
    hid = jnp.arange(d_hy, dtype=I32) // head_dim
    bd = jnp.where(hid[:, None] == hid[None, :], 1.0 / head_dim, 0.0).astype(BF16)

    prm = dict(
        mix_norm_g=mix_norm_g, w_in=w_in[0].astype(BF16), short_w=hy_short_w[0], short_b=hy_short_b,
        skip=hy_skip[0], ln_g=gm_ln_g, ln_b=gm_ln_b,
        ws_stack=gm_w_s[0].reshape(GM_HEADS * GM_CHUNK, GM_CHUNK).astype(BF16),
        bias_t=jnp.repeat(gm_b_s[0].T, head_dim, axis=1),
        mix_out_g=mix_out_g, bd=bd, w_out=w_out[0].astype(BF16), ffn_norm_g=ffn_norm_g,
        wr_cat=jnp.concatenate([wr_hi, wr_lo], axis=1), br=br,
        w_e_gate=w_e_gate[0].astype(BF16), w_e_up=w_e_up[0].astype(BF16), w_e_down=w_e_down[0].astype(BF16),
        final_norm_g=final_norm_g[None, :])
    y_prompt = _encoder(x_prompt, prm, kf, tables_bf, nf)
    y_sample = _encoder(x_sample, prm, kf, tables_bf, nf)
    return (y_prompt, y_sample)
```

```python
import functools
import math

import jax
import jax.numpy as jnp
from jax import lax
from jax.experimental import pallas as pl
from jax.experimental.pallas import tpu as pltpu

F32 = jnp.float32
BF16 = jnp.bfloat16
U32 = jnp.uint32
I32 = jnp.int32

EPS = 1e-6
HY_ORDER = 2
HY_HEADS = 8
GM_HEADS = 8
GM_CHUNK = 128
N_GROUPS = 4
EXPERTS_PER_GROUP = 8
N_EXPERTS = N_GROUPS * EXPERTS_PER_GROUP
TOP_K = 2
FILTER_EMB = 5
DECAY_TARGET = 1e-2
FAST_DECAY_PCT = 0.3
SLOW_DECAY_PCT = 1.5

LANES = 128
ROUTE_LANES = LANES
SLAB = 16
G_PITCH = SLAB + 8
CONV_LANES = 256
PHASE2_UNROLL = 8
ROW_TILE = 512
IN_A = 8
IN_SLABS = 4
EXPERT_ROWS = 1024
CHUNK = 8
MOE_TILE = 256
STAGE_ROWS = MOE_TILE * TOP_K + N_EXPERTS * CHUNK
VMEM_LIMIT = 56 * 1024 * 1024


def _cparams(n_axes, vmem=None):
    return pltpu.CompilerParams(dimension_semantics=("arbitrary",) * n_axes,
                                vmem_limit_bytes=vmem)


def _inproj_kernel(x_ref, xp_ref, xn_ref, g_ref, w_ref, sw_ref, sb_ref, hy_ref, gm_ref, stage_ref,
                   *, nf, d_hy3, tiles_per_seq):
    i = pl.program_id(0)
    rows = x_ref.shape[0]
    pitch = stage_ref.shape[1] // IN_A
    halo = xp_ref.shape[0]
    x = jnp.concatenate([xp_ref[...], x_ref[...], xn_ref[...]], axis=0)
    ms = jnp.mean(x * x, axis=-1, keepdims=True)
    n = (x * lax.rsqrt(ms + EPS) * g_ref[...]).astype(BF16)
    gm = jnp.dot(n, w_ref[:, d_hy3:], preferred_element_type=F32)
    gm_ref[...] = gm[halo:halo + rows].astype(BF16)

    pos = i % tiles_per_seq
    has_prev = jnp.where(pos == 0, 0.0, 1.0)
    has_next = jnp.where(pos == tiles_per_seq - 1, 0.0, 1.0)
    row = lax.broadcasted_iota(I32, (rows, 1), 0)
    prev_w = jnp.where(row == 0, has_prev, 1.0)
    next_w = jnp.where(row == rows - 1, has_next, 1.0)
    n_slabs = stage_ref.shape[0]
    cols = n_slabs * LANES
    ext = rows + 2 * halo
    for c0 in range(0, d_hy3, cols):
        p = jnp.dot(n, w_ref[:, c0:c0 + cols], preferred_element_type=F32)
        prev = pltpu.roll(p, 1, 0)[halo:halo + rows] * prev_w
        nxt = pltpu.roll(p, ext - 1, 0)[halo:halo + rows] * next_w
        res = (sw_ref[0:1, c0:c0 + cols] * prev + sw_ref[1:2, c0:c0 + cols] * p[halo:halo + rows]
               + sw_ref[2:3, c0:c0 + cols] * nxt + sb_ref[:, c0:c0 + cols])
        for s in range(n_slabs):
            for al in range(IN_A):
                stage_ref[s, al * pitch:al * pitch + nf, :] = res[al * nf:(al + 1) * nf,
                                                                s * LANES:(s + 1) * LANES]
        for s in range(n_slabs):
            for b in range(nf):
                hy_ref[0, c0 // LANES + s, b] = stage_ref[s, pl.ds(b, IN_A, stride=pitch), :]


def _inproj(x2, g, w_bf, short_w, short_b, n_seq, seq_len, nf, d_hy3):
    t, d = x2.shape
    dp = w_bf.shape[1]
    rows = IN_A * nf
    tiles_per_seq = seq_len // rows
    hb = rows // 8
    nhb = t // 8
    full = lambda a: pl.BlockSpec(a.shape, lambda i: (0,) * a.ndim)
    return pl.pallas_call(
        functools.partial(_inproj_kernel, nf=nf, d_hy3=d_hy3, tiles_per_seq=tiles_per_seq),
        grid=(t // rows,),
        in_specs=[pl.BlockSpec((rows, d), lambda i: (i, 0)),
                  pl.BlockSpec((8, d), lambda i: (jnp.maximum(i * hb - 1, 0), 0)),
                  pl.BlockSpec((8, d), lambda i: (jnp.minimum((i + 1) * hb, nhb - 1), 0)),
                  full(g), full(w_bf), full(short_w), full(short_b)],
        out_specs=[pl.BlockSpec((1, d_hy3 // LANES, nf, IN_A, LANES),
                                lambda i: (i // tiles_per_seq, 0, 0, i % tiles_per_seq, 0)),
                   pl.BlockSpec((rows, dp - d_hy3), lambda i: (i, 0))],
        out_shape=[jax.ShapeDtypeStruct((n_seq, d_hy3 // LANES, nf, nf // 2, LANES), F32),
                   jax.ShapeDtypeStruct((t, dp - d_hy3), BF16)],
        scratch_shapes=[pltpu.VMEM((IN_SLABS, IN_A * (nf + 8), LANES), F32)],
        compiler_params=_cparams(1, VMEM_LIMIT),
        name="inproj_shortconv",
    )(x2, x2, x2, g, w_bf, short_w, short_b)


def _cos_sin(m, period):
    ang = m.astype(F32) * (2.0 * math.pi / period)
    return jnp.cos(ang), jnp.sin(ang)


def _stack_complex(mr, mi):
    top = jnp.concatenate([mr, -mi], axis=-1)
    bot = jnp.concatenate([mi, mr], axis=-1)
    return jnp.concatenate([top, bot], axis=-2)


def _dft_tables(nf):
    na = nf // 2
    n = nf * nf
    idx = jnp.arange(nf, dtype=I32)
    c, s = _cos_sin((idx[:, None] * idx[None, :]) % nf, nf)
    f2 = _stack_complex(c, -s)
    f2i = _stack_complex(c, s)
    b_ = idx[:, None, None]
    ka = idx[None, :, None]
    a_ = idx[None, None, :]
    m1 = (nf * a_ * ka + b_ * ka) % n
    c1, s1 = _cos_sin(m1, n)
    f1 = _stack_complex(c1[:, :, :na], -s1[:, :, :na])
    f1_full = jnp.concatenate([c1, -s1], axis=1)
    c3 = jnp.swapaxes(c1, 1, 2)[:, :na, :]
    s3 = jnp.swapaxes(s1, 1, 2)[:, :na, :]
    f3 = _stack_complex(c3, s3)
    return f1, f2, f2i, f3, f1_full


def _split_bf16(x):
    hi = x.astype(BF16)
    lo = (x - hi.astype(F32)).astype(BF16)
    return hi, lo


def _stack_split_lhs(a):
    hi, lo = _split_bf16(a)
    return jnp.concatenate([hi, lo, hi], axis=-1)


def _stack_split_rhs(b):
    hi, lo = _split_bf16(b)
    return jnp.concatenate([hi, hi, lo], axis=-2)


def _filter_taps_kernel(wemb_ref, bemb_ref, win_ref, bin_ref, freq_ref, wout_ref, delta_ref, o_ref,
                        *, seq_len, n_inner, d_hy):
    rows = o_ref.shape[1]
    j = pl.program_id(0) * rows + lax.broadcasted_iota(I32, (rows, 1), 0)
    tidx = jnp.where(j < seq_len, j, 2 * seq_len - j).astype(F32)
    t = tidx / float(seq_len - 1)
    fr0 = jnp.float32(1e-4)
    ang0 = (2.0 * math.pi / seq_len) * tidx * fr0
    ang1 = (2.0 * math.pi / seq_len) * tidx
    freq = freq_ref[...]
    pre = (t * wemb_ref[0:1, :] + jnp.cos(ang0) * wemb_ref[1:2, :] + jnp.cos(ang1) * wemb_ref[2:3, :]
           - jnp.sin(ang0) * wemb_ref[3:4, :] - jnp.sin(ang1) * wemb_ref[4:5, :] + bemb_ref[...])
    hdn = jnp.sin(freq * pre)
    for i in range(n_inner):
        hdn = jnp.sin(freq * (jnp.dot(_stack_split_lhs(hdn), win_ref[i], preferred_element_type=F32)
                              + bin_ref[i:i + 1, :]))
    h = jnp.dot(_stack_split_lhs(hdn), wout_ref[...], preferred_element_type=F32)
    decay = jnp.exp(-t * delta_ref[...])
    for o in range(HY_ORDER):
        fwd = h[:, (2 * o) * d_hy:(2 * o + 1) * d_hy]
        bwd = h[:, (2 * o + 1) * d_hy:(2 * o + 2) * d_hy]
        val = jnp.where(j == seq_len, 0.0, jnp.where(j < seq_len, fwd, bwd) * decay)
        for cc in range(d_hy // LANES):
            o_ref[o * (d_hy // LANES) + cc] = val[:, cc * LANES:(cc + 1) * LANES]


def _filter_taps(wemb, bemb, win, bin_, freq, wout, deltas, seq_len, d_hy):
    rows = 1024
    n2 = 2 * seq_len
    width = wemb.shape[1]
    n_inner = win.shape[0]
    full = lambda *shape: pl.BlockSpec(shape, lambda i: (0,) * len(shape))
    return pl.pallas_call(
        functools.partial(_filter_taps_kernel, seq_len=seq_len, n_inner=n_inner, d_hy=d_hy),
        grid=(n2 // rows,),
        in_specs=[full(FILTER_EMB, width), full(1, width), full(n_inner, 3 * width, width),
                  full(n_inner, width), full(1, width), full(3 * width, 2 * HY_ORDER * d_hy),
                  full(1, d_hy)],
        out_specs=pl.BlockSpec((HY_ORDER * d_hy // LANES, rows, LANES), lambda i: (0, i, 0)),
        out_shape=jax.ShapeDtypeStruct((HY_ORDER * d_hy // LANES, n2, LANES), F32),
        compiler_params=_cparams(1, VMEM_LIMIT),
        name="filter_taps",
    )(wemb, bemb, _stack_split_rhs(win), bin_, freq, _stack_split_rhs(wout), deltas)


def _filter_fft1_kernel(k_ref, f_ref, o_ref, *, nf):
    halves = k_ref.shape[0]
    k = k_ref.reshape(halves, nf * SLAB, LANES)
    g = o_ref.reshape(halves, 2, nf * SLAB, LANES)
    for bl in range(SLAB):
        col = jnp.concatenate([k[h, pl.ds(bl, nf, stride=SLAB), :] for h in range(halves)], axis=1)
        res = jnp.dot(f_ref[bl], _stack_split_rhs(col), preferred_element_type=F32)
        for h in range(halves):
            for ri in range(2):
                g[h, ri, pl.ds(bl, nf, stride=SLAB), :] = res[ri * nf:(ri + 1) * nf,
                                                              h * LANES:(h + 1) * LANES]


def _filter_fft2_kernel(g_ref, f_ref, skip_ref, o_ref, *, nf, scale):
    halves = g_ref.shape[0]
    lag0 = jnp.concatenate([jnp.broadcast_to(skip_ref[...], (nf, skip_ref.shape[1])),
                            jnp.zeros((nf, skip_ref.shape[1]), F32)], axis=0)
    for kl in range(SLAB):
        rhs = jnp.concatenate([jnp.concatenate([g_ref[h, ri, kl] for h in range(halves)], axis=1)
                               for ri in range(2)], axis=0)
        res = jnp.dot(f_ref[...], _stack_split_rhs(rhs), preferred_element_type=F32)
        o_ref[kl] = ((res + lag0) * scale).astype(o_ref.dtype)


def _filter_spectrum(taps, skip, f1_full, f2, nf):
    nch = taps.shape[0]
    c = nch * LANES
    halves = CONV_LANES // LANES
    k4 = taps.reshape(nch, nf, nf, LANES)
    g = pl.pallas_call(
        functools.partial(_filter_fft1_kernel, nf=nf),
        grid=(nf // SLAB, nch // halves),
        in_specs=[pl.BlockSpec((halves, nf, SLAB, LANES), lambda i, j: (j, 0, i, 0)),
                  pl.BlockSpec((SLAB, 2 * nf, 3 * nf), lambda i, j: (i, 0, 0))],
        out_specs=pl.BlockSpec((halves, 2, nf, SLAB, LANES), lambda i, j: (j, 0, 0, i, 0)),
        out_shape=jax.ShapeDtypeStruct((nch, 2, nf, nf, LANES), F32),
        compiler_params=_cparams(2, VMEM_LIMIT),
        name="filter_fft1",
    )(k4, _stack_split_lhs(f1_full))
    return pl.pallas_call(
        functools.partial(_filter_fft2_kernel, nf=nf, scale=1.0 / (nf * nf)),
        grid=(nf // SLAB, nch // halves),
        in_specs=[pl.BlockSpec((halves, 2, SLAB, nf, LANES), lambda i, j: (j, 0, i, 0, 0)),
                  pl.BlockSpec((2 * nf, 6 * nf), lambda i, j: (0, 0)),
                  pl.BlockSpec((1, CONV_LANES), lambda i, j: (0, j))],
        out_specs=pl.BlockSpec((SLAB, 2 * nf, CONV_LANES), lambda i, j: (i, 0, j)),
        out_shape=jax.ShapeDtypeStruct((nf, 2 * nf, c), BF16),
        compiler_params=_cparams(2, VMEM_LIMIT),
        name="filter_fft2",
    )(g, _stack_split_lhs(f2), skip)


def _pack(re, im):
    rb = lax.bitcast_convert_type(re.astype(BF16).astype(F32), U32)
    ib = lax.bitcast_convert_type(im.astype(BF16).astype(F32), U32)
    return rb | (ib >> 16)


def _unpack(w):
    re = lax.bitcast_convert_type(w & jnp.uint32(0xFFFF0000), F32).astype(BF16)
    im = lax.bitcast_convert_type(w << 16, F32).astype(BF16)
    return re, im


def _longconv_kernel(z1_ref, f1_ref, kf_ref, f2_ref, f2i_ref, f3_ref, gate_ref, o_ref, g_ref,
                     *, nf, out_bmajor):
    na = nf // 2
    ns = nf // SLAB
    halves = g_ref.shape[1]
    t = pl.program_id(2)

    def load_rows(ref5, s, bl):
        return jnp.concatenate([ref5[s, h, bl] for h in range(halves)], axis=1)

    def store_col(blk, bl, val):
        for h in range(halves):
            g_ref[blk, h, pl.ds(bl, nf, stride=G_PITCH), :] = val[:, h * LANES:(h + 1) * LANES]

    def load_col(blk, bl):
        return jnp.concatenate([g_ref[blk, h, pl.ds(bl, nf, stride=G_PITCH), :] for h in range(halves)],
                               axis=1)

    @pl.when(t < ns)
    def _phase1():
        for bl in range(SLAB):
            rhs = jnp.concatenate([load_rows(z1_ref, 0, bl), load_rows(z1_ref, 1, bl)],
                                  axis=0).astype(BF16)
            out = jnp.dot(f1_ref[bl], rhs, preferred_element_type=F32)
            store_col(t, bl, _pack(out[:nf], out[nf:]))

    @pl.when(jnp.logical_and(t >= ns, t < 2 * ns))
    def _phase2():
        i = t - ns

        def load_slab(ka):
            row0 = pl.multiple_of(ka * G_PITCH, CHUNK)
            return jnp.concatenate([g_ref[:, h, pl.ds(row0, SLAB), :].reshape(nf, LANES)
                                    for h in range(halves)], axis=1)

        def transform(w, kl):
            re, im = _unpack(w)
            s = jnp.dot(f2_ref[...], jnp.concatenate([re, im], axis=0),
                        preferred_element_type=F32)
            sr, si = s[:nf], s[nf:]
            kr, ki = kf_ref[kl, :nf, :], kf_ref[kl, nf:, :]
            pr = (sr * kr - si * ki).astype(BF16)
            pi = (sr * ki + si * kr).astype(BF16)
            h = jnp.dot(f2i_ref[...], jnp.concatenate([pr, pi], axis=0),
                        preferred_element_type=F32)
            return _pack(h[:nf], h[nf:])

        def store_slab(ka, packed):
            row0 = pl.multiple_of(ka * G_PITCH, CHUNK)
            for hf in range(halves):
                g_ref[:, hf, pl.ds(row0, SLAB), :] = packed[:, hf * LANES:(hf + 1) * LANES].reshape(
                    ns, SLAB, LANES)

        def body(grp, carry):
            kls = [grp * PHASE2_UNROLL + u for u in range(PHASE2_UNROLL)]
            ws = [load_slab(i * SLAB + kl) for kl in kls]
            outs = [transform(w, kl) for w, kl in zip(ws, kls)]
            for kl, packed in zip(kls, outs):
                store_slab(i * SLAB + kl, packed)
            return carry

        lax.fori_loop(0, SLAB // PHASE2_UNROLL, body, 0)

    @pl.when(t >= 2 * ns)
    def _phase3():
        j = t - 2 * ns
        out_flat = None if out_bmajor else o_ref.reshape(2, halves, na * SLAB, LANES)
        for bl in range(SLAB):
            re, im = _unpack(load_col(j, bl))
            y = jnp.dot(f3_ref[bl], jnp.concatenate([re, im], axis=0),
                        preferred_element_type=F32)
            for s in range(2):
                val = load_rows(gate_ref, s, bl) * y[s * na:(s + 1) * na]
                for h in range(halves):
                    piece = val[:, h * LANES:(h + 1) * LANES]
                    if out_bmajor:
                        o_ref[s, h, bl] = piece.astype(o_ref.dtype)
                    else:
                        out_flat[s, h, pl.ds(bl, na, stride=SLAB), :] = piece


def _longconv(gates5, z5, kf, tables, c, gate_blk, z_blk, kf_col, nf, out_bmajor):
    f1, f2, f2i, f3 = tables
    b, _, _, na, _ = z5.shape
    ns = nf // SLAB
    halves = CONV_LANES // LANES
    nchunk = c // CONV_LANES
    p1 = lambda t: jnp.minimum(t, ns - 1)
    p2 = lambda t: jnp.clip(t - ns, 0, ns - 1)
    p3 = lambda t: jnp.clip(t - 2 * ns, 0, ns - 1)
    cl = CONV_LANES
    blk5 = (2, halves, SLAB, na, LANES)
    if out_bmajor:
        out_spec = pl.BlockSpec(blk5, lambda p, q, t: (p, q, p3(t), 0, 0))
        out_shape = (b, c // LANES, nf, na, LANES)
    else:
        out_spec = pl.BlockSpec((2, halves, na, SLAB, LANES), lambda p, q, t: (p, q, 0, p3(t), 0))
        out_shape = (b, c // LANES, na, nf, LANES)
    return pl.pallas_call(
        functools.partial(_longconv_kernel, nf=nf, out_bmajor=out_bmajor),
        grid=(b // 2, nchunk, 3 * ns),
        in_specs=[
            pl.BlockSpec(blk5, lambda p, q, t: (p, z_blk + q, p1(t), 0, 0)),
            pl.BlockSpec((SLAB, 2 * nf, 2 * na), lambda p, q, t: (p1(t), 0, 0)),
            pl.BlockSpec((SLAB, 2 * nf, cl), lambda p, q, t: (p2(t), 0, kf_col + q)),
            pl.BlockSpec((2 * nf, 2 * nf), lambda p, q, t: (0, 0)),
            pl.BlockSpec((2 * nf, 2 * nf), lambda p, q, t: (0, 0)),
            pl.BlockSpec((SLAB, 2 * na, 2 * nf), lambda p, q, t: (p3(t), 0, 0)),
            pl.BlockSpec(blk5, lambda p, q, t: (p, gate_blk + q, p3(t), 0, 0)),
        ],
        out_specs=out_spec,
        out_shape=jax.ShapeDtypeStruct(out_shape, BF16 if out_bmajor else F32),
        scratch_shapes=[pltpu.VMEM((ns, halves, nf * G_PITCH, LANES), U32)],
        compiler_params=_cparams(3, VMEM_LIMIT),
        name="longconv",
    )(z5, f1, kf, f2, f2i, f3, gates5)


def _gelu_exact(x):
    return 0.5 * x * (1.0 + lax.erf(x * (1.0 / math.sqrt(2.0))))


def _gmlp_kernel(gm_ref, lng_ref, lnb_ref, ws_ref, bias_ref, o_ref, *, d_gm):
    g = _gelu_exact(gm_ref[...].astype(F32))
    u, v = g[:, :d_gm], g[:, d_gm:]
    mu = jnp.mean(v, axis=-1, keepdims=True)
    vc = v - mu
    var = jnp.mean(vc * vc, axis=-1, keepdims=True)
    vh = (vc * lax.rsqrt(var + EPS) * lng_ref[...] + lnb_ref[...]).astype(BF16)
    head = lax.broadcasted_iota(I32, (GM_CHUNK, d_gm), 1) // (d_gm // GM_HEADS)
    for c in range(gm_ref.shape[0] // GM_CHUNK):
        rows = slice(c * GM_CHUNK, (c + 1) * GM_CHUNK)
        r = jnp.dot(ws_ref[...], vh[rows], preferred_element_type=F32)
        s = r[:GM_CHUNK]
        for h in range(1, GM_HEADS):
            s = jnp.where(head == h, r[h * GM_CHUNK:(h + 1) * GM_CHUNK], s)
        o_ref[rows, :] = u[rows] * (s + bias_ref[...])


def _gmlp(gm, ln_g, ln_b, ws_stack, bias_t):
    t, c2 = gm.shape
    d_gm = c2 // 2
    return pl.pallas_call(
        functools.partial(_gmlp_kernel, d_gm=d_gm),
        grid=(t // ROW_TILE,),
        in_specs=[pl.BlockSpec((ROW_TILE, c2), lambda i: (i, 0)),
                  pl.BlockSpec((1, d_gm), lambda i: (0, 0)),
                  pl.BlockSpec((1, d_gm), lambda i: (0, 0)),
                  pl.BlockSpec(ws_stack.shape, lambda i: (0, 0)),
                  pl.BlockSpec(bias_t.shape, lambda i: (0, 0))],
        out_specs=pl.BlockSpec((ROW_TILE, d_gm), lambda i: (i, 0)),
        out_shape=jax.ShapeDtypeStruct((t, d_gm), F32),
        compiler_params=_cparams(1, VMEM_LIMIT),
        name="gmlp",
    )(gm, ln_g, ln_b, ws_stack, bias_t)


def _head_rms(y, bd):
    ms = jnp.dot((y * y).astype(BF16), bd, preferred_element_type=F32)
    return y * lax.rsqrt(ms + EPS)


def _route(logits):
    lane = lax.broadcasted_iota(I32, logits.shape, 1)
    neg = jnp.float32(-1e30)
    big = jnp.int32(ROUTE_LANES)
    gmask = lane < N_GROUPS
    gl = jnp.where(gmask, logits, neg)
    gmax = jnp.max(gl, axis=-1, keepdims=True)
    grp = jnp.min(jnp.where(jnp.logical_and(gl == gmax, gmask), lane, big), axis=-1, keepdims=True)
    psum = jnp.sum(jnp.where(gmask, jnp.exp(gl - gmax), 0.0), axis=-1, keepdims=True)
    p_grp = 1.0 / psum
    lo = N_GROUPS + EXPERTS_PER_GROUP * grp
    emask = jnp.logical_and(lane >= lo, lane < lo + EXPERTS_PER_GROUP)
    el = jnp.where(emask, logits, neg)
    m1 = jnp.max(el, axis=-1, keepdims=True)
    i1 = jnp.min(jnp.where(jnp.logical_and(el == m1, emask), lane, big), axis=-1, keepdims=True)
    emask2 = jnp.logical_and(emask, lane != i1)
    el2 = jnp.where(emask2, logits, neg)
    m2 = jnp.max(el2, axis=-1, keepdims=True)
    i2 = jnp.min(jnp.where(jnp.logical_and(el2 == m2, emask2), lane, big), axis=-1, keepdims=True)
    d = jnp.exp(m2 - m1)
    g1 = p_grp * (1.0 / (1.0 + d))
    g2 = p_grp * (d / (1.0 + d))
    e1 = (i1 - N_GROUPS).astype(F32)
    e2 = (i2 - N_GROUPS).astype(F32)
    return jnp.where(lane == 0, e1, jnp.where(lane == 1, e2, jnp.where(lane == 2, g1,
                     jnp.where(lane == 3, g2, 0.0))))


def _lane_cumsum(v):
    lane = lax.broadcasted_iota(I32, v.shape, 1)
    sh = 1
    while sh < v.shape[1]:
        v = v + jnp.where(lane >= sh, pltpu.roll(v, sh, 1), 0.0)
        sh *= 2
    return v


def _outproj_kernel(x_ref, yh_ref, yg_ref, mg_ref, bd_ref, wo_ref, fg_ref, wr_ref, br_ref,
                    x1_ref, stage_ref, route_ref, count_ref, *, d_hy):
    bd = bd_ref[...]
    mg = mg_ref[...]
    yh = jnp.concatenate([yh_ref[0, c] for c in range(yh_ref.shape[1])], axis=1)
    mh = (_head_rms(yh, bd) * mg[:, :d_hy]).astype(BF16)
    mgm = (_head_rms(yg_ref[...], bd) * mg[:, d_hy:]).astype(BF16)
    wo = wo_ref[...]
    x1 = (x_ref[...] + jnp.dot(mh, wo[:d_hy], preferred_element_type=F32)
          + jnp.dot(mgm, wo[d_hy:], preferred_element_type=F32))
    x1_ref[...] = x1
    ms = jnp.mean(x1 * x1, axis=-1, keepdims=True)
    n2 = x1 * lax.rsqrt(ms + EPS) * fg_ref[...]
    hi, lo = _split_bf16(n2)
    wr = wr_ref[...]
    hw = jnp.dot(hi, wr, preferred_element_type=F32)
    logits = (hw[:, :ROUTE_LANES] + hw[:, ROUTE_LANES:]
              + jnp.dot(lo, wr[:, :ROUTE_LANES], preferred_element_type=F32) + br_ref[...])
    route = _route(logits)

    lane = lax.broadcasted_iota(I32, (MOE_TILE, ROUTE_LANES), 1)
    r = lax.broadcasted_iota(I32, (MOE_TILE, MOE_TILE), 0)
    c = lax.broadcasted_iota(I32, (MOE_TILE, MOE_TILE), 1)
    ltri = jnp.where(c < r, 1.0, 0.0).astype(BF16)
    srow = lax.broadcasted_iota(I32, (STAGE_ROWS, MOE_TILE), 0)
    mlane = lax.broadcasted_iota(I32, (STAGE_ROWS, LANES), 1)
    half = n2.shape[1] // 2
    subs = route.shape[0] // MOE_TILE
    for sub in range(subs):
        rows = slice(sub * MOE_TILE, (sub + 1) * MOE_TILE)
        rsub = route[rows]
        oh0 = lane == rsub[:, 0:1].astype(I32)
        oh1 = lane == rsub[:, 1:2].astype(I32)
        oh = jnp.where(jnp.logical_or(oh0, oh1), 1.0, 0.0)
        before = jnp.dot(ltri, oh.astype(BF16), preferred_element_type=F32)
        cnt = jnp.sum(oh, axis=0, keepdims=True)
        run = jnp.floor((cnt + (CHUNK - 1)) * (1.0 / CHUNK)) * CHUNK
        run_start = _lane_cumsum(jnp.broadcast_to(run, (8, run.shape[1])))[0:1] - run
        tot = before + run_start
        loc0 = jnp.sum(jnp.where(oh0, tot, 0.0), axis=-1, keepdims=True)
        loc1 = jnp.sum(jnp.where(oh1, tot, 0.0), axis=-1, keepdims=True)
        rsub = jnp.where(lane == 4, loc0, jnp.where(lane == 5, loc1, rsub))
        route_ref[rows, :] = rsub
        count_ref[pl.ds(pl.program_id(0) * subs + sub, 1), :] = cnt

        rt = jnp.transpose(rsub)
        p0 = srow == rt[4:5, :].astype(I32)
        p1 = srow == rt[5:6, :].astype(I32)
        perm = jnp.where(jnp.logical_or(p0, p1), 1.0, 0.0).astype(BF16)
        staged = jnp.dot(perm, hi[rows], preferred_element_type=F32)
        gate = jnp.sum(jnp.where(p0, rt[2:3, :], 0.0) + jnp.where(p1, rt[3:4, :], 0.0), axis=-1,
                       keepdims=True)
        meta = jnp.where(mlane == 0,
                         lax.bitcast_convert_type(jnp.broadcast_to(gate, (STAGE_ROWS, LANES)), U32),
                         jnp.uint32(0))
        stage_ref[sub * STAGE_ROWS:(sub + 1) * STAGE_ROWS, :] = jnp.concatenate(
            [_pack(staged[:, :half], staged[:, half:]), meta], axis=1)


def _outproj(x2, yh, yg, mix_g, bd, wo_bf, ffn_g, wr_cat, br):
    t, d = x2.shape
    _, nch, seq_len, _ = yh.shape
    d_hy = nch * LANES
    tiles_per_seq = seq_len // ROW_TILE
    n_tiles = t // ROW_TILE
    row = lambda w: pl.BlockSpec((ROW_TILE, w), lambda i: (i, 0))
    full = lambda a: pl.BlockSpec(a.shape, lambda i: (0,) * a.ndim)
    sw = d // 2 + LANES
    subs = ROW_TILE // MOE_TILE
    n_sub = t // MOE_TILE
    return pl.pallas_call(
        functools.partial(_outproj_kernel, d_hy=d_hy),
        grid=(n_tiles,),
        in_specs=[row(d),
                  pl.BlockSpec((1, nch, ROW_TILE, LANES),
                               lambda i: (i // tiles_per_seq, 0, i % tiles_per_seq, 0)),
                  row(yg.shape[1]), full(mix_g), full(bd), full(wo_bf), full(ffn_g),
                  full(wr_cat), full(br)],
        out_specs=[row(d), pl.BlockSpec((subs * STAGE_ROWS, sw), lambda i: (i, 0)), row(ROUTE_LANES),
                   pl.BlockSpec((n_sub, ROUTE_LANES), lambda i: (0, 0))],
        out_shape=[jax.ShapeDtypeStruct((t, d), F32),
                   jax.ShapeDtypeStruct((n_sub * STAGE_ROWS, sw), U32),
                   jax.ShapeDtypeStruct((t, ROUTE_LANES), F32),
                   jax.ShapeDtypeStruct((n_sub, ROUTE_LANES), F32)],
        compiler_params=_cparams(1, VMEM_LIMIT),
        name="outproj_router",
    )(x2, yh, yg, mix_g, bd, wo_bf, ffn_g, wr_cat, br)


def _moe_plan(tile_cnt, n_tokens):
    n_tiles = tile_cnt.shape[0]
    cnt = tile_cnt[:, :N_EXPERTS].astype(I32)
    run = (cnt + CHUNK - 1) // CHUNK * CHUNK
    run_start = jnp.cumsum(run, axis=1) - run
    used = jnp.sum(run, axis=1)
    e_rows = jnp.sum(run, axis=0)
    e_pad = (e_rows + EXPERT_ROWS - 1) // EXPERT_ROWS * EXPERT_ROWS
    e_end = jnp.cumsum(e_pad)
    e_start = e_end - e_pad
    n_blocks = -(-(n_tokens * TOP_K + n_tiles * N_EXPERTS * (CHUNK - 1)) // EXPERT_ROWS) + N_EXPERTS
    n_used = (e_end[-1:] // EXPERT_ROWS).astype(I32)
    block_row = jnp.arange(n_blocks, dtype=I32) * EXPERT_ROWS
    block_e = jnp.minimum(jnp.sum((e_end[None, :] <= block_row[:, None]).astype(I32), axis=1),
                          N_EXPERTS - 1)
    run_end_in_e = jnp.cumsum(run, axis=0)
    chunk_row = jnp.arange(n_blocks * EXPERT_ROWS // CHUNK, dtype=I32) * CHUNK
    e_of_chunk = jnp.repeat(block_e, EXPERT_ROWS // CHUNK)
    e_sel = jnp.arange(N_EXPERTS, dtype=I32)[None, :] == e_of_chunk[:, None]
    pick = lambda tbl: jnp.dot(e_sel.astype(F32), tbl.astype(F32),
                               precision=lax.Precision.HIGHEST).astype(I32)
    q = chunk_row - pick(e_start[:, None])[:, 0]
    valid = q < pick(e_rows[:, None])[:, 0]
    ends = pick(run_end_in_e.T)
    tile = jnp.minimum(jnp.sum((ends <= q[:, None]).astype(I32), axis=1), n_tiles - 1)
    t_sel = jnp.arange(n_tiles, dtype=I32)[None, :] == tile[:, None]
    at_tile = lambda tbl: jnp.sum(jnp.where(t_sel, pick(tbl.T), 0), axis=1)
    within = q - (at_tile(run_end_in_e) - at_tile(run))
    src_row = tile * STAGE_ROWS + at_tile(run_start) + within
    zero_chunk = STAGE_ROWS // CHUNK - 1
    per_block = EXPERT_ROWS // CHUNK
    src_chunk = jnp.where(valid, src_row // CHUNK, zero_chunk).reshape(n_blocks, 1, per_block)
    dst_chunk = jnp.where(valid, src_row // CHUNK, 0).reshape(n_blocks, 1, per_block)
    n_valid = jnp.sum(valid.reshape(n_blocks, per_block).astype(I32), axis=1)
    return (block_e.astype(I32), n_used, n_valid.astype(I32), src_chunk.astype(I32),
            dst_chunk.astype(I32), used.astype(I32))


def _ffn_kernel(be_ref, nu_ref, nv_ref, used_ref, src_now_ref, src_next_ref, dst_ref, stage_ref,
                wg_ref, wu_ref, wd_ref, ost_ref, xbuf, obuf, zero_ref, in_sem, out_sem, zero_sem):
    del be_ref
    b = pl.program_id(0)
    nu = nu_ref[0]
    per_block = dst_ref.shape[-1]
    slot = b % 2

    def fetch(map_ref, s):
        def body(j, carry):
            row = pl.multiple_of(map_ref[0, 0, j] * CHUNK, CHUNK)
            pltpu.make_async_copy(stage_ref.at[pl.ds(row, CHUNK)], xbuf.at[s, pl.ds(j * CHUNK, CHUNK)],
                                  in_sem.at[s]).start()
            return carry
        lax.fori_loop(0, per_block, body, 0, unroll=8)

    def out_copy(j, chunk, s):
        row = pl.multiple_of(chunk * CHUNK, CHUNK)
        return pltpu.make_async_copy(obuf.at[s, pl.ds(j * CHUNK, CHUNK)], ost_ref.at[pl.ds(row, CHUNK)],
                                     out_sem.at[s])

    def drain(s, n):
        @pl.when(n == per_block)
        def _():
            pltpu.make_async_copy(obuf.at[s], ost_ref.at[pl.ds(0, EXPERT_ROWS)], out_sem.at[s]).wait()

        @pl.when(n != per_block)
        def _():
            def body(j, carry):
                out_copy(0, 0, s).wait()
                return carry
            lax.fori_loop(0, n, body, 0)

    def zero_tails(wait):
        def tile(i, carry):
            def chunk(c, carry2):
                row = pl.multiple_of(i * STAGE_ROWS + c * CHUNK, CHUNK)
                cp = pltpu.make_async_copy(zero_ref, ost_ref.at[pl.ds(row, CHUNK)], zero_sem)
                if wait:
                    cp.wait()
                else:
                    cp.start()
                return carry2
            return lax.fori_loop(used_ref[i] // CHUNK, STAGE_ROWS // CHUNK, chunk, carry)
        lax.fori_loop(0, used_ref.shape[0], tile, 0)

    @pl.when(b == 0)
    def _prologue():
        fetch(src_now_ref, 0)
        zero_ref[...] = jnp.zeros_like(zero_ref)
        zero_tails(wait=False)
        zero_tails(wait=True)

    @pl.when(b + 1 < nu)
    def _prefetch_next():
        fetch(src_next_ref, 1 - slot)

    @pl.when(b < nu)
    def _compute():
        pltpu.make_async_copy(stage_ref.at[pl.ds(0, EXPERT_ROWS)], xbuf.at[slot], in_sem.at[slot]).wait()

        @pl.when(b >= 2)
        def _():
            drain(slot, nv_ref[b - 2])

        x = xbuf[slot]
        half = wg_ref.shape[1] // 2
        xb = jnp.concatenate(_unpack(x[:, :half]), axis=1)
        gate = lax.bitcast_convert_type(x[:, half:half + 1], F32)
        g = jnp.dot(xb, wg_ref[0], preferred_element_type=F32)
        u = jnp.dot(xb, wu_ref[0], preferred_element_type=F32)
        h = (g * (1.0 / (1.0 + jnp.exp(-g))) * u).astype(BF16)
        o = jnp.dot(h, wd_ref[0], preferred_element_type=F32) * gate
        obuf[slot] = _pack(o[:, :half], o[:, half:])

        def body(j, carry):
            out_copy(j, dst_ref[0, 0, j], slot).start()
            return carry

        @pl.when(nv_ref[b] == per_block)
        def _():
            lax.fori_loop(0, per_block, body, 0, unroll=8)

        @pl.when(nv_ref[b] != per_block)
        def _():
            lax.fori_loop(0, nv_ref[b], body, 0)

        @pl.when(b == nu - 1)
        def _():
            drain(slot, nv_ref[b])

            @pl.when(b >= 1)
            def _():
                drain(1 - slot, nv_ref[b - 1])


def _expert_ffn(stage, plan, wg, wu, wd):
    block_e, n_used, n_valid, src_chunk, dst_chunk, used = plan
    nb = block_e.shape[0]
    sw = stage.shape[1]
    d, de = wg.shape[1], wg.shape[2]
    per_block = EXPERT_ROWS // CHUNK
    cur = lambda i, be, nu: jnp.minimum(i, nu[0] - 1)
    wspec = lambda shape: pl.BlockSpec(shape, lambda i, be, nu, nv, us: (be[cur(i, be, nu)], 0, 0))
    smem = lambda imap: pl.BlockSpec((1, 1, per_block), imap, memory_space=pltpu.SMEM)
    return pl.pallas_call(
        _ffn_kernel,
        grid_spec=pltpu.PrefetchScalarGridSpec(
            num_scalar_prefetch=4,
            grid=(nb,),
            in_specs=[smem(lambda i, be, nu, nv, us: (i, 0, 0)),
                      smem(lambda i, be, nu, nv, us: (jnp.minimum(i + 1, nb - 1), 0, 0)),
                      smem(lambda i, be, nu, nv, us: (i, 0, 0)),
                      pl.BlockSpec(memory_space=pl.ANY),
                      wspec((1, d, de)), wspec((1, d, de)), wspec((1, de, d))],
            out_specs=pl.BlockSpec(memory_space=pl.ANY),
            scratch_shapes=[pltpu.VMEM((2, EXPERT_ROWS, sw), U32), pltpu.VMEM((2, EXPERT_ROWS, d // 2), U32),
                            pltpu.VMEM((CHUNK, d // 2), U32),
                            pltpu.SemaphoreType.DMA((2,)), pltpu.SemaphoreType.DMA((2,)),
                            pltpu.SemaphoreType.DMA(())]),
        out_shape=jax.ShapeDtypeStruct((stage.shape[0], d // 2), U32),
        compiler_params=_cparams(1, VMEM_LIMIT),
        name="expert_ffn",
    )(block_e, n_used, n_valid, used, src_chunk, src_chunk, dst_chunk, stage, wg, wu, wd)


def _combine_kernel(ost_ref, route_ref, x1_ref, fg_ref, o_ref):
    col = lax.broadcasted_iota(I32, (MOE_TILE, STAGE_ROWS), 1)
    for sub in range(route_ref.shape[0] // MOE_TILE):
        rows = slice(sub * MOE_TILE, (sub + 1) * MOE_TILE)
        route = route_ref[rows, :]
        sel = jnp.logical_or(col == route[:, 4:5].astype(I32), col == route[:, 5:6].astype(I32))
        perm = jnp.where(sel, 1.0, 0.0).astype(BF16)
        ost = ost_ref[sub * STAGE_ROWS:(sub + 1) * STAGE_ROWS, :]
        y = jnp.dot(perm, jnp.concatenate(_unpack(ost), axis=1), preferred_element_type=F32)
        x2 = x1_ref[rows, :] + y
        ms = jnp.mean(x2 * x2, axis=-1, keepdims=True)
        o_ref[rows, :] = x2 * lax.rsqrt(ms + EPS) * fg_ref[...]


def _combine(ostage, route, x1, final_g):
    t, d = x1.shape
    return pl.pallas_call(
        _combine_kernel,
        grid=(t // ROW_TILE,),
        in_specs=[pl.BlockSpec((ROW_TILE // MOE_TILE * STAGE_ROWS, d // 2), lambda i: (i, 0)),
                  pl.BlockSpec((ROW_TILE, ROUTE_LANES), lambda i: (i, 0)),
                  pl.BlockSpec((ROW_TILE, d), lambda i: (i, 0)),
                  pl.BlockSpec((1, d), lambda i: (0, 0))],
        out_specs=pl.BlockSpec((ROW_TILE, d), lambda i: (i, 0)),
        out_shape=jax.ShapeDtypeStruct((t, d), F32),
        compiler_params=_cparams(1, VMEM_LIMIT),
        name="moe_combine",
    )(ostage, route, x1, final_g)


def _hier_moe_and_norm(x1, stage, route, tile_cnt, wg, wu, wd, final_g):
    t, d = x1.shape
    ostage = _expert_ffn(stage, _moe_plan(tile_cnt, t), wg, wu, wd)
    return _combine(ostage, route, x1, final_g)


def _encoder(x, prm, kf, tables, nf):
    b, l, d = x.shape
    t = b * l
    d_hy = prm["skip"].shape[1]
    x2 = x.reshape(t, d)
    gates5, gm = _inproj(x2, prm["mix_norm_g"], prm["w_in"], prm["short_w"], prm["short_b"], b, l, nf,
                         (HY_ORDER + 1) * d_hy)
    z5 = gates5
    z_blk = HY_ORDER * d_hy // CONV_LANES
    for o in range(HY_ORDER):
        z5 = _longconv(gates5, z5, kf, tables, d_hy, gate_blk=o * d_hy // CONV_LANES,
                       z_blk=z_blk, kf_col=o * d_hy // CONV_LANES, nf=nf, out_bmajor=o < HY_ORDER - 1)
        z_blk = 0
    y_hy = z5.reshape(b, d_hy // LANES, l, LANES)
    y_gm = _gmlp(gm, prm["ln_g"], prm["ln_b"], prm["ws_stack"], prm["bias_t"])
    x1, stage, route, tile_cnt = _outproj(x2, y_hy, y_gm, prm["mix_out_g"], prm["bd"], prm["w_out"], prm["ffn_norm_g"],
                             prm["wr_cat"], prm["br"])
    out = _hier_moe_and_norm(x1, stage, route, tile_cnt, prm["w_e_gate"], prm["w_e_up"], prm["w_e_down"],
                             prm["final_norm_g"])
    return out.reshape(b, l, d)


def kernel(x_prompt, x_sample, mix_norm_g, w_in, hy_short_w, hy_short_b, hy_filt_w_emb, hy_filt_b_emb,
           hy_filt_w_inner, hy_filt_b_inner, hy_filt_freq, hy_filt_w_out, hy_skip, gm_ln_g, gm_ln_b,
           gm_w_s, gm_b_s, mix_out_g, w_out, ffn_norm_g, w_group, b_group, w_expert_router,
           b_expert_router, w_e_gate, w_e_up, w_e_down, final_norm_g):
    assert w_in.shape[0] == 1, "one layer"
    l = x_prompt.shape[1]
    assert x_sample.shape[1] == l
    nf = math.isqrt(2 * l)
    assert nf * nf == 2 * l and nf % SLAB == 0
    d = x_prompt.shape[2]
    d_hy = hy_skip.shape[2]
    d_gm = gm_ln_g.shape[1]
    head_dim = d_gm // GM_HEADS
    assert d_hy // HY_HEADS == head_dim and d_hy == d_gm

    tables = _dft_tables(nf)
    f1, f2, f2i, f3, f1_full = tables
    tables_bf = tuple(a.astype(BF16) for a in (f1, f2, f2i, f3))

    max_decay = math.log(DECAY_TARGET) / FAST_DECAY_PCT
    min_decay = math.log(DECAY_TARGET) / SLOW_DECAY_PCT
    deltas = jnp.abs(jnp.linspace(min_decay, max_decay, d_hy, dtype=F32))[None, :]
    taps = _filter_taps(hy_filt_w_emb[0], hy_filt_b_emb[0][None, :], hy_filt_w_inner[0], hy_filt_b_inner[0],
                        hy_filt_freq[0][None, :], hy_filt_w_out[0], deltas, l, d_hy)
    kf = _filter_spectrum(taps, hy_skip[0].reshape(1, HY_ORDER * d_hy), f1_full, f2, nf)

    n_route = N_GROUPS + N_EXPERTS
    wr = jnp.zeros((d, ROUTE_LANES), F32).at[:, :n_route].set(
        jnp.concatenate([w_group[0], w_expert_router[0]], axis=1))
    wr_hi = wr.astype(BF16)
    wr_lo = (wr - wr_hi.astype(F32)).astype(BF16)
    br = jnp.zeros((1, ROUTE_LANES), F32).at[0, :n_route].set(jnp.concatenate([b_group[0], b_expert_router[0]]))
    hid = jnp.arange(d_hy, dtype=I32) // head_dim
    bd = jnp.where(hid[:, None] == hid[None, :], 1.0 / head_dim, 0.0).astype(BF16)

    prm = dict(
        mix_norm_g=mix_norm_g, w_in=w_in[0].astype(BF16), short_w=hy_short_w[0], short_b=hy_short_b,
        skip=hy_skip[0], ln_g=gm_ln_g, ln_b=gm_ln_b,
        ws_stack=gm_w_s[0].reshape(GM_HEADS * GM_CHUNK, GM_CHUNK).astype(BF16),
        bias_t=jnp.repeat(gm_b_s[0].T, head_dim, axis=1),
        mix_out_g=mix_out_g, bd=bd, w_out=w_out[0].astype(BF16), ffn_norm_g=ffn_norm_g,
        wr_cat=jnp.concatenate([wr_hi, wr_lo], axis=1), br=br,
        w_e_gate=w_e_gate[0].astype(BF16), w_e_up=w_e_up[0].astype(BF16), w_e_down=w_e_down[0].astype(BF16),
        final_norm_g=final_norm_g[None, :])
    y_prompt = _encoder(x_prompt, prm, kf, tables_bf, nf)
    y_sample = _encoder(x_sample, prm, kf, tables_bf, nf)
    return (y_prompt, y_sample)
```

```python
import functools
import math

import jax
import jax.numpy as jnp
from jax import lax
from jax.experimental import pallas as pl
from jax.experimental.pallas import tpu as pltpu

F32 = jnp.float32
BF16 = jnp.bfloat16
U32 = jnp.uint32
I32 = jnp.int32

EPS = 1e-6
HY_ORDER = 2
HY_HEADS = 8
GM_HEADS = 8
GM_CHUNK = 128
N_GROUPS = 4
EXPERTS_PER_GROUP = 8
N_EXPERTS = N_GROUPS * EXPERTS_PER_GROUP
TOP_K = 2
FILTER_EMB = 5
DECAY_TARGET = 1e-2
FAST_DECAY_PCT = 0.3
SLOW_DECAY_PCT = 1.5

LANES = 128
ROUTE_LANES = LANES
SLAB = 16
G_PITCH = SLAB + 8
CONV_LANES = 256
PHASE2_UNROLL = 8
ROW_TILE = 512
IN_A = 8
IN_SLABS = 4
EXPERT_ROWS = 1024
CHUNK = 8
MOE_TILE = 512
DENSE_ROWS = 256
STAGE_ROWS = MOE_TILE * TOP_K + N_EXPERTS * CHUNK
VMEM_LIMIT = 56 * 1024 * 1024


def _cparams(n_axes, vmem=None):
    return pltpu.CompilerParams(dimension_semantics=("arbitrary",) * n_axes,
                                vmem_limit_bytes=vmem)


def _inproj_kernel(x_ref, xp_ref, xn_ref, g_ref, w_ref, sw_ref, sb_ref, hy_ref, gm_ref, stage_ref,
                   *, nf, d_hy3, tiles_per_seq):
    i = pl.program_id(0)
    rows = x_ref.shape[0]
    pitch = stage_ref.shape[1] // IN_A
    halo = xp_ref.shape[0]
    x = jnp.concatenate([xp_ref[...], x_ref[...], xn_ref[...]], axis=0)
    ms = jnp.mean(x * x, axis=-1, keepdims=True)
    n = (x * lax.rsqrt(ms + EPS) * g_ref[...]).astype(BF16)
    gm = jnp.dot(n, w_ref[:, d_hy3:], preferred_element_type=F32)
    gm_ref[...] = gm[halo:halo + rows].astype(BF16)

    pos = i % tiles_per_seq
    has_prev = jnp.where(pos == 0, 0.0, 1.0)
    has_next = jnp.where(pos == tiles_per_seq - 1, 0.0, 1.0)
    row = lax.broadcasted_iota(I32, (rows, 1), 0)
    prev_w = jnp.where(row == 0, has_prev, 1.0)
    next_w = jnp.where(row == rows - 1, has_next, 1.0)
    n_slabs = stage_ref.shape[0]
    cols = n_slabs * LANES
    ext = rows + 2 * halo
    for c0 in range(0, d_hy3, cols):
        p = jnp.dot(n, w_ref[:, c0:c0 + cols], preferred_element_type=F32)
        prev = pltpu.roll(p, 1, 0)[halo:halo + rows] * prev_w
        nxt = pltpu.roll(p, ext - 1, 0)[halo:halo + rows] * next_w
        res = (sw_ref[0:1, c0:c0 + cols] * prev + sw_ref[1:2, c0:c0 + cols] * p[halo:halo + rows]
               + sw_ref[2:3, c0:c0 + cols] * nxt + sb_ref[:, c0:c0 + cols])
        for s in range(n_slabs):
            for al in range(IN_A):
                stage_ref[s, al * pitch:al * pitch + nf, :] = res[al * nf:(al + 1) * nf,
                                                                s * LANES:(s + 1) * LANES]
        for s in range(n_slabs):
            for b in range(nf):
                hy_ref[0, c0 // LANES + s, b] = stage_ref[s, pl.ds(b, IN_A, stride=pitch), :]


def _inproj(x2, g, w_bf, short_w, short_b, n_seq, seq_len, nf, d_hy3):
    t, d = x2.shape
    dp = w_bf.shape[1]
    rows = IN_A * nf
    tiles_per_seq = seq_len // rows
    hb = rows // 8
    nhb = t // 8
    full = lambda a: pl.BlockSpec(a.shape, lambda i: (0,) * a.ndim)
    return pl.pallas_call(
        functools.partial(_inproj_kernel, nf=nf, d_hy3=d_hy3, tiles_per_seq=tiles_per_seq),
        grid=(t // rows,),
        in_specs=[pl.BlockSpec((rows, d), lambda i: (i, 0)),
                  pl.BlockSpec((8, d), lambda i: (jnp.maximum(i * hb - 1, 0), 0)),
                  pl.BlockSpec((8, d), lambda i: (jnp.minimum((i + 1) * hb, nhb - 1), 0)),
                  full(g), full(w_bf), full(short_w), full(short_b)],
        out_specs=[pl.BlockSpec((1, d_hy3 // LANES, nf, IN_A, LANES),
                                lambda i: (i // tiles_per_seq, 0, 0, i % tiles_per_seq, 0)),
                   pl.BlockSpec((rows, dp - d_hy3), lambda i: (i, 0))],
        out_shape=[jax.ShapeDtypeStruct((n_seq, d_hy3 // LANES, nf, nf // 2, LANES), F32),
                   jax.ShapeDtypeStruct((t, dp - d_hy3), BF16)],
        scratch_shapes=[pltpu.VMEM((IN_SLABS, IN_A * (nf + 8), LANES), F32)],
        compiler_params=_cparams(1, VMEM_LIMIT),
        name="inproj_shortconv",
    )(x2, x2, x2, g, w_bf, short_w, short_b)


def _cos_sin(m, period):
    ang = m.astype(F32) * (2.0 * math.pi / period)
    return jnp.cos(ang), jnp.sin(ang)


def _stack_complex(mr, mi):
    top = jnp.concatenate([mr, -mi], axis=-1)
    bot = jnp.concatenate([mi, mr], axis=-1)
    return jnp.concatenate([top, bot], axis=-2)


def _dft_tables(nf):
    na = nf // 2
    n = nf * nf
    idx = jnp.arange(nf, dtype=I32)
    c, s = _cos_sin((idx[:, None] * idx[None, :]) % nf, nf)
    f2 = _stack_complex(c, -s)
    f2i = _stack_complex(c, s)
    b_ = idx[:, None, None]
    ka = idx[None, :, None]
    a_ = idx[None, None, :]
    m1 = (nf * a_ * ka + b_ * ka) % n
    c1, s1 = _cos_sin(m1, n)
    f1 = _stack_complex(c1[:, :, :na], -s1[:, :, :na])
    f1_full = jnp.concatenate([c1, -s1], axis=1)
    c3 = jnp.swapaxes(c1, 1, 2)[:, :na, :]
    s3 = jnp.swapaxes(s1, 1, 2)[:, :na, :]
    f3 = _stack_complex(c3, s3)
    return f1, f2, f2i, f3, f1_full


def _split_bf16(x):
    hi = x.astype(BF16)
    lo = (x - hi.astype(F32)).astype(BF16)
    return hi, lo


def _stack_split_lhs(a):
    hi, lo = _split_bf16(a)
    return jnp.concatenate([hi, lo, hi], axis=-1)


def _stack_split_rhs(b):
    hi, lo = _split_bf16(b)
    return jnp.concatenate([hi, hi, lo], axis=-2)


def _filter_taps_kernel(wemb_ref, bemb_ref, win_ref, bin_ref, freq_ref, wout_ref, delta_ref, o_ref,
                        *, seq_len, n_inner, d_hy):
    rows = o_ref.shape[1]
    j = pl.program_id(0) * rows + lax.broadcasted_iota(I32, (rows, 1), 0)
    tidx = jnp.where(j < seq_len, j, 2 * seq_len - j).astype(F32)
    t = tidx / float(seq_len - 1)
    fr0 = jnp.float32(1e-4)
    ang0 = (2.0 * math.pi / seq_len) * tidx * fr0
    ang1 = (2.0 * math.pi / seq_len) * tidx
    freq = freq_ref[...]
    pre = (t * wemb_ref[0:1, :] + jnp.cos(ang0) * wemb_ref[1:2, :] + jnp.cos(ang1) * wemb_ref[2:3, :]
           - jnp.sin(ang0) * wemb_ref[3:4, :] - jnp.sin(ang1) * wemb_ref[4:5, :] + bemb_ref[...])
    hdn = jnp.sin(freq * pre)
    for i in range(n_inner):
        hdn = jnp.sin(freq * (jnp.dot(_stack_split_lhs(hdn), win_ref[i], preferred_element_type=F32)
                              + bin_ref[i:i + 1, :]))
    h = jnp.dot(_stack_split_lhs(hdn), wout_ref[...], preferred_element_type=F32)
    decay = jnp.exp(-t * delta_ref[...])
    for o in range(HY_ORDER):
        fwd = h[:, (2 * o) * d_hy:(2 * o + 1) * d_hy]
        bwd = h[:, (2 * o + 1) * d_hy:(2 * o + 2) * d_hy]
        val = jnp.where(j == seq_len, 0.0, jnp.where(j < seq_len, fwd, bwd) * decay)
        for cc in range(d_hy // LANES):
            o_ref[o * (d_hy // LANES) + cc] = val[:, cc * LANES:(cc + 1) * LANES]


def _filter_taps(wemb, bemb, win, bin_, freq, wout, deltas, seq_len, d_hy):
    rows = 1024
    n2 = 2 * seq_len
    width = wemb.shape[1]
    n_inner = win.shape[0]
    full = lambda *shape: pl.BlockSpec(shape, lambda i: (0,) * len(shape))
    return pl.pallas_call(
        functools.partial(_filter_taps_kernel, seq_len=seq_len, n_inner=n_inner, d_hy=d_hy),
        grid=(n2 // rows,),
        in_specs=[full(FILTER_EMB, width), full(1, width), full(n_inner, 3 * width, width),
                  full(n_inner, width), full(1, width), full(3 * width, 2 * HY_ORDER * d_hy),
                  full(1, d_hy)],
        out_specs=pl.BlockSpec((HY_ORDER * d_hy // LANES, rows, LANES), lambda i: (0, i, 0)),
        out_shape=jax.ShapeDtypeStruct((HY_ORDER * d_hy // LANES, n2, LANES), F32),
        compiler_params=_cparams(1, VMEM_LIMIT),
        name="filter_taps",
    )(wemb, bemb, _stack_split_rhs(win), bin_, freq, _stack_split_rhs(wout), deltas)


def _filter_fft1_kernel(k_ref, f_ref, o_ref, *, nf):
    halves = k_ref.shape[0]
    k = k_ref.reshape(halves, nf * SLAB, LANES)
    g = o_ref.reshape(halves, 2, nf * SLAB, LANES)
    for bl in range(SLAB):
        col = jnp.concatenate([k[h, pl.ds(bl, nf, stride=SLAB), :] for h in range(halves)], axis=1)
        res = jnp.dot(f_ref[bl], _stack_split_rhs(col), preferred_element_type=F32)
        for h in range(halves):
            for ri in range(2):
                g[h, ri, pl.ds(bl, nf, stride=SLAB), :] = res[ri * nf:(ri + 1) * nf,
                                                              h * LANES:(h + 1) * LANES]


def _filter_fft2_kernel(g_ref, f_ref, skip_ref, o_ref, *, nf, scale):
    halves = g_ref.shape[0]
    lag0 = jnp.concatenate([jnp.broadcast_to(skip_ref[...], (nf, skip_ref.shape[1])),
                            jnp.zeros((nf, skip_ref.shape[1]), F32)], axis=0)
    for kl in range(SLAB):
        rhs = jnp.concatenate([jnp.concatenate([g_ref[h, ri, kl] for h in range(halves)], axis=1)
                               for ri in range(2)], axis=0)
        res = jnp.dot(f_ref[...], _stack_split_rhs(rhs), preferred_element_type=F32)
        o_ref[kl] = ((res + lag0) * scale).astype(o_ref.dtype)


def _filter_spectrum(taps, skip, f1_full, f2, nf):
    nch = taps.shape[0]
    c = nch * LANES
    halves = CONV_LANES // LANES
    k4 = taps.reshape(nch, nf, nf, LANES)
    g = pl.pallas_call(
        functools.partial(_filter_fft1_kernel, nf=nf),
        grid=(nf // SLAB, nch // halves),
        in_specs=[pl.BlockSpec((halves, nf, SLAB, LANES), lambda i, j: (j, 0, i, 0)),
                  pl.BlockSpec((SLAB, 2 * nf, 3 * nf), lambda i, j: (i, 0, 0))],
        out_specs=pl.BlockSpec((halves, 2, nf, SLAB, LANES), lambda i, j: (j, 0, 0, i, 0)),
        out_shape=jax.ShapeDtypeStruct((nch, 2, nf, nf, LANES), F32),
        compiler_params=_cparams(2, VMEM_LIMIT),
        name="filter_fft1",
    )(k4, _stack_split_lhs(f1_full))
    return pl.pallas_call(
        functools.partial(_filter_fft2_kernel, nf=nf, scale=1.0 / (nf * nf)),
        grid=(nf // SLAB, nch // halves),
        in_specs=[pl.BlockSpec((halves, 2, SLAB, nf, LANES), lambda i, j: (j, 0, i, 0, 0)),
                  pl.BlockSpec((2 * nf, 6 * nf), lambda i, j: (0, 0)),
                  pl.BlockSpec((1, CONV_LANES), lambda i, j: (0, j))],
        out_specs=pl.BlockSpec((SLAB, 2 * nf, CONV_LANES), lambda i, j: (i, 0, j)),
        out_shape=jax.ShapeDtypeStruct((nf, 2 * nf, c), BF16),
        compiler_params=_cparams(2, VMEM_LIMIT),
        name="filter_fft2",
    )(g, _stack_split_lhs(f2), skip)


def _pack(re, im):
    rb = lax.bitcast_convert_type(re.astype(BF16).astype(F32), U32)
    ib = lax.bitcast_convert_type(im.astype(BF16).astype(F32), U32)
    return rb | (ib >> 16)


def _unpack(w):
    re = lax.bitcast_convert_type(w & jnp.uint32(0xFFFF0000), F32).astype(BF16)
    im = lax.bitcast_convert_type(w << 16, F32).astype(BF16)
    return re, im


def _longconv_kernel(z1_ref, f1_ref, kf_ref, f2_ref, f2i_ref, f3_ref, gate_ref, o_ref, g_ref,
                     *, nf, out_bmajor):
    na = nf // 2
    ns = nf // SLAB
    halves = g_ref.shape[1]
    t = pl.program_id(2)

    def load_rows(ref5, s, bl):
        return jnp.concatenate([ref5[s, h, bl] for h in range(halves)], axis=1)

    def store_col(blk, bl, val):
        for h in range(halves):
            g_ref[blk, h, pl.ds(bl, nf, stride=G_PITCH), :] = val[:, h * LANES:(h + 1) * LANES]

    def load_col(blk, bl):
        return jnp.concatenate([g_ref[blk, h, pl.ds(bl, nf, stride=G_PITCH), :] for h in range(halves)],
                               axis=1)

    @pl.when(t < ns)
    def _phase1():
        for bl in range(SLAB):
            rhs = jnp.concatenate([load_rows(z1_ref, 0, bl), load_rows(z1_ref, 1, bl)],
                                  axis=0).astype(BF16)
            out = jnp.dot(f1_ref[bl], rhs, preferred_element_type=F32)
            store_col(t, bl, _pack(out[:nf], out[nf:]))

    @pl.when(jnp.logical_and(t >= ns, t < 2 * ns))
    def _phase2():
        i = t - ns

        def load_slab(ka):
            row0 = pl.multiple_of(ka * G_PITCH, CHUNK)
            return jnp.concatenate([g_ref[:, h, pl.ds(row0, SLAB), :].reshape(nf, LANES)
                                    for h in range(halves)], axis=1)

        def transform(w, kl):
            re, im = _unpack(w)
            s = jnp.dot(f2_ref[...], jnp.concatenate([re, im], axis=0),
                        preferred_element_type=F32)
            sr, si = s[:nf], s[nf:]
            kr, ki = kf_ref[kl, :nf, :], kf_ref[kl, nf:, :]
            pr = (sr * kr - si * ki).astype(BF16)
            pi = (sr * ki + si * kr).astype(BF16)
            h = jnp.dot(f2i_ref[...], jnp.concatenate([pr, pi], axis=0),
                        preferred_element_type=F32)
            return _pack(h[:nf], h[nf:])

        def store_slab(ka, packed):
            row0 = pl.multiple_of(ka * G_PITCH, CHUNK)
            for hf in range(halves):
                g_ref[:, hf, pl.ds(row0, SLAB), :] = packed[:, hf * LANES:(hf + 1) * LANES].reshape(
                    ns, SLAB, LANES)

        def body(grp, carry):
            kls = [grp * PHASE2_UNROLL + u for u in range(PHASE2_UNROLL)]
            ws = [load_slab(i * SLAB + kl) for kl in kls]
            outs = [transform(w, kl) for w, kl in zip(ws, kls)]
            for kl, packed in zip(kls, outs):
                store_slab(i * SLAB + kl, packed)
            return carry

        lax.fori_loop(0, SLAB // PHASE2_UNROLL, body, 0)

    @pl.when(t >= 2 * ns)
    def _phase3():
        j = t - 2 * ns
        out_flat = None if out_bmajor else o_ref.reshape(2, halves, na * SLAB, LANES)
        for bl in range(SLAB):
            re, im = _unpack(load_col(j, bl))
            y = jnp.dot(f3_ref[bl], jnp.concatenate([re, im], axis=0),
                        preferred_element_type=F32)
            for s in range(2):
                val = load_rows(gate_ref, s, bl) * y[s * na:(s + 1) * na]
                for h in range(halves):
                    piece = val[:, h * LANES:(h + 1) * LANES]
                    if out_bmajor:
                        o_ref[s, h, bl] = piece.astype(o_ref.dtype)
                    else:
                        out_flat[s, h, pl.ds(bl, na, stride=SLAB), :] = piece


def _longconv(gates5, z5, kf, tables, c, gate_blk, z_blk, kf_col, nf, out_bmajor):
    f1, f2, f2i, f3 = tables
    b, _, _, na, _ = z5.shape
    ns = nf // SLAB
    halves = CONV_LANES // LANES
    nchunk = c // CONV_LANES
    p1 = lambda t: jnp.minimum(t, ns - 1)
    p2 = lambda t: jnp.clip(t - ns, 0, ns - 1)
    p3 = lambda t: jnp.clip(t - 2 * ns, 0, ns - 1)
    cl = CONV_LANES
    blk5 = (2, halves, SLAB, na, LANES)
    if out_bmajor:
        out_spec = pl.BlockSpec(blk5, lambda p, q, t: (p, q, p3(t), 0, 0))
        out_shape = (b, c // LANES, nf, na, LANES)
    else:
        out_spec = pl.BlockSpec((2, halves, na, SLAB, LANES), lambda p, q, t: (p, q, 0, p3(t), 0))
        out_shape = (b, c // LANES, na, nf, LANES)
    return pl.pallas_call(
        functools.partial(_longconv_kernel, nf=nf, out_bmajor=out_bmajor),
        grid=(b // 2, nchunk, 3 * ns),
        in_specs=[
            pl.BlockSpec(blk5, lambda p, q, t: (p, z_blk + q, p1(t), 0, 0)),
            pl.BlockSpec((SLAB, 2 * nf, 2 * na), lambda p, q, t: (p1(t), 0, 0)),
            pl.BlockSpec((SLAB, 2 * nf, cl), lambda p, q, t: (p2(t), 0, kf_col + q)),
            pl.BlockSpec((2 * nf, 2 * nf), lambda p, q, t: (0, 0)),
            pl.BlockSpec((2 * nf, 2 * nf), lambda p, q, t: (0, 0)),
            pl.BlockSpec((SLAB, 2 * na, 2 * nf), lambda p, q, t: (p3(t), 0, 0)),
            pl.BlockSpec(blk5, lambda p, q, t: (p, gate_blk + q, p3(t), 0, 0)),
        ],
        out_specs=out_spec,
        out_shape=jax.ShapeDtypeStruct(out_shape, BF16 if out_bmajor else F32),
        scratch_shapes=[pltpu.VMEM((ns, halves, nf * G_PITCH, LANES), U32)],
        compiler_params=_cparams(3, VMEM_LIMIT),
        name="longconv",
    )(z5, f1, kf, f2, f2i, f3, gates5)


def _gelu_exact(x):
    return 0.5 * x * (1.0 + lax.erf(x * (1.0 / math.sqrt(2.0))))


def _gmlp_kernel(gm_ref, lng_ref, lnb_ref, ws_ref, bias_ref, o_ref, *, d_gm):
    g = _gelu_exact(gm_ref[...].astype(F32))
    u, v = g[:, :d_gm], g[:, d_gm:]
    mu = jnp.mean(v, axis=-1, keepdims=True)
    vc = v - mu
    var = jnp.mean(vc * vc, axis=-1, keepdims=True)
    vh = (vc * lax.rsqrt(var + EPS) * lng_ref[...] + lnb_ref[...]).astype(BF16)
    head_dim = d_gm // GM_HEADS
    per_group = CONV_LANES // head_dim
    head = lax.broadcasted_iota(I32, (GM_CHUNK, CONV_LANES), 1) // head_dim
    for c in range(gm_ref.shape[0] // GM_CHUNK):
        rows = slice(c * GM_CHUNK, (c + 1) * GM_CHUNK)
        groups = []
        for grp in range(GM_HEADS // per_group):
            lanes = slice(grp * CONV_LANES, (grp + 1) * CONV_LANES)
            w = ws_ref[grp * per_group * GM_CHUNK:(grp + 1) * per_group * GM_CHUNK, :]
            r = jnp.dot(w, vh[rows, lanes], preferred_element_type=F32)
            s = r[:GM_CHUNK]
            for h in range(1, per_group):
                s = jnp.where(head == h, r[h * GM_CHUNK:(h + 1) * GM_CHUNK], s)
            groups.append(s)
        o_ref[rows, :] = u[rows] * (jnp.concatenate(groups, axis=1) + bias_ref[...])


def _gmlp(gm, ln_g, ln_b, ws_stack, bias_t):
    t, c2 = gm.shape
    d_gm = c2 // 2
    return pl.pallas_call(
        functools.partial(_gmlp_kernel, d_gm=d_gm),
        grid=(t // ROW_TILE,),
        in_specs=[pl.BlockSpec((ROW_TILE, c2), lambda i: (i, 0)),
                  pl.BlockSpec((1, d_gm), lambda i: (0, 0)),
                  pl.BlockSpec((1, d_gm), lambda i: (0, 0)),
                  pl.BlockSpec(ws_stack.shape, lambda i: (0, 0)),
                  pl.BlockSpec(bias_t.shape, lambda i: (0, 0))],
        out_specs=pl.BlockSpec((ROW_TILE, d_gm), lambda i: (i, 0)),
        out_shape=jax.ShapeDtypeStruct((t, d_gm), F32),
        compiler_params=_cparams(1, VMEM_LIMIT),
        name="gmlp",
    )(gm, ln_g, ln_b, ws_stack, bias_t)


def _head_rms(y, bd):
    ms = jnp.dot((y * y).astype(BF16), bd, preferred_element_type=F32)
    return y * lax.rsqrt(ms + EPS)


def _route(logits):
    lane = lax.broadcasted_iota(I32, logits.shape, 1)
    neg = jnp.float32(-1e30)
    big = jnp.int32(ROUTE_LANES)
    gmask = lane < N_GROUPS
    gl = jnp.where(gmask, logits, neg)
    gmax = jnp.max(gl, axis=-1, keepdims=True)
    grp = jnp.min(jnp.where(jnp.logical_and(gl == gmax, gmask), lane, big), axis=-1, keepdims=True)
    psum = jnp.sum(jnp.where(gmask, jnp.exp(gl - gmax), 0.0), axis=-1, keepdims=True)
    p_grp = 1.0 / psum
    lo = N_GROUPS + EXPERTS_PER_GROUP * grp
    emask = jnp.logical_and(lane >= lo, lane < lo + EXPERTS_PER_GROUP)
    el = jnp.where(emask, logits, neg)
    m1 = jnp.max(el, axis=-1, keepdims=True)
    i1 = jnp.min(jnp.where(jnp.logical_and(el == m1, emask), lane, big), axis=-1, keepdims=True)
    emask2 = jnp.logical_and(emask, lane != i1)
    el2 = jnp.where(emask2, logits, neg)
    m2 = jnp.max(el2, axis=-1, keepdims=True)
    i2 = jnp.min(jnp.where(jnp.logical_and(el2 == m2, emask2), lane, big), axis=-1, keepdims=True)
    d = jnp.exp(m2 - m1)
    g1 = p_grp * (1.0 / (1.0 + d))
    g2 = p_grp * (d / (1.0 + d))
    e1 = (i1 - N_GROUPS).astype(F32)
    e2 = (i2 - N_GROUPS).astype(F32)
    return jnp.where(lane == 0, e1, jnp.where(lane == 1, e2, jnp.where(lane == 2, g1,
                     jnp.where(lane == 3, g2, 0.0))))


def _lane_cumsum(v):
    lane = lax.broadcasted_iota(I32, v.shape, 1)
    sh = 1
    while sh < v.shape[1]:
        v = v + jnp.where(lane >= sh, pltpu.roll(v, sh, 1), 0.0)
        sh *= 2
    return v


def _outproj_kernel(x_ref, yh_ref, yg_ref, mg_ref, bd_ref, wo_ref, fg_ref, wr_ref, br_ref,
                    x1_ref, stage_ref, route_ref, count_ref, *, d_hy):
    bd = bd_ref[...]
    mg = mg_ref[...]
    wo = wo_ref[...]
    wr = wr_ref[...]

    def dense(rows):
        yh = jnp.concatenate([yh_ref[0, c, rows, :] for c in range(yh_ref.shape[1])], axis=1)
        mh = (_head_rms(yh, bd) * mg[:, :d_hy]).astype(BF16)
        mgm = (_head_rms(yg_ref[rows, :], bd) * mg[:, d_hy:]).astype(BF16)
        x1 = (x_ref[rows, :] + jnp.dot(mh, wo[:d_hy], preferred_element_type=F32)
              + jnp.dot(mgm, wo[d_hy:], preferred_element_type=F32))
        x1_ref[rows, :] = x1
        ms = jnp.mean(x1 * x1, axis=-1, keepdims=True)
        hi, lo = _split_bf16(x1 * lax.rsqrt(ms + EPS) * fg_ref[...])
        hw = jnp.dot(hi, wr, preferred_element_type=F32)
        logits = (hw[:, :ROUTE_LANES] + hw[:, ROUTE_LANES:]
                  + jnp.dot(lo, wr[:, :ROUTE_LANES], preferred_element_type=F32) + br_ref[...])
        return hi, _route(logits)

    n_rows = x_ref.shape[0]
    parts = [dense(slice(r0, r0 + DENSE_ROWS)) for r0 in range(0, n_rows, DENSE_ROWS)]
    hi = jnp.concatenate([p[0] for p in parts], axis=0)
    route = jnp.concatenate([p[1] for p in parts], axis=0)

    lane = lax.broadcasted_iota(I32, (MOE_TILE, ROUTE_LANES), 1)
    r = lax.broadcasted_iota(I32, (MOE_TILE, MOE_TILE), 0)
    c = lax.broadcasted_iota(I32, (MOE_TILE, MOE_TILE), 1)
    ltri = jnp.where(c < r, 1.0, 0.0).astype(BF16)
    srow = lax.broadcasted_iota(I32, (STAGE_ROWS, MOE_TILE), 0)
    mlane = lax.broadcasted_iota(I32, (STAGE_ROWS, LANES), 1)
    half = hi.shape[1] // 2
    subs = route.shape[0] // MOE_TILE
    for sub in range(subs):
        rows = slice(sub * MOE_TILE, (sub + 1) * MOE_TILE)
        rsub = route[rows]
        oh0 = lane == rsub[:, 0:1].astype(I32)
        oh1 = lane == rsub[:, 1:2].astype(I32)
        oh = jnp.where(jnp.logical_or(oh0, oh1), 1.0, 0.0)
        before = jnp.dot(ltri, oh.astype(BF16), preferred_element_type=F32)
        cnt = jnp.sum(oh, axis=0, keepdims=True)
        run = jnp.floor((cnt + (CHUNK - 1)) * (1.0 / CHUNK)) * CHUNK
        run_start = _lane_cumsum(jnp.broadcast_to(run, (8, run.shape[1])))[0:1] - run
        tot = before + run_start
        loc0 = jnp.sum(jnp.where(oh0, tot, 0.0), axis=-1, keepdims=True)
        loc1 = jnp.sum(jnp.where(oh1, tot, 0.0), axis=-1, keepdims=True)
        rsub = jnp.where(lane == 4, loc0, jnp.where(lane == 5, loc1, rsub))
        route_ref[rows, :] = rsub
        count_ref[pl.ds(pl.program_id(0) * subs + sub, 1), :] = cnt

        rt = jnp.transpose(rsub)
        p0 = srow == rt[4:5, :].astype(I32)
        p1 = srow == rt[5:6, :].astype(I32)
        perm = jnp.where(jnp.logical_or(p0, p1), 1.0, 0.0).astype(BF16)
        staged = jnp.dot(perm, hi[rows], preferred_element_type=F32)
        gate = jnp.sum(jnp.where(p0, rt[2:3, :], 0.0) + jnp.where(p1, rt[3:4, :], 0.0), axis=-1,
                       keepdims=True)
        meta = jnp.where(mlane == 0,
                         lax.bitcast_convert_type(jnp.broadcast_to(gate, (STAGE_ROWS, LANES)), U32),
                         jnp.uint32(0))
        stage_ref[sub * STAGE_ROWS:(sub + 1) * STAGE_ROWS, :] = jnp.concatenate(
            [_pack(staged[:, :half], staged[:, half:]), meta], axis=1)


def _outproj(x2, yh, yg, mix_g, bd, wo_bf, ffn_g, wr_cat, br):
    t, d = x2.shape
    _, nch, seq_len, _ = yh.shape
    d_hy = nch * LANES
    tiles_per_seq = seq_len // ROW_TILE
    n_tiles = t // ROW_TILE
    row = lambda w: pl.BlockSpec((ROW_TILE, w), lambda i: (i, 0))
    full = lambda a: pl.BlockSpec(a.shape, lambda i: (0,) * a.ndim)
    sw = d // 2 + LANES
    subs = ROW_TILE // MOE_TILE
    n_sub = t // MOE_TILE
    return pl.pallas_call(
        functools.partial(_outproj_kernel, d_hy=d_hy),
        grid=(n_tiles,),
        in_specs=[row(d),
                  pl.BlockSpec((1, nch, ROW_TILE, LANES),
                               lambda i: (i // tiles_per_seq, 0, i % tiles_per_seq, 0)),
                  row(yg.shape[1]), full(mix_g), full(bd), full(wo_bf), full(ffn_g),
                  full(wr_cat), full(br)],
        out_specs=[row(d), pl.BlockSpec((subs * STAGE_ROWS, sw), lambda i: (i, 0)), row(ROUTE_LANES),
                   pl.BlockSpec((n_sub, ROUTE_LANES), lambda i: (0, 0))],
        out_shape=[jax.ShapeDtypeStruct((t, d), F32),
                   jax.ShapeDtypeStruct((n_sub * STAGE_ROWS, sw), U32),
                   jax.ShapeDtypeStruct((t, ROUTE_LANES), F32),
                   jax.ShapeDtypeStruct((n_sub, ROUTE_LANES), F32)],
        compiler_params=_cparams(1, VMEM_LIMIT),
        name="outproj_router",
    )(x2, yh, yg, mix_g, bd, wo_bf, ffn_g, wr_cat, br)


def _moe_plan(tile_cnt, n_tokens):
    n_tiles = tile_cnt.shape[0]
    cnt = tile_cnt[:, :N_EXPERTS].astype(I32)
    run = (cnt + CHUNK - 1) // CHUNK * CHUNK
    run_start = jnp.cumsum(run, axis=1) - run
    used = jnp.sum(run, axis=1)
    e_rows = jnp.sum(run, axis=0)
    e_pad = (e_rows + EXPERT_ROWS - 1) // EXPERT_ROWS * EXPERT_ROWS
    e_end = jnp.cumsum(e_pad)
    e_start = e_end - e_pad
    n_blocks = -(-(n_tokens * TOP_K + n_tiles * N_EXPERTS * (CHUNK - 1)) // EXPERT_ROWS) + N_EXPERTS
    n_used = (e_end[-1:] // EXPERT_ROWS).astype(I32)
    block_row = jnp.arange(n_blocks, dtype=I32) * EXPERT_ROWS
    block_e = jnp.minimum(jnp.sum((e_end[None, :] <= block_row[:, None]).astype(I32), axis=1),
                          N_EXPERTS - 1)
    run_end_in_e = jnp.cumsum(run, axis=0)
    chunk_row = jnp.arange(n_blocks * EXPERT_ROWS // CHUNK, dtype=I32) * CHUNK
    e_of_chunk = jnp.repeat(block_e, EXPERT_ROWS // CHUNK)
    e_sel = jnp.arange(N_EXPERTS, dtype=I32)[None, :] == e_of_chunk[:, None]
    pick = lambda tbl: jnp.dot(e_sel.astype(F32), tbl.astype(F32),
                               precision=lax.Precision.HIGHEST).astype(I32)
    q = chunk_row - pick(e_start[:, None])[:, 0]
    valid = q < pick(e_rows[:, None])[:, 0]
    ends = pick(run_end_in_e.T)
    tile = jnp.minimum(jnp.sum((ends <= q[:, None]).astype(I32), axis=1), n_tiles - 1)
    t_sel = jnp.arange(n_tiles, dtype=I32)[None, :] == tile[:, None]
    at_tile = lambda tbl: jnp.sum(jnp.where(t_sel, pick(tbl.T), 0), axis=1)
    within = q - (at_tile(run_end_in_e) - at_tile(run))
    src_row = tile * STAGE_ROWS + at_tile(run_start) + within
    zero_chunk = STAGE_ROWS // CHUNK - 1
    per_block = EXPERT_ROWS // CHUNK
    src_chunk = jnp.where(valid, src_row // CHUNK, zero_chunk).reshape(n_blocks, 1, per_block)
    dst_chunk = jnp.where(valid, src_row // CHUNK, 0).reshape(n_blocks, 1, per_block)
    n_valid = jnp.sum(valid.reshape(n_blocks, per_block).astype(I32), axis=1)
    return (block_e.astype(I32), n_used, n_valid.astype(I32), src_chunk.astype(I32),
            dst_chunk.astype(I32), used.astype(I32))


def _ffn_kernel(be_ref, nu_ref, nv_ref, used_ref, src_now_ref, src_next_ref, dst_ref, stage_ref,
                wg_ref, wu_ref, wd_ref, ost_ref, xbuf, obuf, zero_ref, in_sem, out_sem, zero_sem):
    del be_ref
    b = pl.program_id(0)
    nu = nu_ref[0]
    per_block = dst_ref.shape[-1]
    slot = b % 2

    def fetch(map_ref, s):
        def body(j, carry):
            row = pl.multiple_of(map_ref[0, 0, j] * CHUNK, CHUNK)
            pltpu.make_async_copy(stage_ref.at[pl.ds(row, CHUNK)], xbuf.at[s, pl.ds(j * CHUNK, CHUNK)],
                                  in_sem.at[s]).start()
            return carry
        lax.fori_loop(0, per_block, body, 0, unroll=8)

    def out_copy(j, chunk, s):
        row = pl.multiple_of(chunk * CHUNK, CHUNK)
        return pltpu.make_async_copy(obuf.at[s, pl.ds(j * CHUNK, CHUNK)], ost_ref.at[pl.ds(row, CHUNK)],
                                     out_sem.at[s])

    def drain(s, n):
        @pl.when(n == per_block)
        def _():
            pltpu.make_async_copy(obuf.at[s], ost_ref.at[pl.ds(0, EXPERT_ROWS)], out_sem.at[s]).wait()

        @pl.when(n != per_block)
        def _():
            def body(j, carry):
                out_copy(0, 0, s).wait()
                return carry
            lax.fori_loop(0, n, body, 0)

    def zero_tails(wait):
        def tile(i, carry):
            def chunk(c, carry2):
                row = pl.multiple_of(i * STAGE_ROWS + c * CHUNK, CHUNK)
                cp = pltpu.make_async_copy(zero_ref, ost_ref.at[pl.ds(row, CHUNK)], zero_sem)
                if wait:
                    cp.wait()
                else:
                    cp.start()
                return carry2
            return lax.fori_loop(used_ref[i] // CHUNK, STAGE_ROWS // CHUNK, chunk, carry)
        lax.fori_loop(0, used_ref.shape[0], tile, 0)

    @pl.when(b == 0)
    def _prologue():
        fetch(src_now_ref, 0)
        zero_ref[...] = jnp.zeros_like(zero_ref)
        zero_tails(wait=False)
        zero_tails(wait=True)

    @pl.when(b + 1 < nu)
    def _prefetch_next():
        fetch(src_next_ref, 1 - slot)

    @pl.when(b < nu)
    def _compute():
        pltpu.make_async_copy(stage_ref.at[pl.ds(0, EXPERT_ROWS)], xbuf.at[slot], in_sem.at[slot]).wait()

        @pl.when(b >= 2)
        def _():
            drain(slot, nv_ref[b - 2])

        x = xbuf[slot]
        half = wg_ref.shape[1] // 2
        xb = jnp.concatenate(_unpack(x[:, :half]), axis=1)
        gate = lax.bitcast_convert_type(x[:, half:half + 1], F32)
        g = jnp.dot(xb, wg_ref[0], preferred_element_type=F32)
        u = jnp.dot(xb, wu_ref[0], preferred_element_type=F32)
        h = (g * (1.0 / (1.0 + jnp.exp(-g))) * u).astype(BF16)
        o = jnp.dot(h, wd_ref[0], preferred_element_type=F32) * gate
        obuf[slot] = _pack(o[:, :half], o[:, half:])

        def body(j, carry):
            out_copy(j, dst_ref[0, 0, j], slot).start()
            return carry

        @pl.when(nv_ref[b] == per_block)
        def _():
            lax.fori_loop(0, per_block, body, 0, unroll=8)

        @pl.when(nv_ref[b] != per_block)
        def _():
            lax.fori_loop(0, nv_ref[b], body, 0)

        @pl.when(b == nu - 1)
        def _():
            drain(slot, nv_ref[b])

            @pl.when(b >= 1)
            def _():
                drain(1 - slot, nv_ref[b - 1])


def _expert_ffn(stage, plan, wg, wu, wd):
    block_e, n_used, n_valid, src_chunk, dst_chunk, used = plan
    nb = block_e.shape[0]
    sw = stage.shape[1]
    d, de = wg.shape[1], wg.shape[2]
    per_block = EXPERT_ROWS // CHUNK
    cur = lambda i, be, nu: jnp.minimum(i, nu[0] - 1)
    wspec = lambda shape: pl.BlockSpec(shape, lambda i, be, nu, nv, us: (be[cur(i, be, nu)], 0, 0))
    smem = lambda imap: pl.BlockSpec((1, 1, per_block), imap, memory_space=pltpu.SMEM)
    return pl.pallas_call(
        _ffn_kernel,
        grid_spec=pltpu.PrefetchScalarGridSpec(
            num_scalar_prefetch=4,
            grid=(nb,),
            in_specs=[smem(lambda i, be, nu, nv, us: (i, 0, 0)),
                      smem(lambda i, be, nu, nv, us: (jnp.minimum(i + 1, nb - 1), 0, 0)),
                      smem(lambda i, be, nu, nv, us: (i, 0, 0)),
                      pl.BlockSpec(memory_space=pl.ANY),
                      wspec((1, d, de)), wspec((1, d, de)), wspec((1, de, d))],
            out_specs=pl.BlockSpec(memory_space=pl.ANY),
            scratch_shapes=[pltpu.VMEM((2, EXPERT_ROWS, sw), U32), pltpu.VMEM((2, EXPERT_ROWS, d // 2), U32),
                            pltpu.VMEM((CHUNK, d // 2), U32),
                            pltpu.SemaphoreType.DMA((2,)), pltpu.SemaphoreType.DMA((2,)),
                            pltpu.SemaphoreType.DMA(())]),
        out_shape=jax.ShapeDtypeStruct((stage.shape[0], d // 2), U32),
        compiler_params=_cparams(1, VMEM_LIMIT),
        name="expert_ffn",
    )(block_e, n_used, n_valid, used, src_chunk, src_chunk, dst_chunk, stage, wg, wu, wd)


def _combine_kernel(ost_ref, route_ref, x1_ref, fg_ref, o_ref):
    col = lax.broadcasted_iota(I32, (MOE_TILE, STAGE_ROWS), 1)
    for sub in range(route_ref.shape[0] // MOE_TILE):
        rows = slice(sub * MOE_TILE, (sub + 1) * MOE_TILE)
        route = route_ref[rows, :]
        sel = jnp.logical_or(col == route[:, 4:5].astype(I32), col == route[:, 5:6].astype(I32))
        perm = jnp.where(sel, 1.0, 0.0).astype(BF16)
        ost = ost_ref[sub * STAGE_ROWS:(sub + 1) * STAGE_ROWS, :]
        y = jnp.dot(perm, jnp.concatenate(_unpack(ost), axis=1), preferred_element_type=F32)
        x2 = x1_ref[rows, :] + y
        ms = jnp.mean(x2 * x2, axis=-1, keepdims=True)
        o_ref[rows, :] = x2 * lax.rsqrt(ms + EPS) * fg_ref[...]


def _combine(ostage, route, x1, final_g):
    t, d = x1.shape
    return pl.pallas_call(
        _combine_kernel,
        grid=(t // ROW_TILE,),
        in_specs=[pl.BlockSpec((ROW_TILE // MOE_TILE * STAGE_ROWS, d // 2), lambda i: (i, 0)),
                  pl.BlockSpec((ROW_TILE, ROUTE_LANES), lambda i: (i, 0)),
                  pl.BlockSpec((ROW_TILE, d), lambda i: (i, 0)),
                  pl.BlockSpec((1, d), lambda i: (0, 0))],
        out_specs=pl.BlockSpec((ROW_TILE, d), lambda i: (i, 0)),
        out_shape=jax.ShapeDtypeStruct((t, d), F32),
        compiler_params=_cparams(1, VMEM_LIMIT),
        name="moe_combine",
    )(ostage, route, x1, final_g)


def _hier_moe_and_norm(x1, stage, route, tile_cnt, wg, wu, wd, final_g):
    t, d = x1.shape
    ostage = _expert_ffn(stage, _moe_plan(tile_cnt, t), wg, wu, wd)
    return _combine(ostage, route, x1, final_g)


def _encoder(x, prm, kf, tables, nf):
    b, l, d = x.shape
    t = b * l
    d_hy = prm["skip"].shape[1]
    x2 = x.reshape(t, d)
    gates5, gm = _inproj(x2, prm["mix_norm_g"], prm["w_in"], prm["short_w"], prm["short_b"], b, l, nf,
                         (HY_ORDER + 1) * d_hy)
    z5 = gates5
    z_blk = HY_ORDER * d_hy // CONV_LANES
    for o in range(HY_ORDER):
        z5 = _longconv(gates5, z5, kf, tables, d_hy, gate_blk=o * d_hy // CONV_LANES,
                       z_blk=z_blk, kf_col=o * d_hy // CONV_LANES, nf=nf, out_bmajor=o < HY_ORDER - 1)
        z_blk = 0
    y_hy = z5.reshape(b, d_hy // LANES, l, LANES)
    y_gm = _gmlp(gm, prm["ln_g"], prm["ln_b"], prm["ws_stack"], prm["bias_t"])
    x1, stage, route, tile_cnt = _outproj(x2, y_hy, y_gm, prm["mix_out_g"], prm["bd"], prm["w_out"], prm["ffn_norm_g"],
                             prm["wr_cat"], prm["br"])
    out = _hier_moe_and_norm(x1, stage, route, tile_cnt, prm["w_e_gate"], prm["w_e_up"], prm["w_e_down"],
                             prm["final_norm_g"])
    return out.reshape(b, l, d)


def kernel(x_prompt, x_sample, mix_norm_g, w_in, hy_short_w, hy_short_b, hy_filt_w_emb, hy_filt_b_emb,
           hy_filt_w_inner, hy_filt_b_inner, hy_filt_freq, hy_filt_w_out, hy_skip, gm_ln_g, gm_ln_b,
           gm_w_s, gm_b_s, mix_out_g, w_out, ffn_norm_g, w_group, b_group, w_expert_router,
           b_expert_router, w_e_gate, w_e_up, w_e_down, final_norm_g):
    assert w_in.shape[0] == 1, "one layer"
    l = x_prompt.shape[1]
    assert x_sample.shape[1] == l
    nf = math.isqrt(2 * l)
    assert nf * nf == 2 * l and nf % SLAB == 0
    d = x_prompt.shape[2]
    d_hy = hy_skip.shape[2]
    d_gm = gm_ln_g.shape[1]
    head_dim = d_gm // GM_HEADS
    assert d_hy // HY_HEADS == head_dim and d_hy == d_gm

    tables = _dft_tables(nf)
    f1, f2, f2i, f3, f1_full = tables
    tables_bf = tuple(a.astype(BF16) for a in (f1, f2, f2i, f3))

    max_decay = math.log(DECAY_TARGET) / FAST_DECAY_PCT
    min_decay = math.log(DECAY_TARGET) / SLOW_DECAY_PCT
    deltas = jnp.abs(jnp.linspace(min_decay, max_decay, d_hy, dtype=F32))[None, :]
    taps = _filter_taps(hy_filt_w_emb[0], hy_filt_b_emb[0][None, :], hy_filt_w_inner[0], hy_filt_b_inner[0],
                        hy_filt_freq[0][None, :], hy_filt_w_out[0], deltas, l, d_hy)
    kf = _filter_spectrum(taps, hy_skip[0].reshape(1, HY_ORDER * d_hy), f1_full, f2, nf)

    n_route = N_GROUPS + N_EXPERTS
    wr = jnp.zeros((d, ROUTE_LANES), F32).at[:, :n_route].set(
        jnp.concatenate([w_group[0], w_expert_router[0]], axis=1))
    wr_hi = wr.astype(BF16)
    wr_lo = (wr - wr_hi.astype(F32)).astype(BF16)
    br = jnp.zeros((1, ROUTE_LANES), F32).at[0, :n_route].set(jnp.concatenate([b_group[0], b_expert_router[0]]))
    hid = jnp.arange(d_hy, dtype=I32) // head_dim
    bd = jnp.where(hid[:, None] == hid[None, :], 1.0 / head_dim, 0.0).astype(BF16)

    prm = dict(
        mix_norm_g=mix_norm_g, w_in=w_in[0].astype(BF16), short_w=hy_short_w[0], short_b=hy_short_b,
        skip=hy_skip[0], ln_g=gm_ln_g, ln_b=gm_ln_b,
        ws_stack=gm_w_s[0].reshape(GM_HEADS * GM_CHUNK, GM_CHUNK).astype(BF16),
        bias_t=jnp.repeat(gm_b_s[0].T, head_dim, axis=1),
        mix_out_g=mix_out_g, bd=bd, w_out=w_out[0].astype(BF16), ffn_norm_g=ffn_norm_g,
        wr_cat=jnp.concatenate([wr_hi, wr_lo], axis=1), br=br,
        w_e_gate=w_e_gate[0].astype(BF16), w_e_up=w_e_up[0].astype(BF16), w_e_down=w_e_down[0].astype(BF16),
        final_norm_g=final_norm_g[None, :])
    y_prompt = _encoder(x_prompt, prm, kf, tables_bf, nf)
    y_sample = _encoder(x_sample, prm, kf, tables_bf, nf)
    return (y_prompt, y_sample)
```

```python
import functools
import math

import jax
import jax.numpy as jnp
from jax import lax
from jax.experimental import pallas as pl
from jax.experimental.pallas import tpu as pltpu

F32 = jnp.float32
BF16 = jnp.bfloat16
U32 = jnp.uint32
I32 = jnp.int32

EPS = 1e-6
HY_ORDER = 2
HY_HEADS = 8
GM_HEADS = 8
GM_CHUNK = 128
N_GROUPS = 4
EXPERTS_PER_GROUP = 8
N_EXPERTS = N_GROUPS * EXPERTS_PER_GROUP
TOP_K = 2
FILTER_EMB = 5
DECAY_TARGET = 1e-2
FAST_DECAY_PCT = 0.3
SLOW_DECAY_PCT = 1.5

LANES = 128
ROUTE_LANES = LANES
SLAB = 16
G_PITCH = SLAB + 8
CONV_LANES = 256
PHASE2_UNROLL = 8
ROW_TILE = 512
IN_A = 8
IN_SLABS = 4
EXPERT_ROWS = 1024
CHUNK = 8
MOE_TILE = 512
DENSE_ROWS = 256
STAGE_ROWS = MOE_TILE * TOP_K + N_EXPERTS * CHUNK
VMEM_LIMIT = 56 * 1024 * 1024


def _cparams(n_axes, vmem=None):
    return pltpu.CompilerParams(dimension_semantics=("arbitrary",) * n_axes,
                                vmem_limit_bytes=vmem)


def _inproj_kernel(x_ref, xp_ref, xn_ref, g_ref, w_ref, sw_ref, sb_ref, hy_ref, gm_ref, stage_ref,
                   *, nf, d_hy3, tiles_per_seq):
    i = pl.program_id(0)
    rows = x_ref.shape[0]
    pitch = stage_ref.shape[1] // IN_A
    halo = xp_ref.shape[0]
    x = jnp.concatenate([xp_ref[...], x_ref[...], xn_ref[...]], axis=0)
    ms = jnp.mean(x * x, axis=-1, keepdims=True)
    n = (x * lax.rsqrt(ms + EPS) * g_ref[...]).astype(BF16)
    gm = jnp.dot(n, w_ref[:, d_hy3:], preferred_element_type=F32)
    gm_ref[...] = gm[halo:halo + rows].astype(BF16)

    pos = i % tiles_per_seq
    has_prev = jnp.where(pos == 0, 0.0, 1.0)
    has_next = jnp.where(pos == tiles_per_seq - 1, 0.0, 1.0)
    row = lax.broadcasted_iota(I32, (rows, 1), 0)
    prev_w = jnp.where(row == 0, has_prev, 1.0)
    next_w = jnp.where(row == rows - 1, has_next, 1.0)
    n_slabs = stage_ref.shape[0]
    cols = n_slabs * LANES
    ext = rows + 2 * halo
    for c0 in range(0, d_hy3, cols):
        p = jnp.dot(n, w_ref[:, c0:c0 + cols], preferred_element_type=F32)
        prev = pltpu.roll(p, 1, 0)[halo:halo + rows] * prev_w
        nxt = pltpu.roll(p, ext - 1, 0)[halo:halo + rows] * next_w
        res = (sw_ref[0:1, c0:c0 + cols] * prev + sw_ref[1:2, c0:c0 + cols] * p[halo:halo + rows]
               + sw_ref[2:3, c0:c0 + cols] * nxt + sb_ref[:, c0:c0 + cols])
        for s in range(n_slabs):
            for al in range(IN_A):
                stage_ref[s, al * pitch:al * pitch + nf, :] = res[al * nf:(al + 1) * nf,
                                                                s * LANES:(s + 1) * LANES]
        for s in range(n_slabs):
            for b in range(nf):
                hy_ref[0, c0 // LANES + s, b] = stage_ref[s, pl.ds(b, IN_A, stride=pitch), :]


def _inproj(x2, g, w_bf, short_w, short_b, n_seq, seq_len, nf, d_hy3):
    t, d = x2.shape
    dp = w_bf.shape[1]
    rows = IN_A * nf
    tiles_per_seq = seq_len // rows
    hb = rows // 8
    nhb = t // 8
    full = lambda a: pl.BlockSpec(a.shape, lambda i: (0,) * a.ndim)
    return pl.pallas_call(
        functools.partial(_inproj_kernel, nf=nf, d_hy3=d_hy3, tiles_per_seq=tiles_per_seq),
        grid=(t // rows,),
        in_specs=[pl.BlockSpec((rows, d), lambda i: (i, 0)),
                  pl.BlockSpec((8, d), lambda i: (jnp.maximum(i * hb - 1, 0), 0)),
                  pl.BlockSpec((8, d), lambda i: (jnp.minimum((i + 1) * hb, nhb - 1), 0)),
                  full(g), full(w_bf), full(short_w), full(short_b)],
        out_specs=[pl.BlockSpec((1, d_hy3 // LANES, nf, IN_A, LANES),
                                lambda i: (i // tiles_per_seq, 0, 0, i % tiles_per_seq, 0)),
                   pl.BlockSpec((rows, dp - d_hy3), lambda i: (i, 0))],
        out_shape=[jax.ShapeDtypeStruct((n_seq, d_hy3 // LANES, nf, nf // 2, LANES), F32),
                   jax.ShapeDtypeStruct((t, dp - d_hy3), BF16)],
        scratch_shapes=[pltpu.VMEM((IN_SLABS, IN_A * (nf + 8), LANES), F32)],
        compiler_params=_cparams(1, VMEM_LIMIT),
        name="inproj_shortconv",
    )(x2, x2, x2, g, w_bf, short_w, short_b)


def _cos_sin(m, period):
    ang = m.astype(F32) * (2.0 * math.pi / period)
    return jnp.cos(ang), jnp.sin(ang)


def _stack_complex(mr, mi):
    top = jnp.concatenate([mr, -mi], axis=-1)
    bot = jnp.concatenate([mi, mr], axis=-1)
    return jnp.concatenate([top, bot], axis=-2)


def _dft_tables(nf):
    na = nf // 2
    n = nf * nf
    idx = jnp.arange(nf, dtype=I32)
    c, s = _cos_sin((idx[:, None] * idx[None, :]) % nf, nf)
    f2 = _stack_complex(c, -s)
    f2i = _stack_complex(c, s)
    b_ = idx[:, None, None]
    ka = idx[None, :, None]
    a_ = idx[None, None, :]
    m1 = (nf * a_ * ka + b_ * ka) % n
    c1, s1 = _cos_sin(m1, n)
    f1 = _stack_complex(c1[:, :, :na], -s1[:, :, :na])
    f1_full = jnp.concatenate([c1, -s1], axis=1)
    c3 = jnp.swapaxes(c1, 1, 2)[:, :na, :]
    s3 = jnp.swapaxes(s1, 1, 2)[:, :na, :]
    f3 = _stack_complex(c3, s3)
    return f1, f2, f2i, f3, f1_full


def _split_bf16(x):
    hi = x.astype(BF16)
    lo = (x - hi.astype(F32)).astype(BF16)
    return hi, lo


def _stack_split_lhs(a):
    hi, lo = _split_bf16(a)
    return jnp.concatenate([hi, lo, hi], axis=-1)


def _stack_split_rhs(b):
    hi, lo = _split_bf16(b)
    return jnp.concatenate([hi, hi, lo], axis=-2)


def _filter_taps_kernel(wemb_ref, bemb_ref, win_ref, bin_ref, freq_ref, wout_ref, delta_ref, o_ref,
                        *, seq_len, n_inner, d_hy):
    rows = o_ref.shape[1]
    j = pl.program_id(0) * rows + lax.broadcasted_iota(I32, (rows, 1), 0)
    tidx = jnp.where(j < seq_len, j, 2 * seq_len - j).astype(F32)
    t = tidx / float(seq_len - 1)
    fr0 = jnp.float32(1e-4)
    ang0 = (2.0 * math.pi / seq_len) * tidx * fr0
    ang1 = (2.0 * math.pi / seq_len) * tidx
    freq = freq_ref[...]
    pre = (t * wemb_ref[0:1, :] + jnp.cos(ang0) * wemb_ref[1:2, :] + jnp.cos(ang1) * wemb_ref[2:3, :]
           - jnp.sin(ang0) * wemb_ref[3:4, :] - jnp.sin(ang1) * wemb_ref[4:5, :] + bemb_ref[...])
    hdn = jnp.sin(freq * pre)
    for i in range(n_inner):
        hdn = jnp.sin(freq * (jnp.dot(_stack_split_lhs(hdn), win_ref[i], preferred_element_type=F32)
                              + bin_ref[i:i + 1, :]))
    h = jnp.dot(_stack_split_lhs(hdn), wout_ref[...], preferred_element_type=F32)
    decay = jnp.exp(-t * delta_ref[...])
    for o in range(HY_ORDER):
        fwd = h[:, (2 * o) * d_hy:(2 * o + 1) * d_hy]
        bwd = h[:, (2 * o + 1) * d_hy:(2 * o + 2) * d_hy]
        val = jnp.where(j == seq_len, 0.0, jnp.where(j < seq_len, fwd, bwd) * decay)
        for cc in range(d_hy // LANES):
            o_ref[o * (d_hy // LANES) + cc] = val[:, cc * LANES:(cc + 1) * LANES]


def _filter_taps(wemb, bemb, win, bin_, freq, wout, deltas, seq_len, d_hy):
    rows = 1024
    n2 = 2 * seq_len
    width = wemb.shape[1]
    n_inner = win.shape[0]
    full = lambda *shape: pl.BlockSpec(shape, lambda i: (0,) * len(shape))
    return pl.pallas_call(
        functools.partial(_filter_taps_kernel, seq_len=seq_len, n_inner=n_inner, d_hy=d_hy),
        grid=(n2 // rows,),
        in_specs=[full(FILTER_EMB, width), full(1, width), full(n_inner, 3 * width, width),
                  full(n_inner, width), full(1, width), full(3 * width, 2 * HY_ORDER * d_hy),
                  full(1, d_hy)],
        out_specs=pl.BlockSpec((HY_ORDER * d_hy // LANES, rows, LANES), lambda i: (0, i, 0)),
        out_shape=jax.ShapeDtypeStruct((HY_ORDER * d_hy // LANES, n2, LANES), F32),
        compiler_params=_cparams(1, VMEM_LIMIT),
        name="filter_taps",
    )(wemb, bemb, _stack_split_rhs(win), bin_, freq, _stack_split_rhs(wout), deltas)


def _filter_fft1_kernel(k_ref, f_ref, o_ref, *, nf):
    halves = k_ref.shape[0]
    k = k_ref.reshape(halves, nf * SLAB, LANES)
    g = o_ref.reshape(halves, 2, nf * SLAB, LANES)
    for bl in range(SLAB):
        col = jnp.concatenate([k[h, pl.ds(bl, nf, stride=SLAB), :] for h in range(halves)], axis=1)
        res = jnp.dot(f_ref[bl], _stack_split_rhs(col), preferred_element_type=F32)
        for h in range(halves):
            for ri in range(2):
                g[h, ri, pl.ds(bl, nf, stride=SLAB), :] = res[ri * nf:(ri + 1) * nf,
                                                              h * LANES:(h + 1) * LANES]


def _filter_fft2_kernel(g_ref, f_ref, skip_ref, o_ref, *, nf, scale):
    halves = g_ref.shape[0]
    lag0 = jnp.concatenate([jnp.broadcast_to(skip_ref[...], (nf, skip_ref.shape[1])),
                            jnp.zeros((nf, skip_ref.shape[1]), F32)], axis=0)
    for kl in range(SLAB):
        rhs = jnp.concatenate([jnp.concatenate([g_ref[h, ri, kl] for h in range(halves)], axis=1)
                               for ri in range(2)], axis=0)
        res = jnp.dot(f_ref[...], _stack_split_rhs(rhs), preferred_element_type=F32)
        o_ref[kl] = ((res + lag0) * scale).astype(o_ref.dtype)


def _filter_spectrum(taps, skip, f1_full, f2, nf):
    nch = taps.shape[0]
    c = nch * LANES
    halves = CONV_LANES // LANES
    k4 = taps.reshape(nch, nf, nf, LANES)
    g = pl.pallas_call(
        functools.partial(_filter_fft1_kernel, nf=nf),
        grid=(nf // SLAB, nch // halves),
        in_specs=[pl.BlockSpec((halves, nf, SLAB, LANES), lambda i, j: (j, 0, i, 0)),
                  pl.BlockSpec((SLAB, 2 * nf, 3 * nf), lambda i, j: (i, 0, 0))],
        out_specs=pl.BlockSpec((halves, 2, nf, SLAB, LANES), lambda i, j: (j, 0, 0, i, 0)),
        out_shape=jax.ShapeDtypeStruct((nch, 2, nf, nf, LANES), F32),
        compiler_params=_cparams(2, VMEM_LIMIT),
        name="filter_fft1",
    )(k4, _stack_split_lhs(f1_full))
    return pl.pallas_call(
        functools.partial(_filter_fft2_kernel, nf=nf, scale=1.0 / (nf * nf)),
        grid=(nf // SLAB, nch // halves),
        in_specs=[pl.BlockSpec((halves, 2, SLAB, nf, LANES), lambda i, j: (j, 0, i, 0, 0)),
                  pl.BlockSpec((2 * nf, 6 * nf), lambda i, j: (0, 0)),
                  pl.BlockSpec((1, CONV_LANES), lambda i, j: (0, j))],
        out_specs=pl.BlockSpec((SLAB, 2 * nf, CONV_LANES), lambda i, j: (i, 0, j)),
        out_shape=jax.ShapeDtypeStruct((nf, 2 * nf, c), BF16),
        compiler_params=_cparams(2, VMEM_LIMIT),
        name="filter_fft2",
    )(g, _stack_split_lhs(f2), skip)


def _pack(re, im):
    rb = lax.bitcast_convert_type(re.astype(BF16).astype(F32), U32)
    ib = lax.bitcast_convert_type(im.astype(BF16).astype(F32), U32)
    return rb | (ib >> 16)


def _unpack(w):
    re = lax.bitcast_convert_type(w & jnp.uint32(0xFFFF0000), F32).astype(BF16)
    im = lax.bitcast_convert_type(w << 16, F32).astype(BF16)
    return re, im


def _longconv_kernel(z1_ref, f1_ref, kf_ref, f2_ref, f2i_ref, f3_ref, gate_ref, o_ref, g_ref,
                     *, nf, out_bmajor):
    na = nf // 2
    ns = nf // SLAB
    halves = g_ref.shape[1]
    t = pl.program_id(2)

    def load_rows(ref5, s, bl):
        return jnp.concatenate([ref5[s, h, bl] for h in range(halves)], axis=1)

    def store_col(blk, bl, val):
        for h in range(halves):
            g_ref[blk, h, pl.ds(bl, nf, stride=G_PITCH), :] = val[:, h * LANES:(h + 1) * LANES]

    def load_col(blk, bl):
        return jnp.concatenate([g_ref[blk, h, pl.ds(bl, nf, stride=G_PITCH), :] for h in range(halves)],
                               axis=1)

    @pl.when(t < ns)
    def _phase1():
        for bl in range(SLAB):
            rhs = jnp.concatenate([load_rows(z1_ref, 0, bl), load_rows(z1_ref, 1, bl)],
                                  axis=0).astype(BF16)
            out = jnp.dot(f1_ref[bl], rhs, preferred_element_type=F32)
            store_col(t, bl, _pack(out[:nf], out[nf:]))

    @pl.when(jnp.logical_and(t >= ns, t < 2 * ns))
    def _phase2():
        i = t - ns

        def load_slab(ka):
            row0 = pl.multiple_of(ka * G_PITCH, CHUNK)
            return jnp.concatenate([g_ref[:, h, pl.ds(row0, SLAB), :].reshape(nf, LANES)
                                    for h in range(halves)], axis=1)

        def transform(w, kl):
            re, im = _unpack(w)
            s = jnp.dot(f2_ref[...], jnp.concatenate([re, im], axis=0),
                        preferred_element_type=F32)
            sr, si = s[:nf], s[nf:]
            kr, ki = kf_ref[kl, :nf, :], kf_ref[kl, nf:, :]
            pr = (sr * kr - si * ki).astype(BF16)
            pi = (sr * ki + si * kr).astype(BF16)
            h = jnp.dot(f2i_ref[...], jnp.concatenate([pr, pi], axis=0),
                        preferred_element_type=F32)
            return _pack(h[:nf], h[nf:])

        def store_slab(ka, packed):
            row0 = pl.multiple_of(ka * G_PITCH, CHUNK)
            for hf in range(halves):
                g_ref[:, hf, pl.ds(row0, SLAB), :] = packed[:, hf * LANES:(hf + 1) * LANES].reshape(
                    ns, SLAB, LANES)

        def body(grp, carry):
            kls = [grp * PHASE2_UNROLL + u for u in range(PHASE2_UNROLL)]
            ws = [load_slab(i * SLAB + kl) for kl in kls]
            outs = [transform(w, kl) for w, kl in zip(ws, kls)]
            for kl, packed in zip(kls, outs):
                store_slab(i * SLAB + kl, packed)
            return carry

        lax.fori_loop(0, SLAB // PHASE2_UNROLL, body, 0)

    @pl.when(t >= 2 * ns)
    def _phase3():
        j = t - 2 * ns
        out_flat = None if out_bmajor else o_ref.reshape(2, halves, na * SLAB, LANES)
        for bl in range(SLAB):
            re, im = _unpack(load_col(j, bl))
            y = jnp.dot(f3_ref[bl], jnp.concatenate([re, im], axis=0),
                        preferred_element_type=F32)
            for s in range(2):
                val = load_rows(gate_ref, s, bl) * y[s * na:(s + 1) * na]
                for h in range(halves):
                    piece = val[:, h * LANES:(h + 1) * LANES]
                    if out_bmajor:
                        o_ref[s, h, bl] = piece.astype(o_ref.dtype)
                    else:
                        out_flat[s, h, pl.ds(bl, na, stride=SLAB), :] = piece


def _longconv(gates5, z5, kf, tables, c, gate_blk, z_blk, kf_col, nf, out_bmajor):
    f1, f2, f2i, f3 = tables
    b, _, _, na, _ = z5.shape
    ns = nf // SLAB
    halves = CONV_LANES // LANES
    nchunk = c // CONV_LANES
    p1 = lambda t: jnp.minimum(t, ns - 1)
    p2 = lambda t: jnp.clip(t - ns, 0, ns - 1)
    p3 = lambda t: jnp.clip(t - 2 * ns, 0, ns - 1)
    cl = CONV_LANES
    blk5 = (2, halves, SLAB, na, LANES)
    if out_bmajor:
        out_spec = pl.BlockSpec(blk5, lambda p, q, t: (p, q, p3(t), 0, 0))
        out_shape = (b, c // LANES, nf, na, LANES)
    else:
        out_spec = pl.BlockSpec((2, halves, na, SLAB, LANES), lambda p, q, t: (p, q, 0, p3(t), 0))
        out_shape = (b, c // LANES, na, nf, LANES)
    return pl.pallas_call(
        functools.partial(_longconv_kernel, nf=nf, out_bmajor=out_bmajor),
        grid=(b // 2, nchunk, 3 * ns),
        in_specs=[
            pl.BlockSpec(blk5, lambda p, q, t: (p, z_blk + q, p1(t), 0, 0)),
            pl.BlockSpec((SLAB, 2 * nf, 2 * na), lambda p, q, t: (p1(t), 0, 0)),
            pl.BlockSpec((SLAB, 2 * nf, cl), lambda p, q, t: (p2(t), 0, kf_col + q)),
            pl.BlockSpec((2 * nf, 2 * nf), lambda p, q, t: (0, 0)),
            pl.BlockSpec((2 * nf, 2 * nf), lambda p, q, t: (0, 0)),
            pl.BlockSpec((SLAB, 2 * na, 2 * nf), lambda p, q, t: (p3(t), 0, 0)),
            pl.BlockSpec(blk5, lambda p, q, t: (p, gate_blk + q, p3(t), 0, 0)),
        ],
        out_specs=out_spec,
        out_shape=jax.ShapeDtypeStruct(out_shape, BF16 if out_bmajor else F32),
        scratch_shapes=[pltpu.VMEM((ns, halves, nf * G_PITCH, LANES), U32)],
        compiler_params=_cparams(3, VMEM_LIMIT),
        name="longconv",
    )(z5, f1, kf, f2, f2i, f3, gates5)


def _gelu_exact(x):
    return 0.5 * x * (1.0 + lax.erf(x * (1.0 / math.sqrt(2.0))))


def _gmlp_kernel(gm_ref, lng_ref, lnb_ref, ws_ref, bias_ref, o_ref, *, d_gm):
    g = _gelu_exact(gm_ref[...].astype(F32))
    u, v = g[:, :d_gm], g[:, d_gm:]
    mu = jnp.mean(v, axis=-1, keepdims=True)
    vc = v - mu
    var = jnp.mean(vc * vc, axis=-1, keepdims=True)
    vh = (vc * lax.rsqrt(var + EPS) * lng_ref[...] + lnb_ref[...]).astype(BF16)
    head_dim = d_gm // GM_HEADS
    per_group = CONV_LANES // head_dim
    head = lax.broadcasted_iota(I32, (GM_CHUNK, CONV_LANES), 1) // head_dim
    for c in range(gm_ref.shape[0] // GM_CHUNK):
        rows = slice(c * GM_CHUNK, (c + 1) * GM_CHUNK)
        groups = []
        for grp in range(GM_HEADS // per_group):
            lanes = slice(grp * CONV_LANES, (grp + 1) * CONV_LANES)
            w = ws_ref[grp * per_group * GM_CHUNK:(grp + 1) * per_group * GM_CHUNK, :]
            r = jnp.dot(w, vh[rows, lanes], preferred_element_type=F32)
            s = r[:GM_CHUNK]
            for h in range(1, per_group):
                s = jnp.where(head == h, r[h * GM_CHUNK:(h + 1) * GM_CHUNK], s)
            groups.append(s)
        o_ref[rows, :] = u[rows] * (jnp.concatenate(groups, axis=1) + bias_ref[...])


def _gmlp(gm, ln_g, ln_b, ws_stack, bias_t):
    t, c2 = gm.shape
    d_gm = c2 // 2
    return pl.pallas_call(
        functools.partial(_gmlp_kernel, d_gm=d_gm),
        grid=(t // ROW_TILE,),
        in_specs=[pl.BlockSpec((ROW_TILE, c2), lambda i: (i, 0)),
                  pl.BlockSpec((1, d_gm), lambda i: (0, 0)),
                  pl.BlockSpec((1, d_gm), lambda i: (0, 0)),
                  pl.BlockSpec(ws_stack.shape, lambda i: (0, 0)),
                  pl.BlockSpec(bias_t.shape, lambda i: (0, 0))],
        out_specs=pl.BlockSpec((ROW_TILE, d_gm), lambda i: (i, 0)),
        out_shape=jax.ShapeDtypeStruct((t, d_gm), F32),
        compiler_params=_cparams(1, VMEM_LIMIT),
        name="gmlp",
    )(gm, ln_g, ln_b, ws_stack, bias_t)


def _head_rms(y, bd):
    ms = jnp.dot((y * y).astype(BF16), bd, preferred_element_type=F32)
    return y * lax.rsqrt(ms + EPS)


def _route(logits):
    lane = lax.broadcasted_iota(I32, logits.shape, 1)
    neg = jnp.float32(-1e30)
    big = jnp.int32(ROUTE_LANES)
    gmask = lane < N_GROUPS
    gl = jnp.where(gmask, logits, neg)
    gmax = jnp.max(gl, axis=-1, keepdims=True)
    grp = jnp.min(jnp.where(jnp.logical_and(gl == gmax, gmask), lane, big), axis=-1, keepdims=True)
    psum = jnp.sum(jnp.where(gmask, jnp.exp(gl - gmax), 0.0), axis=-1, keepdims=True)
    p_grp = 1.0 / psum
    lo = N_GROUPS + EXPERTS_PER_GROUP * grp
    emask = jnp.logical_and(lane >= lo, lane < lo + EXPERTS_PER_GROUP)
    el = jnp.where(emask, logits, neg)
    m1 = jnp.max(el, axis=-1, keepdims=True)
    i1 = jnp.min(jnp.where(jnp.logical_and(el == m1, emask), lane, big), axis=-1, keepdims=True)
    emask2 = jnp.logical_and(emask, lane != i1)
    el2 = jnp.where(emask2, logits, neg)
    m2 = jnp.max(el2, axis=-1, keepdims=True)
    i2 = jnp.min(jnp.where(jnp.logical_and(el2 == m2, emask2), lane, big), axis=-1, keepdims=True)
    d = jnp.exp(m2 - m1)
    g1 = p_grp * (1.0 / (1.0 + d))
    g2 = p_grp * (d / (1.0 + d))
    e1 = (i1 - N_GROUPS).astype(F32)
    e2 = (i2 - N_GROUPS).astype(F32)
    return jnp.where(lane == 0, e1, jnp.where(lane == 1, e2, jnp.where(lane == 2, g1,
                     jnp.where(lane == 3, g2, 0.0))))


def _lane_cumsum(v):
    lane = lax.broadcasted_iota(I32, v.shape, 1)
    sh = 1
    while sh < v.shape[1]:
        v = v + jnp.where(lane >= sh, pltpu.roll(v, sh, 1), 0.0)
        sh *= 2
    return v


def _outproj_kernel(x_ref, yh_ref, yg_ref, mg_ref, bd_ref, wo_ref, fg_ref, wr_ref, br_ref,
                    x1_ref, stage_ref, route_ref, count_ref, *, d_hy):
    bd = bd_ref[...]
    mg = mg_ref[...]
    wo = wo_ref[...]
    wr = wr_ref[...]

    def dense(rows):
        yh = jnp.concatenate([yh_ref[0, c, rows, :] for c in range(yh_ref.shape[1])], axis=1)
        mh = (_head_rms(yh, bd) * mg[:, :d_hy]).astype(BF16)
        mgm = (_head_rms(yg_ref[rows, :], bd) * mg[:, d_hy:]).astype(BF16)
        x1 = (x_ref[rows, :] + jnp.dot(mh, wo[:d_hy], preferred_element_type=F32)
              + jnp.dot(mgm, wo[d_hy:], preferred_element_type=F32))
        x1_ref[rows, :] = x1
        ms = jnp.mean(x1 * x1, axis=-1, keepdims=True)
        hi, lo = _split_bf16(x1 * lax.rsqrt(ms + EPS) * fg_ref[...])
        hw = jnp.dot(hi, wr, preferred_element_type=F32)
        logits = (hw[:, :ROUTE_LANES] + hw[:, ROUTE_LANES:]
                  + jnp.dot(lo, wr[:, :ROUTE_LANES], preferred_element_type=F32) + br_ref[...])
        return hi, _route(logits)

    n_rows = x_ref.shape[0]
    parts = [dense(slice(r0, r0 + DENSE_ROWS)) for r0 in range(0, n_rows, DENSE_ROWS)]
    hi = jnp.concatenate([p[0] for p in parts], axis=0)
    route = jnp.concatenate([p[1] for p in parts], axis=0)

    lane = lax.broadcasted_iota(I32, (MOE_TILE, ROUTE_LANES), 1)
    r = lax.broadcasted_iota(I32, (MOE_TILE, MOE_TILE), 0)
    c = lax.broadcasted_iota(I32, (MOE_TILE, MOE_TILE), 1)
    ltri = jnp.where(c < r, 1.0, 0.0).astype(BF16)
    srow = lax.broadcasted_iota(I32, (STAGE_ROWS, MOE_TILE), 0)
    mlane = lax.broadcasted_iota(I32, (STAGE_ROWS, LANES), 1)
    half = hi.shape[1] // 2
    subs = route.shape[0] // MOE_TILE
    for sub in range(subs):
        rows = slice(sub * MOE_TILE, (sub + 1) * MOE_TILE)
        rsub = route[rows]
        oh0 = lane == rsub[:, 0:1].astype(I32)
        oh1 = lane == rsub[:, 1:2].astype(I32)
        oh = jnp.where(jnp.logical_or(oh0, oh1), 1.0, 0.0)
        before = jnp.dot(ltri, oh.astype(BF16), preferred_element_type=F32)
        cnt = jnp.sum(oh, axis=0, keepdims=True)
        run = jnp.floor((cnt + (CHUNK - 1)) * (1.0 / CHUNK)) * CHUNK
        run_start = _lane_cumsum(jnp.broadcast_to(run, (8, run.shape[1])))[0:1] - run
        tot = before + run_start
        loc0 = jnp.sum(jnp.where(oh0, tot, 0.0), axis=-1, keepdims=True)
        loc1 = jnp.sum(jnp.where(oh1, tot, 0.0), axis=-1, keepdims=True)
        rsub = jnp.where(lane == 4, loc0, jnp.where(lane == 5, loc1, rsub))
        route_ref[rows, :] = rsub
        count_ref[pl.ds(pl.program_id(0) * subs + sub, 1), :] = cnt

        rt = jnp.transpose(rsub)
        p0 = srow == rt[4:5, :].astype(I32)
        p1 = srow == rt[5:6, :].astype(I32)
        perm = jnp.where(jnp.logical_or(p0, p1), 1.0, 0.0).astype(BF16)
        staged = jnp.dot(perm, hi[rows], preferred_element_type=F32)
        gate = jnp.sum(jnp.where(p0, rt[2:3, :], 0.0) + jnp.where(p1, rt[3:4, :], 0.0), axis=-1,
                       keepdims=True)
        meta = jnp.where(mlane == 0,
                         lax.bitcast_convert_type(jnp.broadcast_to(gate, (STAGE_ROWS, LANES)), U32),
                         jnp.uint32(0))
        stage_ref[sub * STAGE_ROWS:(sub + 1) * STAGE_ROWS, :] = jnp.concatenate(
            [_pack(staged[:, :half], staged[:, half:]), meta], axis=1)


def _outproj(x2, yh, yg, mix_g, bd, wo_bf, ffn_g, wr_cat, br):
    t, d = x2.shape
    _, nch, seq_len, _ = yh.shape
    d_hy = nch * LANES
    tiles_per_seq = seq_len // ROW_TILE
    n_tiles = t // ROW_TILE
    row = lambda w: pl.BlockSpec((ROW_TILE, w), lambda i: (i, 0))
    full = lambda a: pl.BlockSpec(a.shape, lambda i: (0,) * a.ndim)
    sw = d // 2 + LANES
    subs = ROW_TILE // MOE_TILE
    n_sub = t // MOE_TILE
    return pl.pallas_call(
        functools.partial(_outproj_kernel, d_hy=d_hy),
        grid=(n_tiles,),
        in_specs=[row(d),
                  pl.BlockSpec((1, nch, ROW_TILE, LANES),
                               lambda i: (i // tiles_per_seq, 0, i % tiles_per_seq, 0)),
                  row(yg.shape[1]), full(mix_g), full(bd), full(wo_bf), full(ffn_g),
                  full(wr_cat), full(br)],
        out_specs=[row(d), pl.BlockSpec((subs * STAGE_ROWS, sw), lambda i: (i, 0)), row(ROUTE_LANES),
                   pl.BlockSpec((n_sub, ROUTE_LANES), lambda i: (0, 0))],
        out_shape=[jax.ShapeDtypeStruct((t, d), F32),
                   jax.ShapeDtypeStruct((n_sub * STAGE_ROWS, sw), U32),
                   jax.ShapeDtypeStruct((t, ROUTE_LANES), F32),
                   jax.ShapeDtypeStruct((n_sub, ROUTE_LANES), F32)],
        compiler_params=_cparams(1, VMEM_LIMIT),
        name="outproj_router",
    )(x2, yh, yg, mix_g, bd, wo_bf, ffn_g, wr_cat, br)


def _moe_plan(tile_cnt, n_tokens):
    n_tiles = tile_cnt.shape[0]
    cnt = tile_cnt[:, :N_EXPERTS].astype(I32)
    run = (cnt + CHUNK - 1) // CHUNK * CHUNK
    run_start = jnp.cumsum(run, axis=1) - run
    used = jnp.sum(run, axis=1)
    e_rows = jnp.sum(run, axis=0)
    e_pad = (e_rows + EXPERT_ROWS - 1) // EXPERT_ROWS * EXPERT_ROWS
    e_end = jnp.cumsum(e_pad)
    e_start = e_end - e_pad
    n_blocks = -(-(n_tokens * TOP_K + n_tiles * N_EXPERTS * (CHUNK - 1)) // EXPERT_ROWS) + N_EXPERTS
    n_used = (e_end[-1:] // EXPERT_ROWS).astype(I32)
    block_row = jnp.arange(n_blocks, dtype=I32) * EXPERT_ROWS
    block_e = jnp.minimum(jnp.sum((e_end[None, :] <= block_row[:, None]).astype(I32), axis=1),
                          N_EXPERTS - 1)
    run_end_in_e = jnp.cumsum(run, axis=0)
    chunk_row = jnp.arange(n_blocks * EXPERT_ROWS // CHUNK, dtype=I32) * CHUNK
    e_of_chunk = jnp.repeat(block_e, EXPERT_ROWS // CHUNK)
    e_sel = jnp.arange(N_EXPERTS, dtype=I32)[None, :] == e_of_chunk[:, None]
    pick = lambda tbl: jnp.dot(e_sel.astype(F32), tbl.astype(F32),
                               precision=lax.Precision.HIGHEST).astype(I32)
    q = chunk_row - pick(e_start[:, None])[:, 0]
    valid = q < pick(e_rows[:, None])[:, 0]
    ends = pick(run_end_in_e.T)
    tile = jnp.minimum(jnp.sum((ends <= q[:, None]).astype(I32), axis=1), n_tiles - 1)
    t_sel = jnp.arange(n_tiles, dtype=I32)[None, :] == tile[:, None]
    at_tile = lambda tbl: jnp.sum(jnp.where(t_sel, pick(tbl.T), 0), axis=1)
    within = q - (at_tile(run_end_in_e) - at_tile(run))
    src_row = tile * STAGE_ROWS + at_tile(run_start) + within
    zero_chunk = STAGE_ROWS // CHUNK - 1
    per_block = EXPERT_ROWS // CHUNK
    src_chunk = jnp.where(valid, src_row // CHUNK, zero_chunk).reshape(n_blocks, 1, per_block)
    dst_chunk = jnp.where(valid, src_row // CHUNK, 0).reshape(n_blocks, 1, per_block)
    n_valid = jnp.sum(valid.reshape(n_blocks, per_block).astype(I32), axis=1)
    return (block_e.astype(I32), n_used, n_valid.astype(I32), src_chunk.astype(I32),
            dst_chunk.astype(I32), used.astype(I32))


def _ffn_kernel(be_ref, nu_ref, nv_ref, used_ref, src_now_ref, src_next_ref, dst_ref, stage_ref,
                wg_ref, wu_ref, wd_ref, ost_ref, xbuf, obuf, zero_ref, wgb, wub, wdb,
                in_sem, out_sem, zero_sem):
    b = pl.program_id(0)
    nu = nu_ref[0]
    per_block = dst_ref.shape[-1]
    slot = b % 2

    def fetch(map_ref, s):
        def body(j, carry):
            row = pl.multiple_of(map_ref[0, 0, j] * CHUNK, CHUNK)
            pltpu.make_async_copy(stage_ref.at[pl.ds(row, CHUNK)], xbuf.at[s, pl.ds(j * CHUNK, CHUNK)],
                                  in_sem.at[s]).start()
            return carry
        lax.fori_loop(0, per_block, body, 0, unroll=8)

    def out_copy(j, chunk, s):
        row = pl.multiple_of(chunk * CHUNK, CHUNK)
        return pltpu.make_async_copy(obuf.at[s, pl.ds(j * CHUNK, CHUNK)], ost_ref.at[pl.ds(row, CHUNK)],
                                     out_sem.at[s])

    def drain(s, n):
        @pl.when(n == per_block)
        def _():
            pltpu.make_async_copy(obuf.at[s], ost_ref.at[pl.ds(0, EXPERT_ROWS)], out_sem.at[s]).wait()

        @pl.when(n != per_block)
        def _():
            def body(j, carry):
                out_copy(0, 0, s).wait()
                return carry
            lax.fori_loop(0, n, body, 0)

    def zero_tails(wait):
        def tile(i, carry):
            def chunk(c, carry2):
                row = pl.multiple_of(i * STAGE_ROWS + c * CHUNK, CHUNK)
                cp = pltpu.make_async_copy(zero_ref, ost_ref.at[pl.ds(row, CHUNK)], zero_sem)
                if wait:
                    cp.wait()
                else:
                    cp.start()
                return carry2
            return lax.fori_loop(used_ref[i] // CHUNK, STAGE_ROWS // CHUNK, chunk, carry)
        lax.fori_loop(0, used_ref.shape[0], tile, 0)

    @pl.when(b == 0)
    def _prologue():
        fetch(src_now_ref, 0)
        zero_ref[...] = jnp.zeros_like(zero_ref)
        zero_tails(wait=False)
        zero_tails(wait=True)

    @pl.when(b + 1 < nu)
    def _prefetch_next():
        fetch(src_next_ref, 1 - slot)

    @pl.when(b < nu)
    def _compute():
        pltpu.make_async_copy(stage_ref.at[pl.ds(0, EXPERT_ROWS)], xbuf.at[slot], in_sem.at[slot]).wait()

        @pl.when(b >= 2)
        def _():
            drain(slot, nv_ref[b - 2])

        @pl.when(jnp.logical_or(b == 0, be_ref[b] != be_ref[jnp.maximum(b - 1, 0)]))
        def _new_expert():
            wgb[...] = wg_ref[0].astype(BF16)
            wub[...] = wu_ref[0].astype(BF16)
            wdb[...] = wd_ref[0].astype(BF16)

        x = xbuf[slot]
        half = wg_ref.shape[1] // 2
        xb = jnp.concatenate(_unpack(x[:, :half]), axis=1)
        gate = lax.bitcast_convert_type(x[:, half:half + 1], F32)
        g = jnp.dot(xb, wgb[...], preferred_element_type=F32)
        u = jnp.dot(xb, wub[...], preferred_element_type=F32)
        h = (g * (1.0 / (1.0 + jnp.exp(-g))) * u).astype(BF16)
        o = jnp.dot(h, wdb[...], preferred_element_type=F32) * gate
        obuf[slot] = _pack(o[:, :half], o[:, half:])

        def body(j, carry):
            out_copy(j, dst_ref[0, 0, j], slot).start()
            return carry

        @pl.when(nv_ref[b] == per_block)
        def _():
            lax.fori_loop(0, per_block, body, 0, unroll=8)

        @pl.when(nv_ref[b] != per_block)
        def _():
            lax.fori_loop(0, nv_ref[b], body, 0)

        @pl.when(b == nu - 1)
        def _():
            drain(slot, nv_ref[b])

            @pl.when(b >= 1)
            def _():
                drain(1 - slot, nv_ref[b - 1])


def _expert_ffn(stage, plan, wg, wu, wd):
    block_e, n_used, n_valid, src_chunk, dst_chunk, used = plan
    nb = block_e.shape[0]
    sw = stage.shape[1]
    d, de = wg.shape[1], wg.shape[2]
    per_block = EXPERT_ROWS // CHUNK
    cur = lambda i, be, nu: jnp.minimum(i, nu[0] - 1)
    wspec = lambda shape: pl.BlockSpec(shape, lambda i, be, nu, nv, us: (be[cur(i, be, nu)], 0, 0))
    smem = lambda imap: pl.BlockSpec((1, 1, per_block), imap, memory_space=pltpu.SMEM)
    return pl.pallas_call(
        _ffn_kernel,
        grid_spec=pltpu.PrefetchScalarGridSpec(
            num_scalar_prefetch=4,
            grid=(nb,),
            in_specs=[smem(lambda i, be, nu, nv, us: (i, 0, 0)),
                      smem(lambda i, be, nu, nv, us: (jnp.minimum(i + 1, nb - 1), 0, 0)),
                      smem(lambda i, be, nu, nv, us: (i, 0, 0)),
                      pl.BlockSpec(memory_space=pl.ANY),
                      wspec((1, d, de)), wspec((1, d, de)), wspec((1, de, d))],
            out_specs=pl.BlockSpec(memory_space=pl.ANY),
            scratch_shapes=[pltpu.VMEM((2, EXPERT_ROWS, sw), U32), pltpu.VMEM((2, EXPERT_ROWS, d // 2), U32),
                            pltpu.VMEM((CHUNK, d // 2), U32),
                            pltpu.VMEM((d, de), BF16), pltpu.VMEM((d, de), BF16), pltpu.VMEM((de, d), BF16),
                            pltpu.SemaphoreType.DMA((2,)), pltpu.SemaphoreType.DMA((2,)),
                            pltpu.SemaphoreType.DMA(())]),
        out_shape=jax.ShapeDtypeStruct((stage.shape[0], d // 2), U32),
        compiler_params=_cparams(1, VMEM_LIMIT),
        name="expert_ffn",
    )(block_e, n_used, n_valid, used, src_chunk, src_chunk, dst_chunk, stage, wg, wu, wd)


def _combine_kernel(ost_ref, route_ref, x1_ref, fg_ref, o_ref):
    col = lax.broadcasted_iota(I32, (MOE_TILE, STAGE_ROWS), 1)
    for sub in range(route_ref.shape[0] // MOE_TILE):
        rows = slice(sub * MOE_TILE, (sub + 1) * MOE_TILE)
        route = route_ref[rows, :]
        sel = jnp.logical_or(col == route[:, 4:5].astype(I32), col == route[:, 5:6].astype(I32))
        perm = jnp.where(sel, 1.0, 0.0).astype(BF16)
        ost = ost_ref[sub * STAGE_ROWS:(sub + 1) * STAGE_ROWS, :]
        y = jnp.dot(perm, jnp.concatenate(_unpack(ost), axis=1), preferred_element_type=F32)
        x2 = x1_ref[rows, :] + y
        ms = jnp.mean(x2 * x2, axis=-1, keepdims=True)
        o_ref[rows, :] = x2 * lax.rsqrt(ms + EPS) * fg_ref[...]


def _combine(ostage, route, x1, final_g):
    t, d = x1.shape
    return pl.pallas_call(
        _combine_kernel,
        grid=(t // ROW_TILE,),
        in_specs=[pl.BlockSpec((ROW_TILE // MOE_TILE * STAGE_ROWS, d // 2), lambda i: (i, 0)),
                  pl.BlockSpec((ROW_TILE, ROUTE_LANES), lambda i: (i, 0)),
                  pl.BlockSpec((ROW_TILE, d), lambda i: (i, 0)),
                  pl.BlockSpec((1, d), lambda i: (0, 0))],
        out_specs=pl.BlockSpec((ROW_TILE, d), lambda i: (i, 0)),
        out_shape=jax.ShapeDtypeStruct((t, d), F32),
        compiler_params=_cparams(1, VMEM_LIMIT),
        name="moe_combine",
    )(ostage, route, x1, final_g)


def _hier_moe_and_norm(x1, stage, route, tile_cnt, wg, wu, wd, final_g):
    t, d = x1.shape
    ostage = _expert_ffn(stage, _moe_plan(tile_cnt, t), wg, wu, wd)
    return _combine(ostage, route, x1, final_g)


def _encoder(x, prm, kf, tables, nf):
    b, l, d = x.shape
    t = b * l
    d_hy = prm["skip"].shape[1]
    x2 = x.reshape(t, d)
    gates5, gm = _inproj(x2, prm["mix_norm_g"], prm["w_in"], prm["short_w"], prm["short_b"], b, l, nf,
                         (HY_ORDER + 1) * d_hy)
    z5 = gates5
    z_blk = HY_ORDER * d_hy // CONV_LANES
    for o in range(HY_ORDER):
        z5 = _longconv(gates5, z5, kf, tables, d_hy, gate_blk=o * d_hy // CONV_LANES,
                       z_blk=z_blk, kf_col=o * d_hy // CONV_LANES, nf=nf, out_bmajor=o < HY_ORDER - 1)
        z_blk = 0
    y_hy = z5.reshape(b, d_hy // LANES, l, LANES)
    y_gm = _gmlp(gm, prm["ln_g"], prm["ln_b"], prm["ws_stack"], prm["bias_t"])
    x1, stage, route, tile_cnt = _outproj(x2, y_hy, y_gm, prm["mix_out_g"], prm["bd"], prm["w_out"], prm["ffn_norm_g"],
                             prm["wr_cat"], prm["br"])
    out = _hier_moe_and_norm(x1, stage, route, tile_cnt, prm["w_e_gate"], prm["w_e_up"], prm["w_e_down"],
                             prm["final_norm_g"])
    return out.reshape(b, l, d)


def kernel(x_prompt, x_sample, mix_norm_g, w_in, hy_short_w, hy_short_b, hy_filt_w_emb, hy_filt_b_emb,
           hy_filt_w_inner, hy_filt_b_inner, hy_filt_freq, hy_filt_w_out, hy_skip, gm_ln_g, gm_ln_b,
           gm_w_s, gm_b_s, mix_out_g, w_out, ffn_norm_g, w_group, b_group, w_expert_router,
           b_expert_router, w_e_gate, w_e_up, w_e_down, final_norm_g):
    assert w_in.shape[0] == 1, "one layer"
    l = x_prompt.shape[1]
    assert x_sample.shape[1] == l
    nf = math.isqrt(2 * l)
    assert nf * nf == 2 * l and nf % SLAB == 0
    d = x_prompt.shape[2]
    d_hy = hy_skip.shape[2]
    d_gm = gm_ln_g.shape[1]
    head_dim = d_gm // GM_HEADS
    assert d_hy // HY_HEADS == head_dim and d_hy == d_gm

    tables = _dft_tables(nf)
    f1, f2, f2i, f3, f1_full = tables
    tables_bf = tuple(a.astype(BF16) for a in (f1, f2, f2i, f3))

    max_decay = math.log(DECAY_TARGET) / FAST_DECAY_PCT
    min_decay = math.log(DECAY_TARGET) / SLOW_DECAY_PCT
    deltas = jnp.abs(jnp.linspace(min_decay, max_decay, d_hy, dtype=F32))[None, :]
    taps = _filter_taps(hy_filt_w_emb[0], hy_filt_b_emb[0][None, :], hy_filt_w_inner[0], hy_filt_b_inner[0],
                        hy_filt_freq[0][None, :], hy_filt_w_out[0], deltas, l, d_hy)
    kf = _filter_spectrum(taps, hy_skip[0].reshape(1, HY_ORDER * d_hy), f1_full, f2, nf)

    n_route = N_GROUPS + N_EXPERTS
    wr = jnp.zeros((d, ROUTE_LANES), F32).at[:, :n_route].set(
        jnp.concatenate([w_group[0], w_expert_router[0]], axis=1))
    wr_hi = wr.astype(BF16)
    wr_lo = (wr - wr_hi.astype(F32)).astype(BF16)
    br = jnp.zeros((1, ROUTE_LANES), F32).at[0, :n_route].set(jnp.concatenate([b_group[0], b_expert_router[0]]))
    hid = jnp.arange(d_hy, dtype=I32) // head_dim
    bd = jnp.where(hid[:, None] == hid[None, :], 1.0 / head_dim, 0.0).astype(BF16)

    prm = dict(
        mix_norm_g=mix_norm_g, w_in=w_in[0].astype(BF16), short_w=hy_short_w[0], short_b=hy_short_b,
        skip=hy_skip[0], ln_g=gm_ln_g, ln_b=gm_ln_b,
        ws_stack=gm_w_s[0].reshape(GM_HEADS * GM_CHUNK, GM_CHUNK).astype(BF16),
        bias_t=jnp.repeat(gm_b_s[0].T, head_dim, axis=1),
        mix_out_g=mix_out_g, bd=bd, w_out=w_out[0].astype(BF16), ffn_norm_g=ffn_norm_g,
        wr_cat=jnp.concatenate([wr_hi, wr_lo], axis=1), br=br,
        w_e_gate=w_e_gate[0], w_e_up=w_e_up[0], w_e_down=w_e_down[0],
        final_norm_g=final_norm_g[None, :])
    y_prompt = _encoder(x_prompt, prm, kf, tables_bf, nf)
    y_sample = _encoder(x_sample, prm, kf, tables_bf, nf)
    return (y_prompt, y_sample)
```

```python
import functools
import math

import jax
import jax.numpy as jnp
from jax import lax
from jax.experimental import pallas as pl
from jax.experimental.pallas import tpu as pltpu

F32 = jnp.float32
BF16 = jnp.bfloat16
U32 = jnp.uint32
I32 = jnp.int32

EPS = 1e-6
HY_ORDER = 2
HY_HEADS = 8
GM_HEADS = 8
GM_CHUNK = 128
N_GROUPS = 4
EXPERTS_PER_GROUP = 8
N_EXPERTS = N_GROUPS * EXPERTS_PER_GROUP
TOP_K = 2
FILTER_EMB = 5
DECAY_TARGET = 1e-2
FAST_DECAY_PCT = 0.3
SLOW_DECAY_PCT = 1.5

LANES = 128
ROUTE_LANES = LANES
SLAB = 16
G_PITCH = SLAB + 8
CONV_LANES = 256
PHASE2_UNROLL = 8
ROW_TILE = 512
IN_A = 8
IN_SLABS = 4
EXPERT_ROWS = 1024
CHUNK = 8
MOE_TILE = 512
STAGE_ROWS = MOE_TILE * TOP_K + N_EXPERTS * CHUNK
VMEM_LIMIT = 56 * 1024 * 1024


def _cparams(n_axes, vmem=None):
    return pltpu.CompilerParams(dimension_semantics=("arbitrary",) * n_axes,
                                vmem_limit_bytes=vmem)


def _inproj_kernel(x_ref, xp_ref, xn_ref, g_ref, w_ref, sw_ref, sb_ref, hy_ref, gm_ref, stage_ref,
                   *, nf, d_hy3, tiles_per_seq):
    i = pl.program_id(0)
    rows = x_ref.shape[0]
    pitch = stage_ref.shape[1] // IN_A
    halo = xp_ref.shape[0]
    x = jnp.concatenate([xp_ref[...], x_ref[...], xn_ref[...]], axis=0)
    ms = jnp.mean(x * x, axis=-1, keepdims=True)
    n = (x * lax.rsqrt(ms + EPS) * g_ref[...]).astype(BF16)
    gm = jnp.dot(n, w_ref[:, d_hy3:], preferred_element_type=F32)
    gm_ref[...] = gm[halo:halo + rows].astype(BF16)

    pos = i % tiles_per_seq
    has_prev = jnp.where(pos == 0, 0.0, 1.0)
    has_next = jnp.where(pos == tiles_per_seq - 1, 0.0, 1.0)
    row = lax.broadcasted_iota(I32, (rows, 1), 0)
    prev_w = jnp.where(row == 0, has_prev, 1.0)
    next_w = jnp.where(row == rows - 1, has_next, 1.0)
    n_slabs = stage_ref.shape[0]
    cols = n_slabs * LANES
    ext = rows + 2 * halo
    for c0 in range(0, d_hy3, cols):
        p = jnp.dot(n, w_ref[:, c0:c0 + cols], preferred_element_type=F32)
        prev = pltpu.roll(p, 1, 0)[halo:halo + rows] * prev_w
        nxt = pltpu.roll(p, ext - 1, 0)[halo:halo + rows] * next_w
        res = (sw_ref[0:1, c0:c0 + cols] * prev + sw_ref[1:2, c0:c0 + cols] * p[halo:halo + rows]
               + sw_ref[2:3, c0:c0 + cols] * nxt + sb_ref[:, c0:c0 + cols])
        for s in range(n_slabs):
            for al in range(IN_A):
                stage_ref[s, al * pitch:al * pitch + nf, :] = res[al * nf:(al + 1) * nf,
                                                                s * LANES:(s + 1) * LANES]
        for s in range(n_slabs):
            for b in range(nf):
                hy_ref[0, c0 // LANES + s, b] = stage_ref[s, pl.ds(b, IN_A, stride=pitch), :]


def _inproj(x2, g, w_bf, short_w, short_b, n_seq, seq_len, nf, d_hy3):
    t, d = x2.shape
    dp = w_bf.shape[1]
    rows = IN_A * nf
    tiles_per_seq = seq_len // rows
    hb = rows // 8
    nhb = t // 8
    full = lambda a: pl.BlockSpec(a.shape, lambda i: (0,) * a.ndim)
    return pl.pallas_call(
        functools.partial(_inproj_kernel, nf=nf, d_hy3=d_hy3, tiles_per_seq=tiles_per_seq),
        grid=(t // rows,),
        in_specs=[pl.BlockSpec((rows, d), lambda i: (i, 0)),
                  pl.BlockSpec((8, d), lambda i: (jnp.maximum(i * hb - 1, 0), 0)),
                  pl.BlockSpec((8, d), lambda i: (jnp.minimum((i + 1) * hb, nhb - 1), 0)),
                  full(g), full(w_bf), full(short_w), full(short_b)],
        out_specs=[pl.BlockSpec((1, d_hy3 // LANES, nf, IN_A, LANES),
                                lambda i: (i // tiles_per_seq, 0, 0, i % tiles_per_seq, 0)),
                   pl.BlockSpec((rows, dp - d_hy3), lambda i: (i, 0))],
        out_shape=[jax.ShapeDtypeStruct((n_seq, d_hy3 // LANES, nf, nf // 2, LANES), F32),
                   jax.ShapeDtypeStruct((t, dp - d_hy3), BF16)],
        scratch_shapes=[pltpu.VMEM((IN_SLABS, IN_A * (nf + 8), LANES), F32)],
        compiler_params=_cparams(1, VMEM_LIMIT),
        name="inproj_shortconv",
    )(x2, x2, x2, g, w_bf, short_w, short_b)


def _cos_sin(m, period):
    ang = m.astype(F32) * (2.0 * math.pi / period)
    return jnp.cos(ang), jnp.sin(ang)


def _stack_complex(mr, mi):
    top = jnp.concatenate([mr, -mi], axis=-1)
    bot = jnp.concatenate([mi, mr], axis=-1)
    return jnp.concatenate([top, bot], axis=-2)


def _dft_tables(nf):
    na = nf // 2
    n = nf * nf
    idx = jnp.arange(nf, dtype=I32)
    c, s = _cos_sin((idx[:, None] * idx[None, :]) % nf, nf)
    f2 = _stack_complex(c, -s)
    f2i = _stack_complex(c, s)
    b_ = idx[:, None, None]
    ka = idx[None, :, None]
    a_ = idx[None, None, :]
    m1 = (nf * a_ * ka + b_ * ka) % n
    c1, s1 = _cos_sin(m1, n)
    f1 = _stack_complex(c1[:, :, :na], -s1[:, :, :na])
    f1_full = jnp.concatenate([c1, -s1], axis=1)
    c3 = jnp.swapaxes(c1, 1, 2)[:, :na, :]
    s3 = jnp.swapaxes(s1, 1, 2)[:, :na, :]
    f3 = _stack_complex(c3, s3)
    return f1, f2, f2i, f3, f1_full


def _split_bf16(x):
    hi = x.astype(BF16)
    lo = (x - hi.astype(F32)).astype(BF16)
    return hi, lo


def _stack_split_lhs(a):
    hi, lo = _split_bf16(a)
    return jnp.concatenate([hi, lo, hi], axis=-1)


def _stack_split_rhs(b):
    hi, lo = _split_bf16(b)
    return jnp.concatenate([hi, hi, lo], axis=-2)


def _filter_taps_kernel(wemb_ref, bemb_ref, win_ref, bin_ref, freq_ref, wout_ref, delta_ref, o_ref,
                        *, seq_len, n_inner, d_hy):
    rows = o_ref.shape[1]
    j = pl.program_id(0) * rows + lax.broadcasted_iota(I32, (rows, 1), 0)
    tidx = jnp.where(j < seq_len, j, 2 * seq_len - j).astype(F32)
    t = tidx / float(seq_len - 1)
    fr0 = jnp.float32(1e-4)
    ang0 = (2.0 * math.pi / seq_len) * tidx * fr0
    ang1 = (2.0 * math.pi / seq_len) * tidx
    freq = freq_ref[...]
    pre = (t * wemb_ref[0:1, :] + jnp.cos(ang0) * wemb_ref[1:2, :] + jnp.cos(ang1) * wemb_ref[2:3, :]
           - jnp.sin(ang0) * wemb_ref[3:4, :] - jnp.sin(ang1) * wemb_ref[4:5, :] + bemb_ref[...])
    hdn = jnp.sin(freq * pre)
    for i in range(n_inner):
        hdn = jnp.sin(freq * (jnp.dot(_stack_split_lhs(hdn), win_ref[i], preferred_element_type=F32)
                              + bin_ref[i:i + 1, :]))
    h = jnp.dot(_stack_split_lhs(hdn), wout_ref[...], preferred_element_type=F32)
    decay = jnp.exp(-t * delta_ref[...])
    for o in range(HY_ORDER):
        fwd = h[:, (2 * o) * d_hy:(2 * o + 1) * d_hy]
        bwd = h[:, (2 * o + 1) * d_hy:(2 * o + 2) * d_hy]
        val = jnp.where(j == seq_len, 0.0, jnp.where(j < seq_len, fwd, bwd) * decay)
        for cc in range(d_hy // LANES):
            o_ref[o * (d_hy // LANES) + cc] = val[:, cc * LANES:(cc + 1) * LANES]


def _filter_taps(wemb, bemb, win, bin_, freq, wout, deltas, seq_len, d_hy):
    rows = 1024
    n2 = 2 * seq_len
    width = wemb.shape[1]
    n_inner = win.shape[0]
    full = lambda *shape: pl.BlockSpec(shape, lambda i: (0,) * len(shape))
    return pl.pallas_call(
        functools.partial(_filter_taps_kernel, seq_len=seq_len, n_inner=n_inner, d_hy=d_hy),
        grid=(n2 // rows,),
        in_specs=[full(FILTER_EMB, width), full(1, width), full(n_inner, 3 * width, width),
                  full(n_inner, width), full(1, width), full(3 * width, 2 * HY_ORDER * d_hy),
                  full(1, d_hy)],
        out_specs=pl.BlockSpec((HY_ORDER * d_hy // LANES, rows, LANES), lambda i: (0, i, 0)),
        out_shape=jax.ShapeDtypeStruct((HY_ORDER * d_hy // LANES, n2, LANES), F32),
        compiler_params=_cparams(1, VMEM_LIMIT),
        name="filter_taps",
    )(wemb, bemb, _stack_split_rhs(win), bin_, freq, _stack_split_rhs(wout), deltas)


def _filter_fft1_kernel(k_ref, f_ref, o_ref, *, nf):
    halves = k_ref.shape[0]
    k = k_ref.reshape(halves, nf * SLAB, LANES)
    g = o_ref.reshape(halves, 2, nf * SLAB, LANES)
    for bl in range(SLAB):
        col = jnp.concatenate([k[h, pl.ds(bl, nf, stride=SLAB), :] for h in range(halves)], axis=1)
        res = jnp.dot(f_ref[bl], _stack_split_rhs(col), preferred_element_type=F32)
        for h in range(halves):
            for ri in range(2):
                g[h, ri, pl.ds(bl, nf, stride=SLAB), :] = res[ri * nf:(ri + 1) * nf,
                                                              h * LANES:(h + 1) * LANES]


def _filter_fft2_kernel(g_ref, f_ref, skip_ref, o_ref, *, nf, scale):
    halves = g_ref.shape[0]
    lag0 = jnp.concatenate([jnp.broadcast_to(skip_ref[...], (nf, skip_ref.shape[1])),
                            jnp.zeros((nf, skip_ref.shape[1]), F32)], axis=0)
    for kl in range(SLAB):
        rhs = jnp.concatenate([jnp.concatenate([g_ref[h, ri, kl] for h in range(halves)], axis=1)
                               for ri in range(2)], axis=0)
        res = jnp.dot(f_ref[...], _stack_split_rhs(rhs), preferred_element_type=F32)
        o_ref[kl] = ((res + lag0) * scale).astype(o_ref.dtype)


def _filter_spectrum(taps, skip, f1_full, f2, nf):
    nch = taps.shape[0]
    c = nch * LANES
    halves = CONV_LANES // LANES
    k4 = taps.reshape(nch, nf, nf, LANES)
    g = pl.pallas_call(
        functools.partial(_filter_fft1_kernel, nf=nf),
        grid=(nf // SLAB, nch // halves),
        in_specs=[pl.BlockSpec((halves, nf, SLAB, LANES), lambda i, j: (j, 0, i, 0)),
                  pl.BlockSpec((SLAB, 2 * nf, 3 * nf), lambda i, j: (i, 0, 0))],
        out_specs=pl.BlockSpec((halves, 2, nf, SLAB, LANES), lambda i, j: (j, 0, 0, i, 0)),
        out_shape=jax.ShapeDtypeStruct((nch, 2, nf, nf, LANES), F32),
        compiler_params=_cparams(2, VMEM_LIMIT),
        name="filter_fft1",
    )(k4, _stack_split_lhs(f1_full))
    return pl.pallas_call(
        functools.partial(_filter_fft2_kernel, nf=nf, scale=1.0 / (nf * nf)),
        grid=(nf // SLAB, nch // halves),
        in_specs=[pl.BlockSpec((halves, 2, SLAB, nf, LANES), lambda i, j: (j, 0, i, 0, 0)),
                  pl.BlockSpec((2 * nf, 6 * nf), lambda i, j: (0, 0)),
                  pl.BlockSpec((1, CONV_LANES), lambda i, j: (0, j))],
        out_specs=pl.BlockSpec((SLAB, 2 * nf, CONV_LANES), lambda i, j: (i, 0, j)),
        out_shape=jax.ShapeDtypeStruct((nf, 2 * nf, c), BF16),
        compiler_params=_cparams(2, VMEM_LIMIT),
        name="filter_fft2",
    )(g, _stack_split_lhs(f2), skip)


def _pack(re, im):
    rb = lax.bitcast_convert_type(re.astype(BF16).astype(F32), U32)
    ib = lax.bitcast_convert_type(im.astype(BF16).astype(F32), U32)
    return rb | (ib >> 16)


def _unpack(w):
    re = lax.bitcast_convert_type(w & jnp.uint32(0xFFFF0000), F32).astype(BF16)
    im = lax.bitcast_convert_type(w << 16, F32).astype(BF16)
    return re, im


def _longconv_kernel(z1_ref, f1_ref, kf_ref, f2_ref, f2i_ref, f3_ref, gate_ref, o_ref, g_ref,
                     *, nf, out_bmajor):
    na = nf // 2
    ns = nf // SLAB
    halves = g_ref.shape[1]
    t = pl.program_id(2)

    def load_rows(ref5, s, bl):
        return jnp.concatenate([ref5[s, h, bl] for h in range(halves)], axis=1)

    def store_col(blk, bl, val):
        for h in range(halves):
            g_ref[blk, h, pl.ds(bl, nf, stride=G_PITCH), :] = val[:, h * LANES:(h + 1) * LANES]

    def load_col(blk, bl):
        return jnp.concatenate([g_ref[blk, h, pl.ds(bl, nf, stride=G_PITCH), :] for h in range(halves)],
                               axis=1)

    @pl.when(t < ns)
    def _phase1():
        for bl in range(SLAB):
            rhs = jnp.concatenate([load_rows(z1_ref, 0, bl), load_rows(z1_ref, 1, bl)],
                                  axis=0).astype(BF16)
            out = jnp.dot(f1_ref[bl], rhs, preferred_element_type=F32)
            store_col(t, bl, _pack(out[:nf], out[nf:]))

    @pl.when(jnp.logical_and(t >= ns, t < 2 * ns))
    def _phase2():
        i = t - ns

        def load_slab(ka):
            row0 = pl.multiple_of(ka * G_PITCH, CHUNK)
            return jnp.concatenate([g_ref[:, h, pl.ds(row0, SLAB), :].reshape(nf, LANES)
                                    for h in range(halves)], axis=1)

        def transform(w, kl):
            re, im = _unpack(w)
            s = jnp.dot(f2_ref[...], jnp.concatenate([re, im], axis=0),
                        preferred_element_type=F32)
            sr, si = s[:nf], s[nf:]
            kr, ki = kf_ref[kl, :nf, :], kf_ref[kl, nf:, :]
            pr = (sr * kr - si * ki).astype(BF16)
            pi = (sr * ki + si * kr).astype(BF16)
            h = jnp.dot(f2i_ref[...], jnp.concatenate([pr, pi], axis=0),
                        preferred_element_type=F32)
            return _pack(h[:nf], h[nf:])

        def store_slab(ka, packed):
            row0 = pl.multiple_of(ka * G_PITCH, CHUNK)
            for hf in range(halves):
                g_ref[:, hf, pl.ds(row0, SLAB), :] = packed[:, hf * LANES:(hf + 1) * LANES].reshape(
                    ns, SLAB, LANES)

        def body(grp, carry):
            kls = [grp * PHASE2_UNROLL + u for u in range(PHASE2_UNROLL)]
            ws = [load_slab(i * SLAB + kl) for kl in kls]
            outs = [transform(w, kl) for w, kl in zip(ws, kls)]
            for kl, packed in zip(kls, outs):
                store_slab(i * SLAB + kl, packed)
            return carry

        lax.fori_loop(0, SLAB // PHASE2_UNROLL, body, 0)

    @pl.when(t >= 2 * ns)
    def _phase3():
        j = t - 2 * ns
        out_flat = None if out_bmajor else o_ref.reshape(2, halves, na * SLAB, LANES)
        for bl in range(SLAB):
            re, im = _unpack(load_col(j, bl))
            y = jnp.dot(f3_ref[bl], jnp.concatenate([re, im], axis=0),
                        preferred_element_type=F32)
            for s in range(2):
                val = load_rows(gate_ref, s, bl) * y[s * na:(s + 1) * na]
                for h in range(halves):
                    piece = val[:, h * LANES:(h + 1) * LANES]
                    if out_bmajor:
                        o_ref[s, h, bl] = piece.astype(o_ref.dtype)
                    else:
                        out_flat[s, h, pl.ds(bl, na, stride=SLAB), :] = piece


def _longconv(gates5, z5, kf, tables, c, gate_blk, z_blk, kf_col, nf, out_bmajor):
    f1, f2, f2i, f3 = tables
    b, _, _, na, _ = z5.shape
    ns = nf // SLAB
    halves = CONV_LANES // LANES
    nchunk = c // CONV_LANES
    p1 = lambda t: jnp.minimum(t, ns - 1)
    p2 = lambda t: jnp.clip(t - ns, 0, ns - 1)
    p3 = lambda t: jnp.clip(t - 2 * ns, 0, ns - 1)
    cl = CONV_LANES
    blk5 = (2, halves, SLAB, na, LANES)
    if out_bmajor:
        out_spec = pl.BlockSpec(blk5, lambda p, q, t: (p, q, p3(t), 0, 0))
        out_shape = (b, c // LANES, nf, na, LANES)
    else:
        out_spec = pl.BlockSpec((2, halves, na, SLAB, LANES), lambda p, q, t: (p, q, 0, p3(t), 0))
        out_shape = (b, c // LANES, na, nf, LANES)
    return pl.pallas_call(
        functools.partial(_longconv_kernel, nf=nf, out_bmajor=out_bmajor),
        grid=(b // 2, nchunk, 3 * ns),
        in_specs=[
            pl.BlockSpec(blk5, lambda p, q, t: (p, z_blk + q, p1(t), 0, 0)),
            pl.BlockSpec((SLAB, 2 * nf, 2 * na), lambda p, q, t: (p1(t), 0, 0)),
            pl.BlockSpec((SLAB, 2 * nf, cl), lambda p, q, t: (p2(t), 0, kf_col + q)),
            pl.BlockSpec((2 * nf, 2 * nf), lambda p, q, t: (0, 0)),
            pl.BlockSpec((2 * nf, 2 * nf), lambda p, q, t: (0, 0)),
            pl.BlockSpec((SLAB, 2 * na, 2 * nf), lambda p, q, t: (p3(t), 0, 0)),
            pl.BlockSpec(blk5, lambda p, q, t: (p, gate_blk + q, p3(t), 0, 0)),
        ],
        out_specs=out_spec,
        out_shape=jax.ShapeDtypeStruct(out_shape, BF16 if out_bmajor else F32),
        scratch_shapes=[pltpu.VMEM((ns, halves, nf * G_PITCH, LANES), U32)],
        compiler_params=_cparams(3, VMEM_LIMIT),
        name="longconv",
    )(z5, f1, kf, f2, f2i, f3, gates5)


def _gelu_exact(x):
    return 0.5 * x * (1.0 + lax.erf(x * (1.0 / math.sqrt(2.0))))


def _gmlp_kernel(gm_ref, lng_ref, lnb_ref, ws_ref, bias_ref, o_ref, *, d_gm):
    g = _gelu_exact(gm_ref[...].astype(F32))
    u, v = g[:, :d_gm], g[:, d_gm:]
    mu = jnp.mean(v, axis=-1, keepdims=True)
    vc = v - mu
    var = jnp.mean(vc * vc, axis=-1, keepdims=True)
    vh = (vc * lax.rsqrt(var + EPS) * lng_ref[...] + lnb_ref[...]).astype(BF16)
    head_dim = d_gm // GM_HEADS
    per_group = CONV_LANES // head_dim
    head = lax.broadcasted_iota(I32, (GM_CHUNK, CONV_LANES), 1) // head_dim
    for c in range(gm_ref.shape[0] // GM_CHUNK):
        rows = slice(c * GM_CHUNK, (c + 1) * GM_CHUNK)
        groups = []
        for grp in range(GM_HEADS // per_group):
            lanes = slice(grp * CONV_LANES, (grp + 1) * CONV_LANES)
            w = ws_ref[grp * per_group * GM_CHUNK:(grp + 1) * per_group * GM_CHUNK, :]
            r = jnp.dot(w, vh[rows, lanes], preferred_element_type=F32)
            s = r[:GM_CHUNK]
            for h in range(1, per_group):
                s = jnp.where(head == h, r[h * GM_CHUNK:(h + 1) * GM_CHUNK], s)
            groups.append(s)
        o_ref[rows, :] = u[rows] * (jnp.concatenate(groups, axis=1) + bias_ref[...])


def _gmlp(gm, ln_g, ln_b, ws_stack, bias_t):
    t, c2 = gm.shape
    d_gm = c2 // 2
    return pl.pallas_call(
        functools.partial(_gmlp_kernel, d_gm=d_gm),
        grid=(t // ROW_TILE,),
        in_specs=[pl.BlockSpec((ROW_TILE, c2), lambda i: (i, 0)),
                  pl.BlockSpec((1, d_gm), lambda i: (0, 0)),
                  pl.BlockSpec((1, d_gm), lambda i: (0, 0)),
                  pl.BlockSpec(ws_stack.shape, lambda i: (0, 0)),
                  pl.BlockSpec(bias_t.shape, lambda i: (0, 0))],
        out_specs=pl.BlockSpec((ROW_TILE, d_gm), lambda i: (i, 0)),
        out_shape=jax.ShapeDtypeStruct((t, d_gm), F32),
        compiler_params=_cparams(1, VMEM_LIMIT),
        name="gmlp",
    )(gm, ln_g, ln_b, ws_stack, bias_t)


def _head_rms(y, bd):
    ms = jnp.dot((y * y).astype(BF16), bd, preferred_element_type=F32)
    return y * lax.rsqrt(ms + EPS)


def _route(logits):
    lane = lax.broadcasted_iota(I32, logits.shape, 1)
    neg = jnp.float32(-1e30)
    big = jnp.int32(ROUTE_LANES)
    gmask = lane < N_GROUPS
    gl = jnp.where(gmask, logits, neg)
    gmax = jnp.max(gl, axis=-1, keepdims=True)
    grp = jnp.min(jnp.where(jnp.logical_and(gl == gmax, gmask), lane, big), axis=-1, keepdims=True)
    psum = jnp.sum(jnp.where(gmask, jnp.exp(gl - gmax), 0.0), axis=-1, keepdims=True)
    p_grp = 1.0 / psum
    lo = N_GROUPS + EXPERTS_PER_GROUP * grp
    emask = jnp.logical_and(lane >= lo, lane < lo + EXPERTS_PER_GROUP)
    el = jnp.where(emask, logits, neg)
    m1 = jnp.max(el, axis=-1, keepdims=True)
    i1 = jnp.min(jnp.where(jnp.logical_and(el == m1, emask), lane, big), axis=-1, keepdims=True)
    emask2 = jnp.logical_and(emask, lane != i1)
    el2 = jnp.where(emask2, logits, neg)
    m2 = jnp.max(el2, axis=-1, keepdims=True)
    i2 = jnp.min(jnp.where(jnp.logical_and(el2 == m2, emask2), lane, big), axis=-1, keepdims=True)
    d = jnp.exp(m2 - m1)
    g1 = p_grp * (1.0 / (1.0 + d))
    g2 = p_grp * (d / (1.0 + d))
    e1 = (i1 - N_GROUPS).astype(F32)
    e2 = (i2 - N_GROUPS).astype(F32)
    return jnp.where(lane == 0, e1, jnp.where(lane == 1, e2, jnp.where(lane == 2, g1,
                     jnp.where(lane == 3, g2, 0.0))))


def _lane_cumsum(v):
    lane = lax.broadcasted_iota(I32, v.shape, 1)
    sh = 1
    while sh < v.shape[1]:
        v = v + jnp.where(lane >= sh, pltpu.roll(v, sh, 1), 0.0)
        sh *= 2
    return v


def _outproj_kernel(x_ref, yh_ref, yg_ref, mg_ref, bd_ref, wo_ref, fg_ref, wr_ref, br_ref,
                    x1_ref, stage_ref, route_ref, count_ref, *, d_hy):
    bd = bd_ref[...]
    mg = mg_ref[...]
    wo = wo_ref[...]
    wr = wr_ref[...]

    yh = jnp.concatenate([yh_ref[0, c] for c in range(yh_ref.shape[1])], axis=1)
    mh = (_head_rms(yh, bd) * mg[:, :d_hy]).astype(BF16)
    mgm = (_head_rms(yg_ref[...], bd) * mg[:, d_hy:]).astype(BF16)
    x1 = (x_ref[...] + jnp.dot(mh, wo[:d_hy], preferred_element_type=F32)
          + jnp.dot(mgm, wo[d_hy:], preferred_element_type=F32))
    x1_ref[...] = x1
    ms = jnp.mean(x1 * x1, axis=-1, keepdims=True)
    hi, lo = _split_bf16(x1 * lax.rsqrt(ms + EPS) * fg_ref[...])
    hw = jnp.dot(hi, wr, preferred_element_type=F32)
    logits = (hw[:, :ROUTE_LANES] + hw[:, ROUTE_LANES:]
              + jnp.dot(lo, wr[:, :ROUTE_LANES], preferred_element_type=F32) + br_ref[...])
    route = _route(logits)

    lane = lax.broadcasted_iota(I32, (MOE_TILE, ROUTE_LANES), 1)
    r = lax.broadcasted_iota(I32, (MOE_TILE, MOE_TILE), 0)
    c = lax.broadcasted_iota(I32, (MOE_TILE, MOE_TILE), 1)
    ltri = jnp.where(c < r, 1.0, 0.0).astype(BF16)
    srow = lax.broadcasted_iota(I32, (STAGE_ROWS, MOE_TILE), 0)
    mlane = lax.broadcasted_iota(I32, (STAGE_ROWS, LANES), 1)
    half = hi.shape[1] // 2
    subs = route.shape[0] // MOE_TILE
    for sub in range(subs):
        rows = slice(sub * MOE_TILE, (sub + 1) * MOE_TILE)
        rsub = route[rows]
        oh0 = lane == rsub[:, 0:1].astype(I32)
        oh1 = lane == rsub[:, 1:2].astype(I32)
        oh = jnp.where(jnp.logical_or(oh0, oh1), 1.0, 0.0)
        before = jnp.dot(ltri, oh.astype(BF16), preferred_element_type=F32)
        cnt = jnp.sum(oh, axis=0, keepdims=True)
        run = jnp.floor((cnt + (CHUNK - 1)) * (1.0 / CHUNK)) * CHUNK
        run_start = _lane_cumsum(jnp.broadcast_to(run, (8, run.shape[1])))[0:1] - run
        tot = before + run_start
        loc0 = jnp.sum(jnp.where(oh0, tot, 0.0), axis=-1, keepdims=True)
        loc1 = jnp.sum(jnp.where(oh1, tot, 0.0), axis=-1, keepdims=True)
        rsub = jnp.where(lane == 4, loc0, jnp.where(lane == 5, loc1, rsub))
        route_ref[rows, :] = rsub
        count_ref[pl.ds(pl.program_id(0) * subs + sub, 1), :] = cnt

        rt = jnp.transpose(rsub)
        p0 = srow == rt[4:5, :].astype(I32)
        p1 = srow == rt[5:6, :].astype(I32)
        perm = jnp.where(jnp.logical_or(p0, p1), 1.0, 0.0).astype(BF16)
        staged = jnp.dot(perm, hi[rows], preferred_element_type=F32)
        gate = jnp.sum(jnp.where(p0, rt[2:3, :], 0.0) + jnp.where(p1, rt[3:4, :], 0.0), axis=-1,
                       keepdims=True)
        meta = jnp.where(mlane == 0,
                         lax.bitcast_convert_type(jnp.broadcast_to(gate, (STAGE_ROWS, LANES)), U32),
                         jnp.uint32(0))
        stage_ref[sub * STAGE_ROWS:(sub + 1) * STAGE_ROWS, :] = jnp.concatenate(
            [_pack(staged[:, :half], staged[:, half:]), meta], axis=1)


def _outproj(x2, yh, yg, mix_g, bd, wo_bf, ffn_g, wr_cat, br):
    t, d = x2.shape
    _, nch, seq_len, _ = yh.shape
    d_hy = nch * LANES
    tiles_per_seq = seq_len // ROW_TILE
    n_tiles = t // ROW_TILE
    row = lambda w: pl.BlockSpec((ROW_TILE, w), lambda i: (i, 0))
    full = lambda a: pl.BlockSpec(a.shape, lambda i: (0,) * a.ndim)
    sw = d // 2 + LANES
    subs = ROW_TILE // MOE_TILE
    n_sub = t // MOE_TILE
    return pl.pallas_call(
        functools.partial(_outproj_kernel, d_hy=d_hy),
        grid=(n_tiles,),
        in_specs=[row(d),
                  pl.BlockSpec((1, nch, ROW_TILE, LANES),
                               lambda i: (i // tiles_per_seq, 0, i % tiles_per_seq, 0)),
                  row(yg.shape[1]), full(mix_g), full(bd), full(wo_bf), full(ffn_g),
                  full(wr_cat), full(br)],
        out_specs=[row(d), pl.BlockSpec((subs * STAGE_ROWS, sw), lambda i: (i, 0)), row(ROUTE_LANES),
                   pl.BlockSpec((n_sub, ROUTE_LANES), lambda i: (0, 0))],
        out_shape=[jax.ShapeDtypeStruct((t, d), F32),
                   jax.ShapeDtypeStruct((n_sub * STAGE_ROWS, sw), U32),
                   jax.ShapeDtypeStruct((t, ROUTE_LANES), F32),
                   jax.ShapeDtypeStruct((n_sub, ROUTE_LANES), F32)],
        compiler_params=_cparams(1, VMEM_LIMIT),
        name="outproj_router",
    )(x2, yh, yg, mix_g, bd, wo_bf, ffn_g, wr_cat, br)


def _moe_plan(tile_cnt, n_tokens):
    n_tiles = tile_cnt.shape[0]
    cnt = tile_cnt[:, :N_EXPERTS].astype(I32)
    run = (cnt + CHUNK - 1) // CHUNK * CHUNK
    run_start = jnp.cumsum(run, axis=1) - run
    used = jnp.sum(run, axis=1)
    e_rows = jnp.sum(run, axis=0)
    e_pad = (e_rows + EXPERT_ROWS - 1) // EXPERT_ROWS * EXPERT_ROWS
    e_end = jnp.cumsum(e_pad)
    e_start = e_end - e_pad
    n_blocks = -(-(n_tokens * TOP_K + n_tiles * N_EXPERTS * (CHUNK - 1)) // EXPERT_ROWS) + N_EXPERTS
    n_used = (e_end[-1:] // EXPERT_ROWS).astype(I32)
    block_row = jnp.arange(n_blocks, dtype=I32) * EXPERT_ROWS
    block_e = jnp.minimum(jnp.sum((e_end[None, :] <= block_row[:, None]).astype(I32), axis=1),
                          N_EXPERTS - 1)
    run_end_in_e = jnp.cumsum(run, axis=0)
    chunk_row = jnp.arange(n_blocks * EXPERT_ROWS // CHUNK, dtype=I32) * CHUNK
    e_of_chunk = jnp.repeat(block_e, EXPERT_ROWS // CHUNK)
    e_sel = jnp.arange(N_EXPERTS, dtype=I32)[None, :] == e_of_chunk[:, None]
    pick = lambda tbl: jnp.dot(e_sel.astype(F32), tbl.astype(F32),
                               precision=lax.Precision.HIGHEST).astype(I32)
    q = chunk_row - pick(e_start[:, None])[:, 0]
    valid = q < pick(e_rows[:, None])[:, 0]
    ends = pick(run_end_in_e.T)
    tile = jnp.minimum(jnp.sum((ends <= q[:, None]).astype(I32), axis=1), n_tiles - 1)
    t_sel = jnp.arange(n_tiles, dtype=I32)[None, :] == tile[:, None]
    at_tile = lambda tbl: jnp.sum(jnp.where(t_sel, pick(tbl.T), 0), axis=1)
    within = q - (at_tile(run_end_in_e) - at_tile(run))
    src_row = tile * STAGE_ROWS + at_tile(run_start) + within
    zero_chunk = STAGE_ROWS // CHUNK - 1
    per_block = EXPERT_ROWS // CHUNK
    src_chunk = jnp.where(valid, src_row // CHUNK, zero_chunk).reshape(n_blocks, 1, per_block)
    dst_chunk = jnp.where(valid, src_row // CHUNK, 0).reshape(n_blocks, 1, per_block)
    n_valid = jnp.sum(valid.reshape(n_blocks, per_block).astype(I32), axis=1)
    return (block_e.astype(I32), n_used, n_valid.astype(I32), src_chunk.astype(I32),
            dst_chunk.astype(I32), used.astype(I32))


def _ffn_kernel(be_ref, nu_ref, nv_ref, used_ref, src_now_ref, src_next_ref, dst_ref, stage_ref,
                wg_ref, wu_ref, wd_ref, ost_ref, xbuf, obuf, zero_ref, wgb, wub, wdb,
                in_sem, out_sem, zero_sem):
    b = pl.program_id(0)
    nu = nu_ref[0]
    per_block = dst_ref.shape[-1]
    slot = b % 2

    def fetch(map_ref, s):
        def body(j, carry):
            row = pl.multiple_of(map_ref[0, 0, j] * CHUNK, CHUNK)
            pltpu.make_async_copy(stage_ref.at[pl.ds(row, CHUNK)], xbuf.at[s, pl.ds(j * CHUNK, CHUNK)],
                                  in_sem.at[s]).start()
            return carry
        lax.fori_loop(0, per_block, body, 0, unroll=8)

    def out_copy(j, chunk, s):
        row = pl.multiple_of(chunk * CHUNK, CHUNK)
        return pltpu.make_async_copy(obuf.at[s, pl.ds(j * CHUNK, CHUNK)], ost_ref.at[pl.ds(row, CHUNK)],
                                     out_sem.at[s])

    def drain(s, n):
        @pl.when(n == per_block)
        def _():
            pltpu.make_async_copy(obuf.at[s], ost_ref.at[pl.ds(0, EXPERT_ROWS)], out_sem.at[s]).wait()

        @pl.when(n != per_block)
        def _():
            def body(j, carry):
                out_copy(0, 0, s).wait()
                return carry
            lax.fori_loop(0, n, body, 0)

    def zero_tails(wait):
        def tile(i, carry):
            def chunk(c, carry2):
                row = pl.multiple_of(i * STAGE_ROWS + c * CHUNK, CHUNK)
                cp = pltpu.make_async_copy(zero_ref, ost_ref.at[pl.ds(row, CHUNK)], zero_sem)
                if wait:
                    cp.wait()
                else:
                    cp.start()
                return carry2
            return lax.fori_loop(used_ref[i] // CHUNK, STAGE_ROWS // CHUNK, chunk, carry)
        lax.fori_loop(0, used_ref.shape[0], tile, 0)

    @pl.when(b == 0)
    def _prologue():
        fetch(src_now_ref, 0)
        zero_ref[...] = jnp.zeros_like(zero_ref)
        zero_tails(wait=False)
        zero_tails(wait=True)

    @pl.when(b + 1 < nu)
    def _prefetch_next():
        fetch(src_next_ref, 1 - slot)

    @pl.when(b < nu)
    def _compute():
        pltpu.make_async_copy(stage_ref.at[pl.ds(0, EXPERT_ROWS)], xbuf.at[slot], in_sem.at[slot]).wait()

        @pl.when(b >= 2)
        def _():
            drain(slot, nv_ref[b - 2])

        @pl.when(jnp.logical_or(b == 0, be_ref[b] != be_ref[jnp.maximum(b - 1, 0)]))
        def _new_expert():
            wgb[...] = wg_ref[0].astype(BF16)
            wub[...] = wu_ref[0].astype(BF16)
            wdb[...] = wd_ref[0].astype(BF16)

        x = xbuf[slot]
        half = wg_ref.shape[1] // 2
        xb = jnp.concatenate(_unpack(x[:, :half]), axis=1)
        gate = lax.bitcast_convert_type(x[:, half:half + 1], F32)
        g = jnp.dot(xb, wgb[...], preferred_element_type=F32)
        u = jnp.dot(xb, wub[...], preferred_element_type=F32)
        h = (g * (1.0 / (1.0 + jnp.exp(-g))) * u).astype(BF16)
        o = jnp.dot(h, wdb[...], preferred_element_type=F32) * gate
        obuf[slot] = _pack(o[:, :half], o[:, half:])

        def body(j, carry):
            out_copy(j, dst_ref[0, 0, j], slot).start()
            return carry

        @pl.when(nv_ref[b] == per_block)
        def _():
            lax.fori_loop(0, per_block, body, 0, unroll=8)

        @pl.when(nv_ref[b] != per_block)
        def _():
            lax.fori_loop(0, nv_ref[b], body, 0)

        @pl.when(b == nu - 1)
        def _():
            drain(slot, nv_ref[b])

            @pl.when(b >= 1)
            def _():
                drain(1 - slot, nv_ref[b - 1])


def _expert_ffn(stage, plan, wg, wu, wd):
    block_e, n_used, n_valid, src_chunk, dst_chunk, used = plan
    nb = block_e.shape[0]
    sw = stage.shape[1]
    d, de = wg.shape[1], wg.shape[2]
    per_block = EXPERT_ROWS // CHUNK
    cur = lambda i, be, nu: jnp.minimum(i, nu[0] - 1)
    wspec = lambda shape: pl.BlockSpec(shape, lambda i, be, nu, nv, us: (be[cur(i, be, nu)], 0, 0))
    smem = lambda imap: pl.BlockSpec((1, 1, per_block), imap, memory_space=pltpu.SMEM)
    return pl.pallas_call(
        _ffn_kernel,
        grid_spec=pltpu.PrefetchScalarGridSpec(
            num_scalar_prefetch=4,
            grid=(nb,),
            in_specs=[smem(lambda i, be, nu, nv, us: (i, 0, 0)),
                      smem(lambda i, be, nu, nv, us: (jnp.minimum(i + 1, nb - 1), 0, 0)),
                      smem(lambda i, be, nu, nv, us: (i, 0, 0)),
                      pl.BlockSpec(memory_space=pl.ANY),
                      wspec((1, d, de)), wspec((1, d, de)), wspec((1, de, d))],
            out_specs=pl.BlockSpec(memory_space=pl.ANY),
            scratch_shapes=[pltpu.VMEM((2, EXPERT_ROWS, sw), U32), pltpu.VMEM((2, EXPERT_ROWS, d // 2), U32),
                            pltpu.VMEM((CHUNK, d // 2), U32),
                            pltpu.VMEM((d, de), BF16), pltpu.VMEM((d, de), BF16), pltpu.VMEM((de, d), BF16),
                            pltpu.SemaphoreType.DMA((2,)), pltpu.SemaphoreType.DMA((2,)),
                            pltpu.SemaphoreType.DMA(())]),
        out_shape=jax.ShapeDtypeStruct((stage.shape[0], d // 2), U32),
        compiler_params=_cparams(1, VMEM_LIMIT),
        name="expert_ffn",
    )(block_e, n_used, n_valid, used, src_chunk, src_chunk, dst_chunk, stage, wg, wu, wd)


def _combine_kernel(ost_ref, route_ref, x1_ref, fg_ref, o_ref):
    col = lax.broadcasted_iota(I32, (MOE_TILE, STAGE_ROWS), 1)
    for sub in range(route_ref.shape[0] // MOE_TILE):
        rows = slice(sub * MOE_TILE, (sub + 1) * MOE_TILE)
        route = route_ref[rows, :]
        sel = jnp.logical_or(col == route[:, 4:5].astype(I32), col == route[:, 5:6].astype(I32))
        perm = jnp.where(sel, 1.0, 0.0).astype(BF16)
        ost = ost_ref[sub * STAGE_ROWS:(sub + 1) * STAGE_ROWS, :]
        y = jnp.dot(perm, jnp.concatenate(_unpack(ost), axis=1), preferred_element_type=F32)
        x2 = x1_ref[rows, :] + y
        ms = jnp.mean(x2 * x2, axis=-1, keepdims=True)
        o_ref[rows, :] = x2 * lax.rsqrt(ms + EPS) * fg_ref[...]


def _combine(ostage, route, x1, final_g):
    t, d = x1.shape
    return pl.pallas_call(
        _combine_kernel,
        grid=(t // ROW_TILE,),
        in_specs=[pl.BlockSpec((ROW_TILE // MOE_TILE * STAGE_ROWS, d // 2), lambda i: (i, 0)),
                  pl.BlockSpec((ROW_TILE, ROUTE_LANES), lambda i: (i, 0)),
                  pl.BlockSpec((ROW_TILE, d), lambda i: (i, 0)),
                  pl.BlockSpec((1, d), lambda i: (0, 0))],
        out_specs=pl.BlockSpec((ROW_TILE, d), lambda i: (i, 0)),
        out_shape=jax.ShapeDtypeStruct((t, d), F32),
        compiler_params=_cparams(1, VMEM_LIMIT),
        name="moe_combine",
    )(ostage, route, x1, final_g)


def _hier_moe_and_norm(x1, stage, route, tile_cnt, wg, wu, wd, final_g):
    t, d = x1.shape
    ostage = _expert_ffn(stage, _moe_plan(tile_cnt, t), wg, wu, wd)
    return _combine(ostage, route, x1, final_g)


def _encoder(x, prm, kf, tables, nf):
    b, l, d = x.shape
    t = b * l
    d_hy = prm["skip"].shape[1]
    x2 = x.reshape(t, d)
    gates5, gm = _inproj(x2, prm["mix_norm_g"], prm["w_in"], prm["short_w"], prm["short_b"], b, l, nf,
                         (HY_ORDER + 1) * d_hy)
    z5 = gates5
    z_blk = HY_ORDER * d_hy // CONV_LANES
    for o in range(HY_ORDER):
        z5 = _longconv(gates5, z5, kf, tables, d_hy, gate_blk=o * d_hy // CONV_LANES,
                       z_blk=z_blk, kf_col=o * d_hy // CONV_LANES, nf=nf, out_bmajor=o < HY_ORDER - 1)
        z_blk = 0
    y_hy = z5.reshape(b, d_hy // LANES, l, LANES)
    y_gm = _gmlp(gm, prm["ln_g"], prm["ln_b"], prm["ws_stack"], prm["bias_t"])
    x1, stage, route, tile_cnt = _outproj(x2, y_hy, y_gm, prm["mix_out_g"], prm["bd"], prm["w_out"], prm["ffn_norm_g"],
                             prm["wr_cat"], prm["br"])
    out = _hier_moe_and_norm(x1, stage, route, tile_cnt, prm["w_e_gate"], prm["w_e_up"], prm["w_e_down"],
                             prm["final_norm_g"])
    return out.reshape(b, l, d)


def kernel(x_prompt, x_sample, mix_norm_g, w_in, hy_short_w, hy_short_b, hy_filt_w_emb, hy_filt_b_emb,
           hy_filt_w_inner, hy_filt_b_inner, hy_filt_freq, hy_filt_w_out, hy_skip, gm_ln_g, gm_ln_b,
           gm_w_s, gm_b_s, mix_out_g, w_out, ffn_norm_g, w_group, b_group, w_expert_router,
           b_expert_router, w_e_gate, w_e_up, w_e_down, final_norm_g):
    assert w_in.shape[0] == 1, "one layer"
    l = x_prompt.shape[1]
    assert x_sample.shape[1] == l
    nf = math.isqrt(2 * l)
    assert nf * nf == 2 * l and nf % SLAB == 0
    d = x_prompt.shape[2]
    d_hy = hy_skip.shape[2]
    d_gm = gm_ln_g.shape[1]
    head_dim = d_gm // GM_HEADS
    assert d_hy // HY_HEADS == head_dim and d_hy == d_gm
    assert x_prompt.shape[0] % 2 == 0 and x_sample.shape[0] % 2 == 0, "sequences are convolved in pairs"
    assert l % (IN_A * nf) == 0 and l % ROW_TILE == 0 and ROW_TILE % MOE_TILE == 0
    assert d_hy % CONV_LANES == 0 and gm_w_s.shape[2] == GM_CHUNK and w_e_gate.shape[1] == N_EXPERTS

    tables = _dft_tables(nf)
    f1, f2, f2i, f3, f1_full = tables
    tables_bf = tuple(a.astype(BF16) for a in (f1, f2, f2i, f3))

    max_decay = math.log(DECAY_TARGET) / FAST_DECAY_PCT
    min_decay = math.log(DECAY_TARGET) / SLOW_DECAY_PCT
    deltas = jnp.abs(jnp.linspace(min_decay, max_decay, d_hy, dtype=F32))[None, :]
    taps = _filter_taps(hy_filt_w_emb[0], hy_filt_b_emb[0][None, :], hy_filt_w_inner[0], hy_filt_b_inner[0],
                        hy_filt_freq[0][None, :], hy_filt_w_out[0], deltas, l, d_hy)
    kf = _filter_spectrum(taps, hy_skip[0].reshape(1, HY_ORDER * d_hy), f1_full, f2, nf)

    n_route = N_GROUPS + N_EXPERTS
    wr = jnp.zeros((d, ROUTE_LANES), F32).at[:, :n_route].set(
        jnp.concatenate([w_group[0], w_expert_router[0]], axis=1))
    wr_hi = wr.astype(BF16)
    wr_lo = (wr - wr_hi.astype(F32)).astype(BF16)
    br = jnp.zeros((1, ROUTE_LANES), F32).at[0, :n_route].set(jnp.concatenate([b_group[0], b_expert_router[0]]))
    hid = jnp.arange(d_hy, dtype=I32) // head_dim
    bd = jnp.where(hid[:, None] == hid[None, :], 1.0 / head_dim, 0.0).astype(BF16)

    prm = dict(
        mix_norm_g=mix_norm_g, w_in=w_in[0].astype(BF16), short_w=hy_short_w[0], short_b=hy_short_b,
        skip=hy_skip[0], ln_g=gm_ln_g, ln_b=gm_ln_b,
        ws_stack=gm_w_s[0].reshape(GM_HEADS * GM_CHUNK, GM_CHUNK).astype(BF16),
        bias_t=jnp.repeat(gm_b_s[0].T, head_dim, axis=1),
        mix_out_g=mix_out_g, bd=bd, w_out=w_out[0].astype(BF16), ffn_norm_g=ffn_norm_g,
        wr_cat=jnp.concatenate([wr_hi, wr_lo], axis=1), br=br,
        w_e_gate=w_e_gate[0], w_e_up=w_e_up[0], w_e_down=w_e_down[0],
        final_norm_g=final_norm_g[None, :])
    y_prompt = _encoder(x_prompt, prm, kf, tables_bf, nf)
    y_sample = _encoder(x_sample, prm, kf, tables_bf, nf)
    return (y_prompt, y_sample)
```

```python
import functools
import math

import jax
import jax.numpy as jnp
from jax import lax
from jax.experimental import pallas as pl
from jax.experimental.pallas import tpu as pltpu

F32 = jnp.float32
BF16 = jnp.bfloat16
U32 = jnp.uint32
I32 = jnp.int32

EPS = 1e-6
HY_ORDER = 2
HY_HEADS = 8
GM_HEADS = 8
GM_CHUNK = 128
N_GROUPS = 4
EXPERTS_PER_GROUP = 8
N_EXPERTS = N_GROUPS * EXPERTS_PER_GROUP
TOP_K = 2
FILTER_EMB = 5
DECAY_TARGET = 1e-2
FAST_DECAY_PCT = 0.3
SLOW_DECAY_PCT = 1.5

LANES = 128
ROUTE_LANES = LANES
SLAB = 16
G_PITCH = SLAB + 8
CONV_LANES = 256
PHASE2_UNROLL = 16
ROW_TILE = 512
GM_TILE = 1024
IN_A = 8
IN_SLABS = 4
EXPERT_ROWS = 1024
CHUNK = 8
MOE_TILE = 512
STAGE_ROWS = MOE_TILE * TOP_K + N_EXPERTS * CHUNK
VMEM_LIMIT = 56 * 1024 * 1024


def _cparams(n_axes, vmem=None):
    return pltpu.CompilerParams(dimension_semantics=("arbitrary",) * n_axes,
                                vmem_limit_bytes=vmem)


def _inproj_kernel(x_ref, xp_ref, xn_ref, g_ref, w_ref, sw_ref, sb_ref, hy_ref, gm_ref, stage_ref,
                   *, nf, d_hy3, tiles_per_seq):
    i = pl.program_id(0)
    rows = x_ref.shape[0]
    pitch = stage_ref.shape[1] // IN_A
    halo = xp_ref.shape[0]
    x = jnp.concatenate([xp_ref[...], x_ref[...], xn_ref[...]], axis=0)
    ms = jnp.mean(x * x, axis=-1, keepdims=True)
    n = (x * lax.rsqrt(ms + EPS) * g_ref[...]).astype(BF16)
    gm = jnp.dot(n, w_ref[:, d_hy3:], preferred_element_type=F32)
    gm_ref[...] = gm[halo:halo + rows].astype(BF16)

    pos = i % tiles_per_seq
    has_prev = jnp.where(pos == 0, 0.0, 1.0)
    has_next = jnp.where(pos == tiles_per_seq - 1, 0.0, 1.0)
    row = lax.broadcasted_iota(I32, (rows, 1), 0)
    prev_w = jnp.where(row == 0, has_prev, 1.0)
    next_w = jnp.where(row == rows - 1, has_next, 1.0)
    n_slabs = stage_ref.shape[0]
    cols = n_slabs * LANES
    ext = rows + 2 * halo
    for c0 in range(0, d_hy3, cols):
        p = jnp.dot(n, w_ref[:, c0:c0 + cols], preferred_element_type=F32)
        prev = pltpu.roll(p, 1, 0)[halo:halo + rows] * prev_w
        nxt = pltpu.roll(p, ext - 1, 0)[halo:halo + rows] * next_w
        res = (sw_ref[0:1, c0:c0 + cols] * prev + sw_ref[1:2, c0:c0 + cols] * p[halo:halo + rows]
               + sw_ref[2:3, c0:c0 + cols] * nxt + sb_ref[:, c0:c0 + cols])
        for s in range(n_slabs):
            for al in range(IN_A):
                stage_ref[s, al * pitch:al * pitch + nf, :] = res[al * nf:(al + 1) * nf,
                                                                s * LANES:(s + 1) * LANES]
        for s in range(n_slabs):
            for b in range(nf):
                hy_ref[0, c0 // LANES + s, b] = stage_ref[s, pl.ds(b, IN_A, stride=pitch), :]


def _inproj(x2, g, w_bf, short_w, short_b, n_seq, seq_len, nf, d_hy3):
    t, d = x2.shape
    dp = w_bf.shape[1]
    rows = IN_A * nf
    tiles_per_seq = seq_len // rows
    hb = rows // 8
    nhb = t // 8
    full = lambda a: pl.BlockSpec(a.shape, lambda i: (0,) * a.ndim)
    return pl.pallas_call(
        functools.partial(_inproj_kernel, nf=nf, d_hy3=d_hy3, tiles_per_seq=tiles_per_seq),
        grid=(t // rows,),
        in_specs=[pl.BlockSpec((rows, d), lambda i: (i, 0)),
                  pl.BlockSpec((8, d), lambda i: (jnp.maximum(i * hb - 1, 0), 0)),
                  pl.BlockSpec((8, d), lambda i: (jnp.minimum((i + 1) * hb, nhb - 1), 0)),
                  full(g), full(w_bf), full(short_w), full(short_b)],
        out_specs=[pl.BlockSpec((1, d_hy3 // LANES, nf, IN_A, LANES),
                                lambda i: (i // tiles_per_seq, 0, 0, i % tiles_per_seq, 0)),
                   pl.BlockSpec((rows, dp - d_hy3), lambda i: (i, 0))],
        out_shape=[jax.ShapeDtypeStruct((n_seq, d_hy3 // LANES, nf, nf // 2, LANES), F32),
                   jax.ShapeDtypeStruct((t, dp - d_hy3), BF16)],
        scratch_shapes=[pltpu.VMEM((IN_SLABS, IN_A * (nf + 8), LANES), F32)],
        compiler_params=_cparams(1, VMEM_LIMIT),
        name="inproj_shortconv",
    )(x2, x2, x2, g, w_bf, short_w, short_b)


def _cos_sin(m, period):
    ang = m.astype(F32) * (2.0 * math.pi / period)
    return jnp.cos(ang), jnp.sin(ang)


def _stack_complex(mr, mi):
    top = jnp.concatenate([mr, -mi], axis=-1)
    bot = jnp.concatenate([mi, mr], axis=-1)
    return jnp.concatenate([top, bot], axis=-2)


def _dft_tables(nf):
    na = nf // 2
    n = nf * nf
    idx = jnp.arange(nf, dtype=I32)
    c, s = _cos_sin((idx[:, None] * idx[None, :]) % nf, nf)
    f2 = _stack_complex(c, -s)
    f2i = _stack_complex(c, s)
    b_ = idx[:, None, None]
    ka = idx[None, :, None]
    a_ = idx[None, None, :]
    m1 = (nf * a_ * ka + b_ * ka) % n
    c1, s1 = _cos_sin(m1, n)
    f1 = _stack_complex(c1[:, :, :na], -s1[:, :, :na])
    f1_full = jnp.concatenate([c1, -s1], axis=1)
    c3 = jnp.swapaxes(c1, 1, 2)[:, :na, :]
    s3 = jnp.swapaxes(s1, 1, 2)[:, :na, :]
    f3 = _stack_complex(c3, s3)
    return f1, f2, f2i, f3, f1_full


def _split_bf16(x):
    hi = x.astype(BF16)
    lo = (x - hi.astype(F32)).astype(BF16)
    return hi, lo


def _stack_split_lhs(a):
    hi, lo = _split_bf16(a)
    return jnp.concatenate([hi, lo, hi], axis=-1)


def _stack_split_rhs(b):
    hi, lo = _split_bf16(b)
    return jnp.concatenate([hi, hi, lo], axis=-2)


def _filter_taps_kernel(wemb_ref, bemb_ref, win_ref, bin_ref, freq_ref, wout_ref, delta_ref, o_ref,
                        *, seq_len, n_inner, d_hy):
    rows = o_ref.shape[1]
    j = pl.program_id(0) * rows + lax.broadcasted_iota(I32, (rows, 1), 0)
    tidx = jnp.where(j < seq_len, j, 2 * seq_len - j).astype(F32)
    t = tidx / float(seq_len - 1)
    fr0 = jnp.float32(1e-4)
    ang0 = (2.0 * math.pi / seq_len) * tidx * fr0
    ang1 = (2.0 * math.pi / seq_len) * tidx
    freq = freq_ref[...]
    pre = (t * wemb_ref[0:1, :] + jnp.cos(ang0) * wemb_ref[1:2, :] + jnp.cos(ang1) * wemb_ref[2:3, :]
           - jnp.sin(ang0) * wemb_ref[3:4, :] - jnp.sin(ang1) * wemb_ref[4:5, :] + bemb_ref[...])
    hdn = jnp.sin(freq * pre)
    for i in range(n_inner):
        hdn = jnp.sin(freq * (jnp.dot(_stack_split_lhs(hdn), win_ref[i], preferred_element_type=F32)
                              + bin_ref[i:i + 1, :]))
    h = jnp.dot(_stack_split_lhs(hdn), wout_ref[...], preferred_element_type=F32)
    decay = jnp.exp(-t * delta_ref[...])
    for o in range(HY_ORDER):
        fwd = h[:, (2 * o) * d_hy:(2 * o + 1) * d_hy]
        bwd = h[:, (2 * o + 1) * d_hy:(2 * o + 2) * d_hy]
        val = jnp.where(j == seq_len, 0.0, jnp.where(j < seq_len, fwd, bwd) * decay)
        for cc in range(d_hy // LANES):
            o_ref[o * (d_hy // LANES) + cc] = val[:, cc * LANES:(cc + 1) * LANES]


def _filter_taps(wemb, bemb, win, bin_, freq, wout, deltas, seq_len, d_hy):
    rows = 1024
    n2 = 2 * seq_len
    width = wemb.shape[1]
    n_inner = win.shape[0]
    full = lambda *shape: pl.BlockSpec(shape, lambda i: (0,) * len(shape))
    return pl.pallas_call(
        functools.partial(_filter_taps_kernel, seq_len=seq_len, n_inner=n_inner, d_hy=d_hy),
        grid=(n2 // rows,),
        in_specs=[full(FILTER_EMB, width), full(1, width), full(n_inner, 3 * width, width),
                  full(n_inner, width), full(1, width), full(3 * width, 2 * HY_ORDER * d_hy),
                  full(1, d_hy)],
        out_specs=pl.BlockSpec((HY_ORDER * d_hy // LANES, rows, LANES), lambda i: (0, i, 0)),
        out_shape=jax.ShapeDtypeStruct((HY_ORDER * d_hy // LANES, n2, LANES), F32),
        compiler_params=_cparams(1, VMEM_LIMIT),
        name="filter_taps",
    )(wemb, bemb, _stack_split_rhs(win), bin_, freq, _stack_split_rhs(wout), deltas)


def _filter_fft1_kernel(k_ref, f_ref, o_ref, *, nf):
    halves = k_ref.shape[0]
    k = k_ref.reshape(halves, nf * SLAB, LANES)
    g = o_ref.reshape(halves, 2, nf * SLAB, LANES)
    for bl in range(SLAB):
        col = jnp.concatenate([k[h, pl.ds(bl, nf, stride=SLAB), :] for h in range(halves)], axis=1)
        res = jnp.dot(f_ref[bl], _stack_split_rhs(col), preferred_element_type=F32)
        for h in range(halves):
            for ri in range(2):
                g[h, ri, pl.ds(bl, nf, stride=SLAB), :] = res[ri * nf:(ri + 1) * nf,
                                                              h * LANES:(h + 1) * LANES]


def _filter_fft2_kernel(g_ref, f_ref, skip_ref, o_ref, *, nf, scale):
    halves = g_ref.shape[0]
    lag0 = jnp.concatenate([jnp.broadcast_to(skip_ref[...], (nf, skip_ref.shape[1])),
                            jnp.zeros((nf, skip_ref.shape[1]), F32)], axis=0)
    for kl in range(SLAB):
        rhs = jnp.concatenate([jnp.concatenate([g_ref[h, ri, kl] for h in range(halves)], axis=1)
                               for ri in range(2)], axis=0)
        res = jnp.dot(f_ref[...], _stack_split_rhs(rhs), preferred_element_type=F32)
        o_ref[kl] = ((res + lag0) * scale).astype(o_ref.dtype)


def _filter_spectrum(taps, skip, f1_full, f2, nf):
    nch = taps.shape[0]
    c = nch * LANES
    halves = CONV_LANES // LANES
    k4 = taps.reshape(nch, nf, nf, LANES)
    g = pl.pallas_call(
        functools.partial(_filter_fft1_kernel, nf=nf),
        grid=(nf // SLAB, nch // halves),
        in_specs=[pl.BlockSpec((halves, nf, SLAB, LANES), lambda i, j: (j, 0, i, 0)),
                  pl.BlockSpec((SLAB, 2 * nf, 3 * nf), lambda i, j: (i, 0, 0))],
        out_specs=pl.BlockSpec((halves, 2, nf, SLAB, LANES), lambda i, j: (j, 0, 0, i, 0)),
        out_shape=jax.ShapeDtypeStruct((nch, 2, nf, nf, LANES), F32),
        compiler_params=_cparams(2, VMEM_LIMIT),
        name="filter_fft1",
    )(k4, _stack_split_lhs(f1_full))
    return pl.pallas_call(
        functools.partial(_filter_fft2_kernel, nf=nf, scale=1.0 / (nf * nf)),
        grid=(nf // SLAB, nch // halves),
        in_specs=[pl.BlockSpec((halves, 2, SLAB, nf, LANES), lambda i, j: (j, 0, i, 0, 0)),
                  pl.BlockSpec((2 * nf, 6 * nf), lambda i, j: (0, 0)),
                  pl.BlockSpec((1, CONV_LANES), lambda i, j: (0, j))],
        out_specs=pl.BlockSpec((SLAB, 2 * nf, CONV_LANES), lambda i, j: (i, 0, j)),
        out_shape=jax.ShapeDtypeStruct((nf, 2 * nf, c), BF16),
        compiler_params=_cparams(2, VMEM_LIMIT),
        name="filter_fft2",
    )(g, _stack_split_lhs(f2), skip)


def _pack(re, im):
    rb = lax.bitcast_convert_type(re.astype(BF16).astype(F32), U32)
    ib = lax.bitcast_convert_type(im.astype(BF16).astype(F32), U32)
    return rb | (ib >> 16)


def _unpack(w):
    re = lax.bitcast_convert_type(w & jnp.uint32(0xFFFF0000), F32).astype(BF16)
    im = lax.bitcast_convert_type(w << 16, F32).astype(BF16)
    return re, im


def _longconv_kernel(z1_ref, f1_ref, kf_ref, f2_ref, f2i_ref, f3_ref, gate_ref, o_ref, g_ref,
                     *, nf, out_bmajor):
    na = nf // 2
    ns = nf // SLAB
    halves = g_ref.shape[1]
    t = pl.program_id(2)

    def load_rows(ref5, s, bl):
        return jnp.concatenate([ref5[s, h, bl] for h in range(halves)], axis=1)

    def store_col(blk, bl, val):
        for h in range(halves):
            g_ref[blk, h, pl.ds(bl, nf, stride=G_PITCH), :] = val[:, h * LANES:(h + 1) * LANES]

    def load_col(blk, bl):
        return jnp.concatenate([g_ref[blk, h, pl.ds(bl, nf, stride=G_PITCH), :] for h in range(halves)],
                               axis=1)

    @pl.when(t < ns)
    def _phase1():
        for bl in range(SLAB):
            rhs = jnp.concatenate([load_rows(z1_ref, 0, bl), load_rows(z1_ref, 1, bl)],
                                  axis=0).astype(BF16)
            out = jnp.dot(f1_ref[bl], rhs, preferred_element_type=F32)
            store_col(t, bl, _pack(out[:nf], out[nf:]))

    @pl.when(jnp.logical_and(t >= ns, t < 2 * ns))
    def _phase2():
        i = t - ns

        def load_slab(ka):
            row0 = pl.multiple_of(ka * G_PITCH, CHUNK)
            return jnp.concatenate([g_ref[:, h, pl.ds(row0, SLAB), :].reshape(nf, LANES)
                                    for h in range(halves)], axis=1)

        def transform(w, kl):
            re, im = _unpack(w)
            s = jnp.dot(f2_ref[...], jnp.concatenate([re, im], axis=0),
                        preferred_element_type=F32)
            sr, si = s[:nf], s[nf:]
            kr, ki = kf_ref[kl, :nf, :], kf_ref[kl, nf:, :]
            pr = (sr * kr - si * ki).astype(BF16)
            pi = (sr * ki + si * kr).astype(BF16)
            h = jnp.dot(f2i_ref[...], jnp.concatenate([pr, pi], axis=0),
                        preferred_element_type=F32)
            return _pack(h[:nf], h[nf:])

        def store_slab(ka, packed):
            row0 = pl.multiple_of(ka * G_PITCH, CHUNK)
            for hf in range(halves):
                g_ref[:, hf, pl.ds(row0, SLAB), :] = packed[:, hf * LANES:(hf + 1) * LANES].reshape(
                    ns, SLAB, LANES)

        def body(grp, carry):
            kls = [grp * PHASE2_UNROLL + u for u in range(PHASE2_UNROLL)]
            ws = [load_slab(i * SLAB + kl) for kl in kls]
            outs = [transform(w, kl) for w, kl in zip(ws, kls)]
            for kl, packed in zip(kls, outs):
                store_slab(i * SLAB + kl, packed)
            return carry

        lax.fori_loop(0, SLAB // PHASE2_UNROLL, body, 0)

    @pl.when(t >= 2 * ns)
    def _phase3():
        j = t - 2 * ns
        out_flat = None if out_bmajor else o_ref.reshape(2, halves, na * SLAB, LANES)
        for bl in range(SLAB):
            re, im = _unpack(load_col(j, bl))
            y = jnp.dot(f3_ref[bl], jnp.concatenate([re, im], axis=0),
                        preferred_element_type=F32)
            for s in range(2):
                val = load_rows(gate_ref, s, bl) * y[s * na:(s + 1) * na]
                for h in range(halves):
                    piece = val[:, h * LANES:(h + 1) * LANES]
                    if out_bmajor:
                        o_ref[s, h, bl] = piece.astype(o_ref.dtype)
                    else:
                        out_flat[s, h, pl.ds(bl, na, stride=SLAB), :] = piece


def _longconv(gates5, z5, kf, tables, c, gate_blk, z_blk, kf_col, nf, out_bmajor):
    f1, f2, f2i, f3 = tables
    b, _, _, na, _ = z5.shape
    ns = nf // SLAB
    halves = CONV_LANES // LANES
    nchunk = c // CONV_LANES
    p1 = lambda t: jnp.minimum(t, ns - 1)
    p2 = lambda t: jnp.clip(t - ns, 0, ns - 1)
    p3 = lambda t: jnp.clip(t - 2 * ns, 0, ns - 1)
    cl = CONV_LANES
    blk5 = (2, halves, SLAB, na, LANES)
    if out_bmajor:
        out_spec = pl.BlockSpec(blk5, lambda p, q, t: (p, q, p3(t), 0, 0))
        out_shape = (b, c // LANES, nf, na, LANES)
    else:
        out_spec = pl.BlockSpec((2, halves, na, SLAB, LANES), lambda p, q, t: (p, q, 0, p3(t), 0))
        out_shape = (b, c // LANES, na, nf, LANES)
    return pl.pallas_call(
        functools.partial(_longconv_kernel, nf=nf, out_bmajor=out_bmajor),
        grid=(b // 2, nchunk, 3 * ns),
        in_specs=[
            pl.BlockSpec(blk5, lambda p, q, t: (p, z_blk + q, p1(t), 0, 0)),
            pl.BlockSpec((SLAB, 2 * nf, 2 * na), lambda p, q, t: (p1(t), 0, 0)),
            pl.BlockSpec((SLAB, 2 * nf, cl), lambda p, q, t: (p2(t), 0, kf_col + q)),
            pl.BlockSpec((2 * nf, 2 * nf), lambda p, q, t: (0, 0)),
            pl.BlockSpec((2 * nf, 2 * nf), lambda p, q, t: (0, 0)),
            pl.BlockSpec((SLAB, 2 * na, 2 * nf), lambda p, q, t: (p3(t), 0, 0)),
            pl.BlockSpec(blk5, lambda p, q, t: (p, gate_blk + q, p3(t), 0, 0)),
        ],
        out_specs=out_spec,
        out_shape=jax.ShapeDtypeStruct(out_shape, BF16 if out_bmajor else F32),
        scratch_shapes=[pltpu.VMEM((ns, halves, nf * G_PITCH, LANES), U32)],
        compiler_params=_cparams(3, VMEM_LIMIT),
        name="longconv",
    )(z5, f1, kf, f2, f2i, f3, gates5)


def _gelu_exact(x):
    return 0.5 * x * (1.0 + lax.erf(x * (1.0 / math.sqrt(2.0))))


def _gmlp_kernel(gm_ref, lng_ref, lnb_ref, ws_ref, bias_ref, o_ref, *, d_gm):
    g = _gelu_exact(gm_ref[...].astype(F32))
    u, v = g[:, :d_gm], g[:, d_gm:]
    mu = jnp.mean(v, axis=-1, keepdims=True)
    vc = v - mu
    var = jnp.mean(vc * vc, axis=-1, keepdims=True)
    vh = (vc * lax.rsqrt(var + EPS) * lng_ref[...] + lnb_ref[...]).astype(BF16)
    head_dim = d_gm // GM_HEADS
    per_group = CONV_LANES // head_dim
    head = lax.broadcasted_iota(I32, (GM_CHUNK, CONV_LANES), 1) // head_dim
    for c in range(gm_ref.shape[0] // GM_CHUNK):
        rows = slice(c * GM_CHUNK, (c + 1) * GM_CHUNK)
        groups = []
        for grp in range(GM_HEADS // per_group):
            lanes = slice(grp * CONV_LANES, (grp + 1) * CONV_LANES)
            w = ws_ref[grp * per_group * GM_CHUNK:(grp + 1) * per_group * GM_CHUNK, :]
            r = jnp.dot(w, vh[rows, lanes], preferred_element_type=F32)
            s = r[:GM_CHUNK]
            for h in range(1, per_group):
                s = jnp.where(head == h, r[h * GM_CHUNK:(h + 1) * GM_CHUNK], s)
            groups.append(s)
        o_ref[rows, :] = u[rows] * (jnp.concatenate(groups, axis=1) + bias_ref[...])


def _gmlp(gm, ln_g, ln_b, ws_stack, bias_t):
    t, c2 = gm.shape
    d_gm = c2 // 2
    return pl.pallas_call(
        functools.partial(_gmlp_kernel, d_gm=d_gm),
        grid=(t // GM_TILE,),
        in_specs=[pl.BlockSpec((GM_TILE, c2), lambda i: (i, 0)),
                  pl.BlockSpec((1, d_gm), lambda i: (0, 0)),
                  pl.BlockSpec((1, d_gm), lambda i: (0, 0)),
                  pl.BlockSpec(ws_stack.shape, lambda i: (0, 0)),
                  pl.BlockSpec(bias_t.shape, lambda i: (0, 0))],
        out_specs=pl.BlockSpec((GM_TILE, d_gm), lambda i: (i, 0)),
        out_shape=jax.ShapeDtypeStruct((t, d_gm), F32),
        compiler_params=_cparams(1, VMEM_LIMIT),
        name="gmlp",
    )(gm, ln_g, ln_b, ws_stack, bias_t)


def _head_rms(y, bd):
    ms = jnp.dot((y * y).astype(BF16), bd, preferred_element_type=F32)
    return y * lax.rsqrt(ms + EPS)


def _route(logits):
    lane = lax.broadcasted_iota(I32, logits.shape, 1)
    neg = jnp.float32(-1e30)
    big = jnp.int32(ROUTE_LANES)
    gmask = lane < N_GROUPS
    gl = jnp.where(gmask, logits, neg)
    gmax = jnp.max(gl, axis=-1, keepdims=True)
    grp = jnp.min(jnp.where(jnp.logical_and(gl == gmax, gmask), lane, big), axis=-1, keepdims=True)
    psum = jnp.sum(jnp.where(gmask, jnp.exp(gl - gmax), 0.0), axis=-1, keepdims=True)
    p_grp = 1.0 / psum
    lo = N_GROUPS + EXPERTS_PER_GROUP * grp
    emask = jnp.logical_and(lane >= lo, lane < lo + EXPERTS_PER_GROUP)
    el = jnp.where(emask, logits, neg)
    m1 = jnp.max(el, axis=-1, keepdims=True)
    i1 = jnp.min(jnp.where(jnp.logical_and(el == m1, emask), lane, big), axis=-1, keepdims=True)
    emask2 = jnp.logical_and(emask, lane != i1)
    el2 = jnp.where(emask2, logits, neg)
    m2 = jnp.max(el2, axis=-1, keepdims=True)
    i2 = jnp.min(jnp.where(jnp.logical_and(el2 == m2, emask2), lane, big), axis=-1, keepdims=True)
    d = jnp.exp(m2 - m1)
    g1 = p_grp * (1.0 / (1.0 + d))
    g2 = p_grp * (d / (1.0 + d))
    e1 = (i1 - N_GROUPS).astype(F32)
    e2 = (i2 - N_GROUPS).astype(F32)
    return jnp.where(lane == 0, e1, jnp.where(lane == 1, e2, jnp.where(lane == 2, g1,
                     jnp.where(lane == 3, g2, 0.0))))


def _lane_cumsum(v):
    lane = lax.broadcasted_iota(I32, v.shape, 1)
    sh = 1
    while sh < v.shape[1]:
        v = v + jnp.where(lane >= sh, pltpu.roll(v, sh, 1), 0.0)
        sh *= 2
    return v


def _outproj_kernel(x_ref, yh_ref, yg_ref, mg_ref, bd_ref, wo_ref, fg_ref, wr_ref, br_ref,
                    x1_ref, stage_ref, route_ref, count_ref, *, d_hy):
    bd = bd_ref[...]
    mg = mg_ref[...]
    wo = wo_ref[...]
    wr = wr_ref[...]

    yh = jnp.concatenate([yh_ref[0, c] for c in range(yh_ref.shape[1])], axis=1)
    mh = (_head_rms(yh, bd) * mg[:, :d_hy]).astype(BF16)
    mgm = (_head_rms(yg_ref[...], bd) * mg[:, d_hy:]).astype(BF16)
    x1 = (x_ref[...] + jnp.dot(mh, wo[:d_hy], preferred_element_type=F32)
          + jnp.dot(mgm, wo[d_hy:], preferred_element_type=F32))
    x1_ref[...] = x1
    ms = jnp.mean(x1 * x1, axis=-1, keepdims=True)
    hi, lo = _split_bf16(x1 * lax.rsqrt(ms + EPS) * fg_ref[...])
    hw = jnp.dot(hi, wr, preferred_element_type=F32)
    logits = (hw[:, :ROUTE_LANES] + hw[:, ROUTE_LANES:]
              + jnp.dot(lo, wr[:, :ROUTE_LANES], preferred_element_type=F32) + br_ref[...])
    route = _route(logits)

    lane = lax.broadcasted_iota(I32, (MOE_TILE, ROUTE_LANES), 1)
    r = lax.broadcasted_iota(I32, (MOE_TILE, MOE_TILE), 0)
    c = lax.broadcasted_iota(I32, (MOE_TILE, MOE_TILE), 1)
    ltri = jnp.where(c < r, 1.0, 0.0).astype(BF16)
    srow = lax.broadcasted_iota(I32, (STAGE_ROWS, MOE_TILE), 0)
    mlane = lax.broadcasted_iota(I32, (STAGE_ROWS, LANES), 1)
    half = hi.shape[1] // 2
    subs = route.shape[0] // MOE_TILE
    for sub in range(subs):
        rows = slice(sub * MOE_TILE, (sub + 1) * MOE_TILE)
        rsub = route[rows]
        oh0 = lane == rsub[:, 0:1].astype(I32)
        oh1 = lane == rsub[:, 1:2].astype(I32)
        oh = jnp.where(jnp.logical_or(oh0, oh1), 1.0, 0.0)
        before = jnp.dot(ltri, oh.astype(BF16), preferred_element_type=F32)
        cnt = jnp.sum(oh, axis=0, keepdims=True)
        run = jnp.floor((cnt + (CHUNK - 1)) * (1.0 / CHUNK)) * CHUNK
        run_start = _lane_cumsum(jnp.broadcast_to(run, (8, run.shape[1])))[0:1] - run
        tot = before + run_start
        loc0 = jnp.sum(jnp.where(oh0, tot, 0.0), axis=-1, keepdims=True)
        loc1 = jnp.sum(jnp.where(oh1, tot, 0.0), axis=-1, keepdims=True)
        rsub = jnp.where(lane == 4, loc0, jnp.where(lane == 5, loc1, rsub))
        route_ref[rows, :] = rsub
        count_ref[pl.ds(pl.program_id(0) * subs + sub, 1), :] = cnt

        rt = jnp.transpose(rsub)
        p0 = srow == rt[4:5, :].astype(I32)
        p1 = srow == rt[5:6, :].astype(I32)
        perm = jnp.where(jnp.logical_or(p0, p1), 1.0, 0.0).astype(BF16)
        staged = jnp.dot(perm, hi[rows], preferred_element_type=F32)
        gate = jnp.sum(jnp.where(p0, rt[2:3, :], 0.0) + jnp.where(p1, rt[3:4, :], 0.0), axis=-1,
                       keepdims=True)
        meta = jnp.where(mlane == 0,
                         lax.bitcast_convert_type(jnp.broadcast_to(gate, (STAGE_ROWS, LANES)), U32),
                         jnp.uint32(0))
        stage_ref[sub * STAGE_ROWS:(sub + 1) * STAGE_ROWS, :] = jnp.concatenate(
            [_pack(staged[:, :half], staged[:, half:]), meta], axis=1)


def _outproj(x2, yh, yg, mix_g, bd, wo_bf, ffn_g, wr_cat, br):
    t, d = x2.shape
    _, nch, seq_len, _ = yh.shape
    d_hy = nch * LANES
    tiles_per_seq = seq_len // ROW_TILE
    n_tiles = t // ROW_TILE
    row = lambda w: pl.BlockSpec((ROW_TILE, w), lambda i: (i, 0))
    full = lambda a: pl.BlockSpec(a.shape, lambda i: (0,) * a.ndim)
    sw = d // 2 + LANES
    subs = ROW_TILE // MOE_TILE
    n_sub = t // MOE_TILE
    return pl.pallas_call(
        functools.partial(_outproj_kernel, d_hy=d_hy),
        grid=(n_tiles,),
        in_specs=[row(d),
                  pl.BlockSpec((1, nch, ROW_TILE, LANES),
                               lambda i: (i // tiles_per_seq, 0, i % tiles_per_seq, 0)),
                  row(yg.shape[1]), full(mix_g), full(bd), full(wo_bf), full(ffn_g),
                  full(wr_cat), full(br)],
        out_specs=[row(d), pl.BlockSpec((subs * STAGE_ROWS, sw), lambda i: (i, 0)), row(ROUTE_LANES),
                   pl.BlockSpec((n_sub, ROUTE_LANES), lambda i: (0, 0))],
        out_shape=[jax.ShapeDtypeStruct((t, d), F32),
                   jax.ShapeDtypeStruct((n_sub * STAGE_ROWS, sw), U32),
                   jax.ShapeDtypeStruct((t, ROUTE_LANES), F32),
                   jax.ShapeDtypeStruct((n_sub, ROUTE_LANES), F32)],
        compiler_params=_cparams(1, VMEM_LIMIT),
        name="outproj_router",
    )(x2, yh, yg, mix_g, bd, wo_bf, ffn_g, wr_cat, br)


def _moe_plan(tile_cnt, n_tokens):
    n_tiles = tile_cnt.shape[0]
    cnt = tile_cnt[:, :N_EXPERTS].astype(I32)
    run = (cnt + CHUNK - 1) // CHUNK * CHUNK
    run_start = jnp.cumsum(run, axis=1) - run
    used = jnp.sum(run, axis=1)
    e_rows = jnp.sum(run, axis=0)
    e_pad = (e_rows + EXPERT_ROWS - 1) // EXPERT_ROWS * EXPERT_ROWS
    e_end = jnp.cumsum(e_pad)
    e_start = e_end - e_pad
    n_blocks = -(-(n_tokens * TOP_K + n_tiles * N_EXPERTS * (CHUNK - 1)) // EXPERT_ROWS) + N_EXPERTS
    n_used = (e_end[-1:] // EXPERT_ROWS).astype(I32)
    block_row = jnp.arange(n_blocks, dtype=I32) * EXPERT_ROWS
    block_e = jnp.minimum(jnp.sum((e_end[None, :] <= block_row[:, None]).astype(I32), axis=1),
                          N_EXPERTS - 1)
    run_end_in_e = jnp.cumsum(run, axis=0)
    chunk_row = jnp.arange(n_blocks * EXPERT_ROWS // CHUNK, dtype=I32) * CHUNK
    e_of_chunk = jnp.repeat(block_e, EXPERT_ROWS // CHUNK)
    e_sel = jnp.arange(N_EXPERTS, dtype=I32)[None, :] == e_of_chunk[:, None]
    pick = lambda tbl: jnp.dot(e_sel.astype(F32), tbl.astype(F32),
                               precision=lax.Precision.HIGHEST).astype(I32)
    q = chunk_row - pick(e_start[:, None])[:, 0]
    valid = q < pick(e_rows[:, None])[:, 0]
    ends = pick(run_end_in_e.T)
    tile = jnp.minimum(jnp.sum((ends <= q[:, None]).astype(I32), axis=1), n_tiles - 1)
    t_sel = jnp.arange(n_tiles, dtype=I32)[None, :] == tile[:, None]
    at_tile = lambda tbl: jnp.sum(jnp.where(t_sel, pick(tbl.T), 0), axis=1)
    within = q - (at_tile(run_end_in_e) - at_tile(run))
    src_row = tile * STAGE_ROWS + at_tile(run_start) + within
    zero_chunk = STAGE_ROWS // CHUNK - 1
    per_block = EXPERT_ROWS // CHUNK
    src_chunk = jnp.where(valid, src_row // CHUNK, zero_chunk).reshape(n_blocks, 1, per_block)
    dst_chunk = jnp.where(valid, src_row // CHUNK, 0).reshape(n_blocks, 1, per_block)
    n_valid = jnp.sum(valid.reshape(n_blocks, per_block).astype(I32), axis=1)
    return (block_e.astype(I32), n_used, n_valid.astype(I32), src_chunk.astype(I32),
            dst_chunk.astype(I32), used.astype(I32))


def _ffn_kernel(be_ref, nu_ref, nv_ref, used_ref, src_now_ref, src_next_ref, dst_ref, stage_ref,
                wg_ref, wu_ref, wd_ref, ost_ref, xbuf, obuf, zero_ref, wgb, wub, wdb,
                in_sem, out_sem, zero_sem):
    b = pl.program_id(0)
    nu = nu_ref[0]
    per_block = dst_ref.shape[-1]
    slot = b % 2

    def fetch(map_ref, s):
        def body(j, carry):
            row = pl.multiple_of(map_ref[0, 0, j] * CHUNK, CHUNK)
            pltpu.make_async_copy(stage_ref.at[pl.ds(row, CHUNK)], xbuf.at[s, pl.ds(j * CHUNK, CHUNK)],
                                  in_sem.at[s]).start()
            return carry
        lax.fori_loop(0, per_block, body, 0, unroll=8)

    def out_copy(j, chunk, s):
        row = pl.multiple_of(chunk * CHUNK, CHUNK)
        return pltpu.make_async_copy(obuf.at[s, pl.ds(j * CHUNK, CHUNK)], ost_ref.at[pl.ds(row, CHUNK)],
                                     out_sem.at[s])

    def drain(s, n):
        @pl.when(n == per_block)
        def _():
            pltpu.make_async_copy(obuf.at[s], ost_ref.at[pl.ds(0, EXPERT_ROWS)], out_sem.at[s]).wait()

        @pl.when(n != per_block)
        def _():
            def body(j, carry):
                out_copy(0, 0, s).wait()
                return carry
            lax.fori_loop(0, n, body, 0)

    def zero_tails(wait):
        def tile(i, carry):
            def chunk(c, carry2):
                row = pl.multiple_of(i * STAGE_ROWS + c * CHUNK, CHUNK)
                cp = pltpu.make_async_copy(zero_ref, ost_ref.at[pl.ds(row, CHUNK)], zero_sem)
                if wait:
                    cp.wait()
                else:
                    cp.start()
                return carry2
            return lax.fori_loop(used_ref[i] // CHUNK, STAGE_ROWS // CHUNK, chunk, carry)
        lax.fori_loop(0, used_ref.shape[0], tile, 0)

    @pl.when(b == 0)
    def _prologue():
        fetch(src_now_ref, 0)
        zero_ref[...] = jnp.zeros_like(zero_ref)
        zero_tails(wait=False)
        zero_tails(wait=True)

    @pl.when(b + 1 < nu)
    def _prefetch_next():
        fetch(src_next_ref, 1 - slot)

    @pl.when(b < nu)
    def _compute():
        pltpu.make_async_copy(stage_ref.at[pl.ds(0, EXPERT_ROWS)], xbuf.at[slot], in_sem.at[slot]).wait()

        @pl.when(b >= 2)
        def _():
            drain(slot, nv_ref[b - 2])

        @pl.when(jnp.logical_or(b == 0, be_ref[b] != be_ref[jnp.maximum(b - 1, 0)]))
        def _new_expert():
            wgb[...] = wg_ref[0].astype(BF16)
            wub[...] = wu_ref[0].astype(BF16)
            wdb[...] = wd_ref[0].astype(BF16)

        x = xbuf[slot]
        half = wg_ref.shape[1] // 2
        xb = jnp.concatenate(_unpack(x[:, :half]), axis=1)
        gate = lax.bitcast_convert_type(x[:, half:half + 1], F32)
        g = jnp.dot(xb, wgb[...], preferred_element_type=F32)
        u = jnp.dot(xb, wub[...], preferred_element_type=F32)
        h = (g * (1.0 / (1.0 + jnp.exp(-g))) * u).astype(BF16)
        o = jnp.dot(h, wdb[...], preferred_element_type=F32) * gate
        obuf[slot] = _pack(o[:, :half], o[:, half:])

        def body(j, carry):
            out_copy(j, dst_ref[0, 0, j], slot).start()
            return carry

        @pl.when(nv_ref[b] == per_block)
        def _():
            lax.fori_loop(0, per_block, body, 0, unroll=8)

        @pl.when(nv_ref[b] != per_block)
        def _():
            lax.fori_loop(0, nv_ref[b], body, 0)

        @pl.when(b == nu - 1)
        def _():
            drain(slot, nv_ref[b])

            @pl.when(b >= 1)
            def _():
                drain(1 - slot, nv_ref[b - 1])


def _expert_ffn(stage, plan, wg, wu, wd):
    block_e, n_used, n_valid, src_chunk, dst_chunk, used = plan
    nb = block_e.shape[0]
    sw = stage.shape[1]
    d, de = wg.shape[1], wg.shape[2]
    per_block = EXPERT_ROWS // CHUNK
    cur = lambda i, be, nu: jnp.minimum(i, nu[0] - 1)
    wspec = lambda shape: pl.BlockSpec(shape, lambda i, be, nu, nv, us: (be[cur(i, be, nu)], 0, 0))
    smem = lambda imap: pl.BlockSpec((1, 1, per_block), imap, memory_space=pltpu.SMEM)
    return pl.pallas_call(
        _ffn_kernel,
        grid_spec=pltpu.PrefetchScalarGridSpec(
            num_scalar_prefetch=4,
            grid=(nb,),
            in_specs=[smem(lambda i, be, nu, nv, us: (i, 0, 0)),
                      smem(lambda i, be, nu, nv, us: (jnp.minimum(i + 1, nb - 1), 0, 0)),
                      smem(lambda i, be, nu, nv, us: (i, 0, 0)),
                      pl.BlockSpec(memory_space=pl.ANY),
                      wspec((1, d, de)), wspec((1, d, de)), wspec((1, de, d))],
            out_specs=pl.BlockSpec(memory_space=pl.ANY),
            scratch_shapes=[pltpu.VMEM((2, EXPERT_ROWS, sw), U32), pltpu.VMEM((2, EXPERT_ROWS, d // 2), U32),
                            pltpu.VMEM((CHUNK, d // 2), U32),
                            pltpu.VMEM((d, de), BF16), pltpu.VMEM((d, de), BF16), pltpu.VMEM((de, d), BF16),
                            pltpu.SemaphoreType.DMA((2,)), pltpu.SemaphoreType.DMA((2,)),
                            pltpu.SemaphoreType.DMA(())]),
        out_shape=jax.ShapeDtypeStruct((stage.shape[0], d // 2), U32),
        compiler_params=_cparams(1, VMEM_LIMIT),
        name="expert_ffn",
    )(block_e, n_used, n_valid, used, src_chunk, src_chunk, dst_chunk, stage, wg, wu, wd)


def _combine_kernel(ost_ref, route_ref, x1_ref, fg_ref, o_ref):
    col = lax.broadcasted_iota(I32, (MOE_TILE, STAGE_ROWS), 1)
    for sub in range(route_ref.shape[0] // MOE_TILE):
        rows = slice(sub * MOE_TILE, (sub + 1) * MOE_TILE)
        route = route_ref[rows, :]
        sel = jnp.logical_or(col == route[:, 4:5].astype(I32), col == route[:, 5:6].astype(I32))
        perm = jnp.where(sel, 1.0, 0.0).astype(BF16)
        ost = ost_ref[sub * STAGE_ROWS:(sub + 1) * STAGE_ROWS, :]
        y = jnp.dot(perm, jnp.concatenate(_unpack(ost), axis=1), preferred_element_type=F32)
        x2 = x1_ref[rows, :] + y
        ms = jnp.mean(x2 * x2, axis=-1, keepdims=True)
        o_ref[rows, :] = x2 * lax.rsqrt(ms + EPS) * fg_ref[...]


def _combine(ostage, route, x1, final_g):
    t, d = x1.shape
    return pl.pallas_call(
        _combine_kernel,
        grid=(t // ROW_TILE,),
        in_specs=[pl.BlockSpec((ROW_TILE // MOE_TILE * STAGE_ROWS, d // 2), lambda i: (i, 0)),
                  pl.BlockSpec((ROW_TILE, ROUTE_LANES), lambda i: (i, 0)),
                  pl.BlockSpec((ROW_TILE, d), lambda i: (i, 0)),
                  pl.BlockSpec((1, d), lambda i: (0, 0))],
        out_specs=pl.BlockSpec((ROW_TILE, d), lambda i: (i, 0)),
        out_shape=jax.ShapeDtypeStruct((t, d), F32),
        compiler_params=_cparams(1, VMEM_LIMIT),
        name="moe_combine",
    )(ostage, route, x1, final_g)


def _hier_moe_and_norm(x1, stage, route, tile_cnt, wg, wu, wd, final_g):
    t, d = x1.shape
    ostage = _expert_ffn(stage, _moe_plan(tile_cnt, t), wg, wu, wd)
    return _combine(ostage, route, x1, final_g)


def _encoder(x, prm, kf, tables, nf):
    b, l, d = x.shape
    t = b * l
    d_hy = prm["skip"].shape[1]
    x2 = x.reshape(t, d)
    gates5, gm = _inproj(x2, prm["mix_norm_g"], prm["w_in"], prm["short_w"], prm["short_b"], b, l, nf,
                         (HY_ORDER + 1) * d_hy)
    z5 = gates5
    z_blk = HY_ORDER * d_hy // CONV_LANES
    for o in range(HY_ORDER):
        z5 = _longconv(gates5, z5, kf, tables, d_hy, gate_blk=o * d_hy // CONV_LANES,
                       z_blk=z_blk, kf_col=o * d_hy // CONV_LANES, nf=nf, out_bmajor=o < HY_ORDER - 1)
        z_blk = 0
    y_hy = z5.reshape(b, d_hy // LANES, l, LANES)
    y_gm = _gmlp(gm, prm["ln_g"], prm["ln_b"], prm["ws_stack"], prm["bias_t"])
    x1, stage, route, tile_cnt = _outproj(x2, y_hy, y_gm, prm["mix_out_g"], prm["bd"], prm["w_out"], prm["ffn_norm_g"],
                             prm["wr_cat"], prm["br"])
    out = _hier_moe_and_norm(x1, stage, route, tile_cnt, prm["w_e_gate"], prm["w_e_up"], prm["w_e_down"],
                             prm["final_norm_g"])
    return out.reshape(b, l, d)


def kernel(x_prompt, x_sample, mix_norm_g, w_in, hy_short_w, hy_short_b, hy_filt_w_emb, hy_filt_b_emb,
           hy_filt_w_inner, hy_filt_b_inner, hy_filt_freq, hy_filt_w_out, hy_skip, gm_ln_g, gm_ln_b,
           gm_w_s, gm_b_s, mix_out_g, w_out, ffn_norm_g, w_group, b_group, w_expert_router,
           b_expert_router, w_e_gate, w_e_up, w_e_down, final_norm_g):
    assert w_in.shape[0] == 1, "one layer"
    l = x_prompt.shape[1]
    assert x_sample.shape[1] == l
    nf = math.isqrt(2 * l)
    assert nf * nf == 2 * l and nf % SLAB == 0
    d = x_prompt.shape[2]
    d_hy = hy_skip.shape[2]
    d_gm = gm_ln_g.shape[1]
    head_dim = d_gm // GM_HEADS
    assert d_hy // HY_HEADS == head_dim and d_hy == d_gm
    assert x_prompt.shape[0] % 2 == 0 and x_sample.shape[0] % 2 == 0, "sequences are convolved in pairs"
    assert l % (IN_A * nf) == 0 and l % ROW_TILE == 0 and l % GM_TILE == 0 and ROW_TILE % MOE_TILE == 0
    assert d_hy % CONV_LANES == 0 and gm_w_s.shape[2] == GM_CHUNK and w_e_gate.shape[1] == N_EXPERTS

    tables = _dft_tables(nf)
    f1, f2, f2i, f3, f1_full = tables
    tables_bf = tuple(a.astype(BF16) for a in (f1, f2, f2i, f3))

    max_decay = math.log(DECAY_TARGET) / FAST_DECAY_PCT
    min_decay = math.log(DECAY_TARGET) / SLOW_DECAY_PCT
    deltas = jnp.abs(jnp.linspace(min_decay, max_decay, d_hy, dtype=F32))[None, :]
    taps = _filter_taps(hy_filt_w_emb[0], hy_filt_b_emb[0][None, :], hy_filt_w_inner[0], hy_filt_b_inner[0],
                        hy_filt_freq[0][None, :], hy_filt_w_out[0], deltas, l, d_hy)
    kf = _filter_spectrum(taps, hy_skip[0].reshape(1, HY_ORDER * d_hy), f1_full, f2, nf)

    n_route = N_GROUPS + N_EXPERTS
    wr = jnp.zeros((d, ROUTE_LANES), F32).at[:, :n_route].set(
        jnp.concatenate([w_group[0], w_expert_router[0]], axis=1))
    wr_hi = wr.astype(BF16)
    wr_lo = (wr - wr_hi.astype(F32)).astype(BF16)
    br = jnp.zeros((1, ROUTE_LANES), F32).at[0, :n_route].set(jnp.concatenate([b_group[0], b_expert_router[0]]))
    hid = jnp.arange(d_hy, dtype=I32) // head_dim
    bd = jnp.where(hid[:, None] == hid[None, :], 1.0 / head_dim, 0.0).astype(BF16)

    prm = dict(
        mix_norm_g=mix_norm_g, w_in=w_in[0].astype(BF16), short_w=hy_short_w[0], short_b=hy_short_b,
        skip=hy_skip[0], ln_g=gm_ln_g, ln_b=gm_ln_b,
        ws_stack=gm_w_s[0].reshape(GM_HEADS * GM_CHUNK, GM_CHUNK).astype(BF16),
        bias_t=jnp.repeat(gm_b_s[0].T, head_dim, axis=1),
        mix_out_g=mix_out_g, bd=bd, w_out=w_out[0].astype(BF16), ffn_norm_g=ffn_norm_g,
        wr_cat=jnp.concatenate([wr_hi, wr_lo], axis=1), br=br,
        w_e_gate=w_e_gate[0], w_e_up=w_e_up[0], w_e_down=w_e_down[0],
        final_norm_g=final_norm_g[None, :])
    y_prompt = _encoder(x_prompt, prm, kf, tables_bf, nf)
    y_sample = _encoder(x_sample, prm, kf, tables_bf, nf)
    return (y_prompt, y_sample)
```

```python
import functools
import math

import jax
import jax.numpy as jnp
from jax import lax
from jax.experimental import pallas as pl
from jax.experimental.pallas import tpu as pltpu

F32 = jnp.float32
BF16 = jnp.bfloat16
U32 = jnp.uint32
I32 = jnp.int32

EPS = 1e-6
HY_ORDER = 2
HY_HEADS = 8
GM_HEADS = 8
GM_CHUNK = 128
N_GROUPS = 4
EXPERTS_PER_GROUP = 8
N_EXPERTS = N_GROUPS * EXPERTS_PER_GROUP
TOP_K = 2
FILTER_EMB = 5
DECAY_TARGET = 1e-2
FAST_DECAY_PCT = 0.3
SLOW_DECAY_PCT = 1.5

LANES = 128
ROUTE_LANES = LANES
SLAB = 16
G_PITCH = SLAB + 8
CONV_LANES = 256
PHASE2_UNROLL = 16
ROW_TILE = 512
GM_TILE = 1024
IN_A = 8
IN_SLABS = 4
EXPERT_ROWS = 1024
CHUNK = 8
MOE_TILE = 512
STAGE_ROWS = MOE_TILE * TOP_K + N_EXPERTS * CHUNK
VMEM_LIMIT = 56 * 1024 * 1024


def _cparams(n_axes, vmem=None):
    return pltpu.CompilerParams(dimension_semantics=("arbitrary",) * n_axes,
                                vmem_limit_bytes=vmem)


def _inproj_kernel(x_ref, xp_ref, xn_ref, g_ref, w_ref, sw_ref, sb_ref, hy_ref, gm_ref, stage_ref,
                   *, nf, d_hy3, tiles_per_seq):
    i = pl.program_id(0)
    rows = x_ref.shape[0]
    pitch = stage_ref.shape[1] // IN_A
    halo = xp_ref.shape[0]
    x = jnp.concatenate([xp_ref[...], x_ref[...], xn_ref[...]], axis=0)
    ms = jnp.mean(x * x, axis=-1, keepdims=True)
    n = (x * lax.rsqrt(ms + EPS) * g_ref[...]).astype(BF16)
    gm = jnp.dot(n, w_ref[:, d_hy3:], preferred_element_type=F32)
    gm_ref[...] = gm[halo:halo + rows].astype(BF16)

    pos = i % tiles_per_seq
    has_prev = jnp.where(pos == 0, 0.0, 1.0)
    has_next = jnp.where(pos == tiles_per_seq - 1, 0.0, 1.0)
    row = lax.broadcasted_iota(I32, (rows, 1), 0)
    prev_w = jnp.where(row == 0, has_prev, 1.0)
    next_w = jnp.where(row == rows - 1, has_next, 1.0)
    n_slabs = stage_ref.shape[0]
    cols = n_slabs * LANES
    ext = rows + 2 * halo
    for c0 in range(0, d_hy3, cols):
        p = jnp.dot(n, w_ref[:, c0:c0 + cols], preferred_element_type=F32)
        prev = pltpu.roll(p, 1, 0)[halo:halo + rows] * prev_w
        nxt = pltpu.roll(p, ext - 1, 0)[halo:halo + rows] * next_w
        res = (sw_ref[0:1, c0:c0 + cols] * prev + sw_ref[1:2, c0:c0 + cols] * p[halo:halo + rows]
               + sw_ref[2:3, c0:c0 + cols] * nxt + sb_ref[:, c0:c0 + cols])
        for s in range(n_slabs):
            for al in range(IN_A):
                stage_ref[s, al * pitch:al * pitch + nf, :] = res[al * nf:(al + 1) * nf,
                                                                s * LANES:(s + 1) * LANES]
        for s in range(n_slabs):
            for b in range(nf):
                hy_ref[0, c0 // LANES + s, b] = stage_ref[s, pl.ds(b, IN_A, stride=pitch), :]


def _inproj(x2, g, w_bf, short_w, short_b, n_seq, seq_len, nf, d_hy3):
    t, d = x2.shape
    dp = w_bf.shape[1]
    rows = IN_A * nf
    tiles_per_seq = seq_len // rows
    hb = rows // 8
    nhb = t // 8
    full = lambda a: pl.BlockSpec(a.shape, lambda i: (0,) * a.ndim)
    return pl.pallas_call(
        functools.partial(_inproj_kernel, nf=nf, d_hy3=d_hy3, tiles_per_seq=tiles_per_seq),
        grid=(t // rows,),
        in_specs=[pl.BlockSpec((rows, d), lambda i: (i, 0)),
                  pl.BlockSpec((8, d), lambda i: (jnp.maximum(i * hb - 1, 0), 0)),
                  pl.BlockSpec((8, d), lambda i: (jnp.minimum((i + 1) * hb, nhb - 1), 0)),
                  full(g), full(w_bf), full(short_w), full(short_b)],
        out_specs=[pl.BlockSpec((1, d_hy3 // LANES, nf, IN_A, LANES),
                                lambda i: (i // tiles_per_seq, 0, 0, i % tiles_per_seq, 0)),
                   pl.BlockSpec((rows, dp - d_hy3), lambda i: (i, 0))],
        out_shape=[jax.ShapeDtypeStruct((n_seq, d_hy3 // LANES, nf, nf // 2, LANES), F32),
                   jax.ShapeDtypeStruct((t, dp - d_hy3), BF16)],
        scratch_shapes=[pltpu.VMEM((IN_SLABS, IN_A * (nf + 8), LANES), F32)],
        compiler_params=_cparams(1, VMEM_LIMIT),
        name="inproj_shortconv",
    )(x2, x2, x2, g, w_bf, short_w, short_b)


def _cos_sin(m, period):
    ang = m.astype(F32) * (2.0 * math.pi / period)
    return jnp.cos(ang), jnp.sin(ang)


def _stack_complex(mr, mi):
    top = jnp.concatenate([mr, -mi], axis=-1)
    bot = jnp.concatenate([mi, mr], axis=-1)
    return jnp.concatenate([top, bot], axis=-2)


def _dft_tables(nf):
    na = nf // 2
    n = nf * nf
    idx = jnp.arange(nf, dtype=I32)
    c, s = _cos_sin((idx[:, None] * idx[None, :]) % nf, nf)
    f2 = _stack_complex(c, -s)
    f2i = _stack_complex(c, s)
    b_ = idx[:, None, None]
    ka = idx[None, :, None]
    a_ = idx[None, None, :]
    m1 = (nf * a_ * ka + b_ * ka) % n
    c1, s1 = _cos_sin(m1, n)
    f1 = _stack_complex(c1[:, :, :na], -s1[:, :, :na])
    f1_full = jnp.concatenate([c1, -s1], axis=1)
    c3 = jnp.swapaxes(c1, 1, 2)[:, :na, :]
    s3 = jnp.swapaxes(s1, 1, 2)[:, :na, :]
    f3 = _stack_complex(c3, s3)
    return f1, f2, f2i, f3, f1_full


def _split_bf16(x):
    hi = x.astype(BF16)
    lo = (x - hi.astype(F32)).astype(BF16)
    return hi, lo


def _stack_split_lhs(a):
    hi, lo = _split_bf16(a)
    return jnp.concatenate([hi, lo, hi], axis=-1)


def _stack_split_rhs(b):
    hi, lo = _split_bf16(b)
    return jnp.concatenate([hi, hi, lo], axis=-2)


def _filter_taps_kernel(wemb_ref, bemb_ref, win_ref, bin_ref, freq_ref, wout_ref, delta_ref, o_ref,
                        *, seq_len, n_inner, d_hy):
    rows = o_ref.shape[1]
    j = pl.program_id(0) * rows + lax.broadcasted_iota(I32, (rows, 1), 0)
    tidx = jnp.where(j < seq_len, j, 2 * seq_len - j).astype(F32)
    t = tidx / float(seq_len - 1)
    fr0 = jnp.float32(1e-4)
    ang0 = (2.0 * math.pi / seq_len) * tidx * fr0
    ang1 = (2.0 * math.pi / seq_len) * tidx
    freq = freq_ref[...]
    pre = (t * wemb_ref[0:1, :] + jnp.cos(ang0) * wemb_ref[1:2, :] + jnp.cos(ang1) * wemb_ref[2:3, :]
           - jnp.sin(ang0) * wemb_ref[3:4, :] - jnp.sin(ang1) * wemb_ref[4:5, :] + bemb_ref[...])
    hdn = jnp.sin(freq * pre)
    for i in range(n_inner):
        hdn = jnp.sin(freq * (jnp.dot(_stack_split_lhs(hdn), win_ref[i], preferred_element_type=F32)
                              + bin_ref[i:i + 1, :]))
    h = jnp.dot(_stack_split_lhs(hdn), wout_ref[...], preferred_element_type=F32)
    decay = jnp.exp(-t * delta_ref[...])
    for o in range(HY_ORDER):
        fwd = h[:, (2 * o) * d_hy:(2 * o + 1) * d_hy]
        bwd = h[:, (2 * o + 1) * d_hy:(2 * o + 2) * d_hy]
        val = jnp.where(j == seq_len, 0.0, jnp.where(j < seq_len, fwd, bwd) * decay)
        for cc in range(d_hy // LANES):
            o_ref[o * (d_hy // LANES) + cc] = val[:, cc * LANES:(cc + 1) * LANES]


def _filter_taps(wemb, bemb, win, bin_, freq, wout, deltas, seq_len, d_hy):
    rows = 1024
    n2 = 2 * seq_len
    width = wemb.shape[1]
    n_inner = win.shape[0]
    full = lambda *shape: pl.BlockSpec(shape, lambda i: (0,) * len(shape))
    return pl.pallas_call(
        functools.partial(_filter_taps_kernel, seq_len=seq_len, n_inner=n_inner, d_hy=d_hy),
        grid=(n2 // rows,),
        in_specs=[full(FILTER_EMB, width), full(1, width), full(n_inner, 3 * width, width),
                  full(n_inner, width), full(1, width), full(3 * width, 2 * HY_ORDER * d_hy),
                  full(1, d_hy)],
        out_specs=pl.BlockSpec((HY_ORDER * d_hy // LANES, rows, LANES), lambda i: (0, i, 0)),
        out_shape=jax.ShapeDtypeStruct((HY_ORDER * d_hy // LANES, n2, LANES), F32),
        compiler_params=_cparams(1, VMEM_LIMIT),
        name="filter_taps",
    )(wemb, bemb, _stack_split_rhs(win), bin_, freq, _stack_split_rhs(wout), deltas)


def _filter_fft1_kernel(k_ref, f_ref, o_ref, *, nf):
    halves = k_ref.shape[0]
    k = k_ref.reshape(halves, nf * SLAB, LANES)
    g = o_ref.reshape(halves, 2, nf * SLAB, LANES)
    for bl in range(SLAB):
        col = jnp.concatenate([k[h, pl.ds(bl, nf, stride=SLAB), :] for h in range(halves)], axis=1)
        res = jnp.dot(f_ref[bl], _stack_split_rhs(col), preferred_element_type=F32)
        for h in range(halves):
            for ri in range(2):
                g[h, ri, pl.ds(bl, nf, stride=SLAB), :] = res[ri * nf:(ri + 1) * nf,
                                                              h * LANES:(h + 1) * LANES]


def _filter_fft2_kernel(g_ref, f_ref, skip_ref, o_ref, *, nf, scale):
    halves = g_ref.shape[0]
    lag0 = jnp.concatenate([jnp.broadcast_to(skip_ref[...], (nf, skip_ref.shape[1])),
                            jnp.zeros((nf, skip_ref.shape[1]), F32)], axis=0)
    for kl in range(SLAB):
        rhs = jnp.concatenate([jnp.concatenate([g_ref[h, ri, kl] for h in range(halves)], axis=1)
                               for ri in range(2)], axis=0)
        res = jnp.dot(f_ref[...], _stack_split_rhs(rhs), preferred_element_type=F32)
        o_ref[kl] = ((res + lag0) * scale).astype(o_ref.dtype)


def _filter_spectrum(taps, skip, f1_full, f2, nf):
    nch = taps.shape[0]
    c = nch * LANES
    halves = CONV_LANES // LANES
    k4 = taps.reshape(nch, nf, nf, LANES)
    g = pl.pallas_call(
        functools.partial(_filter_fft1_kernel, nf=nf),
        grid=(nf // SLAB, nch // halves),
        in_specs=[pl.BlockSpec((halves, nf, SLAB, LANES), lambda i, j: (j, 0, i, 0)),
                  pl.BlockSpec((SLAB, 2 * nf, 3 * nf), lambda i, j: (i, 0, 0))],
        out_specs=pl.BlockSpec((halves, 2, nf, SLAB, LANES), lambda i, j: (j, 0, 0, i, 0)),
        out_shape=jax.ShapeDtypeStruct((nch, 2, nf, nf, LANES), F32),
        compiler_params=_cparams(2, VMEM_LIMIT),
        name="filter_fft1",
    )(k4, _stack_split_lhs(f1_full))
    return pl.pallas_call(
        functools.partial(_filter_fft2_kernel, nf=nf, scale=1.0 / (nf * nf)),
        grid=(nf // SLAB, nch // halves),
        in_specs=[pl.BlockSpec((halves, 2, SLAB, nf, LANES), lambda i, j: (j, 0, i, 0, 0)),
                  pl.BlockSpec((2 * nf, 6 * nf), lambda i, j: (0, 0)),
                  pl.BlockSpec((1, CONV_LANES), lambda i, j: (0, j))],
        out_specs=pl.BlockSpec((SLAB, 2 * nf, CONV_LANES), lambda i, j: (i, 0, j)),
        out_shape=jax.ShapeDtypeStruct((nf, 2 * nf, c), BF16),
        compiler_params=_cparams(2, VMEM_LIMIT),
        name="filter_fft2",
    )(g, _stack_split_lhs(f2), skip)


def _pack(re, im):
    rb = lax.bitcast_convert_type(re.astype(BF16).astype(F32), U32)
    ib = lax.bitcast_convert_type(im.astype(BF16).astype(F32), U32)
    return rb | (ib >> 16)


def _unpack(w):
    re = lax.bitcast_convert_type(w & jnp.uint32(0xFFFF0000), F32).astype(BF16)
    im = lax.bitcast_convert_type(w << 16, F32).astype(BF16)
    return re, im


def _longconv_kernel(z1_ref, f1_ref, kf_ref, f2_ref, f2i_ref, f3_ref, gate_ref, o_ref, g_ref,
                     *, nf, out_bmajor):
    na = nf // 2
    ns = nf // SLAB
    halves = g_ref.shape[1]
    t = pl.program_id(2)

    def load_rows(ref5, s, bl):
        return jnp.concatenate([ref5[s, h, bl] for h in range(halves)], axis=1)

    def store_col(blk, bl, val):
        for h in range(halves):
            g_ref[blk, h, pl.ds(bl, nf, stride=G_PITCH), :] = val[:, h * LANES:(h + 1) * LANES]

    def load_col(blk, bl):
        return jnp.concatenate([g_ref[blk, h, pl.ds(bl, nf, stride=G_PITCH), :] for h in range(halves)],
                               axis=1)

    @pl.when(t < ns)
    def _phase1():
        for bl in range(SLAB):
            rhs = jnp.concatenate([load_rows(z1_ref, 0, bl), load_rows(z1_ref, 1, bl)],
                                  axis=0).astype(BF16)
            out = jnp.dot(f1_ref[bl], rhs, preferred_element_type=F32)
            store_col(t, bl, _pack(out[:nf], out[nf:]))

    @pl.when(jnp.logical_and(t >= ns, t < 2 * ns))
    def _phase2():
        i = t - ns

        def load_slab(ka):
            row0 = pl.multiple_of(ka * G_PITCH, CHUNK)
            return jnp.concatenate([g_ref[:, h, pl.ds(row0, SLAB), :].reshape(nf, LANES)
                                    for h in range(halves)], axis=1)

        def transform(w, kl):
            re, im = _unpack(w)
            s = jnp.dot(f2_ref[...], jnp.concatenate([re, im], axis=0),
                        preferred_element_type=F32)
            sr, si = s[:nf], s[nf:]
            kr, ki = kf_ref[kl, :nf, :], kf_ref[kl, nf:, :]
            pr = (sr * kr - si * ki).astype(BF16)
            pi = (sr * ki + si * kr).astype(BF16)
            h = jnp.dot(f2i_ref[...], jnp.concatenate([pr, pi], axis=0),
                        preferred_element_type=F32)
            return _pack(h[:nf], h[nf:])

        def store_slab(ka, packed):
            row0 = pl.multiple_of(ka * G_PITCH, CHUNK)
            for hf in range(halves):
                g_ref[:, hf, pl.ds(row0, SLAB), :] = packed[:, hf * LANES:(hf + 1) * LANES].reshape(
                    ns, SLAB, LANES)

        def body(grp, carry):
            kls = [grp * PHASE2_UNROLL + u for u in range(PHASE2_UNROLL)]
            ws = [load_slab(i * SLAB + kl) for kl in kls]
            outs = [transform(w, kl) for w, kl in zip(ws, kls)]
            for kl, packed in zip(kls, outs):
                store_slab(i * SLAB + kl, packed)
            return carry

        lax.fori_loop(0, SLAB // PHASE2_UNROLL, body, 0)

    @pl.when(t >= 2 * ns)
    def _phase3():
        j = t - 2 * ns
        out_flat = None if out_bmajor else o_ref.reshape(2, halves, na * SLAB, LANES)
        for bl in range(SLAB):
            re, im = _unpack(load_col(j, bl))
            y = jnp.dot(f3_ref[bl], jnp.concatenate([re, im], axis=0),
                        preferred_element_type=F32)
            for s in range(2):
                val = load_rows(gate_ref, s, bl) * y[s * na:(s + 1) * na]
                for h in range(halves):
                    piece = val[:, h * LANES:(h + 1) * LANES]
                    if out_bmajor:
                        o_ref[s, h, bl] = piece.astype(o_ref.dtype)
                    else:
                        out_flat[s, h, pl.ds(bl, na, stride=SLAB), :] = piece


def _longconv(gates5, z5, kf, tables, c, gate_blk, z_blk, kf_col, nf, out_bmajor):
    f1, f2, f2i, f3 = tables
    b, _, _, na, _ = z5.shape
    ns = nf // SLAB
    halves = CONV_LANES // LANES
    nchunk = c // CONV_LANES
    p1 = lambda t: jnp.minimum(t, ns - 1)
    p2 = lambda t: jnp.clip(t - ns, 0, ns - 1)
    p3 = lambda t: jnp.clip(t - 2 * ns, 0, ns - 1)
    cl = CONV_LANES
    blk5 = (2, halves, SLAB, na, LANES)
    if out_bmajor:
        out_spec = pl.BlockSpec(blk5, lambda p, q, t: (p, q, p3(t), 0, 0))
        out_shape = (b, c // LANES, nf, na, LANES)
    else:
        out_spec = pl.BlockSpec((2, halves, na, SLAB, LANES), lambda p, q, t: (p, q, 0, p3(t), 0))
        out_shape = (b, c // LANES, na, nf, LANES)
    return pl.pallas_call(
        functools.partial(_longconv_kernel, nf=nf, out_bmajor=out_bmajor),
        grid=(b // 2, nchunk, 3 * ns),
        in_specs=[
            pl.BlockSpec(blk5, lambda p, q, t: (p, z_blk + q, p1(t), 0, 0)),
            pl.BlockSpec((SLAB, 2 * nf, 2 * na), lambda p, q, t: (p1(t), 0, 0)),
            pl.BlockSpec((SLAB, 2 * nf, cl), lambda p, q, t: (p2(t), 0, kf_col + q)),
            pl.BlockSpec((2 * nf, 2 * nf), lambda p, q, t: (0, 0)),
            pl.BlockSpec((2 * nf, 2 * nf), lambda p, q, t: (0, 0)),
            pl.BlockSpec((SLAB, 2 * na, 2 * nf), lambda p, q, t: (p3(t), 0, 0)),
            pl.BlockSpec(blk5, lambda p, q, t: (p, gate_blk + q, p3(t), 0, 0)),
        ],
        out_specs=out_spec,
        out_shape=jax.ShapeDtypeStruct(out_shape, BF16 if out_bmajor else F32),
        scratch_shapes=[pltpu.VMEM((ns, halves, nf * G_PITCH, LANES), U32)],
        compiler_params=_cparams(3, VMEM_LIMIT),
        name="longconv",
    )(z5, f1, kf, f2, f2i, f3, gates5)


def _gelu_exact(x):
    return 0.5 * x * (1.0 + lax.erf(x * (1.0 / math.sqrt(2.0))))


def _gmlp_kernel(gm_ref, lng_ref, lnb_ref, ws_ref, bias_ref, o_ref, *, d_gm):
    g = _gelu_exact(gm_ref[...].astype(F32))
    u, v = g[:, :d_gm], g[:, d_gm:]
    mu = jnp.mean(v, axis=-1, keepdims=True)
    vc = v - mu
    var = jnp.mean(vc * vc, axis=-1, keepdims=True)
    vh = (vc * lax.rsqrt(var + EPS) * lng_ref[...] + lnb_ref[...]).astype(BF16)
    head_dim = d_gm // GM_HEADS
    per_group = CONV_LANES // head_dim
    head = lax.broadcasted_iota(I32, (GM_CHUNK, CONV_LANES), 1) // head_dim
    for c in range(gm_ref.shape[0] // GM_CHUNK):
        rows = slice(c * GM_CHUNK, (c + 1) * GM_CHUNK)
        groups = []
        for grp in range(GM_HEADS // per_group):
            lanes = slice(grp * CONV_LANES, (grp + 1) * CONV_LANES)
            w = ws_ref[grp * per_group * GM_CHUNK:(grp + 1) * per_group * GM_CHUNK, :]
            r = jnp.dot(w, vh[rows, lanes], preferred_element_type=F32)
            s = r[:GM_CHUNK]
            for h in range(1, per_group):
                s = jnp.where(head == h, r[h * GM_CHUNK:(h + 1) * GM_CHUNK], s)
            groups.append(s)
        o_ref[rows, :] = u[rows] * (jnp.concatenate(groups, axis=1) + bias_ref[...])


def _gmlp(gm, ln_g, ln_b, ws_stack, bias_t):
    t, c2 = gm.shape
    d_gm = c2 // 2
    return pl.pallas_call(
        functools.partial(_gmlp_kernel, d_gm=d_gm),
        grid=(t // GM_TILE,),
        in_specs=[pl.BlockSpec((GM_TILE, c2), lambda i: (i, 0)),
                  pl.BlockSpec((1, d_gm), lambda i: (0, 0)),
                  pl.BlockSpec((1, d_gm), lambda i: (0, 0)),
                  pl.BlockSpec(ws_stack.shape, lambda i: (0, 0)),
                  pl.BlockSpec(bias_t.shape, lambda i: (0, 0))],
        out_specs=pl.BlockSpec((GM_TILE, d_gm), lambda i: (i, 0)),
        out_shape=jax.ShapeDtypeStruct((t, d_gm), F32),
        compiler_params=_cparams(1, VMEM_LIMIT),
        name="gmlp",
    )(gm, ln_g, ln_b, ws_stack, bias_t)


def _head_rms(y, bd):
    ms = jnp.dot((y * y).astype(BF16), bd, preferred_element_type=F32)
    return y * lax.rsqrt(ms + EPS)


def _route(logits):
    lane = lax.broadcasted_iota(I32, logits.shape, 1)
    neg = jnp.float32(-1e30)
    big = jnp.int32(ROUTE_LANES)
    gmask = lane < N_GROUPS
    gl = jnp.where(gmask, logits, neg)
    gmax = jnp.max(gl, axis=-1, keepdims=True)
    grp = jnp.min(jnp.where(jnp.logical_and(gl == gmax, gmask), lane, big), axis=-1, keepdims=True)
    psum = jnp.sum(jnp.where(gmask, jnp.exp(gl - gmax), 0.0), axis=-1, keepdims=True)
    p_grp = 1.0 / psum
    lo = N_GROUPS + EXPERTS_PER_GROUP * grp
    emask = jnp.logical_and(lane >= lo, lane < lo + EXPERTS_PER_GROUP)
    el = jnp.where(emask, logits, neg)
    m1 = jnp.max(el, axis=-1, keepdims=True)
    i1 = jnp.min(jnp.where(jnp.logical_and(el == m1, emask), lane, big), axis=-1, keepdims=True)
    emask2 = jnp.logical_and(emask, lane != i1)
    el2 = jnp.where(emask2, logits, neg)
    m2 = jnp.max(el2, axis=-1, keepdims=True)
    i2 = jnp.min(jnp.where(jnp.logical_and(el2 == m2, emask2), lane, big), axis=-1, keepdims=True)
    d = jnp.exp(m2 - m1)
    g1 = p_grp * (1.0 / (1.0 + d))
    g2 = p_grp * (d / (1.0 + d))
    e1 = (i1 - N_GROUPS).astype(F32)
    e2 = (i2 - N_GROUPS).astype(F32)
    return jnp.where(lane == 0, e1, jnp.where(lane == 1, e2, jnp.where(lane == 2, g1,
                     jnp.where(lane == 3, g2, 0.0))))


def _lane_cumsum(v):
    lane = lax.broadcasted_iota(I32, v.shape, 1)
    sh = 1
    while sh < v.shape[1]:
        v = v + jnp.where(lane >= sh, pltpu.roll(v, sh, 1), 0.0)
        sh *= 2
    return v


def _outproj_kernel(x_ref, yh_ref, yg_ref, mg_ref, bd_ref, wo_ref, fg_ref, wr_ref, br_ref,
                    x1_ref, stage_ref, route_ref, count_ref, *, d_hy):
    bd = bd_ref[...]
    mg = mg_ref[...]
    wo = wo_ref[...]
    wr = wr_ref[...]

    yh = jnp.concatenate([yh_ref[0, c] for c in range(yh_ref.shape[1])], axis=1)
    mh = (_head_rms(yh, bd) * mg[:, :d_hy]).astype(BF16)
    mgm = (_head_rms(yg_ref[...], bd) * mg[:, d_hy:]).astype(BF16)
    x1 = (x_ref[...] + jnp.dot(mh, wo[:d_hy], preferred_element_type=F32)
          + jnp.dot(mgm, wo[d_hy:], preferred_element_type=F32))
    x1_ref[...] = x1
    ms = jnp.mean(x1 * x1, axis=-1, keepdims=True)
    hi, lo = _split_bf16(x1 * lax.rsqrt(ms + EPS) * fg_ref[...])
    hw = jnp.dot(hi, wr, preferred_element_type=F32)
    logits = (hw[:, :ROUTE_LANES] + hw[:, ROUTE_LANES:]
              + jnp.dot(lo, wr[:, :ROUTE_LANES], preferred_element_type=F32) + br_ref[...])
    route = _route(logits)

    lane = lax.broadcasted_iota(I32, (MOE_TILE, ROUTE_LANES), 1)
    r = lax.broadcasted_iota(I32, (MOE_TILE, MOE_TILE), 0)
    c = lax.broadcasted_iota(I32, (MOE_TILE, MOE_TILE), 1)
    ltri = jnp.where(c < r, 1.0, 0.0).astype(BF16)
    srow = lax.broadcasted_iota(I32, (STAGE_ROWS, MOE_TILE), 0)
    mlane = lax.broadcasted_iota(I32, (STAGE_ROWS, LANES), 1)
    half = hi.shape[1] // 2
    subs = route.shape[0] // MOE_TILE
    for sub in range(subs):
        rows = slice(sub * MOE_TILE, (sub + 1) * MOE_TILE)
        rsub = route[rows]
        oh0 = lane == rsub[:, 0:1].astype(I32)
        oh1 = lane == rsub[:, 1:2].astype(I32)
        oh = jnp.where(jnp.logical_or(oh0, oh1), 1.0, 0.0)
        before = jnp.dot(ltri, oh.astype(BF16), preferred_element_type=F32)
        cnt = jnp.sum(oh, axis=0, keepdims=True)
        run = jnp.floor((cnt + (CHUNK - 1)) * (1.0 / CHUNK)) * CHUNK
        run_start = _lane_cumsum(jnp.broadcast_to(run, (8, run.shape[1])))[0:1] - run
        tot = before + run_start
        loc0 = jnp.sum(jnp.where(oh0, tot, 0.0), axis=-1, keepdims=True)
        loc1 = jnp.sum(jnp.where(oh1, tot, 0.0), axis=-1, keepdims=True)
        rsub = jnp.where(lane == 4, loc0, jnp.where(lane == 5, loc1, rsub))
        route_ref[rows, :] = rsub
        count_ref[pl.ds(pl.program_id(0) * subs + sub, 1), :] = cnt

        rt = jnp.transpose(rsub)
        p0 = srow == rt[4:5, :].astype(I32)
        p1 = srow == rt[5:6, :].astype(I32)
        perm = jnp.where(jnp.logical_or(p0, p1), 1.0, 0.0).astype(BF16)
        staged = jnp.dot(perm, hi[rows], preferred_element_type=F32)
        gate = jnp.sum(jnp.where(p0, rt[2:3, :], 0.0) + jnp.where(p1, rt[3:4, :], 0.0), axis=-1,
                       keepdims=True)
        meta = jnp.where(mlane == 0,
                         lax.bitcast_convert_type(jnp.broadcast_to(gate, (STAGE_ROWS, LANES)), U32),
                         jnp.uint32(0))
        stage_ref[sub * STAGE_ROWS:(sub + 1) * STAGE_ROWS, :] = jnp.concatenate(
            [_pack(staged[:, :half], staged[:, half:]), meta], axis=1)


def _outproj(x2, yh, yg, mix_g, bd, wo_bf, ffn_g, wr_cat, br):
    t, d = x2.shape
    _, nch, seq_len, _ = yh.shape
    d_hy = nch * LANES
    tiles_per_seq = seq_len // ROW_TILE
    n_tiles = t // ROW_TILE
    row = lambda w: pl.BlockSpec((ROW_TILE, w), lambda i: (i, 0))
    full = lambda a: pl.BlockSpec(a.shape, lambda i: (0,) * a.ndim)
    sw = d // 2 + LANES
    subs = ROW_TILE // MOE_TILE
    n_sub = t // MOE_TILE
    return pl.pallas_call(
        functools.partial(_outproj_kernel, d_hy=d_hy),
        grid=(n_tiles,),
        in_specs=[row(d),
                  pl.BlockSpec((1, nch, ROW_TILE, LANES),
                               lambda i: (i // tiles_per_seq, 0, i % tiles_per_seq, 0)),
                  row(yg.shape[1]), full(mix_g), full(bd), full(wo_bf), full(ffn_g),
                  full(wr_cat), full(br)],
        out_specs=[row(d), pl.BlockSpec((subs * STAGE_ROWS, sw), lambda i: (i, 0)), row(ROUTE_LANES),
                   pl.BlockSpec((n_sub, ROUTE_LANES), lambda i: (0, 0))],
        out_shape=[jax.ShapeDtypeStruct((t, d), F32),
                   jax.ShapeDtypeStruct((n_sub * STAGE_ROWS, sw), U32),
                   jax.ShapeDtypeStruct((t, ROUTE_LANES), F32),
                   jax.ShapeDtypeStruct((n_sub, ROUTE_LANES), F32)],
        compiler_params=_cparams(1, VMEM_LIMIT),
        name="outproj_router",
    )(x2, yh, yg, mix_g, bd, wo_bf, ffn_g, wr_cat, br)


def _moe_plan(tile_cnt, n_tokens):
    n_tiles = tile_cnt.shape[0]
    cnt = tile_cnt[:, :N_EXPERTS].astype(I32)
    run = (cnt + CHUNK - 1) // CHUNK * CHUNK
    run_start = jnp.cumsum(run, axis=1) - run
    used = jnp.sum(run, axis=1)
    e_rows = jnp.sum(run, axis=0)
    e_pad = (e_rows + EXPERT_ROWS - 1) // EXPERT_ROWS * EXPERT_ROWS
    e_end = jnp.cumsum(e_pad)
    e_start = e_end - e_pad
    n_blocks = -(-(n_tokens * TOP_K + n_tiles * N_EXPERTS * (CHUNK - 1)) // EXPERT_ROWS) + N_EXPERTS
    n_used = (e_end[-1:] // EXPERT_ROWS).astype(I32)
    block_row = jnp.arange(n_blocks, dtype=I32) * EXPERT_ROWS
    block_e = jnp.minimum(jnp.sum((e_end[None, :] <= block_row[:, None]).astype(I32), axis=1),
                          N_EXPERTS - 1)
    run_end_in_e = jnp.cumsum(run, axis=0)
    chunk_row = jnp.arange(n_blocks * EXPERT_ROWS // CHUNK, dtype=I32) * CHUNK
    e_of_chunk = jnp.repeat(block_e, EXPERT_ROWS // CHUNK)
    e_sel = jnp.arange(N_EXPERTS, dtype=I32)[None, :] == e_of_chunk[:, None]
    pick = lambda tbl: jnp.dot(e_sel.astype(F32), tbl.astype(F32),
                               precision=lax.Precision.HIGHEST).astype(I32)
    q = chunk_row - pick(e_start[:, None])[:, 0]
    valid = q < pick(e_rows[:, None])[:, 0]
    ends = pick(run_end_in_e.T)
    tile = jnp.minimum(jnp.sum((ends <= q[:, None]).astype(I32), axis=1), n_tiles - 1)
    t_sel = jnp.arange(n_tiles, dtype=I32)[None, :] == tile[:, None]
    at_tile = lambda tbl: jnp.sum(jnp.where(t_sel, pick(tbl.T), 0), axis=1)
    within = q - (at_tile(run_end_in_e) - at_tile(run))
    src_row = tile * STAGE_ROWS + at_tile(run_start) + within
    zero_chunk = STAGE_ROWS // CHUNK - 1
    per_block = EXPERT_ROWS // CHUNK
    src_chunk = jnp.where(valid, src_row // CHUNK, zero_chunk).reshape(n_blocks, 1, per_block)
    dst_chunk = jnp.where(valid, src_row // CHUNK, 0).reshape(n_blocks, 1, per_block)
    n_valid = jnp.sum(valid.reshape(n_blocks, per_block).astype(I32), axis=1)
    return (block_e.astype(I32), n_used, n_valid.astype(I32), src_chunk.astype(I32),
            dst_chunk.astype(I32), used.astype(I32))


def _ffn_kernel(be_ref, nu_ref, nv_ref, used_ref, src_now_ref, src_next_ref, dst_ref, stage_ref,
                wg_ref, wu_ref, wd_ref, ost_ref, xbuf, obuf, zero_ref, wgb, wub, wdb,
                in_sem, out_sem, zero_sem):
    b = pl.program_id(0)
    nu = nu_ref[0]
    per_block = dst_ref.shape[-1]
    slot = b % 2

    def fetch(map_ref, s):
        def body(j, carry):
            row = pl.multiple_of(map_ref[0, 0, j] * CHUNK, CHUNK)
            pltpu.make_async_copy(stage_ref.at[pl.ds(row, CHUNK)], xbuf.at[s, pl.ds(j * CHUNK, CHUNK)],
                                  in_sem.at[s]).start()
            return carry
        lax.fori_loop(0, per_block, body, 0, unroll=8)

    def out_copy(j, chunk, s):
        row = pl.multiple_of(chunk * CHUNK, CHUNK)
        return pltpu.make_async_copy(obuf.at[s, pl.ds(j * CHUNK, CHUNK)], ost_ref.at[pl.ds(row, CHUNK)],
                                     out_sem.at[s])

    def drain(s, n):
        @pl.when(n == per_block)
        def _():
            pltpu.make_async_copy(obuf.at[s], ost_ref.at[pl.ds(0, EXPERT_ROWS)], out_sem.at[s]).wait()

        @pl.when(n != per_block)
        def _():
            def body(j, carry):
                out_copy(0, 0, s).wait()
                return carry
            lax.fori_loop(0, n, body, 0)

    def zero_tails(wait):
        def tile(i, carry):
            def chunk(c, carry2):
                row = pl.multiple_of(i * STAGE_ROWS + c * CHUNK, CHUNK)
                cp = pltpu.make_async_copy(zero_ref, ost_ref.at[pl.ds(row, CHUNK)], zero_sem)
                if wait:
                    cp.wait()
                else:
                    cp.start()
                return carry2
            return lax.fori_loop(used_ref[i] // CHUNK, STAGE_ROWS // CHUNK, chunk, carry)
        lax.fori_loop(0, used_ref.shape[0], tile, 0)

    @pl.when(b == 0)
    def _prologue():
        fetch(src_now_ref, 0)
        zero_ref[...] = jnp.zeros_like(zero_ref)
        zero_tails(wait=False)
        zero_tails(wait=True)

    @pl.when(b + 1 < nu)
    def _prefetch_next():
        fetch(src_next_ref, 1 - slot)

    @pl.when(b < nu)
    def _compute():
        pltpu.make_async_copy(stage_ref.at[pl.ds(0, EXPERT_ROWS)], xbuf.at[slot], in_sem.at[slot]).wait()

        @pl.when(b >= 2)
        def _():
            drain(slot, nv_ref[b - 2])

        @pl.when(jnp.logical_or(b == 0, be_ref[b] != be_ref[jnp.maximum(b - 1, 0)]))
        def _new_expert():
            wgb[...] = wg_ref[0].astype(BF16)
            wub[...] = wu_ref[0].astype(BF16)
            wdb[...] = wd_ref[0].astype(BF16)

        x = xbuf[slot]
        half = wg_ref.shape[1] // 2
        xb = jnp.concatenate(_unpack(x[:, :half]), axis=1)
        gate = lax.bitcast_convert_type(x[:, half:half + 1], F32)
        g = jnp.dot(xb, wgb[...], preferred_element_type=F32)
        u = jnp.dot(xb, wub[...], preferred_element_type=F32)
        h = (g * (1.0 / (1.0 + jnp.exp(-g))) * u).astype(BF16)
        o = jnp.dot(h, wdb[...], preferred_element_type=F32) * gate
        obuf[slot] = _pack(o[:, :half], o[:, half:])

        def body(j, carry):
            out_copy(j, dst_ref[0, 0, j], slot).start()
            return carry

        @pl.when(nv_ref[b] == per_block)
        def _():
            lax.fori_loop(0, per_block, body, 0, unroll=8)

        @pl.when(nv_ref[b] != per_block)
        def _():
            lax.fori_loop(0, nv_ref[b], body, 0)

        @pl.when(b == nu - 1)
        def _():
            drain(slot, nv_ref[b])

            @pl.when(b >= 1)
            def _():
                drain(1 - slot, nv_ref[b - 1])


def _expert_ffn(stage, plan, wg, wu, wd):
    block_e, n_used, n_valid, src_chunk, dst_chunk, used = plan
    nb = block_e.shape[0]
    sw = stage.shape[1]
    d, de = wg.shape[1], wg.shape[2]
    per_block = EXPERT_ROWS // CHUNK
    cur = lambda i, be, nu: jnp.minimum(i, nu[0] - 1)
    wspec = lambda shape: pl.BlockSpec(shape, lambda i, be, nu, nv, us: (be[cur(i, be, nu)], 0, 0))
    smem = lambda imap: pl.BlockSpec((1, 1, per_block), imap, memory_space=pltpu.SMEM)
    return pl.pallas_call(
        _ffn_kernel,
        grid_spec=pltpu.PrefetchScalarGridSpec(
            num_scalar_prefetch=4,
            grid=(nb,),
            in_specs=[smem(lambda i, be, nu, nv, us: (i, 0, 0)),
                      smem(lambda i, be, nu, nv, us: (jnp.minimum(i + 1, nb - 1), 0, 0)),
                      smem(lambda i, be, nu, nv, us: (i, 0, 0)),
                      pl.BlockSpec(memory_space=pl.ANY),
                      wspec((1, d, de)), wspec((1, d, de)), wspec((1, de, d))],
            out_specs=pl.BlockSpec(memory_space=pl.ANY),
            scratch_shapes=[pltpu.VMEM((2, EXPERT_ROWS, sw), U32), pltpu.VMEM((2, EXPERT_ROWS, d // 2), U32),
                            pltpu.VMEM((CHUNK, d // 2), U32),
                            pltpu.VMEM((d, de), BF16), pltpu.VMEM((d, de), BF16), pltpu.VMEM((de, d), BF16),
                            pltpu.SemaphoreType.DMA((2,)), pltpu.SemaphoreType.DMA((2,)),
                            pltpu.SemaphoreType.DMA(())]),
        out_shape=jax.ShapeDtypeStruct((stage.shape[0], d // 2), U32),
        compiler_params=_cparams(1, VMEM_LIMIT),
        name="expert_ffn",
    )(block_e, n_used, n_valid, used, src_chunk, src_chunk, dst_chunk, stage, wg, wu, wd)


def _combine_kernel(ost_ref, route_ref, x1_ref, fg_ref, o_ref):
    col = lax.broadcasted_iota(I32, (MOE_TILE, STAGE_ROWS), 1)
    for sub in range(route_ref.shape[0] // MOE_TILE):
        rows = slice(sub * MOE_TILE, (sub + 1) * MOE_TILE)
        route = route_ref[rows, :]
        sel = jnp.logical_or(col == route[:, 4:5].astype(I32), col == route[:, 5:6].astype(I32))
        perm = jnp.where(sel, 1.0, 0.0).astype(BF16)
        ost = ost_ref[sub * STAGE_ROWS:(sub + 1) * STAGE_ROWS, :]
        y = jnp.dot(perm, jnp.concatenate(_unpack(ost), axis=1), preferred_element_type=F32)
        x2 = x1_ref[rows, :] + y
        ms = jnp.mean(x2 * x2, axis=-1, keepdims=True)
        o_ref[rows, :] = x2 * lax.rsqrt(ms + EPS) * fg_ref[...]


def _combine(ostage, route, x1, final_g):
    t, d = x1.shape
    return pl.pallas_call(
        _combine_kernel,
        grid=(t // GM_TILE,),
        in_specs=[pl.BlockSpec((GM_TILE // MOE_TILE * STAGE_ROWS, d // 2), lambda i: (i, 0)),
                  pl.BlockSpec((GM_TILE, ROUTE_LANES), lambda i: (i, 0)),
                  pl.BlockSpec((GM_TILE, d), lambda i: (i, 0)),
                  pl.BlockSpec((1, d), lambda i: (0, 0))],
        out_specs=pl.BlockSpec((GM_TILE, d), lambda i: (i, 0)),
        out_shape=jax.ShapeDtypeStruct((t, d), F32),
        compiler_params=_cparams(1, VMEM_LIMIT),
        name="moe_combine",
    )(ostage, route, x1, final_g)


def _hier_moe_and_norm(x1, stage, route, tile_cnt, wg, wu, wd, final_g):
    t, d = x1.shape
    ostage = _expert_ffn(stage, _moe_plan(tile_cnt, t), wg, wu, wd)
    return _combine(ostage, route, x1, final_g)


def _encoder(x, prm, kf, tables, nf):
    b, l, d = x.shape
    t = b * l
    d_hy = prm["skip"].shape[1]
    x2 = x.reshape(t, d)
    gates5, gm = _inproj(x2, prm["mix_norm_g"], prm["w_in"], prm["short_w"], prm["short_b"], b, l, nf,
                         (HY_ORDER + 1) * d_hy)
    z5 = gates5
    z_blk = HY_ORDER * d_hy // CONV_LANES
    for o in range(HY_ORDER):
        z5 = _longconv(gates5, z5, kf, tables, d_hy, gate_blk=o * d_hy // CONV_LANES,
                       z_blk=z_blk, kf_col=o * d_hy // CONV_LANES, nf=nf, out_bmajor=o < HY_ORDER - 1)
        z_blk = 0
    y_hy = z5.reshape(b, d_hy // LANES, l, LANES)
    y_gm = _gmlp(gm, prm["ln_g"], prm["ln_b"], prm["ws_stack"], prm["bias_t"])
    x1, stage, route, tile_cnt = _outproj(x2, y_hy, y_gm, prm["mix_out_g"], prm["bd"], prm["w_out"], prm["ffn_norm_g"],
                             prm["wr_cat"], prm["br"])
    out = _hier_moe_and_norm(x1, stage, route, tile_cnt, prm["w_e_gate"], prm["w_e_up"], prm["w_e_down"],
                             prm["final_norm_g"])
    return out.reshape(b, l, d)


def kernel(x_prompt, x_sample, mix_norm_g, w_in, hy_short_w, hy_short_b, hy_filt_w_emb, hy_filt_b_emb,
           hy_filt_w_inner, hy_filt_b_inner, hy_filt_freq, hy_filt_w_out, hy_skip, gm_ln_g, gm_ln_b,
           gm_w_s, gm_b_s, mix_out_g, w_out, ffn_norm_g, w_group, b_group, w_expert_router,
           b_expert_router, w_e_gate, w_e_up, w_e_down, final_norm_g):
    assert w_in.shape[0] == 1, "one layer"
    l = x_prompt.shape[1]
    assert x_sample.shape[1] == l
    nf = math.isqrt(2 * l)
    assert nf * nf == 2 * l and nf % SLAB == 0
    d = x_prompt.shape[2]
    d_hy = hy_skip.shape[2]
    d_gm = gm_ln_g.shape[1]
    head_dim = d_gm // GM_HEADS
    assert d_hy // HY_HEADS == head_dim and d_hy == d_gm
    assert x_prompt.shape[0] % 2 == 0 and x_sample.shape[0] % 2 == 0, "sequences are convolved in pairs"
    assert l % (IN_A * nf) == 0 and l % ROW_TILE == 0 and l % GM_TILE == 0 and ROW_TILE % MOE_TILE == 0
    assert d_hy % CONV_LANES == 0 and gm_w_s.shape[2] == GM_CHUNK and w_e_gate.shape[1] == N_EXPERTS

    tables = _dft_tables(nf)
    f1, f2, f2i, f3, f1_full = tables
    tables_bf = tuple(a.astype(BF16) for a in (f1, f2, f2i, f3))

    max_decay = math.log(DECAY_TARGET) / FAST_DECAY_PCT
    min_decay = math.log(DECAY_TARGET) / SLOW_DECAY_PCT
    deltas = jnp.abs(jnp.linspace(min_decay, max_decay, d_hy, dtype=F32))[None, :]
    taps = _filter_taps(hy_filt_w_emb[0], hy_filt_b_emb[0][None, :], hy_filt_w_inner[0], hy_filt_b_inner[0],
                        hy_filt_freq[0][None, :], hy_filt_w_out[0], deltas, l, d_hy)
    kf = _filter_spectrum(taps, hy_skip[0].reshape(1, HY_ORDER * d_hy), f1_full, f2, nf)

    n_route = N_GROUPS + N_EXPERTS
    wr = jnp.zeros((d, ROUTE_LANES), F32).at[:, :n_route].set(
        jnp.concatenate([w_group[0], w_expert_router[0]], axis=1))
    wr_hi = wr.astype(BF16)
    wr_lo = (wr - wr_hi.astype(F32)).astype(BF16)
    br = jnp.zeros((1, ROUTE_LANES), F32).at[0, :n_route].set(jnp.concatenate([b_group[0], b_expert_router[0]]))
    hid = jnp.arange(d_hy, dtype=I32) // head_dim
    bd = jnp.where(hid[:, None] == hid[None, :], 1.0 / head_dim, 0.0).astype(BF16)

    prm = dict(
        mix_norm_g=mix_norm_g, w_in=w_in[0].astype(BF16), short_w=hy_short_w[0], short_b=hy_short_b,
        skip=hy_skip[0], ln_g=gm_ln_g, ln_b=gm_ln_b,
        ws_stack=gm_w_s[0].reshape(GM_HEADS * GM_CHUNK, GM_CHUNK).astype(BF16),
        bias_t=jnp.repeat(gm_b_s[0].T, head_dim, axis=1),
        mix_out_g=mix_out_g, bd=bd, w_out=w_out[0].astype(BF16), ffn_norm_g=ffn_norm_g,
        wr_cat=jnp.concatenate([wr_hi, wr_lo], axis=1), br=br,
        w_e_gate=w_e_gate[0], w_e_up=w_e_up[0], w_e_down=w_e_down[0],
        final_norm_g=final_norm_g[None, :])
    y_prompt = _encoder(x_prompt, prm, kf, tables_bf, nf)
    y_sample = _encoder(x_sample, prm, kf, tables_bf, nf)
    return (y_prompt, y_sample)
```

```python
import functools
import math

import jax
import jax.numpy as jnp
from jax import lax
from jax.experimental import pallas as pl
from jax.experimental.pallas import tpu as pltpu

F32 = jnp.float32
BF16 = jnp.bfloat16
U32 = jnp.uint32
I32 = jnp.int32

EPS = 1e-6
HY_ORDER = 2
HY_HEADS = 8
GM_HEADS = 8
GM_CHUNK = 128
N_GROUPS = 4
EXPERTS_PER_GROUP = 8
N_EXPERTS = N_GROUPS * EXPERTS_PER_GROUP
TOP_K = 2
FILTER_EMB = 5
DECAY_TARGET = 1e-2
FAST_DECAY_PCT = 0.3
SLOW_DECAY_PCT = 1.5

LANES = 128
ROUTE_LANES = LANES
SLAB = 16
G_PITCH = SLAB + 8
CONV_LANES = 256
PHASE2_UNROLL = 16
ROW_TILE = 1024
GM_TILE = 1024
IN_A = 8
IN_SLABS = 4
EXPERT_ROWS = 1024
CHUNK = 8
MOE_TILE = 512
STAGE_ROWS = MOE_TILE * TOP_K + N_EXPERTS * CHUNK
VMEM_LIMIT = 56 * 1024 * 1024


def _cparams(n_axes, vmem=None):
    return pltpu.CompilerParams(dimension_semantics=("arbitrary",) * n_axes,
                                vmem_limit_bytes=vmem)


def _inproj_kernel(x_ref, xp_ref, xn_ref, g_ref, w_ref, sw_ref, sb_ref, hy_ref, gm_ref, stage_ref,
                   *, nf, d_hy3, tiles_per_seq):
    i = pl.program_id(0)
    rows = x_ref.shape[0]
    pitch = stage_ref.shape[1] // IN_A
    halo = xp_ref.shape[0]
    x = jnp.concatenate([xp_ref[...], x_ref[...], xn_ref[...]], axis=0)
    ms = jnp.mean(x * x, axis=-1, keepdims=True)
    n = (x * lax.rsqrt(ms + EPS) * g_ref[...]).astype(BF16)
    gm = jnp.dot(n, w_ref[:, d_hy3:], preferred_element_type=F32)
    gm_ref[...] = gm[halo:halo + rows].astype(BF16)

    pos = i % tiles_per_seq
    has_prev = jnp.where(pos == 0, 0.0, 1.0)
    has_next = jnp.where(pos == tiles_per_seq - 1, 0.0, 1.0)
    row = lax.broadcasted_iota(I32, (rows, 1), 0)
    prev_w = jnp.where(row == 0, has_prev, 1.0)
    next_w = jnp.where(row == rows - 1, has_next, 1.0)
    n_slabs = stage_ref.shape[0]
    cols = n_slabs * LANES
    ext = rows + 2 * halo
    for c0 in range(0, d_hy3, cols):
        p = jnp.dot(n, w_ref[:, c0:c0 + cols], preferred_element_type=F32)
        prev = pltpu.roll(p, 1, 0)[halo:halo + rows] * prev_w
        nxt = pltpu.roll(p, ext - 1, 0)[halo:halo + rows] * next_w
        res = (sw_ref[0:1, c0:c0 + cols] * prev + sw_ref[1:2, c0:c0 + cols] * p[halo:halo + rows]
               + sw_ref[2:3, c0:c0 + cols] * nxt + sb_ref[:, c0:c0 + cols])
        for s in range(n_slabs):
            for al in range(IN_A):
                stage_ref[s, al * pitch:al * pitch + nf, :] = res[al * nf:(al + 1) * nf,
                                                                s * LANES:(s + 1) * LANES]
        for s in range(n_slabs):
            for b in range(nf):
                hy_ref[0, c0 // LANES + s, b] = stage_ref[s, pl.ds(b, IN_A, stride=pitch), :]


def _inproj(x2, g, w_bf, short_w, short_b, n_seq, seq_len, nf, d_hy3):
    t, d = x2.shape
    dp = w_bf.shape[1]
    rows = IN_A * nf
    tiles_per_seq = seq_len // rows
    hb = rows // 8
    nhb = t // 8
    full = lambda a: pl.BlockSpec(a.shape, lambda i: (0,) * a.ndim)
    return pl.pallas_call(
        functools.partial(_inproj_kernel, nf=nf, d_hy3=d_hy3, tiles_per_seq=tiles_per_seq),
        grid=(t // rows,),
        in_specs=[pl.BlockSpec((rows, d), lambda i: (i, 0)),
                  pl.BlockSpec((8, d), lambda i: (jnp.maximum(i * hb - 1, 0), 0)),
                  pl.BlockSpec((8, d), lambda i: (jnp.minimum((i + 1) * hb, nhb - 1), 0)),
                  full(g), full(w_bf), full(short_w), full(short_b)],
        out_specs=[pl.BlockSpec((1, d_hy3 // LANES, nf, IN_A, LANES),
                                lambda i: (i // tiles_per_seq, 0, 0, i % tiles_per_seq, 0)),
                   pl.BlockSpec((rows, dp - d_hy3), lambda i: (i, 0))],
        out_shape=[jax.ShapeDtypeStruct((n_seq, d_hy3 // LANES, nf, nf // 2, LANES), F32),
                   jax.ShapeDtypeStruct((t, dp - d_hy3), BF16)],
        scratch_shapes=[pltpu.VMEM((IN_SLABS, IN_A * (nf + 8), LANES), F32)],
        compiler_params=_cparams(1, VMEM_LIMIT),
        name="inproj_shortconv",
    )(x2, x2, x2, g, w_bf, short_w, short_b)


def _cos_sin(m, period):
    ang = m.astype(F32) * (2.0 * math.pi / period)
    return jnp.cos(ang), jnp.sin(ang)


def _stack_complex(mr, mi):
    top = jnp.concatenate([mr, -mi], axis=-1)
    bot = jnp.concatenate([mi, mr], axis=-1)
    return jnp.concatenate([top, bot], axis=-2)


def _dft_tables(nf):
    na = nf // 2
    n = nf * nf
    idx = jnp.arange(nf, dtype=I32)
    c, s = _cos_sin((idx[:, None] * idx[None, :]) % nf, nf)
    f2 = _stack_complex(c, -s)
    f2i = _stack_complex(c, s)
    b_ = idx[:, None, None]
    ka = idx[None, :, None]
    a_ = idx[None, None, :]
    m1 = (nf * a_ * ka + b_ * ka) % n
    c1, s1 = _cos_sin(m1, n)
    f1 = _stack_complex(c1[:, :, :na], -s1[:, :, :na])
    f1_full = jnp.concatenate([c1, -s1], axis=1)
    c3 = jnp.swapaxes(c1, 1, 2)[:, :na, :]
    s3 = jnp.swapaxes(s1, 1, 2)[:, :na, :]
    f3 = _stack_complex(c3, s3)
    return f1, f2, f2i, f3, f1_full


def _split_bf16(x):
    hi = x.astype(BF16)
    lo = (x - hi.astype(F32)).astype(BF16)
    return hi, lo


def _stack_split_lhs(a):
    hi, lo = _split_bf16(a)
    return jnp.concatenate([hi, lo, hi], axis=-1)


def _stack_split_rhs(b):
    hi, lo = _split_bf16(b)
    return jnp.concatenate([hi, hi, lo], axis=-2)


def _filter_taps_kernel(wemb_ref, bemb_ref, win_ref, bin_ref, freq_ref, wout_ref, delta_ref, o_ref,
                        *, seq_len, n_inner, d_hy):
    rows = o_ref.shape[1]
    j = pl.program_id(0) * rows + lax.broadcasted_iota(I32, (rows, 1), 0)
    tidx = jnp.where(j < seq_len, j, 2 * seq_len - j).astype(F32)
    t = tidx / float(seq_len - 1)
    fr0 = jnp.float32(1e-4)
    ang0 = (2.0 * math.pi / seq_len) * tidx * fr0
    ang1 = (2.0 * math.pi / seq_len) * tidx
    freq = freq_ref[...]
    pre = (t * wemb_ref[0:1, :] + jnp.cos(ang0) * wemb_ref[1:2, :] + jnp.cos(ang1) * wemb_ref[2:3, :]
           - jnp.sin(ang0) * wemb_ref[3:4, :] - jnp.sin(ang1) * wemb_ref[4:5, :] + bemb_ref[...])
    hdn = jnp.sin(freq * pre)
    for i in range(n_inner):
        hdn = jnp.sin(freq * (jnp.dot(_stack_split_lhs(hdn), win_ref[i], preferred_element_type=F32)
                              + bin_ref[i:i + 1, :]))
    h = jnp.dot(_stack_split_lhs(hdn), wout_ref[...], preferred_element_type=F32)
    decay = jnp.exp(-t * delta_ref[...])
    for o in range(HY_ORDER):
        fwd = h[:, (2 * o) * d_hy:(2 * o + 1) * d_hy]
        bwd = h[:, (2 * o + 1) * d_hy:(2 * o + 2) * d_hy]
        val = jnp.where(j == seq_len, 0.0, jnp.where(j < seq_len, fwd, bwd) * decay)
        for cc in range(d_hy // LANES):
            o_ref[o * (d_hy // LANES) + cc] = val[:, cc * LANES:(cc + 1) * LANES]


def _filter_taps(wemb, bemb, win, bin_, freq, wout, deltas, seq_len, d_hy):
    rows = 1024
    n2 = 2 * seq_len
    width = wemb.shape[1]
    n_inner = win.shape[0]
    full = lambda *shape: pl.BlockSpec(shape, lambda i: (0,) * len(shape))
    return pl.pallas_call(
        functools.partial(_filter_taps_kernel, seq_len=seq_len, n_inner=n_inner, d_hy=d_hy),
        grid=(n2 // rows,),
        in_specs=[full(FILTER_EMB, width), full(1, width), full(n_inner, 3 * width, width),
                  full(n_inner, width), full(1, width), full(3 * width, 2 * HY_ORDER * d_hy),
                  full(1, d_hy)],
        out_specs=pl.BlockSpec((HY_ORDER * d_hy // LANES, rows, LANES), lambda i: (0, i, 0)),
        out_shape=jax.ShapeDtypeStruct((HY_ORDER * d_hy // LANES, n2, LANES), F32),
        compiler_params=_cparams(1, VMEM_LIMIT),
        name="filter_taps",
    )(wemb, bemb, _stack_split_rhs(win), bin_, freq, _stack_split_rhs(wout), deltas)


def _filter_fft1_kernel(k_ref, f_ref, o_ref, *, nf):
    halves = k_ref.shape[0]
    k = k_ref.reshape(halves, nf * SLAB, LANES)
    g = o_ref.reshape(halves, 2, nf * SLAB, LANES)
    for bl in range(SLAB):
        col = jnp.concatenate([k[h, pl.ds(bl, nf, stride=SLAB), :] for h in range(halves)], axis=1)
        res = jnp.dot(f_ref[bl], _stack_split_rhs(col), preferred_element_type=F32)
        for h in range(halves):
            for ri in range(2):
                g[h, ri, pl.ds(bl, nf, stride=SLAB), :] = res[ri * nf:(ri + 1) * nf,
                                                              h * LANES:(h + 1) * LANES]


def _filter_fft2_kernel(g_ref, f_ref, skip_ref, o_ref, *, nf, scale):
    halves = g_ref.shape[0]
    lag0 = jnp.concatenate([jnp.broadcast_to(skip_ref[...], (nf, skip_ref.shape[1])),
                            jnp.zeros((nf, skip_ref.shape[1]), F32)], axis=0)
    for kl in range(SLAB):
        rhs = jnp.concatenate([jnp.concatenate([g_ref[h, ri, kl] for h in range(halves)], axis=1)
                               for ri in range(2)], axis=0)
        res = jnp.dot(f_ref[...], _stack_split_rhs(rhs), preferred_element_type=F32)
        o_ref[kl] = ((res + lag0) * scale).astype(o_ref.dtype)


def _filter_spectrum(taps, skip, f1_full, f2, nf):
    nch = taps.shape[0]
    c = nch * LANES
    halves = CONV_LANES // LANES
    k4 = taps.reshape(nch, nf, nf, LANES)
    g = pl.pallas_call(
        functools.partial(_filter_fft1_kernel, nf=nf),
        grid=(nf // SLAB, nch // halves),
        in_specs=[pl.BlockSpec((halves, nf, SLAB, LANES), lambda i, j: (j, 0, i, 0)),
                  pl.BlockSpec((SLAB, 2 * nf, 3 * nf), lambda i, j: (i, 0, 0))],
        out_specs=pl.BlockSpec((halves, 2, nf, SLAB, LANES), lambda i, j: (j, 0, 0, i, 0)),
        out_shape=jax.ShapeDtypeStruct((nch, 2, nf, nf, LANES), F32),
        compiler_params=_cparams(2, VMEM_LIMIT),
        name="filter_fft1",
    )(k4, _stack_split_lhs(f1_full))
    return pl.pallas_call(
        functools.partial(_filter_fft2_kernel, nf=nf, scale=1.0 / (nf * nf)),
        grid=(nf // SLAB, nch // halves),
        in_specs=[pl.BlockSpec((halves, 2, SLAB, nf, LANES), lambda i, j: (j, 0, i, 0, 0)),
                  pl.BlockSpec((2 * nf, 6 * nf), lambda i, j: (0, 0)),
                  pl.BlockSpec((1, CONV_LANES), lambda i, j: (0, j))],
        out_specs=pl.BlockSpec((SLAB, 2 * nf, CONV_LANES), lambda i, j: (i, 0, j)),
        out_shape=jax.ShapeDtypeStruct((nf, 2 * nf, c), BF16),
        compiler_params=_cparams(2, VMEM_LIMIT),
        name="filter_fft2",
    )(g, _stack_split_lhs(f2), skip)


def _pack(re, im):
    rb = lax.bitcast_convert_type(re.astype(BF16).astype(F32), U32)
    ib = lax.bitcast_convert_type(im.astype(BF16).astype(F32), U32)
    return rb | (ib >> 16)


def _unpack(w):
    re = lax.bitcast_convert_type(w & jnp.uint32(0xFFFF0000), F32).astype(BF16)
    im = lax.bitcast_convert_type(w << 16, F32).astype(BF16)
    return re, im


def _longconv_kernel(z1_ref, f1_ref, kf_ref, f2_ref, f2i_ref, f3_ref, gate_ref, o_ref, g_ref,
                     *, nf, out_bmajor):
    na = nf // 2
    ns = nf // SLAB
    halves = g_ref.shape[1]
    t = pl.program_id(2)

    def load_rows(ref5, s, bl):
        return jnp.concatenate([ref5[s, h, bl] for h in range(halves)], axis=1)

    def store_col(blk, bl, val):
        for h in range(halves):
            g_ref[blk, h, pl.ds(bl, nf, stride=G_PITCH), :] = val[:, h * LANES:(h + 1) * LANES]

    def load_col(blk, bl):
        return jnp.concatenate([g_ref[blk, h, pl.ds(bl, nf, stride=G_PITCH), :] for h in range(halves)],
                               axis=1)

    @pl.when(t < ns)
    def _phase1():
        for bl in range(SLAB):
            rhs = jnp.concatenate([load_rows(z1_ref, 0, bl), load_rows(z1_ref, 1, bl)],
                                  axis=0).astype(BF16)
            out = jnp.dot(f1_ref[bl], rhs, preferred_element_type=F32)
            store_col(t, bl, _pack(out[:nf], out[nf:]))

    @pl.when(jnp.logical_and(t >= ns, t < 2 * ns))
    def _phase2():
        i = t - ns

        def load_slab(ka):
            row0 = pl.multiple_of(ka * G_PITCH, CHUNK)
            return jnp.concatenate([g_ref[:, h, pl.ds(row0, SLAB), :].reshape(nf, LANES)
                                    for h in range(halves)], axis=1)

        def transform(w, kl):
            re, im = _unpack(w)
            s = jnp.dot(f2_ref[...], jnp.concatenate([re, im], axis=0),
                        preferred_element_type=F32)
            sr, si = s[:nf], s[nf:]
            kr, ki = kf_ref[kl, :nf, :], kf_ref[kl, nf:, :]
            pr = (sr * kr - si * ki).astype(BF16)
            pi = (sr * ki + si * kr).astype(BF16)
            h = jnp.dot(f2i_ref[...], jnp.concatenate([pr, pi], axis=0),
                        preferred_element_type=F32)
            return _pack(h[:nf], h[nf:])

        def store_slab(ka, packed):
            row0 = pl.multiple_of(ka * G_PITCH, CHUNK)
            for hf in range(halves):
                g_ref[:, hf, pl.ds(row0, SLAB), :] = packed[:, hf * LANES:(hf + 1) * LANES].reshape(
                    ns, SLAB, LANES)

        def body(grp, carry):
            kls = [grp * PHASE2_UNROLL + u for u in range(PHASE2_UNROLL)]
            ws = [load_slab(i * SLAB + kl) for kl in kls]
            outs = [transform(w, kl) for w, kl in zip(ws, kls)]
            for kl, packed in zip(kls, outs):
                store_slab(i * SLAB + kl, packed)
            return carry

        lax.fori_loop(0, SLAB // PHASE2_UNROLL, body, 0)

    @pl.when(t >= 2 * ns)
    def _phase3():
        j = t - 2 * ns
        out_flat = None if out_bmajor else o_ref.reshape(2, halves, na * SLAB, LANES)
        for bl in range(SLAB):
            re, im = _unpack(load_col(j, bl))
            y = jnp.dot(f3_ref[bl], jnp.concatenate([re, im], axis=0),
                        preferred_element_type=F32)
            for s in range(2):
                val = load_rows(gate_ref, s, bl) * y[s * na:(s + 1) * na]
                for h in range(halves):
                    piece = val[:, h * LANES:(h + 1) * LANES]
                    if out_bmajor:
                        o_ref[s, h, bl] = piece.astype(o_ref.dtype)
                    else:
                        out_flat[s, h, pl.ds(bl, na, stride=SLAB), :] = piece


def _longconv(gates5, z5, kf, tables, c, gate_blk, z_blk, kf_col, nf, out_bmajor):
    f1, f2, f2i, f3 = tables
    b, _, _, na, _ = z5.shape
    ns = nf // SLAB
    halves = CONV_LANES // LANES
    nchunk = c // CONV_LANES
    p1 = lambda t: jnp.minimum(t, ns - 1)
    p2 = lambda t: jnp.clip(t - ns, 0, ns - 1)
    p3 = lambda t: jnp.clip(t - 2 * ns, 0, ns - 1)
    cl = CONV_LANES
    blk5 = (2, halves, SLAB, na, LANES)
    if out_bmajor:
        out_spec = pl.BlockSpec(blk5, lambda p, q, t: (p, q, p3(t), 0, 0))
        out_shape = (b, c // LANES, nf, na, LANES)
    else:
        out_spec = pl.BlockSpec((2, halves, na, SLAB, LANES), lambda p, q, t: (p, q, 0, p3(t), 0))
        out_shape = (b, c // LANES, na, nf, LANES)
    return pl.pallas_call(
        functools.partial(_longconv_kernel, nf=nf, out_bmajor=out_bmajor),
        grid=(b // 2, nchunk, 3 * ns),
        in_specs=[
            pl.BlockSpec(blk5, lambda p, q, t: (p, z_blk + q, p1(t), 0, 0)),
            pl.BlockSpec((SLAB, 2 * nf, 2 * na), lambda p, q, t: (p1(t), 0, 0)),
            pl.BlockSpec((SLAB, 2 * nf, cl), lambda p, q, t: (p2(t), 0, kf_col + q)),
            pl.BlockSpec((2 * nf, 2 * nf), lambda p, q, t: (0, 0)),
            pl.BlockSpec((2 * nf, 2 * nf), lambda p, q, t: (0, 0)),
            pl.BlockSpec((SLAB, 2 * na, 2 * nf), lambda p, q, t: (p3(t), 0, 0)),
            pl.BlockSpec(blk5, lambda p, q, t: (p, gate_blk + q, p3(t), 0, 0)),
        ],
        out_specs=out_spec,
        out_shape=jax.ShapeDtypeStruct(out_shape, BF16 if out_bmajor else F32),
        scratch_shapes=[pltpu.VMEM((ns, halves, nf * G_PITCH, LANES), U32)],
        compiler_params=_cparams(3, VMEM_LIMIT),
        name="longconv",
    )(z5, f1, kf, f2, f2i, f3, gates5)


def _gelu_exact(x):
    return 0.5 * x * (1.0 + lax.erf(x * (1.0 / math.sqrt(2.0))))


def _gmlp_kernel(gm_ref, lng_ref, lnb_ref, ws_ref, bias_ref, o_ref, *, d_gm):
    g = _gelu_exact(gm_ref[...].astype(F32))
    u, v = g[:, :d_gm], g[:, d_gm:]
    mu = jnp.mean(v, axis=-1, keepdims=True)
    vc = v - mu
    var = jnp.mean(vc * vc, axis=-1, keepdims=True)
    vh = (vc * lax.rsqrt(var + EPS) * lng_ref[...] + lnb_ref[...]).astype(BF16)
    head_dim = d_gm // GM_HEADS
    per_group = CONV_LANES // head_dim
    head = lax.broadcasted_iota(I32, (GM_CHUNK, CONV_LANES), 1) // head_dim
    for c in range(gm_ref.shape[0] // GM_CHUNK):
        rows = slice(c * GM_CHUNK, (c + 1) * GM_CHUNK)
        groups = []
        for grp in range(GM_HEADS // per_group):
            lanes = slice(grp * CONV_LANES, (grp + 1) * CONV_LANES)
            w = ws_ref[grp * per_group * GM_CHUNK:(grp + 1) * per_group * GM_CHUNK, :]
            r = jnp.dot(w, vh[rows, lanes], preferred_element_type=F32)
            s = r[:GM_CHUNK]
            for h in range(1, per_group):
                s = jnp.where(head == h, r[h * GM_CHUNK:(h + 1) * GM_CHUNK], s)
            groups.append(s)
        o_ref[rows, :] = u[rows] * (jnp.concatenate(groups, axis=1) + bias_ref[...])


def _gmlp(gm, ln_g, ln_b, ws_stack, bias_t):
    t, c2 = gm.shape
    d_gm = c2 // 2
    return pl.pallas_call(
        functools.partial(_gmlp_kernel, d_gm=d_gm),
        grid=(t // GM_TILE,),
        in_specs=[pl.BlockSpec((GM_TILE, c2), lambda i: (i, 0)),
                  pl.BlockSpec((1, d_gm), lambda i: (0, 0)),
                  pl.BlockSpec((1, d_gm), lambda i: (0, 0)),
                  pl.BlockSpec(ws_stack.shape, lambda i: (0, 0)),
                  pl.BlockSpec(bias_t.shape, lambda i: (0, 0))],
        out_specs=pl.BlockSpec((GM_TILE, d_gm), lambda i: (i, 0)),
        out_shape=jax.ShapeDtypeStruct((t, d_gm), F32),
        compiler_params=_cparams(1, VMEM_LIMIT),
        name="gmlp",
    )(gm, ln_g, ln_b, ws_stack, bias_t)


def _head_rms(y, bd):
    ms = jnp.dot((y * y).astype(BF16), bd, preferred_element_type=F32)
    return y * lax.rsqrt(ms + EPS)


def _route(logits):
    lane = lax.broadcasted_iota(I32, logits.shape, 1)
    neg = jnp.float32(-1e30)
    big = jnp.int32(ROUTE_LANES)
    gmask = lane < N_GROUPS
    gl = jnp.where(gmask, logits, neg)
    gmax = jnp.max(gl, axis=-1, keepdims=True)
    grp = jnp.min(jnp.where(jnp.logical_and(gl == gmax, gmask), lane, big), axis=-1, keepdims=True)
    psum = jnp.sum(jnp.where(gmask, jnp.exp(gl - gmax), 0.0), axis=-1, keepdims=True)
    p_grp = 1.0 / psum
    lo = N_GROUPS + EXPERTS_PER_GROUP * grp
    emask = jnp.logical_and(lane >= lo, lane < lo + EXPERTS_PER_GROUP)
    el = jnp.where(emask, logits, neg)
    m1 = jnp.max(el, axis=-1, keepdims=True)
    i1 = jnp.min(jnp.where(jnp.logical_and(el == m1, emask), lane, big), axis=-1, keepdims=True)
    emask2 = jnp.logical_and(emask, lane != i1)
    el2 = jnp.where(emask2, logits, neg)
    m2 = jnp.max(el2, axis=-1, keepdims=True)
    i2 = jnp.min(jnp.where(jnp.logical_and(el2 == m2, emask2), lane, big), axis=-1, keepdims=True)
    d = jnp.exp(m2 - m1)
    g1 = p_grp * (1.0 / (1.0 + d))
    g2 = p_grp * (d / (1.0 + d))
    e1 = (i1 - N_GROUPS).astype(F32)
    e2 = (i2 - N_GROUPS).astype(F32)
    return jnp.where(lane == 0, e1, jnp.where(lane == 1, e2, jnp.where(lane == 2, g1,
                     jnp.where(lane == 3, g2, 0.0))))


def _lane_cumsum(v):
    lane = lax.broadcasted_iota(I32, v.shape, 1)
    sh = 1
    while sh < v.shape[1]:
        v = v + jnp.where(lane >= sh, pltpu.roll(v, sh, 1), 0.0)
        sh *= 2
    return v


def _outproj_kernel(x_ref, yh_ref, yg_ref, mg_ref, bd_ref, wo_ref, fg_ref, wr_ref, br_ref,
                    x1_ref, stage_ref, route_ref, count_ref, *, d_hy):
    bd = bd_ref[...]
    mg = mg_ref[...]
    wo = wo_ref[...]
    wr = wr_ref[...]

    yh = jnp.concatenate([yh_ref[0, c] for c in range(yh_ref.shape[1])], axis=1)
    mh = (_head_rms(yh, bd) * mg[:, :d_hy]).astype(BF16)
    mgm = (_head_rms(yg_ref[...], bd) * mg[:, d_hy:]).astype(BF16)
    x1 = (x_ref[...] + jnp.dot(mh, wo[:d_hy], preferred_element_type=F32)
          + jnp.dot(mgm, wo[d_hy:], preferred_element_type=F32))
    x1_ref[...] = x1
    ms = jnp.mean(x1 * x1, axis=-1, keepdims=True)
    hi, lo = _split_bf16(x1 * lax.rsqrt(ms + EPS) * fg_ref[...])
    hw = jnp.dot(hi, wr, preferred_element_type=F32)
    logits = (hw[:, :ROUTE_LANES] + hw[:, ROUTE_LANES:]
              + jnp.dot(lo, wr[:, :ROUTE_LANES], preferred_element_type=F32) + br_ref[...])
    route = _route(logits)

    lane = lax.broadcasted_iota(I32, (MOE_TILE, ROUTE_LANES), 1)
    r = lax.broadcasted_iota(I32, (MOE_TILE, MOE_TILE), 0)
    c = lax.broadcasted_iota(I32, (MOE_TILE, MOE_TILE), 1)
    ltri = jnp.where(c < r, 1.0, 0.0).astype(BF16)
    srow = lax.broadcasted_iota(I32, (STAGE_ROWS, MOE_TILE), 0)
    mlane = lax.broadcasted_iota(I32, (STAGE_ROWS, LANES), 1)
    half = hi.shape[1] // 2
    subs = route.shape[0] // MOE_TILE
    for sub in range(subs):
        rows = slice(sub * MOE_TILE, (sub + 1) * MOE_TILE)
        rsub = route[rows]
        oh0 = lane == rsub[:, 0:1].astype(I32)
        oh1 = lane == rsub[:, 1:2].astype(I32)
        oh = jnp.where(jnp.logical_or(oh0, oh1), 1.0, 0.0)
        before = jnp.dot(ltri, oh.astype(BF16), preferred_element_type=F32)
        cnt = jnp.sum(oh, axis=0, keepdims=True)
        run = jnp.floor((cnt + (CHUNK - 1)) * (1.0 / CHUNK)) * CHUNK
        run_start = _lane_cumsum(jnp.broadcast_to(run, (8, run.shape[1])))[0:1] - run
        tot = before + run_start
        loc0 = jnp.sum(jnp.where(oh0, tot, 0.0), axis=-1, keepdims=True)
        loc1 = jnp.sum(jnp.where(oh1, tot, 0.0), axis=-1, keepdims=True)
        rsub = jnp.where(lane == 4, loc0, jnp.where(lane == 5, loc1, rsub))
        route_ref[rows, :] = rsub
        count_ref[pl.ds(pl.program_id(0) * subs + sub, 1), :] = cnt

        rt = jnp.transpose(rsub)
        p0 = srow == rt[4:5, :].astype(I32)
        p1 = srow == rt[5:6, :].astype(I32)
        perm = jnp.where(jnp.logical_or(p0, p1), 1.0, 0.0).astype(BF16)
        staged = jnp.dot(perm, hi[rows], preferred_element_type=F32)
        gate = jnp.sum(jnp.where(p0, rt[2:3, :], 0.0) + jnp.where(p1, rt[3:4, :], 0.0), axis=-1,
                       keepdims=True)
        meta = jnp.where(mlane == 0,
                         lax.bitcast_convert_type(jnp.broadcast_to(gate, (STAGE_ROWS, LANES)), U32),
                         jnp.uint32(0))
        stage_ref[sub * STAGE_ROWS:(sub + 1) * STAGE_ROWS, :] = jnp.concatenate(
            [_pack(staged[:, :half], staged[:, half:]), meta], axis=1)


def _outproj(x2, yh, yg, mix_g, bd, wo_bf, ffn_g, wr_cat, br):
    t, d = x2.shape
    _, nch, seq_len, _ = yh.shape
    d_hy = nch * LANES
    tiles_per_seq = seq_len // ROW_TILE
    n_tiles = t // ROW_TILE
    row = lambda w: pl.BlockSpec((ROW_TILE, w), lambda i: (i, 0))
    full = lambda a: pl.BlockSpec(a.shape, lambda i: (0,) * a.ndim)
    sw = d // 2 + LANES
    subs = ROW_TILE // MOE_TILE
    n_sub = t // MOE_TILE
    return pl.pallas_call(
        functools.partial(_outproj_kernel, d_hy=d_hy),
        grid=(n_tiles,),
        in_specs=[row(d),
                  pl.BlockSpec((1, nch, ROW_TILE, LANES),
                               lambda i: (i // tiles_per_seq, 0, i % tiles_per_seq, 0)),
                  row(yg.shape[1]), full(mix_g), full(bd), full(wo_bf), full(ffn_g),
                  full(wr_cat), full(br)],
        out_specs=[row(d), pl.BlockSpec((subs * STAGE_ROWS, sw), lambda i: (i, 0)), row(ROUTE_LANES),
                   pl.BlockSpec((n_sub, ROUTE_LANES), lambda i: (0, 0))],
        out_shape=[jax.ShapeDtypeStruct((t, d), F32),
                   jax.ShapeDtypeStruct((n_sub * STAGE_ROWS, sw), U32),
                   jax.ShapeDtypeStruct((t, ROUTE_LANES), F32),
                   jax.ShapeDtypeStruct((n_sub, ROUTE_LANES), F32)],
        compiler_params=_cparams(1, VMEM_LIMIT),
        name="outproj_router",
    )(x2, yh, yg, mix_g, bd, wo_bf, ffn_g, wr_cat, br)


def _moe_plan(tile_cnt, n_tokens):
    n_tiles = tile_cnt.shape[0]
    cnt = tile_cnt[:, :N_EXPERTS].astype(I32)
    run = (cnt + CHUNK - 1) // CHUNK * CHUNK
    run_start = jnp.cumsum(run, axis=1) - run
    used = jnp.sum(run, axis=1)
    e_rows = jnp.sum(run, axis=0)
    e_pad = (e_rows + EXPERT_ROWS - 1) // EXPERT_ROWS * EXPERT_ROWS
    e_end = jnp.cumsum(e_pad)
    e_start = e_end - e_pad
    n_blocks = -(-(n_tokens * TOP_K + n_tiles * N_EXPERTS * (CHUNK - 1)) // EXPERT_ROWS) + N_EXPERTS
    n_used = (e_end[-1:] // EXPERT_ROWS).astype(I32)
    block_row = jnp.arange(n_blocks, dtype=I32) * EXPERT_ROWS
    block_e = jnp.minimum(jnp.sum((e_end[None, :] <= block_row[:, None]).astype(I32), axis=1),
                          N_EXPERTS - 1)
    run_end_in_e = jnp.cumsum(run, axis=0)
    chunk_row = jnp.arange(n_blocks * EXPERT_ROWS // CHUNK, dtype=I32) * CHUNK
    e_of_chunk = jnp.repeat(block_e, EXPERT_ROWS // CHUNK)
    e_sel = jnp.arange(N_EXPERTS, dtype=I32)[None, :] == e_of_chunk[:, None]
    pick = lambda tbl: jnp.dot(e_sel.astype(F32), tbl.astype(F32),
                               precision=lax.Precision.HIGHEST).astype(I32)
    q = chunk_row - pick(e_start[:, None])[:, 0]
    valid = q < pick(e_rows[:, None])[:, 0]
    ends = pick(run_end_in_e.T)
    tile = jnp.minimum(jnp.sum((ends <= q[:, None]).astype(I32), axis=1), n_tiles - 1)
    t_sel = jnp.arange(n_tiles, dtype=I32)[None, :] == tile[:, None]
    at_tile = lambda tbl: jnp.sum(jnp.where(t_sel, pick(tbl.T), 0), axis=1)
    within = q - (at_tile(run_end_in_e) - at_tile(run))
    src_row = tile * STAGE_ROWS + at_tile(run_start) + within
    zero_chunk = STAGE_ROWS // CHUNK - 1
    per_block = EXPERT_ROWS // CHUNK
    src_chunk = jnp.where(valid, src_row // CHUNK, zero_chunk).reshape(n_blocks, 1, per_block)
    dst_chunk = jnp.where(valid, src_row // CHUNK, 0).reshape(n_blocks, 1, per_block)
    n_valid = jnp.sum(valid.reshape(n_blocks, per_block).astype(I32), axis=1)
    return (block_e.astype(I32), n_used, n_valid.astype(I32), src_chunk.astype(I32),
            dst_chunk.astype(I32), used.astype(I32))


def _ffn_kernel(be_ref, nu_ref, nv_ref, used_ref, src_now_ref, src_next_ref, dst_ref, stage_ref,
                wg_ref, wu_ref, wd_ref, ost_ref, xbuf, obuf, zero_ref, wgb, wub, wdb,
                in_sem, out_sem, zero_sem):
    b = pl.program_id(0)
    nu = nu_ref[0]
    per_block = dst_ref.shape[-1]
    slot = b % 2

    def fetch(map_ref, s):
        def body(j, carry):
            row = pl.multiple_of(map_ref[0, 0, j] * CHUNK, CHUNK)
            pltpu.make_async_copy(stage_ref.at[pl.ds(row, CHUNK)], xbuf.at[s, pl.ds(j * CHUNK, CHUNK)],
                                  in_sem.at[s]).start()
            return carry
        lax.fori_loop(0, per_block, body, 0, unroll=8)

    def out_copy(j, chunk, s):
        row = pl.multiple_of(chunk * CHUNK, CHUNK)
        return pltpu.make_async_copy(obuf.at[s, pl.ds(j * CHUNK, CHUNK)], ost_ref.at[pl.ds(row, CHUNK)],
                                     out_sem.at[s])

    def drain(s, n):
        @pl.when(n == per_block)
        def _():
            pltpu.make_async_copy(obuf.at[s], ost_ref.at[pl.ds(0, EXPERT_ROWS)], out_sem.at[s]).wait()

        @pl.when(n != per_block)
        def _():
            def body(j, carry):
                out_copy(0, 0, s).wait()
                return carry
            lax.fori_loop(0, n, body, 0)

    def zero_tails(wait):
        def tile(i, carry):
            def chunk(c, carry2):
                row = pl.multiple_of(i * STAGE_ROWS + c * CHUNK, CHUNK)
                cp = pltpu.make_async_copy(zero_ref, ost_ref.at[pl.ds(row, CHUNK)], zero_sem)
                if wait:
                    cp.wait()
                else:
                    cp.start()
                return carry2
            return lax.fori_loop(used_ref[i] // CHUNK, STAGE_ROWS // CHUNK, chunk, carry)
        lax.fori_loop(0, used_ref.shape[0], tile, 0)

    @pl.when(b == 0)
    def _prologue():
        fetch(src_now_ref, 0)
        zero_ref[...] = jnp.zeros_like(zero_ref)
        zero_tails(wait=False)
        zero_tails(wait=True)

    @pl.when(b + 1 < nu)
    def _prefetch_next():
        fetch(src_next_ref, 1 - slot)

    @pl.when(b < nu)
    def _compute():
        pltpu.make_async_copy(stage_ref.at[pl.ds(0, EXPERT_ROWS)], xbuf.at[slot], in_sem.at[slot]).wait()

        @pl.when(b >= 2)
        def _():
            drain(slot, nv_ref[b - 2])

        @pl.when(jnp.logical_or(b == 0, be_ref[b] != be_ref[jnp.maximum(b - 1, 0)]))
        def _new_expert():
            wgb[...] = wg_ref[0].astype(BF16)
            wub[...] = wu_ref[0].astype(BF16)
            wdb[...] = wd_ref[0].astype(BF16)

        x = xbuf[slot]
        half = wg_ref.shape[1] // 2
        xb = jnp.concatenate(_unpack(x[:, :half]), axis=1)
        gate = lax.bitcast_convert_type(x[:, half:half + 1], F32)
        g = jnp.dot(xb, wgb[...], preferred_element_type=F32)
        u = jnp.dot(xb, wub[...], preferred_element_type=F32)
        h = (g * (1.0 / (1.0 + jnp.exp(-g))) * u).astype(BF16)
        o = jnp.dot(h, wdb[...], preferred_element_type=F32) * gate
        obuf[slot] = _pack(o[:, :half], o[:, half:])

        def body(j, carry):
            out_copy(j, dst_ref[0, 0, j], slot).start()
            return carry

        @pl.when(nv_ref[b] == per_block)
        def _():
            lax.fori_loop(0, per_block, body, 0, unroll=8)

        @pl.when(nv_ref[b] != per_block)
        def _():
            lax.fori_loop(0, nv_ref[b], body, 0)

        @pl.when(b == nu - 1)
        def _():
            drain(slot, nv_ref[b])

            @pl.when(b >= 1)
            def _():
                drain(1 - slot, nv_ref[b - 1])


def _expert_ffn(stage, plan, wg, wu, wd):
    block_e, n_used, n_valid, src_chunk, dst_chunk, used = plan
    nb = block_e.shape[0]
    sw = stage.shape[1]
    d, de = wg.shape[1], wg.shape[2]
    per_block = EXPERT_ROWS // CHUNK
    cur = lambda i, be, nu: jnp.minimum(i, nu[0] - 1)
    wspec = lambda shape: pl.BlockSpec(shape, lambda i, be, nu, nv, us: (be[cur(i, be, nu)], 0, 0))
    smem = lambda imap: pl.BlockSpec((1, 1, per_block), imap, memory_space=pltpu.SMEM)
    return pl.pallas_call(
        _ffn_kernel,
        grid_spec=pltpu.PrefetchScalarGridSpec(
            num_scalar_prefetch=4,
            grid=(nb,),
            in_specs=[smem(lambda i, be, nu, nv, us: (i, 0, 0)),
                      smem(lambda i, be, nu, nv, us: (jnp.minimum(i + 1, nb - 1), 0, 0)),
                      smem(lambda i, be, nu, nv, us: (i, 0, 0)),
                      pl.BlockSpec(memory_space=pl.ANY),
                      wspec((1, d, de)), wspec((1, d, de)), wspec((1, de, d))],
            out_specs=pl.BlockSpec(memory_space=pl.ANY),
            scratch_shapes=[pltpu.VMEM((2, EXPERT_ROWS, sw), U32), pltpu.VMEM((2, EXPERT_ROWS, d // 2), U32),
                            pltpu.VMEM((CHUNK, d // 2), U32),
                            pltpu.VMEM((d, de), BF16), pltpu.VMEM((d, de), BF16), pltpu.VMEM((de, d), BF16),
                            pltpu.SemaphoreType.DMA((2,)), pltpu.SemaphoreType.DMA((2,)),
                            pltpu.SemaphoreType.DMA(())]),
        out_shape=jax.ShapeDtypeStruct((stage.shape[0], d // 2), U32),
        compiler_params=_cparams(1, VMEM_LIMIT),
        name="expert_ffn",
    )(block_e, n_used, n_valid, used, src_chunk, src_chunk, dst_chunk, stage, wg, wu, wd)


def _combine_kernel(ost_ref, route_ref, x1_ref, fg_ref, o_ref):
    col = lax.broadcasted_iota(I32, (MOE_TILE, STAGE_ROWS), 1)
    for sub in range(route_ref.shape[0] // MOE_TILE):
        rows = slice(sub * MOE_TILE, (sub + 1) * MOE_TILE)
        route = route_ref[rows, :]
        sel = jnp.logical_or(col == route[:, 4:5].astype(I32), col == route[:, 5:6].astype(I32))
        perm = jnp.where(sel, 1.0, 0.0).astype(BF16)
        ost = ost_ref[sub * STAGE_ROWS:(sub + 1) * STAGE_ROWS, :]
        y = jnp.dot(perm, jnp.concatenate(_unpack(ost), axis=1), preferred_element_type=F32)
        x2 = x1_ref[rows, :] + y
        ms = jnp.mean(x2 * x2, axis=-1, keepdims=True)
        o_ref[rows, :] = x2 * lax.rsqrt(ms + EPS) * fg_ref[...]


def _combine(ostage, route, x1, final_g):
    t, d = x1.shape
    return pl.pallas_call(
        _combine_kernel,
        grid=(t // GM_TILE,),
        in_specs=[pl.BlockSpec((GM_TILE // MOE_TILE * STAGE_ROWS, d // 2), lambda i: (i, 0)),
                  pl.BlockSpec((GM_TILE, ROUTE_LANES), lambda i: (i, 0)),
                  pl.BlockSpec((GM_TILE, d), lambda i: (i, 0)),
                  pl.BlockSpec((1, d), lambda i: (0, 0))],
        out_specs=pl.BlockSpec((GM_TILE, d), lambda i: (i, 0)),
        out_shape=jax.ShapeDtypeStruct((t, d), F32),
        compiler_params=_cparams(1, VMEM_LIMIT),
        name="moe_combine",
    )(ostage, route, x1, final_g)


def _hier_moe_and_norm(x1, stage, route, tile_cnt, wg, wu, wd, final_g):
    t, d = x1.shape
    ostage = _expert_ffn(stage, _moe_plan(tile_cnt, t), wg, wu, wd)
    return _combine(ostage, route, x1, final_g)


def _encoder(x, prm, kf, tables, nf):
    b, l, d = x.shape
    t = b * l
    d_hy = prm["skip"].shape[1]
    x2 = x.reshape(t, d)
    gates5, gm = _inproj(x2, prm["mix_norm_g"], prm["w_in"], prm["short_w"], prm["short_b"], b, l, nf,
                         (HY_ORDER + 1) * d_hy)
    z5 = gates5
    z_blk = HY_ORDER * d_hy // CONV_LANES
    for o in range(HY_ORDER):
        z5 = _longconv(gates5, z5, kf, tables, d_hy, gate_blk=o * d_hy // CONV_LANES,
                       z_blk=z_blk, kf_col=o * d_hy // CONV_LANES, nf=nf, out_bmajor=o < HY_ORDER - 1)
        z_blk = 0
    y_hy = z5.reshape(b, d_hy // LANES, l, LANES)
    y_gm = _gmlp(gm, prm["ln_g"], prm["ln_b"], prm["ws_stack"], prm["bias_t"])
    x1, stage, route, tile_cnt = _outproj(x2, y_hy, y_gm, prm["mix_out_g"], prm["bd"], prm["w_out"], prm["ffn_norm_g"],
                             prm["wr_cat"], prm["br"])
    out = _hier_moe_and_norm(x1, stage, route, tile_cnt, prm["w_e_gate"], prm["w_e_up"], prm["w_e_down"],
                             prm["final_norm_g"])
    return out.reshape(b, l, d)


def kernel(x_prompt, x_sample, mix_norm_g, w_in, hy_short_w, hy_short_b, hy_filt_w_emb, hy_filt_b_emb,
           hy_filt_w_inner, hy_filt_b_inner, hy_filt_freq, hy_filt_w_out, hy_skip, gm_ln_g, gm_ln_b,
           gm_w_s, gm_b_s, mix_out_g, w_out, ffn_norm_g, w_group, b_group, w_expert_router,
           b_expert_router, w_e_gate, w_e_up, w_e_down, final_norm_g):
    assert w_in.shape[0] == 1, "one layer"
    l = x_prompt.shape[1]
    assert x_sample.shape[1] == l
    nf = math.isqrt(2 * l)
    assert nf * nf == 2 * l and nf % SLAB == 0
    d = x_prompt.shape[2]
    d_hy = hy_skip.shape[2]
    d_gm = gm_ln_g.shape[1]
    head_dim = d_gm // GM_HEADS
    assert d_hy // HY_HEADS == head_dim and d_hy == d_gm
    assert x_prompt.shape[0] % 2 == 0 and x_sample.shape[0] % 2 == 0, "sequences are convolved in pairs"
    assert l % (IN_A * nf) == 0 and l % ROW_TILE == 0 and l % GM_TILE == 0 and ROW_TILE % MOE_TILE == 0
    assert d_hy % CONV_LANES == 0 and gm_w_s.shape[2] == GM_CHUNK and w_e_gate.shape[1] == N_EXPERTS

    tables = _dft_tables(nf)
    f1, f2, f2i, f3, f1_full = tables
    tables_bf = tuple(a.astype(BF16) for a in (f1, f2, f2i, f3))

    max_decay = math.log(DECAY_TARGET) / FAST_DECAY_PCT
    min_decay = math.log(DECAY_TARGET) / SLOW_DECAY_PCT
    deltas = jnp.abs(jnp.linspace(min_decay, max_decay, d_hy, dtype=F32))[None, :]
    taps = _filter_taps(hy_filt_w_emb[0], hy_filt_b_emb[0][None, :], hy_filt_w_inner[0], hy_filt_b_inner[0],
                        hy_filt_freq[0][None, :], hy_filt_w_out[0], deltas, l, d_hy)
    kf = _filter_spectrum(taps, hy_skip[0].reshape(1, HY_ORDER * d_hy), f1_full, f2, nf)

    n_route = N_GROUPS + N_EXPERTS
    wr = jnp.zeros((d, ROUTE_LANES), F32).at[:, :n_route].set(
        jnp.concatenate([w_group[0], w_expert_router[0]], axis=1))
    wr_hi = wr.astype(BF16)
    wr_lo = (wr - wr_hi.astype(F32)).astype(BF16)
    br = jnp.zeros((1, ROUTE_LANES), F32).at[0, :n_route].set(jnp.concatenate([b_group[0], b_expert_router[0]]))
    hid = jnp.arange(d_hy, dtype=I32) // head_dim
    bd = jnp.where(hid[:, None] == hid[None, :], 1.0 / head_dim, 0.0).astype(BF16)

    prm = dict(
        mix_norm_g=mix_norm_g, w_in=w_in[0].astype(BF16), short_w=hy_short_w[0], short_b=hy_short_b,
        skip=hy_skip[0], ln_g=gm_ln_g, ln_b=gm_ln_b,
        ws_stack=gm_w_s[0].reshape(GM_HEADS * GM_CHUNK, GM_CHUNK).astype(BF16),
        bias_t=jnp.repeat(gm_b_s[0].T, head_dim, axis=1),
        mix_out_g=mix_out_g, bd=bd, w_out=w_out[0].astype(BF16), ffn_norm_g=ffn_norm_g,
        wr_cat=jnp.concatenate([wr_hi, wr_lo], axis=1), br=br,
        w_e_gate=w_e_gate[0], w_e_up=w_e_up[0], w_e_down=w_e_down[0],
        final_norm_g=final_norm_g[None, :])
    y_prompt = _encoder(x_prompt, prm, kf, tables_bf, nf)
    y_sample = _encoder(x_sample, prm, kf, tables_bf, nf)
    return (y_prompt, y_sample)
```

```python
import functools
import math

import jax
import jax.numpy as jnp
from jax import lax
from jax.experimental import pallas as pl
from jax.experimental.pallas import tpu as pltpu

F32 = jnp.float32
BF16 = jnp.bfloat16
U32 = jnp.uint32
I32 = jnp.int32

EPS = 1e-6
HY_ORDER = 2
HY_HEADS = 8
GM_HEADS = 8
GM_CHUNK = 128
N_GROUPS = 4
EXPERTS_PER_GROUP = 8
N_EXPERTS = N_GROUPS * EXPERTS_PER_GROUP
TOP_K = 2
FILTER_EMB = 5
DECAY_TARGET = 1e-2
FAST_DECAY_PCT = 0.3
SLOW_DECAY_PCT = 1.5

LANES = 128
ROUTE_LANES = LANES
SLAB = 16
G_PITCH = SLAB + 8
CONV_LANES = 256
PHASE2_UNROLL = 16
ROW_TILE = 1024
GM_TILE = 1024
IN_A = 8
IN_SLABS = 4
EXPERT_ROWS = 1024
CHUNK = 8
DMA_QUEUES = 2
MOE_TILE = 512
STAGE_ROWS = MOE_TILE * TOP_K + N_EXPERTS * CHUNK
VMEM_LIMIT = 56 * 1024 * 1024


def _cparams(n_axes, vmem=None):
    return pltpu.CompilerParams(dimension_semantics=("arbitrary",) * n_axes,
                                vmem_limit_bytes=vmem)


def _inproj_kernel(x_ref, xp_ref, xn_ref, g_ref, w_ref, sw_ref, sb_ref, hy_ref, gm_ref, stage_ref,
                   *, nf, d_hy3, tiles_per_seq):
    i = pl.program_id(0)
    rows = x_ref.shape[0]
    pitch = stage_ref.shape[1] // IN_A
    halo = xp_ref.shape[0]
    x = jnp.concatenate([xp_ref[...], x_ref[...], xn_ref[...]], axis=0)
    ms = jnp.mean(x * x, axis=-1, keepdims=True)
    n = (x * lax.rsqrt(ms + EPS) * g_ref[...]).astype(BF16)
    gm = jnp.dot(n, w_ref[:, d_hy3:], preferred_element_type=F32)
    gm_ref[...] = gm[halo:halo + rows].astype(BF16)

    pos = i % tiles_per_seq
    has_prev = jnp.where(pos == 0, 0.0, 1.0)
    has_next = jnp.where(pos == tiles_per_seq - 1, 0.0, 1.0)
    row = lax.broadcasted_iota(I32, (rows, 1), 0)
    prev_w = jnp.where(row == 0, has_prev, 1.0)
    next_w = jnp.where(row == rows - 1, has_next, 1.0)
    n_slabs = stage_ref.shape[0]
    cols = n_slabs * LANES
    ext = rows + 2 * halo
    for c0 in range(0, d_hy3, cols):
        p = jnp.dot(n, w_ref[:, c0:c0 + cols], preferred_element_type=F32)
        prev = pltpu.roll(p, 1, 0)[halo:halo + rows] * prev_w
        nxt = pltpu.roll(p, ext - 1, 0)[halo:halo + rows] * next_w
        res = (sw_ref[0:1, c0:c0 + cols] * prev + sw_ref[1:2, c0:c0 + cols] * p[halo:halo + rows]
               + sw_ref[2:3, c0:c0 + cols] * nxt + sb_ref[:, c0:c0 + cols])
        for s in range(n_slabs):
            for al in range(IN_A):
                stage_ref[s, al * pitch:al * pitch + nf, :] = res[al * nf:(al + 1) * nf,
                                                                s * LANES:(s + 1) * LANES]
        for s in range(n_slabs):
            for b in range(nf):
                hy_ref[0, c0 // LANES + s, b] = stage_ref[s, pl.ds(b, IN_A, stride=pitch), :]


def _inproj(x2, g, w_bf, short_w, short_b, n_seq, seq_len, nf, d_hy3):
    t, d = x2.shape
    dp = w_bf.shape[1]
    rows = IN_A * nf
    tiles_per_seq = seq_len // rows
    hb = rows // 8
    nhb = t // 8
    full = lambda a: pl.BlockSpec(a.shape, lambda i: (0,) * a.ndim)
    return pl.pallas_call(
        functools.partial(_inproj_kernel, nf=nf, d_hy3=d_hy3, tiles_per_seq=tiles_per_seq),
        grid=(t // rows,),
        in_specs=[pl.BlockSpec((rows, d), lambda i: (i, 0)),
                  pl.BlockSpec((8, d), lambda i: (jnp.maximum(i * hb - 1, 0), 0)),
                  pl.BlockSpec((8, d), lambda i: (jnp.minimum((i + 1) * hb, nhb - 1), 0)),
                  full(g), full(w_bf), full(short_w), full(short_b)],
        out_specs=[pl.BlockSpec((1, d_hy3 // LANES, nf, IN_A, LANES),
                                lambda i: (i // tiles_per_seq, 0, 0, i % tiles_per_seq, 0)),
                   pl.BlockSpec((rows, dp - d_hy3), lambda i: (i, 0))],
        out_shape=[jax.ShapeDtypeStruct((n_seq, d_hy3 // LANES, nf, nf // 2, LANES), F32),
                   jax.ShapeDtypeStruct((t, dp - d_hy3), BF16)],
        scratch_shapes=[pltpu.VMEM((IN_SLABS, IN_A * (nf + 8), LANES), F32)],
        compiler_params=_cparams(1, VMEM_LIMIT),
        name="inproj_shortconv",
    )(x2, x2, x2, g, w_bf, short_w, short_b)


def _cos_sin(m, period):
    ang = m.astype(F32) * (2.0 * math.pi / period)
    return jnp.cos(ang), jnp.sin(ang)


def _stack_complex(mr, mi):
    top = jnp.concatenate([mr, -mi], axis=-1)
    bot = jnp.concatenate([mi, mr], axis=-1)
    return jnp.concatenate([top, bot], axis=-2)


def _dft_tables(nf):
    na = nf // 2
    n = nf * nf
    idx = jnp.arange(nf, dtype=I32)
    c, s = _cos_sin((idx[:, None] * idx[None, :]) % nf, nf)
    f2 = _stack_complex(c, -s)
    f2i = _stack_complex(c, s)
    b_ = idx[:, None, None]
    ka = idx[None, :, None]
    a_ = idx[None, None, :]
    m1 = (nf * a_ * ka + b_ * ka) % n
    c1, s1 = _cos_sin(m1, n)
    f1 = _stack_complex(c1[:, :, :na], -s1[:, :, :na])
    f1_full = jnp.concatenate([c1, -s1], axis=1)
    c3 = jnp.swapaxes(c1, 1, 2)[:, :na, :]
    s3 = jnp.swapaxes(s1, 1, 2)[:, :na, :]
    f3 = _stack_complex(c3, s3)
    return f1, f2, f2i, f3, f1_full


def _split_bf16(x):
    hi = x.astype(BF16)
    lo = (x - hi.astype(F32)).astype(BF16)
    return hi, lo


def _stack_split_lhs(a):
    hi, lo = _split_bf16(a)
    return jnp.concatenate([hi, lo, hi], axis=-1)


def _stack_split_rhs(b):
    hi, lo = _split_bf16(b)
    return jnp.concatenate([hi, hi, lo], axis=-2)


def _filter_taps_kernel(wemb_ref, bemb_ref, win_ref, bin_ref, freq_ref, wout_ref, delta_ref, o_ref,
                        *, seq_len, n_inner, d_hy):
    rows = o_ref.shape[1]
    j = pl.program_id(0) * rows + lax.broadcasted_iota(I32, (rows, 1), 0)
    tidx = jnp.where(j < seq_len, j, 2 * seq_len - j).astype(F32)
    t = tidx / float(seq_len - 1)
    fr0 = jnp.float32(1e-4)
    ang0 = (2.0 * math.pi / seq_len) * tidx * fr0
    ang1 = (2.0 * math.pi / seq_len) * tidx
    freq = freq_ref[...]
    pre = (t * wemb_ref[0:1, :] + jnp.cos(ang0) * wemb_ref[1:2, :] + jnp.cos(ang1) * wemb_ref[2:3, :]
           - jnp.sin(ang0) * wemb_ref[3:4, :] - jnp.sin(ang1) * wemb_ref[4:5, :] + bemb_ref[...])
    hdn = jnp.sin(freq * pre)
    for i in range(n_inner):
        hdn = jnp.sin(freq * (jnp.dot(_stack_split_lhs(hdn), win_ref[i], preferred_element_type=F32)
                              + bin_ref[i:i + 1, :]))
    h = jnp.dot(_stack_split_lhs(hdn), wout_ref[...], preferred_element_type=F32)
    decay = jnp.exp(-t * delta_ref[...])
    for o in range(HY_ORDER):
        fwd = h[:, (2 * o) * d_hy:(2 * o + 1) * d_hy]
        bwd = h[:, (2 * o + 1) * d_hy:(2 * o + 2) * d_hy]
        val = jnp.where(j == seq_len, 0.0, jnp.where(j < seq_len, fwd, bwd) * decay)
        for cc in range(d_hy // LANES):
            o_ref[o * (d_hy // LANES) + cc] = val[:, cc * LANES:(cc + 1) * LANES]


def _filter_taps(wemb, bemb, win, bin_, freq, wout, deltas, seq_len, d_hy):
    rows = 1024
    n2 = 2 * seq_len
    width = wemb.shape[1]
    n_inner = win.shape[0]
    full = lambda *shape: pl.BlockSpec(shape, lambda i: (0,) * len(shape))
    return pl.pallas_call(
        functools.partial(_filter_taps_kernel, seq_len=seq_len, n_inner=n_inner, d_hy=d_hy),
        grid=(n2 // rows,),
        in_specs=[full(FILTER_EMB, width), full(1, width), full(n_inner, 3 * width, width),
                  full(n_inner, width), full(1, width), full(3 * width, 2 * HY_ORDER * d_hy),
                  full(1, d_hy)],
        out_specs=pl.BlockSpec((HY_ORDER * d_hy // LANES, rows, LANES), lambda i: (0, i, 0)),
        out_shape=jax.ShapeDtypeStruct((HY_ORDER * d_hy // LANES, n2, LANES), F32),
        compiler_params=_cparams(1, VMEM_LIMIT),
        name="filter_taps",
    )(wemb, bemb, _stack_split_rhs(win), bin_, freq, _stack_split_rhs(wout), deltas)


def _filter_fft1_kernel(k_ref, f_ref, o_ref, *, nf):
    halves = k_ref.shape[0]
    k = k_ref.reshape(halves, nf * SLAB, LANES)
    g = o_ref.reshape(halves, 2, nf * SLAB, LANES)
    for bl in range(SLAB):
        col = jnp.concatenate([k[h, pl.ds(bl, nf, stride=SLAB), :] for h in range(halves)], axis=1)
        res = jnp.dot(f_ref[bl], _stack_split_rhs(col), preferred_element_type=F32)
        for h in range(halves):
            for ri in range(2):
                g[h, ri, pl.ds(bl, nf, stride=SLAB), :] = res[ri * nf:(ri + 1) * nf,
                                                              h * LANES:(h + 1) * LANES]


def _filter_fft2_kernel(g_ref, f_ref, skip_ref, o_ref, *, nf, scale):
    halves = g_ref.shape[0]
    lag0 = jnp.concatenate([jnp.broadcast_to(skip_ref[...], (nf, skip_ref.shape[1])),
                            jnp.zeros((nf, skip_ref.shape[1]), F32)], axis=0)
    for kl in range(SLAB):
        rhs = jnp.concatenate([jnp.concatenate([g_ref[h, ri, kl] for h in range(halves)], axis=1)
                               for ri in range(2)], axis=0)
        res = jnp.dot(f_ref[...], _stack_split_rhs(rhs), preferred_element_type=F32)
        o_ref[kl] = ((res + lag0) * scale).astype(o_ref.dtype)


def _filter_spectrum(taps, skip, f1_full, f2, nf):
    nch = taps.shape[0]
    c = nch * LANES
    halves = CONV_LANES // LANES
    k4 = taps.reshape(nch, nf, nf, LANES)
    g = pl.pallas_call(
        functools.partial(_filter_fft1_kernel, nf=nf),
        grid=(nf // SLAB, nch // halves),
        in_specs=[pl.BlockSpec((halves, nf, SLAB, LANES), lambda i, j: (j, 0, i, 0)),
                  pl.BlockSpec((SLAB, 2 * nf, 3 * nf), lambda i, j: (i, 0, 0))],
        out_specs=pl.BlockSpec((halves, 2, nf, SLAB, LANES), lambda i, j: (j, 0, 0, i, 0)),
        out_shape=jax.ShapeDtypeStruct((nch, 2, nf, nf, LANES), F32),
        compiler_params=_cparams(2, VMEM_LIMIT),
        name="filter_fft1",
    )(k4, _stack_split_lhs(f1_full))
    return pl.pallas_call(
        functools.partial(_filter_fft2_kernel, nf=nf, scale=1.0 / (nf * nf)),
        grid=(nf // SLAB, nch // halves),
        in_specs=[pl.BlockSpec((halves, 2, SLAB, nf, LANES), lambda i, j: (j, 0, i, 0, 0)),
                  pl.BlockSpec((2 * nf, 6 * nf), lambda i, j: (0, 0)),
                  pl.BlockSpec((1, CONV_LANES), lambda i, j: (0, j))],
        out_specs=pl.BlockSpec((SLAB, 2 * nf, CONV_LANES), lambda i, j: (i, 0, j)),
        out_shape=jax.ShapeDtypeStruct((nf, 2 * nf, c), BF16),
        compiler_params=_cparams(2, VMEM_LIMIT),
        name="filter_fft2",
    )(g, _stack_split_lhs(f2), skip)


def _pack(re, im):
    rb = lax.bitcast_convert_type(re.astype(BF16).astype(F32), U32)
    ib = lax.bitcast_convert_type(im.astype(BF16).astype(F32), U32)
    return rb | (ib >> 16)


def _unpack(w):
    re = lax.bitcast_convert_type(w & jnp.uint32(0xFFFF0000), F32).astype(BF16)
    im = lax.bitcast_convert_type(w << 16, F32).astype(BF16)
    return re, im


def _longconv_kernel(z1_ref, f1_ref, kf_ref, f2_ref, f2i_ref, f3_ref, gate_ref, o_ref, g_ref,
                     *, nf, out_bmajor):
    na = nf // 2
    ns = nf // SLAB
    halves = g_ref.shape[1]
    t = pl.program_id(2)

    def load_rows(ref5, s, bl):
        return jnp.concatenate([ref5[s, h, bl] for h in range(halves)], axis=1)

    def store_col(blk, bl, val):
        for h in range(halves):
            g_ref[blk, h, pl.ds(bl, nf, stride=G_PITCH), :] = val[:, h * LANES:(h + 1) * LANES]

    def load_col(blk, bl):
        return jnp.concatenate([g_ref[blk, h, pl.ds(bl, nf, stride=G_PITCH), :] for h in range(halves)],
                               axis=1)

    @pl.when(t < ns)
    def _phase1():
        for bl in range(SLAB):
            rhs = jnp.concatenate([load_rows(z1_ref, 0, bl), load_rows(z1_ref, 1, bl)],
                                  axis=0).astype(BF16)
            out = jnp.dot(f1_ref[bl], rhs, preferred_element_type=F32)
            store_col(t, bl, _pack(out[:nf], out[nf:]))

    @pl.when(jnp.logical_and(t >= ns, t < 2 * ns))
    def _phase2():
        i = t - ns

        def load_slab(ka):
            row0 = pl.multiple_of(ka * G_PITCH, CHUNK)
            return jnp.concatenate([g_ref[:, h, pl.ds(row0, SLAB), :].reshape(nf, LANES)
                                    for h in range(halves)], axis=1)

        def transform(w, kl):
            re, im = _unpack(w)
            s = jnp.dot(f2_ref[...], jnp.concatenate([re, im], axis=0),
                        preferred_element_type=F32)
            sr, si = s[:nf], s[nf:]
            kr, ki = kf_ref[kl, :nf, :], kf_ref[kl, nf:, :]
            pr = (sr * kr - si * ki).astype(BF16)
            pi = (sr * ki + si * kr).astype(BF16)
            h = jnp.dot(f2i_ref[...], jnp.concatenate([pr, pi], axis=0),
                        preferred_element_type=F32)
            return _pack(h[:nf], h[nf:])

        def store_slab(ka, packed):
            row0 = pl.multiple_of(ka * G_PITCH, CHUNK)
            for hf in range(halves):
                g_ref[:, hf, pl.ds(row0, SLAB), :] = packed[:, hf * LANES:(hf + 1) * LANES].reshape(
                    ns, SLAB, LANES)

        def body(grp, carry):
            kls = [grp * PHASE2_UNROLL + u for u in range(PHASE2_UNROLL)]
            ws = [load_slab(i * SLAB + kl) for kl in kls]
            outs = [transform(w, kl) for w, kl in zip(ws, kls)]
            for kl, packed in zip(kls, outs):
                store_slab(i * SLAB + kl, packed)
            return carry

        lax.fori_loop(0, SLAB // PHASE2_UNROLL, body, 0)

    @pl.when(t >= 2 * ns)
    def _phase3():
        j = t - 2 * ns
        out_flat = None if out_bmajor else o_ref.reshape(2, halves, na * SLAB, LANES)
        for bl in range(SLAB):
            re, im = _unpack(load_col(j, bl))
            y = jnp.dot(f3_ref[bl], jnp.concatenate([re, im], axis=0),
                        preferred_element_type=F32)
            for s in range(2):
                val = load_rows(gate_ref, s, bl) * y[s * na:(s + 1) * na]
                for h in range(halves):
                    piece = val[:, h * LANES:(h + 1) * LANES]
                    if out_bmajor:
                        o_ref[s, h, bl] = piece.astype(o_ref.dtype)
                    else:
                        out_flat[s, h, pl.ds(bl, na, stride=SLAB), :] = piece


def _longconv(gates5, z5, kf, tables, c, gate_blk, z_blk, kf_col, nf, out_bmajor):
    f1, f2, f2i, f3 = tables
    b, _, _, na, _ = z5.shape
    ns = nf // SLAB
    halves = CONV_LANES // LANES
    nchunk = c // CONV_LANES
    p1 = lambda t: jnp.minimum(t, ns - 1)
    p2 = lambda t: jnp.clip(t - ns, 0, ns - 1)
    p3 = lambda t: jnp.clip(t - 2 * ns, 0, ns - 1)
    cl = CONV_LANES
    blk5 = (2, halves, SLAB, na, LANES)
    if out_bmajor:
        out_spec = pl.BlockSpec(blk5, lambda p, q, t: (p, q, p3(t), 0, 0))
        out_shape = (b, c // LANES, nf, na, LANES)
    else:
        out_spec = pl.BlockSpec((2, halves, na, SLAB, LANES), lambda p, q, t: (p, q, 0, p3(t), 0))
        out_shape = (b, c // LANES, na, nf, LANES)
    return pl.pallas_call(
        functools.partial(_longconv_kernel, nf=nf, out_bmajor=out_bmajor),
        grid=(b // 2, nchunk, 3 * ns),
        in_specs=[
            pl.BlockSpec(blk5, lambda p, q, t: (p, z_blk + q, p1(t), 0, 0)),
            pl.BlockSpec((SLAB, 2 * nf, 2 * na), lambda p, q, t: (p1(t), 0, 0)),
            pl.BlockSpec((SLAB, 2 * nf, cl), lambda p, q, t: (p2(t), 0, kf_col + q)),
            pl.BlockSpec((2 * nf, 2 * nf), lambda p, q, t: (0, 0)),
            pl.BlockSpec((2 * nf, 2 * nf), lambda p, q, t: (0, 0)),
            pl.BlockSpec((SLAB, 2 * na, 2 * nf), lambda p, q, t: (p3(t), 0, 0)),
            pl.BlockSpec(blk5, lambda p, q, t: (p, gate_blk + q, p3(t), 0, 0)),
        ],
        out_specs=out_spec,
        out_shape=jax.ShapeDtypeStruct(out_shape, BF16 if out_bmajor else F32),
        scratch_shapes=[pltpu.VMEM((ns, halves, nf * G_PITCH, LANES), U32)],
        compiler_params=_cparams(3, VMEM_LIMIT),
        name="longconv",
    )(z5, f1, kf, f2, f2i, f3, gates5)


def _gelu_exact(x):
    return 0.5 * x * (1.0 + lax.erf(x * (1.0 / math.sqrt(2.0))))


def _gmlp_kernel(gm_ref, lng_ref, lnb_ref, ws_ref, bias_ref, o_ref, *, d_gm):
    g = _gelu_exact(gm_ref[...].astype(F32))
    u, v = g[:, :d_gm], g[:, d_gm:]
    mu = jnp.mean(v, axis=-1, keepdims=True)
    vc = v - mu
    var = jnp.mean(vc * vc, axis=-1, keepdims=True)
    vh = (vc * lax.rsqrt(var + EPS) * lng_ref[...] + lnb_ref[...]).astype(BF16)
    head_dim = d_gm // GM_HEADS
    per_group = CONV_LANES // head_dim
    head = lax.broadcasted_iota(I32, (GM_CHUNK, CONV_LANES), 1) // head_dim
    for c in range(gm_ref.shape[0] // GM_CHUNK):
        rows = slice(c * GM_CHUNK, (c + 1) * GM_CHUNK)
        groups = []
        for grp in range(GM_HEADS // per_group):
            lanes = slice(grp * CONV_LANES, (grp + 1) * CONV_LANES)
            w = ws_ref[grp * per_group * GM_CHUNK:(grp + 1) * per_group * GM_CHUNK, :]
            r = jnp.dot(w, vh[rows, lanes], preferred_element_type=F32)
            s = r[:GM_CHUNK]
            for h in range(1, per_group):
                s = jnp.where(head == h, r[h * GM_CHUNK:(h + 1) * GM_CHUNK], s)
            groups.append(s)
        o_ref[rows, :] = u[rows] * (jnp.concatenate(groups, axis=1) + bias_ref[...])


def _gmlp(gm, ln_g, ln_b, ws_stack, bias_t):
    t, c2 = gm.shape
    d_gm = c2 // 2
    return pl.pallas_call(
        functools.partial(_gmlp_kernel, d_gm=d_gm),
        grid=(t // GM_TILE,),
        in_specs=[pl.BlockSpec((GM_TILE, c2), lambda i: (i, 0)),
                  pl.BlockSpec((1, d_gm), lambda i: (0, 0)),
                  pl.BlockSpec((1, d_gm), lambda i: (0, 0)),
                  pl.BlockSpec(ws_stack.shape, lambda i: (0, 0)),
                  pl.BlockSpec(bias_t.shape, lambda i: (0, 0))],
        out_specs=pl.BlockSpec((GM_TILE, d_gm), lambda i: (i, 0)),
        out_shape=jax.ShapeDtypeStruct((t, d_gm), F32),
        compiler_params=_cparams(1, VMEM_LIMIT),
        name="gmlp",
    )(gm, ln_g, ln_b, ws_stack, bias_t)


def _head_rms(y, bd):
    ms = jnp.dot((y * y).astype(BF16), bd, preferred_element_type=F32)
    return y * lax.rsqrt(ms + EPS)


def _route(logits):
    lane = lax.broadcasted_iota(I32, logits.shape, 1)
    neg = jnp.float32(-1e30)
    big = jnp.int32(ROUTE_LANES)
    gmask = lane < N_GROUPS
    gl = jnp.where(gmask, logits, neg)
    gmax = jnp.max(gl, axis=-1, keepdims=True)
    grp = jnp.min(jnp.where(jnp.logical_and(gl == gmax, gmask), lane, big), axis=-1, keepdims=True)
    psum = jnp.sum(jnp.where(gmask, jnp.exp(gl - gmax), 0.0), axis=-1, keepdims=True)
    p_grp = 1.0 / psum
    lo = N_GROUPS + EXPERTS_PER_GROUP * grp
    emask = jnp.logical_and(lane >= lo, lane < lo + EXPERTS_PER_GROUP)
    el = jnp.where(emask, logits, neg)
    m1 = jnp.max(el, axis=-1, keepdims=True)
    i1 = jnp.min(jnp.where(jnp.logical_and(el == m1, emask), lane, big), axis=-1, keepdims=True)
    emask2 = jnp.logical_and(emask, lane != i1)
    el2 = jnp.where(emask2, logits, neg)
    m2 = jnp.max(el2, axis=-1, keepdims=True)
    i2 = jnp.min(jnp.where(jnp.logical_and(el2 == m2, emask2), lane, big), axis=-1, keepdims=True)
    d = jnp.exp(m2 - m1)
    g1 = p_grp * (1.0 / (1.0 + d))
    g2 = p_grp * (d / (1.0 + d))
    e1 = (i1 - N_GROUPS).astype(F32)
    e2 = (i2 - N_GROUPS).astype(F32)
    return jnp.where(lane == 0, e1, jnp.where(lane == 1, e2, jnp.where(lane == 2, g1,
                     jnp.where(lane == 3, g2, 0.0))))


def _lane_cumsum(v):
    lane = lax.broadcasted_iota(I32, v.shape, 1)
    sh = 1
    while sh < v.shape[1]:
        v = v + jnp.where(lane >= sh, pltpu.roll(v, sh, 1), 0.0)
        sh *= 2
    return v


def _outproj_kernel(x_ref, yh_ref, yg_ref, mg_ref, bd_ref, wo_ref, fg_ref, wr_ref, br_ref,
                    x1_ref, stage_ref, route_ref, count_ref, *, d_hy):
    bd = bd_ref[...]
    mg = mg_ref[...]
    wo = wo_ref[...]
    wr = wr_ref[...]

    yh = jnp.concatenate([yh_ref[0, c] for c in range(yh_ref.shape[1])], axis=1)
    mh = (_head_rms(yh, bd) * mg[:, :d_hy]).astype(BF16)
    mgm = (_head_rms(yg_ref[...], bd) * mg[:, d_hy:]).astype(BF16)
    x1 = (x_ref[...] + jnp.dot(mh, wo[:d_hy], preferred_element_type=F32)
          + jnp.dot(mgm, wo[d_hy:], preferred_element_type=F32))
    x1_ref[...] = x1
    ms = jnp.mean(x1 * x1, axis=-1, keepdims=True)
    hi, lo = _split_bf16(x1 * lax.rsqrt(ms + EPS) * fg_ref[...])
    hw = jnp.dot(hi, wr, preferred_element_type=F32)
    logits = (hw[:, :ROUTE_LANES] + hw[:, ROUTE_LANES:]
              + jnp.dot(lo, wr[:, :ROUTE_LANES], preferred_element_type=F32) + br_ref[...])
    route = _route(logits)

    lane = lax.broadcasted_iota(I32, (MOE_TILE, ROUTE_LANES), 1)
    r = lax.broadcasted_iota(I32, (MOE_TILE, MOE_TILE), 0)
    c = lax.broadcasted_iota(I32, (MOE_TILE, MOE_TILE), 1)
    ltri = jnp.where(c < r, 1.0, 0.0).astype(BF16)
    srow = lax.broadcasted_iota(I32, (STAGE_ROWS, MOE_TILE), 0)
    mlane = lax.broadcasted_iota(I32, (STAGE_ROWS, LANES), 1)
    half = hi.shape[1] // 2
    subs = route.shape[0] // MOE_TILE
    for sub in range(subs):
        rows = slice(sub * MOE_TILE, (sub + 1) * MOE_TILE)
        rsub = route[rows]
        oh0 = lane == rsub[:, 0:1].astype(I32)
        oh1 = lane == rsub[:, 1:2].astype(I32)
        oh = jnp.where(jnp.logical_or(oh0, oh1), 1.0, 0.0)
        before = jnp.dot(ltri, oh.astype(BF16), preferred_element_type=F32)
        cnt = jnp.sum(oh, axis=0, keepdims=True)
        run = jnp.floor((cnt + (CHUNK - 1)) * (1.0 / CHUNK)) * CHUNK
        run_start = _lane_cumsum(jnp.broadcast_to(run, (8, run.shape[1])))[0:1] - run
        tot = before + run_start
        loc0 = jnp.sum(jnp.where(oh0, tot, 0.0), axis=-1, keepdims=True)
        loc1 = jnp.sum(jnp.where(oh1, tot, 0.0), axis=-1, keepdims=True)
        rsub = jnp.where(lane == 4, loc0, jnp.where(lane == 5, loc1, rsub))
        route_ref[rows, :] = rsub
        count_ref[pl.ds(pl.program_id(0) * subs + sub, 1), :] = cnt

        rt = jnp.transpose(rsub)
        p0 = srow == rt[4:5, :].astype(I32)
        p1 = srow == rt[5:6, :].astype(I32)
        perm = jnp.where(jnp.logical_or(p0, p1), 1.0, 0.0).astype(BF16)
        staged = jnp.dot(perm, hi[rows], preferred_element_type=F32)
        gate = jnp.sum(jnp.where(p0, rt[2:3, :], 0.0) + jnp.where(p1, rt[3:4, :], 0.0), axis=-1,
                       keepdims=True)
        meta = jnp.where(mlane == 0,
                         lax.bitcast_convert_type(jnp.broadcast_to(gate, (STAGE_ROWS, LANES)), U32),
                         jnp.uint32(0))
        stage_ref[sub * STAGE_ROWS:(sub + 1) * STAGE_ROWS, :] = jnp.concatenate(
            [_pack(staged[:, :half], staged[:, half:]), meta], axis=1)


def _outproj(x2, yh, yg, mix_g, bd, wo_bf, ffn_g, wr_cat, br):
    t, d = x2.shape
    _, nch, seq_len, _ = yh.shape
    d_hy = nch * LANES
    tiles_per_seq = seq_len // ROW_TILE
    n_tiles = t // ROW_TILE
    row = lambda w: pl.BlockSpec((ROW_TILE, w), lambda i: (i, 0))
    full = lambda a: pl.BlockSpec(a.shape, lambda i: (0,) * a.ndim)
    sw = d // 2 + LANES
    subs = ROW_TILE // MOE_TILE
    n_sub = t // MOE_TILE
    return pl.pallas_call(
        functools.partial(_outproj_kernel, d_hy=d_hy),
        grid=(n_tiles,),
        in_specs=[row(d),
                  pl.BlockSpec((1, nch, ROW_TILE, LANES),
                               lambda i: (i // tiles_per_seq, 0, i % tiles_per_seq, 0)),
                  row(yg.shape[1]), full(mix_g), full(bd), full(wo_bf), full(ffn_g),
                  full(wr_cat), full(br)],
        out_specs=[row(d), pl.BlockSpec((subs * STAGE_ROWS, sw), lambda i: (i, 0)), row(ROUTE_LANES),
                   pl.BlockSpec((n_sub, ROUTE_LANES), lambda i: (0, 0))],
        out_shape=[jax.ShapeDtypeStruct((t, d), F32),
                   jax.ShapeDtypeStruct((n_sub * STAGE_ROWS, sw), U32),
                   jax.ShapeDtypeStruct((t, ROUTE_LANES), F32),
                   jax.ShapeDtypeStruct((n_sub, ROUTE_LANES), F32)],
        compiler_params=_cparams(1, VMEM_LIMIT),
        name="outproj_router",
    )(x2, yh, yg, mix_g, bd, wo_bf, ffn_g, wr_cat, br)


def _moe_plan(tile_cnt, n_tokens):
    n_tiles = tile_cnt.shape[0]
    cnt = tile_cnt[:, :N_EXPERTS].astype(I32)
    run = (cnt + CHUNK - 1) // CHUNK * CHUNK
    run_start = jnp.cumsum(run, axis=1) - run
    used = jnp.sum(run, axis=1)
    e_rows = jnp.sum(run, axis=0)
    e_pad = (e_rows + EXPERT_ROWS - 1) // EXPERT_ROWS * EXPERT_ROWS
    e_end = jnp.cumsum(e_pad)
    e_start = e_end - e_pad
    n_blocks = -(-(n_tokens * TOP_K + n_tiles * N_EXPERTS * (CHUNK - 1)) // EXPERT_ROWS) + N_EXPERTS
    n_used = (e_end[-1:] // EXPERT_ROWS).astype(I32)
    block_row = jnp.arange(n_blocks, dtype=I32) * EXPERT_ROWS
    block_e = jnp.minimum(jnp.sum((e_end[None, :] <= block_row[:, None]).astype(I32), axis=1),
                          N_EXPERTS - 1)
    run_end_in_e = jnp.cumsum(run, axis=0)
    chunk_row = jnp.arange(n_blocks * EXPERT_ROWS // CHUNK, dtype=I32) * CHUNK
    e_of_chunk = jnp.repeat(block_e, EXPERT_ROWS // CHUNK)
    e_sel = jnp.arange(N_EXPERTS, dtype=I32)[None, :] == e_of_chunk[:, None]
    pick = lambda tbl: jnp.dot(e_sel.astype(F32), tbl.astype(F32),
                               precision=lax.Precision.HIGHEST).astype(I32)
    q = chunk_row - pick(e_start[:, None])[:, 0]
    valid = q < pick(e_rows[:, None])[:, 0]
    ends = pick(run_end_in_e.T)
    tile = jnp.minimum(jnp.sum((ends <= q[:, None]).astype(I32), axis=1), n_tiles - 1)
    t_sel = jnp.arange(n_tiles, dtype=I32)[None, :] == tile[:, None]
    at_tile = lambda tbl: jnp.sum(jnp.where(t_sel, pick(tbl.T), 0), axis=1)
    within = q - (at_tile(run_end_in_e) - at_tile(run))
    src_row = tile * STAGE_ROWS + at_tile(run_start) + within
    zero_chunk = STAGE_ROWS // CHUNK - 1
    per_block = EXPERT_ROWS // CHUNK
    src_chunk = jnp.where(valid, src_row // CHUNK, zero_chunk).reshape(n_blocks, 1, per_block)
    dst_chunk = jnp.where(valid, src_row // CHUNK, 0).reshape(n_blocks, 1, per_block)
    n_valid = jnp.sum(valid.reshape(n_blocks, per_block).astype(I32), axis=1)
    return (block_e.astype(I32), n_used, n_valid.astype(I32), src_chunk.astype(I32),
            dst_chunk.astype(I32), used.astype(I32))


def _ffn_kernel(be_ref, nu_ref, nv_ref, used_ref, src_now_ref, src_next_ref, dst_ref, stage_ref,
                wg_ref, wu_ref, wd_ref, ost_ref, xbuf, obuf, zero_ref, wgb, wub, wdb,
                in_sem, out_sem, zero_sem):
    b = pl.program_id(0)
    nu = nu_ref[0]
    per_block = dst_ref.shape[-1]
    slot = b % 2

    def fetch(map_ref, s):
        def body(i, carry):
            for prio in range(DMA_QUEUES):
                j = i * DMA_QUEUES + prio
                row = pl.multiple_of(map_ref[0, 0, j] * CHUNK, CHUNK)
                pltpu.make_async_copy(stage_ref.at[pl.ds(row, CHUNK)], xbuf.at[s, pl.ds(j * CHUNK, CHUNK)],
                                      in_sem.at[s]).start(priority=prio)
            return carry
        lax.fori_loop(0, per_block // DMA_QUEUES, body, 0, unroll=4)

    def out_copy(j, chunk, s):
        row = pl.multiple_of(chunk * CHUNK, CHUNK)
        return pltpu.make_async_copy(obuf.at[s, pl.ds(j * CHUNK, CHUNK)], ost_ref.at[pl.ds(row, CHUNK)],
                                     out_sem.at[s])

    def drain(s, n):
        @pl.when(n == per_block)
        def _():
            pltpu.make_async_copy(obuf.at[s], ost_ref.at[pl.ds(0, EXPERT_ROWS)], out_sem.at[s]).wait()

        @pl.when(n != per_block)
        def _():
            def body(j, carry):
                out_copy(0, 0, s).wait()
                return carry
            lax.fori_loop(0, n, body, 0)

    def zero_tails(wait):
        def tile(i, carry):
            def chunk(c, carry2):
                row = pl.multiple_of(i * STAGE_ROWS + c * CHUNK, CHUNK)
                cp = pltpu.make_async_copy(zero_ref, ost_ref.at[pl.ds(row, CHUNK)], zero_sem)
                if wait:
                    cp.wait()
                else:
                    cp.start()
                return carry2
            return lax.fori_loop(used_ref[i] // CHUNK, STAGE_ROWS // CHUNK, chunk, carry)
        lax.fori_loop(0, used_ref.shape[0], tile, 0)

    @pl.when(b == 0)
    def _prologue():
        fetch(src_now_ref, 0)
        zero_ref[...] = jnp.zeros_like(zero_ref)
        zero_tails(wait=False)
        zero_tails(wait=True)

    @pl.when(b + 1 < nu)
    def _prefetch_next():
        fetch(src_next_ref, 1 - slot)

    @pl.when(b < nu)
    def _compute():
        pltpu.make_async_copy(stage_ref.at[pl.ds(0, EXPERT_ROWS)], xbuf.at[slot], in_sem.at[slot]).wait()

        @pl.when(b >= 2)
        def _():
            drain(slot, nv_ref[b - 2])

        @pl.when(jnp.logical_or(b == 0, be_ref[b] != be_ref[jnp.maximum(b - 1, 0)]))
        def _new_expert():
            wgb[...] = wg_ref[0].astype(BF16)
            wub[...] = wu_ref[0].astype(BF16)
            wdb[...] = wd_ref[0].astype(BF16)

        x = xbuf[slot]
        half = wg_ref.shape[1] // 2
        xb = jnp.concatenate(_unpack(x[:, :half]), axis=1)
        gate = lax.bitcast_convert_type(x[:, half:half + 1], F32)
        g = jnp.dot(xb, wgb[...], preferred_element_type=F32)
        u = jnp.dot(xb, wub[...], preferred_element_type=F32)
        h = (g * (1.0 / (1.0 + jnp.exp(-g))) * u).astype(BF16)
        o = jnp.dot(h, wdb[...], preferred_element_type=F32) * gate
        obuf[slot] = _pack(o[:, :half], o[:, half:])

        def body(j, carry):
            out_copy(j, dst_ref[0, 0, j], slot).start()
            return carry

        def pair(i, carry):
            for prio in range(DMA_QUEUES):
                j = i * DMA_QUEUES + prio
                out_copy(j, dst_ref[0, 0, j], slot).start(priority=prio)
            return carry

        @pl.when(nv_ref[b] == per_block)
        def _():
            lax.fori_loop(0, per_block // DMA_QUEUES, pair, 0, unroll=4)

        @pl.when(nv_ref[b] != per_block)
        def _():
            lax.fori_loop(0, nv_ref[b], body, 0)

        @pl.when(b == nu - 1)
        def _():
            drain(slot, nv_ref[b])

            @pl.when(b >= 1)
            def _():
                drain(1 - slot, nv_ref[b - 1])


def _expert_ffn(stage, plan, wg, wu, wd):
    block_e, n_used, n_valid, src_chunk, dst_chunk, used = plan
    nb = block_e.shape[0]
    sw = stage.shape[1]
    d, de = wg.shape[1], wg.shape[2]
    per_block = EXPERT_ROWS // CHUNK
    cur = lambda i, be, nu: jnp.minimum(i, nu[0] - 1)
    wspec = lambda shape: pl.BlockSpec(shape, lambda i, be, nu, nv, us: (be[cur(i, be, nu)], 0, 0))
    smem = lambda imap: pl.BlockSpec((1, 1, per_block), imap, memory_space=pltpu.SMEM)
    return pl.pallas_call(
        _ffn_kernel,
        grid_spec=pltpu.PrefetchScalarGridSpec(
            num_scalar_prefetch=4,
            grid=(nb,),
            in_specs=[smem(lambda i, be, nu, nv, us: (i, 0, 0)),
                      smem(lambda i, be, nu, nv, us: (jnp.minimum(i + 1, nb - 1), 0, 0)),
                      smem(lambda i, be, nu, nv, us: (i, 0, 0)),
                      pl.BlockSpec(memory_space=pl.ANY),
                      wspec((1, d, de)), wspec((1, d, de)), wspec((1, de, d))],
            out_specs=pl.BlockSpec(memory_space=pl.ANY),
            scratch_shapes=[pltpu.VMEM((2, EXPERT_ROWS, sw), U32), pltpu.VMEM((2, EXPERT_ROWS, d // 2), U32),
                            pltpu.VMEM((CHUNK, d // 2), U32),
                            pltpu.VMEM((d, de), BF16), pltpu.VMEM((d, de), BF16), pltpu.VMEM((de, d), BF16),
                            pltpu.SemaphoreType.DMA((2,)), pltpu.SemaphoreType.DMA((2,)),
                            pltpu.SemaphoreType.DMA(())]),
        out_shape=jax.ShapeDtypeStruct((stage.shape[0], d // 2), U32),
        compiler_params=_cparams(1, VMEM_LIMIT),
        name="expert_ffn",
    )(block_e, n_used, n_valid, used, src_chunk, src_chunk, dst_chunk, stage, wg, wu, wd)


def _combine_kernel(ost_ref, route_ref, x1_ref, fg_ref, o_ref):
    col = lax.broadcasted_iota(I32, (MOE_TILE, STAGE_ROWS), 1)
    for sub in range(route_ref.shape[0] // MOE_TILE):
        rows = slice(sub * MOE_TILE, (sub + 1) * MOE_TILE)
        route = route_ref[rows, :]
        sel = jnp.logical_or(col == route[:, 4:5].astype(I32), col == route[:, 5:6].astype(I32))
        perm = jnp.where(sel, 1.0, 0.0).astype(BF16)
        ost = ost_ref[sub * STAGE_ROWS:(sub + 1) * STAGE_ROWS, :]
        y = jnp.dot(perm, jnp.concatenate(_unpack(ost), axis=1), preferred_element_type=F32)
        x2 = x1_ref[rows, :] + y
        ms = jnp.mean(x2 * x2, axis=-1, keepdims=True)
        o_ref[rows, :] = x2 * lax.rsqrt(ms + EPS) * fg_ref[...]


def _combine(ostage, route, x1, final_g):
    t, d = x1.shape
    return pl.pallas_call(
        _combine_kernel,
        grid=(t // GM_TILE,),
        in_specs=[pl.BlockSpec((GM_TILE // MOE_TILE * STAGE_ROWS, d // 2), lambda i: (i, 0)),
                  pl.BlockSpec((GM_TILE, ROUTE_LANES), lambda i: (i, 0)),
                  pl.BlockSpec((GM_TILE, d), lambda i: (i, 0)),
                  pl.BlockSpec((1, d), lambda i: (0, 0))],
        out_specs=pl.BlockSpec((GM_TILE, d), lambda i: (i, 0)),
        out_shape=jax.ShapeDtypeStruct((t, d), F32),
        compiler_params=_cparams(1, VMEM_LIMIT),
        name="moe_combine",
    )(ostage, route, x1, final_g)


def _hier_moe_and_norm(x1, stage, route, tile_cnt, wg, wu, wd, final_g):
    t, d = x1.shape
    ostage = _expert_ffn(stage, _moe_plan(tile_cnt, t), wg, wu, wd)
    return _combine(ostage, route, x1, final_g)


def _encoder(x, prm, kf, tables, nf):
    b, l, d = x.shape
    t = b * l
    d_hy = prm["skip"].shape[1]
    x2 = x.reshape(t, d)
    gates5, gm = _inproj(x2, prm["mix_norm_g"], prm["w_in"], prm["short_w"], prm["short_b"], b, l, nf,
                         (HY_ORDER + 1) * d_hy)
    z5 = gates5
    z_blk = HY_ORDER * d_hy // CONV_LANES
    for o in range(HY_ORDER):
        z5 = _longconv(gates5, z5, kf, tables, d_hy, gate_blk=o * d_hy // CONV_LANES,
                       z_blk=z_blk, kf_col=o * d_hy // CONV_LANES, nf=nf, out_bmajor=o < HY_ORDER - 1)
        z_blk = 0
    y_hy = z5.reshape(b, d_hy // LANES, l, LANES)
    y_gm = _gmlp(gm, prm["ln_g"], prm["ln_b"], prm["ws_stack"], prm["bias_t"])
    x1, stage, route, tile_cnt = _outproj(x2, y_hy, y_gm, prm["mix_out_g"], prm["bd"], prm["w_out"], prm["ffn_norm_g"],
                             prm["wr_cat"], prm["br"])
    out = _hier_moe_and_norm(x1, stage, route, tile_cnt, prm["w_e_gate"], prm["w_e_up"], prm["w_e_down"],
                             prm["final_norm_g"])
    return out.reshape(b, l, d)


def kernel(x_prompt, x_sample, mix_norm_g, w_in, hy_short_w, hy_short_b, hy_filt_w_emb, hy_filt_b_emb,
           hy_filt_w_inner, hy_filt_b_inner, hy_filt_freq, hy_filt_w_out, hy_skip, gm_ln_g, gm_ln_b,
           gm_w_s, gm_b_s, mix_out_g, w_out, ffn_norm_g, w_group, b_group, w_expert_router,
           b_expert_router, w_e_gate, w_e_up, w_e_down, final_norm_g):
    assert w_in.shape[0] == 1, "one layer"
    l = x_prompt.shape[1]
    assert x_sample.shape[1] == l
    nf = math.isqrt(2 * l)
    assert nf * nf == 2 * l and nf % SLAB == 0
    d = x_prompt.shape[2]
    d_hy = hy_skip.shape[2]
    d_gm = gm_ln_g.shape[1]
    head_dim = d_gm // GM_HEADS
    assert d_hy // HY_HEADS == head_dim and d_hy == d_gm
    assert x_prompt.shape[0] % 2 == 0 and x_sample.shape[0] % 2 == 0, "sequences are convolved in pairs"
    assert l % (IN_A * nf) == 0 and l % ROW_TILE == 0 and l % GM_TILE == 0 and ROW_TILE % MOE_TILE == 0
    assert d_hy % CONV_LANES == 0 and gm_w_s.shape[2] == GM_CHUNK and w_e_gate.shape[1] == N_EXPERTS

    tables = _dft_tables(nf)
    f1, f2, f2i, f3, f1_full = tables
    tables_bf = tuple(a.astype(BF16) for a in (f1, f2, f2i, f3))

    max_decay = math.log(DECAY_TARGET) / FAST_DECAY_PCT
    min_decay = math.log(DECAY_TARGET) / SLOW_DECAY_PCT
    deltas = jnp.abs(jnp.linspace(min_decay, max_decay, d_hy, dtype=F32))[None, :]
    taps = _filter_taps(hy_filt_w_emb[0], hy_filt_b_emb[0][None, :], hy_filt_w_inner[0], hy_filt_b_inner[0],
                        hy_filt_freq[0][None, :], hy_filt_w_out[0], deltas, l, d_hy)
    kf = _filter_spectrum(taps, hy_skip[0].reshape(1, HY_ORDER * d_hy), f1_full, f2, nf)

    n_route = N_GROUPS + N_EXPERTS
    wr = jnp.zeros((d, ROUTE_LANES), F32).at[:, :n_route].set(
        jnp.concatenate([w_group[0], w_expert_router[0]], axis=1))
    wr_hi = wr.astype(BF16)
    wr_lo = (wr - wr_hi.astype(F32)).astype(BF16)
    br = jnp.zeros((1, ROUTE_LANES), F32).at[0, :n_route].set(jnp.concatenate([b_group[0], b_expert_router[0]]))
    hid = jnp.arange(d_hy, dtype=I32) // head_dim
    bd = jnp.where(hid[:, None] == hid[None, :], 1.0 / head_dim, 0.0).astype(BF16)

    prm = dict(
        mix_norm_g=mix_norm_g, w_in=w_in[0].astype(BF16), short_w=hy_short_w[0], short_b=hy_short_b,
        skip=hy_skip[0], ln_g=gm_ln_g, ln_b=gm_ln_b,
        ws_stack=gm_w_s[0].reshape(GM_HEADS * GM_CHUNK, GM_CHUNK).astype(BF16),
        bias_t=jnp.repeat(gm_b_s[0].T, head_dim, axis=1),
        mix_out_g=mix_out_g, bd=bd, w_out=w_out[0].astype(BF16), ffn_norm_g=ffn_norm_g,
        wr_cat=jnp.concatenate([wr_hi, wr_lo], axis=1), br=br,
        w_e_gate=w_e_gate[0], w_e_up=w_e_up[0], w_e_down=w_e_down[0],
        final_norm_g=final_norm_g[None, :])
    y_prompt = _encoder(x_prompt, prm, kf, tables_bf, nf)
    y_sample = _encoder(x_sample, prm, kf, tables_bf, nf)
    return (y_prompt, y_sample)
```
